```python
import math
import jax, jax.numpy as jnp
from jax import lax
import numpy as np

D_MODEL = 1024
BATCH = 2
SEQ = 8192
DEPTH = 4
DEC_BATCH = 32
DEC_SEQ = 4
PAST_LEN = 8192
PAGE_SIZE = 128

N_MIXERS = 3
HEAD_DIM = 64
ROPE_THETA = 10000.0
NORM_EPS = 1e-6
NEG_INF = -1e30
Q_BLOCK = 128
NSA_HEADS = 16
NSA_GROUPS = 4
NSA_REP = NSA_HEADS // NSA_GROUPS
NSA_BLOCK = 64
NSA_TOPN = 16
NSA_WINDOW = 512
NSA_FORCED = 1e9
NSA_IN = NSA_HEADS * HEAD_DIM + 6 * NSA_GROUPS * HEAD_DIM + 3 * NSA_HEADS
DIFF_HEADS = D_MODEL // (2 * HEAD_DIM)
DIFF_WIDTH = DIFF_HEADS * 2 * HEAD_DIM
GLA_HEADS = 4
GLA_DK = D_MODEL // 2 // GLA_HEADS
GLA_DV = D_MODEL // GLA_HEADS
GLA_RANK = 16
GLA_TAU = 16.0
GLA_CHUNK = 64
GLA_IN = 2 * GLA_HEADS * GLA_DK + 2 * GLA_HEADS * GLA_DV + GLA_RANK
D_FF = ((8 * D_MODEL // 3 + 127) // 128) * 128
CONV_W = 3

kernel_name = 'hybrid_nsa_diff_gla_convffn_step'


def rmsnorm(x, g):
    xf = x.astype(jnp.float32)
    y = xf * lax.rsqrt(jnp.mean(xf * xf, axis=-1, keepdims=True) + NORM_EPS)
    return (y * g.astype(jnp.float32)).astype(x.dtype)


def rope(x, pos):
    half = x.shape[-1] // 2
    inv = ROPE_THETA ** (-jnp.arange(half, dtype=jnp.float32) / half)
    ang = pos.astype(jnp.float32)[:, None] * inv
    shape = (1, x.shape[1]) + (1,) * (x.ndim - 3) + (half,)
    cos, sin = jnp.cos(ang).reshape(shape), jnp.sin(ang).reshape(shape)
    xf = x.astype(jnp.float32)
    x1, x2 = xf[..., :half], xf[..., half:]
    return jnp.concatenate([x1 * cos - x2 * sin, x1 * sin + x2 * cos], axis=-1).astype(x.dtype)


def nsa_project(h, pos, w_in, qk_g):
    b, t, _ = h.shape
    proj = h @ w_in
    nq, nkv = NSA_HEADS * HEAD_DIM, 6 * NSA_GROUPS * HEAD_DIM
    q = rmsnorm(proj[..., :nq].reshape(b, t, NSA_GROUPS, NSA_REP, HEAD_DIM), qk_g[0])
    kv = proj[..., nq:nq + nkv].reshape(b, t, 6, NSA_GROUPS, HEAD_DIM)
    gates = jax.nn.sigmoid(proj[..., nq + nkv:].reshape(b, t, NSA_GROUPS, NSA_REP, 3))
    k_sel = rope(rmsnorm(kv[:, :, 2], qk_g[1]), pos)
    k_win = rope(rmsnorm(kv[:, :, 4], qk_g[2]), pos)
    rows = jnp.stack([kv[:, :, 0], kv[:, :, 1], k_sel, kv[:, :, 3]], axis=2)
    win = jnp.stack([k_win, kv[:, :, 5]], axis=2)
    return q, rope(q, pos), gates, rows, win


def nsa_keys(rows, pe, w_phi, g_c):
    b, t = rows.shape[:2]
    nc = t // NSA_BLOCK
    ns = -(-t // NSA_BLOCK)
    blk = rows[:, :nc * NSA_BLOCK].reshape(b, nc, NSA_BLOCK, 4, NSA_GROUPS, HEAD_DIM)
    kc = rmsnorm(jnp.einsum('bnlgd,lde->bnge', blk[:, :, :, 0] + pe[0][:, None, :], w_phi[0]), g_c)
    vc = jnp.einsum('bnlgd,lde->bnge', blk[:, :, :, 1] + pe[1][:, None, :], w_phi[1])
    sel = jnp.pad(rows[:, :, 2:4], ((0, 0), (0, ns * NSA_BLOCK - t), (0, 0), (0, 0), (0, 0)))
    sel = sel.reshape(b, ns, NSA_BLOCK, 2, NSA_GROUPS, HEAD_DIM).transpose(3, 0, 4, 1, 2, 5)
    return kc, vc, sel[0], sel[1]


def nsa_attend(q, qr, gates, qpos, kc, vc, ksb, vsb, kw, vw, wpos):
    f32 = jnp.float32
    scale = HEAD_DIM ** -0.5
    b = q.shape[0]
    nc, ns = kc.shape[1], ksb.shape[2]
    s_c = jnp.einsum('bqgrd,bngd->bqgrn', q, kc).astype(f32) * scale
    c_ok = (((jnp.arange(nc) + 1) * NSA_BLOCK - 1)[None, :] <= qpos[:, None])[None, :, None, None, :]
    p_c = jax.nn.softmax(jnp.where(c_ok, s_c, NEG_INF), axis=-1) * c_ok
    o_c = jnp.einsum('bqgrn,bngd->bqgrd', p_c.astype(vc.dtype), vc)
    imp = jnp.pad(p_c.sum(axis=3), ((0, 0), (0, 0), (0, 0), (0, ns - nc)))
    blk = jnp.arange(ns)[None, :]
    cur = (qpos // NSA_BLOCK)[:, None]
    forced = ((blk == 0) | (blk == cur) | (blk == cur - 1))[None, :, None, :]
    imp = jnp.where(forced, NSA_FORCED, imp)
    imp = jnp.where((blk <= cur)[None, :, None, :], imp, NEG_INF)
    _, idx = lax.top_k(imp, min(NSA_TOPN, ns))
    bi = jnp.arange(b)[:, None, None, None]
    gi = jnp.arange(NSA_GROUPS)[None, None, :, None]
    k_sel = ksb[bi, gi, idx]
    v_sel = vsb[bi, gi, idx]
    s_s = jnp.einsum('bqgrd,bqgnld->bqgrnl', qr, k_sel).astype(f32) * scale
    kpos = idx[..., None] * NSA_BLOCK + jnp.arange(NSA_BLOCK)
    s_ok = kpos[:, :, :, None] <= qpos[None, :, None, None, None, None]
    p_s = jax.nn.softmax(jnp.where(s_ok, s_s, NEG_INF), axis=(-2, -1))
    o_s = jnp.einsum('bqgrnl,bqgnld->bqgrd', p_s.astype(v_sel.dtype), v_sel)
    s_w = jnp.einsum('bqgrd,bkgd->bqgrk', qr, kw).astype(f32) * scale
    w_ok = ((wpos[None, :] <= qpos[:, None]) & (wpos[None, :] > qpos[:, None] - NSA_WINDOW)
            & (wpos[None, :] >= 0))[None, :, None, None, :]
    p_w = jax.nn.softmax(jnp.where(w_ok, s_w, NEG_INF), axis=-1)
    o_w = jnp.einsum('bqgrk,bkgd->bqgrd', p_w.astype(vw.dtype), vw)
    return gates[..., 0:1] * o_c + gates[..., 1:2] * o_s + gates[..., 2:3] * o_w


def nsa_prompt(q, qr, gates, kc, vc, ksb, vsb, win):
    b, t = q.shape[:2]
    win_pad = jnp.pad(win, ((0, 0), (NSA_WINDOW, 0), (0, 0), (0, 0), (0, 0)))
    span = NSA_WINDOW + Q_BLOCK

    def block(i):
        start = i * Q_BLOCK
        qpos = start + jnp.arange(Q_BLOCK)
        sl = lambda a: lax.dynamic_slice_in_dim(a, start, Q_BLOCK, axis=1)
        w = lax.dynamic_slice_in_dim(win_pad, start, span, axis=1)
        wpos = start - NSA_WINDOW + jnp.arange(span)
        return nsa_attend(sl(q), sl(qr), sl(gates), qpos, kc, vc, ksb, vsb, w[:, :, 0], w[:, :, 1], wpos)

    o = lax.map(block, jnp.arange(t // Q_BLOCK))
    return jnp.moveaxis(o, 0, 1).reshape(b, t, NSA_HEADS * HEAD_DIM)


def diff_project(h, pos, w_in, qk_g):
    b, t, _ = h.shape
    q, k, v = jnp.split(h @ w_in, 3, axis=-1)
    q = rope(rmsnorm(q.reshape(b, t, DIFF_HEADS, 2, HEAD_DIM), qk_g[0]), pos)
    k = rope(rmsnorm(k.reshape(b, t, DIFF_HEADS, 2, HEAD_DIM), qk_g[1]), pos)
    rows = jnp.stack([k.reshape(b, t, DIFF_HEADS, 2 * HEAD_DIM),
                      v.reshape(b, t, DIFF_HEADS, 2 * HEAD_DIM)], axis=2)
    return q, rows


def diff_kv(rows):
    b, t = rows.shape[:2]
    return rows[:, :, 0].reshape(b, t, DIFF_HEADS, 2, HEAD_DIM), rows[:, :, 1]


def diff_attend(q, qpos, k, v, kpos, lam):
    s = jnp.einsum('bqhcd,bkhcd->bchqk', q, k).astype(jnp.float32) * (HEAD_DIM ** -0.5)
    ok = kpos[None, :] <= qpos[:, None]
    p = jax.nn.softmax(jnp.where(ok, s, NEG_INF), axis=-1)
    a = p[:, 0] - lam * p[:, 1]
    return jnp.einsum('bhqk,bkhe->bqhe', a.astype(v.dtype), v)


def diff_prompt(q, rows, lam):
    b, t = q.shape[:2]
    k, v = diff_kv(rows)
    kpos = jnp.arange(t)

    def block(i):
        start = i * Q_BLOCK
        qb = lax.dynamic_slice_in_dim(q, start, Q_BLOCK, axis=1)
        return diff_attend(qb, start + jnp.arange(Q_BLOCK), k, v, kpos, lam)

    o = lax.map(block, jnp.arange(t // Q_BLOCK))
    return jnp.moveaxis(o, 0, 1).reshape(b, t, DIFF_HEADS, 2 * HEAD_DIM)


def diff_out(o, sub_g, lam_init, w_o):
    b, t = o.shape[:2]
    return (rmsnorm(o, sub_g) * (1.0 - lam_init)).reshape(b, t, DIFF_WIDTH) @ w_o


def gla_chunked(q, k, v, log_a, s0):
    dt = v.dtype
    b, t, h, _ = q.shape
    dv = v.shape[-1]
    c = min(GLA_CHUNK, t)
    n = -(-t // c)
    pad = n * c - t

    def chunks(a):
        a = jnp.pad(a.astype(jnp.float32), ((0, 0), (0, pad), (0, 0), (0, 0)))
        return jnp.moveaxis(a.reshape(b, n, c, h, a.shape[-1]), 1, 0)

    causal = jnp.tril(jnp.ones((c, c), dtype=bool))

    def step(state, xs):
        qc, kc, vc, ac = xs
        cum = jnp.cumsum(ac, axis=1)
        qe = qc * jnp.exp(cum)
        ke = kc * jnp.exp(-cum)
        att = jnp.where(causal, jnp.einsum('bihk,bjhk->bhij', qe, ke), 0.0)
        o = jnp.einsum('bihk,bhkv->bihv', qe, state) + jnp.einsum('bhij,bjhv->bihv', att, vc)
        last = cum[:, -1]
        state = state * jnp.exp(last)[..., None] + jnp.einsum(
            'bjhk,bjhv->bhkv', kc * jnp.exp(last[:, None] - cum), vc)
        return state, o

    s_fin, o = lax.scan(step, s0.astype(jnp.float32), (chunks(q), chunks(k), chunks(v), chunks(log_a)))
    o = jnp.moveaxis(o, 0, 1).reshape(b, n * c, h, dv)[:, :t]
    return o.astype(dt), s_fin.astype(dt)


def gla_mix(h, s0, w_in, w_a2, b_a, out_g, w_o):
    b, t, _ = h.shape
    nk, nv = GLA_HEADS * GLA_DK, GLA_HEADS * GLA_DV
    q, k, v, r, a1 = jnp.split(h @ w_in, [nk, 2 * nk, 2 * nk + nv, 2 * nk + 2 * nv], axis=-1)
    q = q.reshape(b, t, GLA_HEADS, GLA_DK) * (GLA_DK ** -0.5)
    k = k.reshape(b, t, GLA_HEADS, GLA_DK)
    v = v.reshape(b, t, GLA_HEADS, GLA_DV)
    log_a = (jax.nn.log_sigmoid((a1 @ w_a2 + b_a).astype(jnp.float32)) / GLA_TAU).reshape(b, t, GLA_HEADS, GLA_DK)
    o, s_fin = gla_chunked(q, k, v, log_a, s0)
    o = rmsnorm(o, out_g) * jax.nn.silu(r.reshape(b, t, GLA_HEADS, GLA_DV))
    return o.reshape(b, t, nv) @ w_o, s_fin


def conv_ffn(h, buf, w_up, conv_w, conv_b, w_down):
    t = h.shape[1]
    u = h @ w_up
    ucat = jnp.concatenate([buf.astype(u.dtype), u], axis=1)
    c = conv_b
    for j in range(CONV_W):
        c = c + conv_w[j] * ucat[:, j:j + t]
    gate, up = jnp.split(c, 2, axis=-1)
    return (jax.nn.silu(gate) * up) @ w_down, ucat[:, t:]


def setup_inputs(seed: int = 0) -> dict:
    key = jax.random.key(seed)
    keys = iter(jax.random.split(key, 32))

    def nrm(shape, scale=1.0):
        return jax.random.normal(next(keys), shape, jnp.float32) * scale

    def gain(shape):
        return 1.0 + nrm(shape, 0.02)

    n_a, n_b, n_c = (DEPTH + 2) // 3, (DEPTH + 1) // 3, DEPTH // 3
    n_pages = PAST_LEN // PAGE_SIZE
    n_phys = (5 * DEC_BATCH * n_pages + 3) // 4
    wbuf = min(NSA_WINDOW, PAST_LEN)
    d = D_MODEL
    return {
        'x_prompt': nrm((BATCH, SEQ, d)),
        'x_sample': nrm((DEC_BATCH, DEC_SEQ, d)),
        'cache_nsa_kv': nrm((n_a, n_phys, PAGE_SIZE, 4, NSA_GROUPS, HEAD_DIM)),
        'state_nsa_win': nrm((n_a, DEC_BATCH, wbuf, 2, NSA_GROUPS, HEAD_DIM)),
        'cache_diff_kv': nrm((n_b, n_phys, PAGE_SIZE, 2, DIFF_HEADS, 2 * HEAD_DIM)),
        'state_gla': nrm((n_c, DEC_BATCH, GLA_HEADS, GLA_DK, GLA_DV), 0.5),
        'state_ffn': nrm((DEPTH, DEC_BATCH, CONV_W - 1, 2 * D_FF), 0.5),
        'page_table': jax.random.permutation(next(keys), n_phys)[:DEC_BATCH * n_pages]
                      .reshape(DEC_BATCH, n_pages).astype(jnp.int32),
        'norm_g': gain((DEPTH, 2, d)),
        'ffn_w_up': nrm((DEPTH, d, 2 * D_FF), d ** -0.5),
        'ffn_conv_w': nrm((DEPTH, CONV_W, 2 * D_FF), CONV_W ** -0.5),
        'ffn_conv_b': nrm((DEPTH, 2 * D_FF), 0.02),
        'ffn_w_down': nrm((DEPTH, D_FF, d), D_FF ** -0.5),
        'nsa_w_in': nrm((n_a, d, NSA_IN), d ** -0.5),
        'nsa_qk_g': gain((n_a, 4, HEAD_DIM)),
        'nsa_pe': nrm((n_a, 2, NSA_BLOCK, HEAD_DIM), 0.1),
        'nsa_w_phi': nrm((n_a, 2, NSA_BLOCK, HEAD_DIM, HEAD_DIM), (NSA_BLOCK * HEAD_DIM) ** -0.5),
        'nsa_w_o': nrm((n_a, NSA_HEADS * HEAD_DIM, d), (NSA_HEADS * HEAD_DIM) ** -0.5),
        'diff_w_in': nrm((n_b, d, 3 * DIFF_WIDTH), d ** -0.5),
        'diff_qk_g': gain((n_b, 2, HEAD_DIM)),
        'diff_lam': nrm((n_b, 4, HEAD_DIM), 0.1),
        'diff_sub_g': gain((n_b, 2 * HEAD_DIM)),
        'diff_w_o': nrm((n_b, DIFF_WIDTH, d), DIFF_WIDTH ** -0.5),
        'gla_w_in': nrm((n_c, d, GLA_IN), d ** -0.5),
        'gla_w_a2': nrm((n_c, GLA_RANK, GLA_HEADS * GLA_DK), GLA_RANK ** -0.5),
        'gla_b_a': nrm((n_c, GLA_HEADS * GLA_DK), 0.1),
        'gla_out_g': gain((n_c, GLA_DV)),
        'gla_w_o': nrm((n_c, GLA_HEADS * GLA_DV, d), (GLA_HEADS * GLA_DV) ** -0.5),
    }


def reference(x_prompt, x_sample, cache_nsa_kv, state_nsa_win, cache_diff_kv, state_gla, state_ffn,
              page_table, norm_g, ffn_w_up, ffn_conv_w, ffn_conv_b, ffn_w_down, nsa_w_in, nsa_qk_g,
              nsa_pe, nsa_w_phi, nsa_w_o, diff_w_in, diff_qk_g, diff_lam, diff_sub_g, diff_w_o,
              gla_w_in, gla_w_a2, gla_b_a, gla_out_g, gla_w_o):
    xp, xs = x_prompt, x_sample
    bp, tp = xp.shape[:2]
    bs, ts = xs.shape[:2]
    pos_p = jnp.arange(tp)
    pos_s = PAST_LEN + jnp.arange(ts)
    wbuf = state_nsa_win.shape[2]
    nsa_kv_p, nsa_kv_s, nsa_win_p, nsa_win_s = [], [], [], []
    diff_kv_p, diff_kv_s, gla_p, gla_s, ffn_p, ffn_s = [], [], [], [], [], []
    for i in range(DEPTH):
        kind, s = i % N_MIXERS, i // N_MIXERS
        hp = rmsnorm(xp, norm_g[i, 0])
        hs = rmsnorm(xs, norm_g[i, 0])
        if kind == 0:
            q, qr, g, rows, win = nsa_project(hp, pos_p, nsa_w_in[s], nsa_qk_g[s])
            kc, vc, ksb, vsb = nsa_keys(rows, nsa_pe[s], nsa_w_phi[s], nsa_qk_g[s, 3])
            mp = nsa_prompt(q, qr, g, kc, vc, ksb, vsb, win) @ nsa_w_o[s]
            nsa_kv_p.append(rows)
            nsa_win_p.append(win[:, -min(NSA_WINDOW, tp):])
            q, qr, g, rows, win = nsa_project(hs, pos_s, nsa_w_in[s], nsa_qk_g[s])
            past = cache_nsa_kv[s, page_table]
            past = past.reshape((bs, -1) + past.shape[3:]).astype(rows.dtype)
            kc, vc, ksb, vsb = nsa_keys(jnp.concatenate([past, rows], axis=1), nsa_pe[s], nsa_w_phi[s], nsa_qk_g[s, 3])
            win_all = jnp.concatenate([state_nsa_win[s].astype(win.dtype), win], axis=1)
            wpos = PAST_LEN - wbuf + jnp.arange(win_all.shape[1])
            o = nsa_attend(q, qr, g, pos_s, kc, vc, ksb, vsb, win_all[:, :, 0], win_all[:, :, 1], wpos)
            ms = o.reshape(bs, ts, NSA_HEADS * HEAD_DIM) @ nsa_w_o[s]
            nsa_kv_s.append(rows)
            nsa_win_s.append(win_all[:, -min(NSA_WINDOW, win_all.shape[1]):])
        elif kind == 1:
            lam_init = 0.8 - 0.6 * math.exp(-0.3 * i)
            lf = diff_lam[s].astype(jnp.float32)
            lam = jnp.exp(jnp.sum(lf[0] * lf[1])) - jnp.exp(jnp.sum(lf[2] * lf[3])) + lam_init
            q, rows = diff_project(hp, pos_p, diff_w_in[s], diff_qk_g[s])
            mp = diff_out(diff_prompt(q, rows, lam), diff_sub_g[s], lam_init, diff_w_o[s])
            diff_kv_p.append(rows)
            q, rows = diff_project(hs, pos_s, diff_w_in[s], diff_qk_g[s])
            past = cache_diff_kv[s, page_table]
            past = past.reshape((bs, -1) + past.shape[3:]).astype(rows.dtype)
            k_all, v_all = diff_kv(jnp.concatenate([past, rows], axis=1))
            o = diff_attend(q, pos_s, k_all, v_all, jnp.arange(k_all.shape[1]), lam)
            ms = diff_out(o, diff_sub_g[s], lam_init, diff_w_o[s])
            diff_kv_s.append(rows)
        else:
            s0 = jnp.zeros((bp, GLA_HEADS, GLA_DK, GLA_DV), hp.dtype)
            mp, st_p = gla_mix(hp, s0, gla_w_in[s], gla_w_a2[s], gla_b_a[s], gla_out_g[s], gla_w_o[s])
            ms, st_s = gla_mix(hs, state_gla[s], gla_w_in[s], gla_w_a2[s], gla_b_a[s], gla_out_g[s], gla_w_o[s])
            gla_p.append(st_p)
            gla_s.append(st_s)
        xp = xp + mp
        xs = xs + ms
        buf0 = jnp.zeros((bp, CONV_W - 1, 2 * D_FF), xp.dtype)
        fp, bufp = conv_ffn(rmsnorm(xp, norm_g[i, 1]), buf0, ffn_w_up[i], ffn_conv_w[i], ffn_conv_b[i], ffn_w_down[i])
        fs, bufs = conv_ffn(rmsnorm(xs, norm_g[i, 1]), state_ffn[i], ffn_w_up[i], ffn_conv_w[i], ffn_conv_b[i], ffn_w_down[i])
        xp = xp + fp
        xs = xs + fs
        ffn_p.append(bufp)
        ffn_s.append(bufs)
    y_prompt, y_sample = xp, xs
    nsa_kv_prompt, nsa_kv_sample = jnp.stack(nsa_kv_p), jnp.stack(nsa_kv_s)
    nsa_win_prompt, nsa_win_sample = jnp.stack(nsa_win_p), jnp.stack(nsa_win_s)
    diff_kv_prompt, diff_kv_sample = jnp.stack(diff_kv_p), jnp.stack(diff_kv_s)
    gla_prompt, gla_sample = jnp.stack(gla_p), jnp.stack(gla_s)
    ffn_prompt, ffn_sample = jnp.stack(ffn_p), jnp.stack(ffn_s)
    return (y_prompt, y_sample, nsa_kv_prompt, nsa_kv_sample, nsa_win_prompt, nsa_win_sample,
            diff_kv_prompt, diff_kv_sample, gla_prompt, gla_sample, ffn_prompt, ffn_sample)
```

```python
import functools
import math

import jax
import jax.numpy as jnp
from jax import lax
from jax.experimental import pallas as pl
from jax.experimental.pallas import tpu as pltpu

F32 = jnp.float32
BF16 = jnp.bfloat16

D_MODEL = 1024
HEAD_DIM = 64
ROPE_THETA = 10000.0
NORM_EPS = 1e-6
NEG_INF = -1e30
DEPTH = 4
N_MIXERS = 3
PAGE_SIZE = 128
NSA_HEADS = 16
NSA_GROUPS = 4
NSA_REP = 4
NSA_BLOCK = 64
NSA_TOPN = 16
NSA_WINDOW = 512
NSA_FORCED = 1e9
NSA_NQ = NSA_HEADS * HEAD_DIM
NSA_NKV = 6 * NSA_GROUPS * HEAD_DIM
NSA_NGATE = 3 * NSA_HEADS
NSA_IN = NSA_NQ + NSA_NKV + NSA_NGATE
NSA_IN_PAD = 2688
DIFF_HEADS = 8
DIFF_WIDTH = 1024
GLA_HEADS = 4
GLA_DK = 128
GLA_DV = 256
GLA_RANK = 16
GLA_TAU = 16.0
GLA_CHUNK = 64
GLA_MAIN = 2 * GLA_HEADS * GLA_DK + 2 * GLA_HEADS * GLA_DV
GLA_IN_PAD = GLA_MAIN + 128
D_FF = 2816
CONV_W = 3
LANES = 128
VMEM_LIMIT = 56 * 1024 * 1024


def _cparams(*sem):
    return pltpu.CompilerParams(dimension_semantics=sem, vmem_limit_bytes=VMEM_LIMIT)


def _dot(a, b):
    return jnp.dot(a, b, preferred_element_type=F32)


def _dot_nt(a, b):
    return lax.dot_general(a, b, (((1,), (1,)), ((), ())), preferred_element_type=F32)


def _split_dot(x, m):
    hi = x.astype(BF16)
    lo = (x - hi.astype(F32)).astype(BF16)
    return _dot(hi, m) + _dot(lo, m)


def _rms_rows(x, g):
    ms = jnp.mean(x * x, axis=-1, keepdims=True)
    return x * lax.rsqrt(ms + NORM_EPS) * g


def _seg64_norm(x, g, seg_ones):
    ms = _split_dot(x * x, seg_ones) * (1.0 / HEAD_DIM)
    return x * lax.rsqrt(ms + NORM_EPS) * g


def _rope_slab(x, cos, sin_signed):
    lane = lax.broadcasted_iota(jnp.int32, x.shape, 1)
    first = (lane & 63) < 32
    partner = jnp.where(first, pltpu.roll(x, 96, 1), pltpu.roll(x, 32, 1))
    return x * cos + partner * sin_signed


def _rope_tables(pos):
    half = HEAD_DIM // 2
    inv = ROPE_THETA ** (-jnp.arange(half, dtype=F32) / half)
    ang = pos.astype(F32)[:, None] * inv
    cos, sin = jnp.cos(ang), jnp.sin(ang)
    return jnp.tile(cos, (1, 4)), jnp.tile(jnp.concatenate([-sin, sin], axis=1), (1, 2))


def _seg_ones():
    i = jnp.arange(LANES)
    return (i[:, None] // HEAD_DIM == i[None, :] // HEAD_DIM).astype(BF16)


def _norm_matmul_kernel(x_ref, g_ref, w_ref, o_ref, h_ref):
    @pl.when(pl.program_id(1) == 0)
    def _():
        h_ref[...] = _rms_rows(x_ref[...], g_ref[...]).astype(BF16)

    o_ref[...] = _dot(h_ref[...], w_ref[...])


def norm_matmul(x, g, w, tn):
    n, d = x.shape
    nout = w.shape[1]
    tm = min(512, n)
    return pl.pallas_call(
        _norm_matmul_kernel,
        grid=(n // tm, nout // tn),
        in_specs=[pl.BlockSpec((tm, d), lambda i, j: (i, 0)),
                  pl.BlockSpec((1, d), lambda i, j: (0, 0)),
                  pl.BlockSpec((d, tn), lambda i, j: (0, j))],
        out_specs=pl.BlockSpec((tm, tn), lambda i, j: (i, j)),
        out_shape=jax.ShapeDtypeStruct((n, nout), F32),
        scratch_shapes=[pltpu.VMEM((tm, d), BF16)],
        compiler_params=_cparams("parallel", "arbitrary"),
    )(x, g.reshape(1, d), w)


def _matmul_res_kernel(a_ref, w_ref, r_ref, o_ref):
    o_ref[...] = r_ref[...] + _dot(a_ref[...].astype(BF16), w_ref[...])


def matmul_residual(a, w, res):
    n, k = a.shape
    d = w.shape[1]
    tm = min(512, n)
    return pl.pallas_call(
        _matmul_res_kernel,
        grid=(n // tm,),
        in_specs=[pl.BlockSpec((tm, k), lambda i: (i, 0)),
                  pl.BlockSpec((k, d), lambda i: (0, 0)),
                  pl.BlockSpec((tm, d), lambda i: (i, 0))],
        out_specs=pl.BlockSpec((tm, d), lambda i: (i, 0)),
        out_shape=jax.ShapeDtypeStruct((n, d), F32),
        compiler_params=_cparams("parallel"),
    )(a, w, res)


FFN_CHUNK = 256


def _silu(x):
    return x / (1.0 + jnp.exp(-x))


def _causal_conv(u, prev1, prev2, use1, use2, cw_ref, cb_ref):
    u1 = jnp.where(use1, prev1, pltpu.roll(u, 1, 0))
    u2 = jnp.where(use2, prev2, pltpu.roll(u, 2, 0))
    return cb_ref[...] + cw_ref[0:1, :] * u2 + cw_ref[1:2, :] * u1 + cw_ref[2:3, :] * u


def _ffn_prompt_kernel(tiles_per_seq, x_ref, g_ref, wg_ref, wu_ref, cwg_ref, cwu_ref, cbg_ref, cbu_ref,
                       wd_ref, o_ref, tg_ref, tu_ref, h_ref, cg_ref, cu_ref):
    i, j = pl.program_id(0), pl.program_id(1)
    tm = x_ref.shape[0]

    @pl.when(j == 0)
    def _():
        h_ref[...] = _rms_rows(x_ref[...], g_ref[...]).astype(BF16)

    h = h_ref[...]
    ug = _dot(h, wg_ref[...])
    uu = _dot(h, wu_ref[...])
    seq_start = (i % tiles_per_seq) == 0
    row = lax.broadcasted_iota(jnp.int32, ug.shape, 0)

    @pl.when(seq_start)
    def _():
        cg_ref[j] = jnp.zeros(cg_ref.shape[1:], F32)
        cu_ref[j] = jnp.zeros(cu_ref.shape[1:], F32)

    def conv(u, carry_ref, cw_ref, cb_ref):
        prev = carry_ref[j]
        prev2 = jnp.where(row == 0, prev[0:1, :], prev[1:2, :])
        c = _causal_conv(u, prev[1:2, :], prev2, row == 0, row < 2, cw_ref, cb_ref)
        carry_ref[j] = u[tm - 2:tm, :]
        return c

    act = _silu(conv(ug, cg_ref, cwg_ref, cbg_ref)) * conv(uu, cu_ref, cwu_ref, cbu_ref)
    part = _dot(act.astype(BF16), wd_ref[...])
    tg_ref[0] = ug[tm - 2:tm, :]
    tu_ref[0] = uu[tm - 2:tm, :]

    @pl.when(j == 0)
    def _():
        o_ref[...] = x_ref[...] + part

    @pl.when(j > 0)
    def _():
        o_ref[...] += part


def ffn_prompt(x, g, w_up, conv_w, conv_b, w_down, seq_len):
    n, d = x.shape
    c = FFN_CHUNK
    nj = D_FF // c
    tm = min(512, seq_len)
    tps = seq_len // tm
    nseq = n // seq_len
    cb = conv_b.reshape(1, 2 * D_FF)
    out, tg, tu = pl.pallas_call(
        functools.partial(_ffn_prompt_kernel, tps),
        grid=(n // tm, nj),
        in_specs=[pl.BlockSpec((tm, d), lambda i, j: (i, 0)),
                  pl.BlockSpec((1, d), lambda i, j: (0, 0)),
                  pl.BlockSpec((d, c), lambda i, j: (0, j)),
                  pl.BlockSpec((d, c), lambda i, j: (0, j + nj)),
                  pl.BlockSpec((CONV_W, c), lambda i, j: (0, j)),
                  pl.BlockSpec((CONV_W, c), lambda i, j: (0, j + nj)),
                  pl.BlockSpec((1, c), lambda i, j: (0, j)),
                  pl.BlockSpec((1, c), lambda i, j: (0, j + nj)),
                  pl.BlockSpec((c, d), lambda i, j: (j, 0))],
        out_specs=[pl.BlockSpec((tm, d), lambda i, j: (i, 0)),
                   pl.BlockSpec((1, 2, c), lambda i, j: (i, 0, j)),
                   pl.BlockSpec((1, 2, c), lambda i, j: (i, 0, j))],
        out_shape=[jax.ShapeDtypeStruct((n, d), F32),
                   jax.ShapeDtypeStruct((n // tm, 2, D_FF), F32),
                   jax.ShapeDtypeStruct((n // tm, 2, D_FF), F32)],
        scratch_shapes=[pltpu.VMEM((tm, d), BF16),
                        pltpu.VMEM((nj, 2, c), F32),
                        pltpu.VMEM((nj, 2, c), F32)],
        compiler_params=_cparams("arbitrary", "arbitrary"),
    )(x, g.reshape(1, d), w_up, w_up, conv_w, conv_w, cb, cb, w_down)
    return out, jnp.concatenate([tg, tu], axis=-1)[tps - 1::tps]


def _ffn_sample_kernel(t_seq, x_ref, g_ref, wg_ref, wu_ref, cwg_ref, cwu_ref, cbg_ref, cbu_ref, wd_ref,
                       p1g_ref, p2g_ref, p1u_ref, p2u_ref, o_ref, ug_ref, uu_ref, h_ref):
    j = pl.program_id(0)

    @pl.when(j == 0)
    def _():
        h_ref[...] = _rms_rows(x_ref[...], g_ref[...]).astype(BF16)

    h = h_ref[...]
    ug = _dot(h, wg_ref[...])
    uu = _dot(h, wu_ref[...])
    t = lax.broadcasted_iota(jnp.int32, ug.shape, 0) & (t_seq - 1)
    cg = _causal_conv(ug, p1g_ref[...], p2g_ref[...], t == 0, t < 2, cwg_ref, cbg_ref)
    cu = _causal_conv(uu, p1u_ref[...], p2u_ref[...], t == 0, t < 2, cwu_ref, cbu_ref)
    part = _dot((_silu(cg) * cu).astype(BF16), wd_ref[...])
    ug_ref[...] = ug
    uu_ref[...] = uu

    @pl.when(j == 0)
    def _():
        o_ref[...] = x_ref[...] + part

    @pl.when(j > 0)
    def _():
        o_ref[...] += part


def ffn_sample(x, g, w_up, conv_w, conv_b, w_down, buf, t_seq):
    n, d = x.shape
    nseq = n // t_seq
    c = FFN_CHUNK
    nj = D_FF // c
    cb = conv_b.reshape(1, 2 * D_FF)
    reps = t_seq // 2
    prev1 = jnp.concatenate([buf[:, ::-1]] * reps, axis=1).reshape(n, 2 * D_FF)
    prev2 = jnp.concatenate([buf] * reps, axis=1).reshape(n, 2 * D_FF)
    full = lambda blk, off=0: pl.BlockSpec(blk, lambda j: (0, j + off))
    out, ug, uu = pl.pallas_call(
        functools.partial(_ffn_sample_kernel, t_seq),
        grid=(nj,),
        in_specs=[pl.BlockSpec((n, d), lambda j: (0, 0)),
                  pl.BlockSpec((1, d), lambda j: (0, 0)),
                  full((d, c)), full((d, c), nj),
                  full((CONV_W, c)), full((CONV_W, c), nj),
                  full((1, c)), full((1, c), nj),
                  pl.BlockSpec((c, d), lambda j: (j, 0)),
                  full((n, c)), full((n, c)), full((n, c), nj), full((n, c), nj)],
        out_specs=[pl.BlockSpec((n, d), lambda j: (0, 0)),
                   full((n, c)), full((n, c))],
        out_shape=[jax.ShapeDtypeStruct((n, d), F32),
                   jax.ShapeDtypeStruct((n, D_FF), F32),
                   jax.ShapeDtypeStruct((n, D_FF), F32)],
        scratch_shapes=[pltpu.VMEM((n, d), BF16)],
        compiler_params=_cparams("arbitrary"),
    )(x, g.reshape(1, d), w_up, w_up, conv_w, conv_w, cb, cb, w_down, prev1, prev2, prev1, prev2)
    u = jnp.concatenate([ug, uu], axis=-1).reshape(nseq, t_seq, 2 * D_FF)
    return out, u[:, t_seq - (CONV_W - 1):]


def _nsa_post_kernel(p_ref, cos_ref, sin_ref, g_ref, so_ref, qn_ref, qr_ref, rows_ref, win_ref, gate_ref):
    cos, sin = cos_ref[...], sin_ref[...]
    so = so_ref[...]
    scale = HEAD_DIM ** -0.5
    for s in range(NSA_NQ // LANES):
        q = _seg64_norm(p_ref[:, s * LANES:(s + 1) * LANES], g_ref[0:1, :], so)
        qn_ref[:, s * LANES:(s + 1) * LANES] = (q * scale).astype(BF16)
        qr_ref[:, s * LANES:(s + 1) * LANES] = (_rope_slab(q, cos, sin) * scale).astype(BF16)
    kv0 = NSA_NQ
    gw = NSA_GROUPS * HEAD_DIM
    rows_ref[:, 0:2 * gw] = p_ref[:, kv0:kv0 + 2 * gw]
    rows_ref[:, 3 * gw:4 * gw] = p_ref[:, kv0 + 3 * gw:kv0 + 4 * gw]
    win_ref[:, gw:2 * gw] = p_ref[:, kv0 + 5 * gw:kv0 + 6 * gw]
    for s in range(gw // LANES):
        lo = kv0 + 2 * gw + s * LANES
        k = _seg64_norm(p_ref[:, lo:lo + LANES], g_ref[1:2, :], so)
        rows_ref[:, 2 * gw + s * LANES:2 * gw + (s + 1) * LANES] = _rope_slab(k, cos, sin)
        lo = kv0 + 4 * gw + s * LANES
        k = _seg64_norm(p_ref[:, lo:lo + LANES], g_ref[2:3, :], so)
        win_ref[:, s * LANES:(s + 1) * LANES] = _rope_slab(k, cos, sin)
    z = p_ref[:, kv0 + NSA_NKV:kv0 + NSA_NKV + LANES]
    gate_ref[...] = 1.0 / (1.0 + jnp.exp(-z))


def nsa_post(proj, cos, sin, qk_g):
    n = proj.shape[0]
    tm = min(256, n)
    g2 = jnp.tile(qk_g, (1, 2))
    row = lambda w: pl.BlockSpec((tm, w), lambda i: (i, 0))
    const = lambda a: pl.BlockSpec(a.shape, lambda i: (0, 0))
    so = _seg_ones()
    return pl.pallas_call(
        _nsa_post_kernel,
        grid=(n // tm,),
        in_specs=[row(NSA_IN_PAD), row(LANES), row(LANES), const(g2), const(so)],
        out_specs=[row(1024), row(1024), row(1024), row(512), row(LANES)],
        out_shape=[jax.ShapeDtypeStruct((n, 1024), BF16), jax.ShapeDtypeStruct((n, 1024), BF16),
                   jax.ShapeDtypeStruct((n, 1024), F32), jax.ShapeDtypeStruct((n, 512), F32),
                   jax.ShapeDtypeStruct((n, LANES), F32)],
        compiler_params=_cparams("parallel"),
    )(proj, cos, sin, g2, so)


def _nsa_compress_kernel(pt_ref, page_ref, pe_ref, wk_ref, wv_ref, g_ref, so_ref, kc_ref, vc_ref, seq_ref, acc_ref):
    p = pl.program_id(1)
    npages = pl.num_programs(1)
    nblk = seq_ref.shape[1] // NSA_BLOCK
    nslab = seq_ref.shape[0]
    for s in range(nslab):
        seq_ref[s, pl.ds(pl.multiple_of(p * PAGE_SIZE, PAGE_SIZE), PAGE_SIZE), :] = page_ref[0, :, s * LANES:(s + 1) * LANES]

    @pl.when(p == npages - 1)
    def _():
        acc_ref[...] = jnp.zeros_like(acc_ref)
        for l in range(NSA_BLOCK):
            for s in range(nslab):
                sl = slice(s * LANES, (s + 1) * LANES)
                a = (seq_ref[s, pl.ds(l, nblk, stride=NSA_BLOCK), :] + pe_ref[l:l + 1, sl]).astype(BF16)
                acc_ref[:, sl] += _dot(a, wk_ref[l] if s < nslab // 2 else wv_ref[l])
        so = so_ref[...]
        for s in range(2):
            kc_ref[0, :, s * LANES:(s + 1) * LANES] = _seg64_norm(acc_ref[:, s * LANES:(s + 1) * LANES], g_ref[...], so)
        vc_ref[0] = acc_ref[:, 256:512]


def nsa_compress(pages, page_table, pe, w_phi, g_c):
    nseq, npg = page_table.shape
    nblk = npg * PAGE_SIZE // NSA_BLOCK
    pe_t = jnp.concatenate([jnp.tile(pe[0], (1, NSA_GROUPS)), jnp.tile(pe[1], (1, NSA_GROUPS))], axis=1)
    eye2 = jnp.eye(2, dtype=F32)
    wpair = lambda w: jnp.einsum('ab,lde->ladbe', eye2, w).reshape(NSA_BLOCK, LANES, LANES).astype(BF16)
    wk, wv = wpair(w_phi[0]), wpair(w_phi[1])
    g2 = jnp.tile(g_c.reshape(1, HEAD_DIM), (1, 2))
    so = _seg_ones()
    const = lambda a: pl.BlockSpec(a.shape, lambda b, p, pt: (0,) * a.ndim)
    out = pl.BlockSpec((1, nblk, 256), lambda b, p, pt: (b, 0, 0))
    return pl.pallas_call(
        _nsa_compress_kernel,
        grid_spec=pltpu.PrefetchScalarGridSpec(
            num_scalar_prefetch=1,
            grid=(nseq, npg),
            in_specs=[pl.BlockSpec((1, PAGE_SIZE, 512), lambda b, p, pt: (pt[b, p], 0, 0)),
                      const(pe_t), const(wk), const(wv), const(g2), const(so)],
            out_specs=[out, out],
            scratch_shapes=[pltpu.VMEM((4, npg * PAGE_SIZE, LANES), F32), pltpu.VMEM((nblk, 512), F32)]),
        out_shape=[jax.ShapeDtypeStruct((nseq, nblk, 256), F32)] * 2,
        compiler_params=_cparams("arbitrary", "arbitrary"),
    )(page_table, pages, pe_t, wk, wv, g2, so)


def _stack_heads(q):
    head = lax.broadcasted_iota(jnp.int32, q.shape, 1) >> 6
    qf = q.astype(F32)
    return jnp.concatenate([jnp.where(head == r, qf, 0.0) for r in range(NSA_REP)], axis=0).astype(BF16)


def _unstack_heads(o4, tq):
    head = lax.broadcasted_iota(jnp.int32, (tq, o4.shape[1]), 1) >> 6
    out = jnp.zeros((tq, o4.shape[1]), F32)
    for r in range(NSA_REP):
        out = jnp.where(head == r, o4[r * tq:(r + 1) * tq, :], out)
    return out


def _nsa_cmp_kernel(q0, ns_rows, q_ref, kc_ref, vc_ref, oc_ref, sel_ref):
    i = pl.program_id(2)
    tq = q_ref.shape[1]
    nc = kc_ref.shape[2]
    qst = _stack_heads(q_ref[0])
    kc, vc = kc_ref[0, 0], vc_ref[0, 0]
    base = q0 + i * tq
    qpos = base + (lax.broadcasted_iota(jnp.int32, (NSA_REP * tq, nc), 0) & (tq - 1))
    blk_end = (lax.broadcasted_iota(jnp.int32, (NSA_REP * tq, nc), 1) + 1) * NSA_BLOCK - 1
    ok = blk_end <= qpos
    s = jnp.where(ok, _dot_nt(qst, kc), NEG_INF)
    e = jnp.exp(s - jnp.max(s, axis=-1, keepdims=True))
    p = jnp.where(ok, e / jnp.sum(e, axis=-1, keepdims=True), 0.0)
    oc_ref[0] = _unstack_heads(_dot(p.astype(BF16), vc), tq)
    qpos_t = base + (lax.broadcasted_iota(jnp.int32, (nc, NSA_REP * tq), 1) & (tq - 1))
    blk_end_t = (lax.broadcasted_iota(jnp.int32, (nc, NSA_REP * tq), 0) + 1) * NSA_BLOCK - 1
    ok_t = blk_end_t <= qpos_t
    st = jnp.where(ok_t, _dot_nt(kc, qst), NEG_INF)
    et = jnp.exp(st - jnp.max(st, axis=0, keepdims=True))
    pt = jnp.where(ok_t, et / jnp.sum(et, axis=0, keepdims=True), 0.0)
    imp = pt[:, 0:tq]
    for r in range(1, NSA_REP):
        imp = imp + pt[:, r * tq:(r + 1) * tq]
    if ns_rows > nc:
        imp = jnp.concatenate([imp, jnp.zeros((ns_rows - nc, tq), F32)], axis=0)
    blk = lax.broadcasted_iota(jnp.int32, (ns_rows, tq), 0)
    cur = (base + lax.broadcasted_iota(jnp.int32, (ns_rows, tq), 1)) >> 6
    forced = (blk == 0) | (blk == cur) | (blk == cur - 1)
    imp = jnp.where(forced, NSA_FORCED, imp)
    imp = jnp.where(blk <= cur, imp, NEG_INF)
    taken = jnp.float32(-3e38)
    blk_f = blk.astype(F32)

    def pick(_, carry):
        imp, sel = carry
        m = jnp.max(imp, axis=0, keepdims=True)
        first = jnp.min(jnp.where(imp == m, blk_f, float(ns_rows)), axis=0, keepdims=True)
        hit = blk_f == first
        return jnp.where(hit, taken, imp), jnp.where(hit, 0.0, sel)

    _, sel = lax.fori_loop(0, NSA_TOPN, pick, (imp, jnp.full((ns_rows, tq), NEG_INF, F32)))
    sel_ref[0, 0] = sel.T.astype(BF16)


def nsa_cmp_select(qn, kc4, vc4, q0, ns_rows):
    b, t, _ = qn.shape
    tq = 128
    nc = kc4.shape[2]
    kv = pl.BlockSpec((1, 1, nc, 256), lambda bi, g, i: (bi, g, 0, 0))
    return pl.pallas_call(
        functools.partial(_nsa_cmp_kernel, q0, ns_rows),
        grid=(b, NSA_GROUPS, t // tq),
        in_specs=[pl.BlockSpec((1, tq, 256), lambda bi, g, i: (bi, i, g)), kv, kv],
        out_specs=[pl.BlockSpec((1, tq, 256), lambda bi, g, i: (bi, i, g)),
                   pl.BlockSpec((1, 1, tq, ns_rows), lambda bi, g, i: (bi, g, i, 0))],
        out_shape=[jax.ShapeDtypeStruct((b, t, 1024), F32),
                   jax.ShapeDtypeStruct((b, NSA_GROUPS, t, ns_rows), BF16)],
        compiler_params=_cparams("parallel", "parallel", "parallel"),
    )(qn, kc4, vc4)


def _nsa_window_kernel(q_ref, k_ref, v_ref, o_ref):
    i = pl.program_id(2)
    tq = q_ref.shape[1]
    span = NSA_WINDOW + tq
    qst = _stack_heads(q_ref[0])
    lo = pl.multiple_of(jnp.maximum(i * tq - NSA_WINDOW, 0), tq)
    k = k_ref[0, 0, pl.ds(lo, span), :]
    v = v_ref[0, 0, pl.ds(lo, span), :]
    qpos = i * tq + (lax.broadcasted_iota(jnp.int32, (NSA_REP * tq, span), 0) & (tq - 1))
    kpos = lo + lax.broadcasted_iota(jnp.int32, (NSA_REP * tq, span), 1)
    ok = (kpos <= qpos) & (kpos > qpos - NSA_WINDOW)
    s = jnp.where(ok, _dot_nt(qst, k), NEG_INF)
    e = jnp.exp(s - jnp.max(s, axis=-1, keepdims=True))
    p = e / jnp.sum(e, axis=-1, keepdims=True)
    o_ref[0] = _unstack_heads(_dot(p.astype(BF16), v), tq)


def nsa_window_prompt(qr, kw4, vw4):
    b, t, _ = qr.shape
    tq = 128
    kv = pl.BlockSpec((1, 1, t, 256), lambda bi, g, i: (bi, g, 0, 0))
    return pl.pallas_call(
        _nsa_window_kernel,
        grid=(b, NSA_GROUPS, t // tq),
        in_specs=[pl.BlockSpec((1, tq, 256), lambda bi, g, i: (bi, i, g)), kv, kv],
        out_specs=pl.BlockSpec((1, tq, 256), lambda bi, g, i: (bi, i, g)),
        out_shape=jax.ShapeDtypeStruct((b, t, 1024), F32),
        compiler_params=_cparams("parallel", "parallel", "arbitrary"),
    )(qr, kw4, vw4)


SEL_TK = 512


def _nsa_selected_kernel(q_ref, sel_ref, k_ref, v_ref, e_ref, o_ref):
    i = pl.program_id(2)
    tq = q_ref.shape[1]
    tk = SEL_TK
    rows = NSA_REP * tq
    qst = _stack_heads(q_ref[0])
    sel = sel_ref[0, 0]

    def scores(j):
        lo = pl.multiple_of(j * tk, tk)
        bias = _dot_nt(sel, e_ref[pl.ds(lo, tk), :])
        s = _dot_nt(qst, k_ref[0, 0, pl.ds(lo, tk), :])
        return s + jnp.concatenate([bias] * NSA_REP, axis=0), v_ref[0, 0, pl.ds(lo, tk), :]

    def update(carry, s, v):
        m, l, acc = carry
        m_new = jnp.maximum(m, jnp.max(s, axis=-1, keepdims=True))
        alpha = jnp.exp(m - m_new)
        p = jnp.exp(s - m_new)
        return m_new, alpha * l + jnp.sum(p, axis=-1, keepdims=True), alpha * acc + _dot(p.astype(BF16), v)

    def body(j, carry):
        s, v = scores(j)
        return update(carry, s, v)

    nfull = (i * tq) // tk
    init = (jnp.full((rows, 1), NEG_INF, F32), jnp.zeros((rows, 1), F32), jnp.zeros((rows, 256), F32))
    carry = lax.fori_loop(0, nfull, body, init)
    s, v = scores(nfull)
    qpos = i * tq + (lax.broadcasted_iota(jnp.int32, (rows, tk), 0) & (tq - 1))
    kpos = nfull * tk + lax.broadcasted_iota(jnp.int32, (rows, tk), 1)
    _, l, acc = update(carry, jnp.where(kpos <= qpos, s, NEG_INF), v)
    o_ref[0] = _unstack_heads(acc / l, tq)


def nsa_selected_prompt(qr, sel, ks4, vs4):
    b, t, _ = qr.shape
    tq = 128
    e_t = (jnp.arange(t)[:, None] // NSA_BLOCK == jnp.arange(sel.shape[-1])[None, :]).astype(BF16)
    kv = pl.BlockSpec((1, 1, t, 256), lambda bi, g, i: (bi, g, 0, 0))
    return pl.pallas_call(
        _nsa_selected_kernel,
        grid=(b, NSA_GROUPS, t // tq),
        in_specs=[pl.BlockSpec((1, tq, 256), lambda bi, g, i: (bi, i, g)),
                  pl.BlockSpec((1, 1, tq, sel.shape[-1]), lambda bi, g, i: (bi, g, i, 0)),
                  kv, kv,
                  pl.BlockSpec(e_t.shape, lambda bi, g, i: (0, 0))],
        out_specs=pl.BlockSpec((1, tq, 256), lambda bi, g, i: (bi, i, g)),
        out_shape=jax.ShapeDtypeStruct((b, t, 1024), F32),
        compiler_params=_cparams("parallel", "parallel", "arbitrary"),
    )(qr, sel, ks4, vs4, e_t)


def _nsa_out_kernel(oc_ref, os_ref, ow_ref, gate_ref, ex_ref, w_ref, r_ref, o_ref):
    gate = gate_ref[...]
    comb = (_split_dot(gate, ex_ref[0]) * oc_ref[...] + _split_dot(gate, ex_ref[1]) * os_ref[...]
            + _split_dot(gate, ex_ref[2]) * ow_ref[...])
    o_ref[...] = r_ref[...] + _dot(comb.astype(BF16), w_ref[...])


def nsa_out(oc, os_, ow, gates, w_o, res):
    n, d = res.shape
    tm = min(256, n)
    lane = jnp.arange(1024) // HEAD_DIM
    ex = jnp.stack([(jnp.arange(LANES)[:, None] == lane[None, :] * 3 + k) for k in range(3)]).astype(BF16)
    row = lambda w: pl.BlockSpec((tm, w), lambda i: (i, 0))
    return pl.pallas_call(
        _nsa_out_kernel,
        grid=(n // tm,),
        in_specs=[row(1024), row(1024), row(1024), row(LANES),
                  pl.BlockSpec(ex.shape, lambda i: (0, 0, 0)),
                  pl.BlockSpec(w_o.shape, lambda i: (0, 0)), row(d)],
        out_specs=row(d),
        out_shape=jax.ShapeDtypeStruct((n, d), F32),
        compiler_params=_cparams("parallel"),
    )(oc, os_, ow, gates, ex, w_o, res)


def _paged_attn_kernel(pt_ref, q_ref, kp_ref, vp_ref, kn_ref, vn_ref, b_ref, o_ref, m_ref, l_ref, acc_ref):
    p = pl.program_id(1)
    last = pl.num_programs(1) - 1

    @pl.when(p == 0)
    def _():
        m_ref[...] = jnp.full_like(m_ref, NEG_INF)
        l_ref[...] = jnp.zeros_like(l_ref)
        acc_ref[...] = jnp.zeros_like(acc_ref)

    def step(k, v):
        s = _dot_nt(q_ref[0], k) + b_ref[0]
        m = m_ref[...]
        m_new = jnp.maximum(m, jnp.max(s, axis=-1, keepdims=True))
        alpha = jnp.exp(m - m_new)
        e = jnp.exp(s - m_new)
        m_ref[...] = m_new
        l_ref[...] = alpha * l_ref[...] + jnp.sum(e, axis=-1, keepdims=True)
        acc_ref[...] = alpha * acc_ref[...] + _dot(e.astype(BF16), v)

    @pl.when(p < last)
    def _():
        step(kp_ref[0].astype(BF16), vp_ref[0].astype(BF16))

    @pl.when(p == last)
    def _():
        step(kn_ref[0], vn_ref[0])
        o_ref[0] = acc_ref[...] / l_ref[...]


def paged_attention(q, pages, page_table, k_blk, v_blk, k_new, v_new, bias):
    nseq, rows, lk = q.shape
    lv = v_new.shape[-1]
    npg = page_table.shape[1]
    per_seq_bias = bias.shape[0] != 1
    page = lambda w, blk: pl.BlockSpec((1, PAGE_SIZE, w), lambda b, p, pt: (pt[b, jnp.minimum(p, npg - 1)], 0, blk))
    seq = lambda r, w: pl.BlockSpec((1, r, w), lambda b, p, pt: (b, 0, 0))
    return pl.pallas_call(
        _paged_attn_kernel,
        grid_spec=pltpu.PrefetchScalarGridSpec(
            num_scalar_prefetch=1,
            grid=(nseq, npg + 1),
            in_specs=[seq(rows, lk), page(lk, k_blk), page(lv, v_blk), seq(PAGE_SIZE, lk), seq(PAGE_SIZE, lv),
                      pl.BlockSpec((1, rows, PAGE_SIZE), lambda b, p, pt: (b if per_seq_bias else 0, 0, p))],
            out_specs=seq(rows, lv),
            scratch_shapes=[pltpu.VMEM((rows, 1), F32), pltpu.VMEM((rows, 1), F32), pltpu.VMEM((rows, lv), F32)]),
        out_shape=jax.ShapeDtypeStruct((nseq, rows, lv), F32),
        compiler_params=_cparams("arbitrary", "arbitrary"),
    )(page_table, q, pages, pages, k_new, v_new, bias)


def _diff_post_kernel(p_ref, cos_ref, sin_ref, g_ref, so_ref, q_ref, kvf_ref, kvb_ref):
    cos, sin = cos_ref[...], sin_ref[...]
    so = so_ref[...]
    scale = HEAD_DIM ** -0.5
    for s in range(DIFF_WIDTH // LANES):
        sl = slice(s * LANES, (s + 1) * LANES)
        q = _rope_slab(_seg64_norm(p_ref[:, sl], g_ref[0:1, :], so), cos, sin)
        q_ref[:, sl] = (q * scale).astype(BF16)
        k = _rope_slab(_seg64_norm(p_ref[:, DIFF_WIDTH + s * LANES:DIFF_WIDTH + (s + 1) * LANES], g_ref[1:2, :], so),
                       cos, sin)
        kvf_ref[:, sl] = k
        kvb_ref[:, sl] = k.astype(BF16)
    v = p_ref[:, 2 * DIFF_WIDTH:3 * DIFF_WIDTH]
    kvf_ref[:, DIFF_WIDTH:] = v
    kvb_ref[:, DIFF_WIDTH:] = v.astype(BF16)


def diff_post(proj, cos, sin, qk_g):
    n = proj.shape[0]
    tm = min(256, n)
    g2 = jnp.tile(qk_g, (1, 2))
    so = _seg_ones()
    row = lambda w: pl.BlockSpec((tm, w), lambda i: (i, 0))
    const = lambda a: pl.BlockSpec(a.shape, lambda i: (0, 0))
    return pl.pallas_call(
        _diff_post_kernel,
        grid=(n // tm,),
        in_specs=[row(3 * DIFF_WIDTH), row(LANES), row(LANES), const(g2), const(so)],
        out_specs=[row(DIFF_WIDTH), row(2 * DIFF_WIDTH), row(2 * DIFF_WIDTH)],
        out_shape=[jax.ShapeDtypeStruct((n, DIFF_WIDTH), BF16), jax.ShapeDtypeStruct((n, 2 * DIFF_WIDTH), F32),
                   jax.ShapeDtypeStruct((n, 2 * DIFF_WIDTH), BF16)],
        compiler_params=_cparams("parallel"),
    )(proj, cos, sin, g2, so)


def _diff_lambda(lam_ref, lam_init):
    lf = lam_ref[...]
    a = jnp.sum(lf[0:1, :] * lf[1:2, :], axis=-1, keepdims=True)
    b = jnp.sum(lf[2:3, :] * lf[3:4, :], axis=-1, keepdims=True)
    return jnp.exp(a) - jnp.exp(b) + lam_init


DIFF_TQ = 256


def _diff_flash_kernel(lam_init, q_ref, k_ref, v_ref, lam_ref, o_ref):
    i = pl.program_id(2)
    tq = q_ref.shape[1]
    tk = tq
    rows = 2 * tq
    q = q_ref[0].astype(F32)
    comp = lax.broadcasted_iota(jnp.int32, q.shape, 1) >> 6
    qst = jnp.concatenate([jnp.where(comp == c, q, 0.0) for c in range(2)], axis=0).astype(BF16)

    def tile(j):
        lo = pl.multiple_of(j * tk, tk)
        return _dot_nt(qst, k_ref[0, pl.ds(lo, tk), :]), v_ref[0, pl.ds(lo, tk), :]

    def update(carry, s, v):
        m, l, acc = carry
        m_new = jnp.maximum(m, jnp.max(s, axis=-1, keepdims=True))
        alpha = jnp.exp(m - m_new)
        p = jnp.exp(s - m_new)
        return m_new, alpha * l + jnp.sum(p, axis=-1, keepdims=True), alpha * acc + _dot(p.astype(BF16), v)

    def body(j, carry):
        s, v = tile(j)
        return update(carry, s, v)

    init = (jnp.full((rows, 1), NEG_INF, F32), jnp.zeros((rows, 1), F32), jnp.zeros((rows, LANES), F32))
    carry = lax.fori_loop(0, i, body, init)
    s, v = tile(i)
    qpos = lax.broadcasted_iota(jnp.int32, (rows, tk), 0) & (tq - 1)
    kpos = lax.broadcasted_iota(jnp.int32, (rows, tk), 1)
    _, l, acc = update(carry, jnp.where(kpos <= qpos, s, NEG_INF), v)
    o = acc / l
    o_ref[0] = o[0:tq, :] - _diff_lambda(lam_ref, lam_init) * o[tq:rows, :]


def diff_flash_prompt(q, kvb, lam, lam_init):
    b, t, _ = q.shape
    tq = min(DIFF_TQ, t)
    return pl.pallas_call(
        functools.partial(_diff_flash_kernel, lam_init),
        grid=(b, DIFF_HEADS, t // tq),
        in_specs=[pl.BlockSpec((1, tq, LANES), lambda bi, h, i: (bi, i, h)),
                  pl.BlockSpec((1, t, LANES), lambda bi, h, i: (bi, 0, h)),
                  pl.BlockSpec((1, t, LANES), lambda bi, h, i: (bi, 0, DIFF_HEADS + h)),
                  pl.BlockSpec(lam.shape, lambda bi, h, i: (0, 0))],
        out_specs=pl.BlockSpec((1, tq, LANES), lambda bi, h, i: (bi, i, h)),
        out_shape=jax.ShapeDtypeStruct((b, t, DIFF_WIDTH), F32),
        compiler_params=_cparams("parallel", "parallel", "arbitrary"),
    )(q, kvb, kvb, lam)


def _diff_out_kernel(lam_init, two, *refs):
    if two:
        o0_ref, o1_ref, lam_ref, g_ref, w_ref, r_ref, out_ref, h_ref = refs
        o = o0_ref[...] - _diff_lambda(lam_ref, lam_init) * o1_ref[...]
    else:
        o0_ref, g_ref, w_ref, r_ref, out_ref, h_ref = refs
        o = o0_ref[...]
    for s in range(DIFF_HEADS):
        sl = slice(s * LANES, (s + 1) * LANES)
        h_ref[:, sl] = (_rms_rows(o[:, sl], g_ref[...]) * (1.0 - lam_init)).astype(BF16)
    out_ref[...] = r_ref[...] + _dot(h_ref[...], w_ref[...])


def diff_out(o, sub_g, w_o, res, lam_init, o1=None, lam=None):
    n, d = res.shape
    tm = min(512, n)
    two = o1 is not None
    row = lambda w: pl.BlockSpec((tm, w), lambda i: (i, 0))
    const = lambda a: pl.BlockSpec(a.shape, lambda i: (0, 0))
    g = sub_g.reshape(1, LANES)
    ins = [o, o1, lam, g, w_o, res] if two else [o, g, w_o, res]
    specs = ([row(DIFF_WIDTH), row(DIFF_WIDTH), const(lam)] if two else [row(DIFF_WIDTH)]) + [const(g), const(w_o), row(d)]
    return pl.pallas_call(
        functools.partial(_diff_out_kernel, lam_init, two),
        grid=(n // tm,),
        in_specs=specs,
        out_specs=row(d),
        out_shape=jax.ShapeDtypeStruct((n, d), F32),
        scratch_shapes=[pltpu.VMEM((tm, DIFF_WIDTH), BF16)],
        compiler_params=_cparams("parallel"),
    )(*ins)


def _gla_kernel(chunk, nchunk, t_valid, p_ref, s0_ref, wa_ref, ba_ref, g_ref, tri_ref, y_ref, sfin_ref, st_ref):
    step = pl.program_id(1)

    @pl.when(step == 0)
    def _():
        st_ref[...] = s0_ref[0]

    nk = GLA_HEADS * GLA_DK
    nv = GLA_HEADS * GLA_DV
    tri = tri_ref[...]
    causal = lax.broadcasted_iota(jnp.int32, (chunk, chunk), 0) >= lax.broadcasted_iota(jnp.int32, (chunk, chunk), 1)
    for ci in range(nchunk):
        r0 = ci * chunk
        rows = slice(r0, r0 + chunk)
        a1 = p_ref[0, rows, GLA_MAIN:GLA_MAIN + LANES].astype(BF16)
        z = _dot(a1, wa_ref[...]) + ba_ref[...]
        log_a = (jnp.minimum(z, 0.0) - jnp.log(1.0 + jnp.exp(-jnp.abs(z)))) * (1.0 / GLA_TAU)
        if t_valid < chunk:
            log_a = jnp.where(lax.broadcasted_iota(jnp.int32, log_a.shape, 0) < t_valid, log_a, 0.0)
        cum = _split_dot_left(tri, log_a)
        for h in range(GLA_HEADS):
            ksl = slice(h * GLA_DK, (h + 1) * GLA_DK)
            vsl = slice(h * GLA_DV, (h + 1) * GLA_DV)
            q = p_ref[0, rows, h * GLA_DK:(h + 1) * GLA_DK] * (GLA_DK ** -0.5)
            k = p_ref[0, rows, nk + h * GLA_DK:nk + (h + 1) * GLA_DK]
            v = p_ref[0, rows, 2 * nk + h * GLA_DV:2 * nk + (h + 1) * GLA_DV]
            r = p_ref[0, rows, 2 * nk + nv + h * GLA_DV:2 * nk + nv + (h + 1) * GLA_DV]
            cm = cum[:, ksl]
            last = cm[chunk - 1:chunk, :]
            qe = (q * jnp.exp(cm)).astype(BF16)
            ke = (k * jnp.exp(-cm)).astype(BF16)
            vb = v.astype(BF16)
            att = jnp.where(causal, _dot_nt(qe, ke), 0.0)
            st = st_ref[h]
            o = _dot_nt(qe, st.astype(BF16)) + _dot(att.astype(BF16), vb)
            kd = (k * jnp.exp(last - cm)).astype(BF16)
            st_ref[h] = st * jnp.exp(last) + _dot(v.T.astype(BF16), kd)
            y_ref[0, rows, vsl] = _rms_rows(o, g_ref[...]) * _silu(r)

    @pl.when(step == pl.num_programs(1) - 1)
    def _():
        sfin_ref[0] = st_ref[...]


def _split_dot_left(m, x):
    hi = x.astype(BF16)
    lo = (x - hi.astype(F32)).astype(BF16)
    return _dot(m, hi) + _dot(m, lo)


def gla_core(proj, s0_t, w_a2, b_a, out_g, chunk, nchunk, t_valid):
    b, t, _ = proj.shape
    rows = chunk * nchunk
    wa = jnp.zeros((LANES, GLA_HEADS * GLA_DK), F32).at[:GLA_RANK].set(w_a2).astype(BF16)
    ba = b_a.reshape(1, -1)
    g = out_g.reshape(1, GLA_DV)
    tri = (jnp.arange(chunk)[:, None] >= jnp.arange(chunk)[None, :]).astype(BF16)
    const = lambda a: pl.BlockSpec(a.shape, lambda bi, s: (0,) * a.ndim)
    st_spec = pl.BlockSpec((1, GLA_HEADS, GLA_DV, GLA_DK), lambda bi, s: (bi, 0, 0, 0))
    return pl.pallas_call(
        functools.partial(_gla_kernel, chunk, nchunk, t_valid),
        grid=(b, t // rows),
        in_specs=[pl.BlockSpec((1, rows, GLA_IN_PAD), lambda bi, s: (bi, s, 0)), st_spec,
                  const(wa), const(ba), const(g), const(tri)],
        out_specs=[pl.BlockSpec((1, rows, GLA_HEADS * GLA_DV), lambda bi, s: (bi, s, 0)), st_spec],
        out_shape=[jax.ShapeDtypeStruct((b, t, GLA_HEADS * GLA_DV), F32),
                   jax.ShapeDtypeStruct((b, GLA_HEADS, GLA_DV, GLA_DK), F32)],
        scratch_shapes=[pltpu.VMEM((GLA_HEADS, GLA_DV, GLA_DK), F32)],
        compiler_params=_cparams("parallel", "arbitrary"),
    )(proj, s0_t, wa, ba, g, tri)


def _pad_cols(w, width):
    return jnp.pad(w, ((0, 0), (0, width - w.shape[1]))).astype(BF16)


def _tile4(x):
    b, t, _ = x.shape
    xg = x.reshape(b, t, NSA_GROUPS, 1, HEAD_DIM).transpose(0, 2, 1, 3, 4)
    return jnp.broadcast_to(xg, (b, NSA_GROUPS, t, NSA_REP, HEAD_DIM)).reshape(b, NSA_GROUPS, t, 256).astype(BF16)


def _group_diag(q, t_seq):
    nseq = q.shape[0] // t_seq
    qg = q.reshape(nseq, t_seq, NSA_GROUPS, NSA_REP, HEAD_DIM).transpose(0, 2, 1, 3, 4)
    eye = jnp.eye(NSA_GROUPS, dtype=q.dtype)
    out = qg[:, :, :, :, None, :] * eye[None, :, None, None, :, None]
    return out.reshape(nseq, NSA_GROUPS * t_seq * NSA_REP, NSA_GROUPS * HEAD_DIM)


def _group_undiag(o, t_seq):
    nseq = o.shape[0]
    o6 = o.reshape(nseq, NSA_GROUPS, t_seq, NSA_REP, NSA_GROUPS, HEAD_DIM)
    od = jnp.stack([o6[:, g, :, :, g, :] for g in range(NSA_GROUPS)], axis=1)
    return od.transpose(0, 2, 1, 3, 4).reshape(nseq * t_seq, NSA_HEADS * HEAD_DIM)


def _pad_rows(x, rows):
    return jnp.pad(x, ((0, 0), (0, rows - x.shape[1]), (0, 0)))


def _nsa_layer(xp, xs, seq_p, t_s, past_len, cache, state_win, page_table, norm_g, w_in, qk_g, pe, w_phi, w_o):
    bp = xp.shape[0] // seq_p
    bs = xs.shape[0] // t_s
    w_in_b = _pad_cols(w_in, NSA_IN_PAD)
    w_o_b = w_o.astype(BF16)
    npg = past_len // PAGE_SIZE

    cos_p, sin_p = _rope_tables(jnp.arange(seq_p))
    proj = norm_matmul(xp, norm_g, w_in_b, 896)
    qn, qr, rows, win, gates = nsa_post(proj, jnp.tile(cos_p, (bp, 1)), jnp.tile(sin_p, (bp, 1)), qk_g)
    ident = jnp.arange(bp * seq_p // PAGE_SIZE, dtype=jnp.int32).reshape(bp, seq_p // PAGE_SIZE)
    kc, vc = nsa_compress(rows.reshape(-1, PAGE_SIZE, 1024), ident, pe, w_phi, qk_g[3])
    rows3 = rows.reshape(bp, seq_p, 1024)
    win3 = win.reshape(bp, seq_p, 512)
    qn3, qr3 = qn.reshape(bp, seq_p, 1024), qr.reshape(bp, seq_p, 1024)
    oc, sel = nsa_cmp_select(qn3, _tile4(kc), _tile4(vc), 0, seq_p // NSA_BLOCK)
    os_ = nsa_selected_prompt(qr3, sel, _tile4(rows3[:, :, 512:768]), _tile4(rows3[:, :, 768:1024]))
    ow = nsa_window_prompt(qr3, _tile4(win3[:, :, 0:256]), _tile4(win3[:, :, 256:512]))
    xp_new = nsa_out(oc.reshape(-1, 1024), os_.reshape(-1, 1024), ow.reshape(-1, 1024), gates, w_o_b, xp)
    kv_p = rows3.reshape(bp, seq_p, 4, NSA_GROUPS, HEAD_DIM)
    win_p = win3[:, seq_p - min(NSA_WINDOW, seq_p):].reshape(bp, -1, 2, NSA_GROUPS, HEAD_DIM)

    pos_s = past_len + jnp.arange(t_s)
    cos_s, sin_s = _rope_tables(pos_s)
    proj = norm_matmul(xs, norm_g, w_in_b, 896)
    qn, qr, rows, win, gates = nsa_post(proj, jnp.tile(cos_s, (bs, 1)), jnp.tile(sin_s, (bs, 1)), qk_g)
    pages = cache.reshape(cache.shape[0], PAGE_SIZE, 1024)
    kc, vc = nsa_compress(pages, page_table, pe, w_phi, qk_g[3])
    qn_pad = _pad_rows(qn.reshape(bs, t_s, 1024), 128)
    ns = -(-(past_len + t_s) // NSA_BLOCK)
    ns_rows = 256
    oc, sel = nsa_cmp_select(qn_pad, _tile4(kc), _tile4(vc), past_len, ns_rows)
    oc = oc[:, :t_s].reshape(bs * t_s, 1024)
    sel_f = sel[:, :, :t_s, :ns].astype(F32)
    past_bias = jnp.repeat(sel_f[..., :past_len // NSA_BLOCK], NSA_BLOCK, axis=-1)
    tt = jnp.arange(t_s)
    new_ok = (jnp.arange(PAGE_SIZE)[None, :] <= tt[:, None])
    new_bias = jnp.where(new_ok[None, None], sel_f[..., past_len // NSA_BLOCK][..., None], NEG_INF)
    bias = jnp.concatenate([past_bias, new_bias], axis=-1)
    bias = jnp.broadcast_to(bias[:, :, :, None, :], (bs, NSA_GROUPS, t_s, NSA_REP, bias.shape[-1]))
    bias = bias.reshape(bs, NSA_GROUPS * t_s * NSA_REP, -1)
    q_bd = _group_diag(qr, t_s)
    rows_s3 = rows.reshape(bs, t_s, 1024)
    k_new = _pad_rows(rows_s3[:, :, 512:768], PAGE_SIZE).astype(BF16)
    v_new = _pad_rows(rows_s3[:, :, 768:1024], PAGE_SIZE).astype(BF16)
    os_ = _group_undiag(paged_attention(q_bd, pages, page_table, 2, 3, k_new, v_new, bias), t_s)
    wbuf = state_win.shape[1]
    wpages = state_win.reshape(bs * wbuf // PAGE_SIZE, PAGE_SIZE, 512)
    wtable = jnp.arange(bs * wbuf // PAGE_SIZE, dtype=jnp.int32).reshape(bs, wbuf // PAGE_SIZE)
    kidx = jnp.arange(wbuf + PAGE_SIZE)
    wpos = jnp.where(kidx < wbuf, past_len - wbuf + kidx, past_len + kidx - wbuf)
    w_ok = ((wpos[None, :] <= pos_s[:, None]) & (wpos[None, :] > pos_s[:, None] - NSA_WINDOW) & (wpos[None, :] >= 0)
            & (kidx[None, :] < wbuf + t_s))
    wbias = jnp.where(w_ok, 0.0, NEG_INF).astype(F32)
    wbias = jnp.broadcast_to(wbias[None, :, None, :], (NSA_GROUPS, t_s, NSA_REP, wbias.shape[-1]))
    wbias = wbias.reshape(1, NSA_GROUPS * t_s * NSA_REP, -1)
    win_s3 = win.reshape(bs, t_s, 512)
    kw_new = _pad_rows(win_s3[:, :, 0:256], PAGE_SIZE).astype(BF16)
    vw_new = _pad_rows(win_s3[:, :, 256:512], PAGE_SIZE).astype(BF16)
    ow = _group_undiag(paged_attention(q_bd, wpages, wtable, 0, 1, kw_new, vw_new, wbias), t_s)
    xs_new = nsa_out(oc, os_, ow, gates, w_o_b, xs)
    kv_s = rows_s3.reshape(bs, t_s, 4, NSA_GROUPS, HEAD_DIM)
    win_all = jnp.concatenate([state_win.reshape(bs, wbuf, 512), win_s3], axis=1)
    win_s = win_all[:, -min(NSA_WINDOW, win_all.shape[1]):].reshape(bs, -1, 2, NSA_GROUPS, HEAD_DIM)
    return xp_new, xs_new, kv_p, kv_s, win_p, win_s


def _head_diag(q, t_seq):
    nseq = q.shape[0] // t_seq
    hc = DIFF_HEADS * 2
    qh = q.reshape(nseq, t_seq, hc, HEAD_DIM).transpose(0, 2, 1, 3)
    eye = jnp.eye(hc, dtype=q.dtype)
    out = qh[:, :, :, None, :] * eye[None, :, None, :, None]
    return out.reshape(nseq, hc * t_seq, hc * HEAD_DIM)


def _diff_layer(layer, xp, xs, seq_p, t_s, past_len, cache, page_table, norm_g, w_in, qk_g, lam, sub_g, w_o):
    bp = xp.shape[0] // seq_p
    bs = xs.shape[0] // t_s
    lam_init = 0.8 - 0.6 * math.exp(-0.3 * layer)
    w_in_b = w_in.astype(BF16)
    w_o_b = w_o.astype(BF16)
    cos_p, sin_p = _rope_tables(jnp.arange(seq_p))
    proj = norm_matmul(xp, norm_g, w_in_b, 768)
    q, kvf, kvb = diff_post(proj, jnp.tile(cos_p, (bp, 1)), jnp.tile(sin_p, (bp, 1)), qk_g)
    o = diff_flash_prompt(q.reshape(bp, seq_p, DIFF_WIDTH), kvb.reshape(bp, seq_p, 2 * DIFF_WIDTH), lam, lam_init)
    xp_new = diff_out(o.reshape(-1, DIFF_WIDTH), sub_g, w_o_b, xp, lam_init)
    kv_p = kvf.reshape(bp, seq_p, 2, DIFF_HEADS, 2 * HEAD_DIM)
    pos_s = past_len + jnp.arange(t_s)
    cos_s, sin_s = _rope_tables(pos_s)
    proj = norm_matmul(xs, norm_g, w_in_b, 768)
    q, kvf, kvb = diff_post(proj, jnp.tile(cos_s, (bs, 1)), jnp.tile(sin_s, (bs, 1)), qk_g)
    pages = cache.reshape(cache.shape[0], PAGE_SIZE, 2 * DIFF_WIDTH)
    kvb3 = kvb.reshape(bs, t_s, 2 * DIFF_WIDTH)
    k_new = _pad_rows(kvb3[:, :, :DIFF_WIDTH], PAGE_SIZE)
    v_new = _pad_rows(kvb3[:, :, DIFF_WIDTH:], PAGE_SIZE)
    tt = jnp.arange(t_s)
    new_ok = jnp.arange(PAGE_SIZE)[None, :] <= tt[:, None]
    bias = jnp.concatenate([jnp.zeros((t_s, past_len), F32), jnp.where(new_ok, 0.0, NEG_INF).astype(F32)], axis=1)
    bias = jnp.broadcast_to(bias[None, None], (1, DIFF_HEADS * 2, t_s, bias.shape[-1])).reshape(1, DIFF_HEADS * 2 * t_s, -1)
    o = paged_attention(_head_diag(q, t_s), pages, page_table, 0, 1, k_new, v_new, bias)
    o6 = o.reshape(bs, DIFF_HEADS, 2, t_s, DIFF_HEADS, 2 * HEAD_DIM)
    od = jnp.stack([o6[:, h, :, :, h, :] for h in range(DIFF_HEADS)], axis=1)
    od = od.transpose(2, 0, 3, 1, 4).reshape(2, bs * t_s, DIFF_WIDTH)
    xs_new = diff_out(od[0], sub_g, w_o_b, xs, lam_init, o1=od[1], lam=lam)
    kv_s = kvf.reshape(bs, t_s, 2, DIFF_HEADS, 2 * HEAD_DIM)
    return xp_new, xs_new, kv_p, kv_s


def _gla_layer(xp, xs, seq_p, t_s, state, norm_g, w_in, w_a2, b_a, out_g, w_o):
    bp = xp.shape[0] // seq_p
    bs = xs.shape[0] // t_s
    w_in_b = _pad_cols(w_in, GLA_IN_PAD)
    w_o_b = w_o.astype(BF16)
    proj = norm_matmul(xp, norm_g, w_in_b, 640).reshape(bp, seq_p, GLA_IN_PAD)
    chunk = min(GLA_CHUNK, seq_p)
    s0 = jnp.zeros((bp, GLA_HEADS, GLA_DV, GLA_DK), F32)
    y, st = gla_core(proj, s0, w_a2, b_a, out_g, chunk, 4 if seq_p % (4 * chunk) == 0 else 1, chunk)
    xp_new = matmul_residual(y.reshape(-1, GLA_HEADS * GLA_DV), w_o_b, xp)
    st_p = st.transpose(0, 1, 3, 2)
    chunk_s = GLA_CHUNK
    proj = norm_matmul(xs, norm_g, w_in_b, 640).reshape(bs, t_s, GLA_IN_PAD)
    proj = _pad_rows(proj, chunk_s)
    y, st = gla_core(proj, state.transpose(0, 1, 3, 2), w_a2, b_a, out_g, chunk_s, 1, t_s)
    xs_new = matmul_residual(y[:, :t_s].reshape(-1, GLA_HEADS * GLA_DV), w_o_b, xs)
    st_s = st.transpose(0, 1, 3, 2)
    return xp_new, xs_new, st_p, st_s


def kernel(x_prompt, x_sample, cache_nsa_kv, state_nsa_win, cache_diff_kv, state_gla, state_ffn, page_table, norm_g, ffn_w_up, ffn_conv_w, ffn_conv_b, ffn_w_down, nsa_w_in, nsa_qk_g, nsa_pe, nsa_w_phi, nsa_w_o, diff_w_in, diff_qk_g, diff_lam, diff_sub_g, diff_w_o, gla_w_in, gla_w_a2, gla_b_a, gla_out_g, gla_w_o):
    bp, seq_p, d = x_prompt.shape
    bs, t_s, _ = x_sample.shape
    past_len = page_table.shape[1] * PAGE_SIZE
    xp = x_prompt.reshape(bp * seq_p, d)
    xs = x_sample.reshape(bs * t_s, d)
    nsa_kv_p, nsa_kv_s, nsa_win_p, nsa_win_s = [], [], [], []
    diff_kv_p, diff_kv_s, gla_p, gla_s, ffn_p, ffn_s = [], [], [], [], [], []
    for i in range(DEPTH):
        kind, s = i % N_MIXERS, i // N_MIXERS
        if kind == 0:
            xp, xs, kvp, kvs, wp, ws = _nsa_layer(
                xp, xs, seq_p, t_s, past_len, cache_nsa_kv[s], state_nsa_win[s].reshape(bs, -1, 512), page_table,
                norm_g[i, 0], nsa_w_in[s], nsa_qk_g[s], nsa_pe[s], nsa_w_phi[s], nsa_w_o[s])
            nsa_kv_p.append(kvp); nsa_kv_s.append(kvs); nsa_win_p.append(wp); nsa_win_s.append(ws)
        elif kind == 1:
            xp, xs, kvp, kvs = _diff_layer(
                i, xp, xs, seq_p, t_s, past_len, cache_diff_kv[s], page_table, norm_g[i, 0], diff_w_in[s],
                diff_qk_g[s], diff_lam[s], diff_sub_g[s], diff_w_o[s])
            diff_kv_p.append(kvp); diff_kv_s.append(kvs)
        else:
            xp, xs, stp, sts = _gla_layer(xp, xs, seq_p, t_s, state_gla[s], norm_g[i, 0], gla_w_in[s], gla_w_a2[s],
                                          gla_b_a[s], gla_out_g[s], gla_w_o[s])
            gla_p.append(stp); gla_s.append(sts)
        w_up_b = ffn_w_up[i].astype(BF16)
        w_dn_b = ffn_w_down[i].astype(BF16)
        xp, tail_p = ffn_prompt(xp, norm_g[i, 1], w_up_b, ffn_conv_w[i], ffn_conv_b[i], w_dn_b, seq_p)
        xs, tail_s = ffn_sample(xs, norm_g[i, 1], w_up_b, ffn_conv_w[i], ffn_conv_b[i], w_dn_b, state_ffn[i], t_s)
        ffn_p.append(tail_p); ffn_s.append(tail_s)
    return (xp.reshape(bp, seq_p, d), xs.reshape(bs, t_s, d),
            jnp.stack(nsa_kv_p), jnp.stack(nsa_kv_s), jnp.stack(nsa_win_p), jnp.stack(nsa_win_s),
            jnp.stack(diff_kv_p), jnp.stack(diff_kv_s), jnp.stack(gla_p), jnp.stack(gla_s),
            jnp.stack(ffn_p), jnp.stack(ffn_s))
```

```python
import functools
import math

import jax
import jax.numpy as jnp
from jax import lax
from jax.experimental import pallas as pl
from jax.experimental.pallas import tpu as pltpu

F32 = jnp.float32
BF16 = jnp.bfloat16

D_MODEL = 1024
HEAD_DIM = 64
ROPE_THETA = 10000.0
NORM_EPS = 1e-6
NEG_INF = -1e30
DEPTH = 4
N_MIXERS = 3
PAGE_SIZE = 128
NSA_HEADS = 16
NSA_GROUPS = 4
NSA_REP = 4
NSA_BLOCK = 64
NSA_TOPN = 16
NSA_WINDOW = 512
NSA_FORCED = 1e9
NSA_NQ = NSA_HEADS * HEAD_DIM
NSA_NKV = 6 * NSA_GROUPS * HEAD_DIM
NSA_NGATE = 3 * NSA_HEADS
NSA_GW = NSA_GROUPS * HEAD_DIM
DIFF_HEADS = 8
DIFF_WIDTH = 1024
GLA_HEADS = 4
GLA_DK = 128
GLA_DV = 256
GLA_RANK = 16
GLA_TAU = 16.0
GLA_CHUNK = 64
GLA_MAIN = 2 * GLA_HEADS * GLA_DK + 2 * GLA_HEADS * GLA_DV
GLA_IN_PAD = GLA_MAIN + 128
D_FF = 2816
CONV_W = 3
LANES = 128
VMEM_LIMIT = 56 * 1024 * 1024
SEL_TK = 512
PAGES_PER_STEP = 4


def _cparams(*sem):
    return pltpu.CompilerParams(dimension_semantics=sem, vmem_limit_bytes=VMEM_LIMIT)


def _dot(a, b):
    return jnp.dot(a, b, preferred_element_type=F32)


def _dot_nt(a, b):
    return lax.dot_general(a, b, (((1,), (1,)), ((), ())), preferred_element_type=F32)


def _split_dot(x, m):
    hi = x.astype(BF16)
    lo = (x - hi.astype(F32)).astype(BF16)
    return _dot(hi, m) + _dot(lo, m)


def _split_dot_left(m, x):
    hi = x.astype(BF16)
    lo = (x - hi.astype(F32)).astype(BF16)
    return _dot(m, hi) + _dot(m, lo)


def _rms_rows(x, g):
    ms = jnp.mean(x * x, axis=-1, keepdims=True)
    return x * lax.rsqrt(ms + NORM_EPS) * g


def _seg64_norm(x, g, seg_ones):
    ms = _split_dot(x * x, seg_ones) * (1.0 / HEAD_DIM)
    return x * lax.rsqrt(ms + NORM_EPS) * g


def _rope_slab(x, cos, sin_signed):
    lane = lax.broadcasted_iota(jnp.int32, x.shape, 1)
    first = (lane & 63) < 32
    partner = jnp.where(first, pltpu.roll(x, 96, 1), pltpu.roll(x, 32, 1))
    return x * cos + partner * sin_signed


def _rope_angles(pos):
    half = HEAD_DIM // 2
    inv = ROPE_THETA ** (-jnp.arange(half, dtype=F32) / half)
    ang = pos.astype(F32)[:, None] * inv
    return jnp.cos(ang), jnp.sin(ang)


def _rope_tables(pos):
    cos, sin = _rope_angles(pos)
    return jnp.tile(cos, (1, 4)), jnp.tile(jnp.concatenate([-sin, sin], axis=1), (1, 2))


def _seg_ones():
    i = jnp.arange(LANES)
    return (i[:, None] // HEAD_DIM == i[None, :] // HEAD_DIM).astype(BF16)


def _softmax_update(s, v_dot, m_ref, l_ref, acc_ref):
    m = m_ref[...]
    m_new = jnp.maximum(m, jnp.max(s, axis=-1, keepdims=True))
    alpha = jnp.exp(m - m_new)
    p = jnp.exp(s - m_new)
    m_ref[...] = m_new
    l_ref[...] = alpha * l_ref[...] + jnp.sum(p, axis=-1, keepdims=True)
    acc_ref[...] = alpha * acc_ref[...] + v_dot(p.astype(BF16))


def _softmax_init(m_ref, l_ref, acc_ref):
    m_ref[...] = jnp.full_like(m_ref, NEG_INF)
    l_ref[...] = jnp.zeros_like(l_ref)
    acc_ref[...] = jnp.zeros_like(acc_ref)


def _norm_matmul_kernel(x_ref, g_ref, w_ref, o_ref, h_ref):
    @pl.when(pl.program_id(1) == 0)
    def _():
        h_ref[...] = _rms_rows(x_ref[...], g_ref[...]).astype(BF16)

    o_ref[...] = _dot(h_ref[...], w_ref[...])


def norm_matmul(x, g, w, tn):
    n, d = x.shape
    nout = w.shape[1]
    tm = min(512, n)
    return pl.pallas_call(
        _norm_matmul_kernel,
        grid=(n // tm, nout // tn),
        in_specs=[pl.BlockSpec((tm, d), lambda i, j: (i, 0)),
                  pl.BlockSpec((1, d), lambda i, j: (0, 0)),
                  pl.BlockSpec((d, tn), lambda i, j: (0, j))],
        out_specs=pl.BlockSpec((tm, tn), lambda i, j: (i, j)),
        out_shape=jax.ShapeDtypeStruct((n, nout), F32),
        scratch_shapes=[pltpu.VMEM((tm, d), BF16)],
        compiler_params=_cparams("parallel", "arbitrary"),
        name="norm_matmul",
    )(x, g.reshape(1, d), w)


def _matmul_res_kernel(a_ref, w_ref, r_ref, o_ref):
    o_ref[...] = r_ref[...] + _dot(a_ref[...].astype(BF16), w_ref[...])


def matmul_residual(a, w, res):
    n, k = a.shape
    d = w.shape[1]
    tm = min(512, n)
    return pl.pallas_call(
        _matmul_res_kernel,
        grid=(n // tm,),
        in_specs=[pl.BlockSpec((tm, k), lambda i: (i, 0)),
                  pl.BlockSpec((k, d), lambda i: (0, 0)),
                  pl.BlockSpec((tm, d), lambda i: (i, 0))],
        out_specs=pl.BlockSpec((tm, d), lambda i: (i, 0)),
        out_shape=jax.ShapeDtypeStruct((n, d), F32),
        compiler_params=_cparams("parallel"),
        name="matmul_residual",
    )(a, w, res)


FFN_CHUNK = 256


def _silu(x):
    return x / (1.0 + jnp.exp(-x))


def _causal_conv(u, prev1, prev2, use1, use2, cw_ref, cb_ref):
    u1 = jnp.where(use1, prev1, pltpu.roll(u, 1, 0))
    u2 = jnp.where(use2, prev2, pltpu.roll(u, 2, 0))
    return cb_ref[...] + cw_ref[0:1, :] * u2 + cw_ref[1:2, :] * u1 + cw_ref[2:3, :] * u


def _ffn_prompt_kernel(tiles_per_seq, x_ref, g_ref, wg_ref, wu_ref, cwg_ref, cwu_ref, cbg_ref, cbu_ref,
                       wd_ref, o_ref, tg_ref, tu_ref, h_ref, cg_ref, cu_ref):
    i, j = pl.program_id(0), pl.program_id(1)
    tm = x_ref.shape[0]

    @pl.when(j == 0)
    def _():
        h_ref[...] = _rms_rows(x_ref[...], g_ref[...]).astype(BF16)

    h = h_ref[...]
    ug = _dot(h, wg_ref[...])
    uu = _dot(h, wu_ref[...])
    seq_start = (i % tiles_per_seq) == 0
    row = lax.broadcasted_iota(jnp.int32, ug.shape, 0)

    @pl.when(seq_start)
    def _():
        cg_ref[j] = jnp.zeros(cg_ref.shape[1:], F32)
        cu_ref[j] = jnp.zeros(cu_ref.shape[1:], F32)

    def conv(u, carry_ref, cw_ref, cb_ref):
        prev = carry_ref[j]
        prev2 = jnp.where(row == 0, prev[0:1, :], prev[1:2, :])
        c = _causal_conv(u, prev[1:2, :], prev2, row == 0, row < 2, cw_ref, cb_ref)
        carry_ref[j] = u[tm - 2:tm, :]
        return c

    act = _silu(conv(ug, cg_ref, cwg_ref, cbg_ref)) * conv(uu, cu_ref, cwu_ref, cbu_ref)
    part = _dot(act.astype(BF16), wd_ref[...])
    tg_ref[0] = ug[tm - 2:tm, :]
    tu_ref[0] = uu[tm - 2:tm, :]

    @pl.when(j == 0)
    def _():
        o_ref[...] = x_ref[...] + part

    @pl.when(j > 0)
    def _():
        o_ref[...] += part


def ffn_prompt(x, g, w_up, conv_w, conv_b, w_down, seq_len):
    n, d = x.shape
    c = FFN_CHUNK
    nj = D_FF // c
    tm = min(512, seq_len)
    tps = seq_len // tm
    cb = conv_b.reshape(1, 2 * D_FF)
    out, tg, tu = pl.pallas_call(
        functools.partial(_ffn_prompt_kernel, tps),
        grid=(n // tm, nj),
        in_specs=[pl.BlockSpec((tm, d), lambda i, j: (i, 0)),
                  pl.BlockSpec((1, d), lambda i, j: (0, 0)),
                  pl.BlockSpec((d, c), lambda i, j: (0, j)),
                  pl.BlockSpec((d, c), lambda i, j: (0, j + nj)),
                  pl.BlockSpec((CONV_W, c), lambda i, j: (0, j)),
                  pl.BlockSpec((CONV_W, c), lambda i, j: (0, j + nj)),
                  pl.BlockSpec((1, c), lambda i, j: (0, j)),
                  pl.BlockSpec((1, c), lambda i, j: (0, j + nj)),
                  pl.BlockSpec((c, d), lambda i, j: (j, 0))],
        out_specs=[pl.BlockSpec((tm, d), lambda i, j: (i, 0)),
                   pl.BlockSpec((1, 2, c), lambda i, j: (i, 0, j)),
                   pl.BlockSpec((1, 2, c), lambda i, j: (i, 0, j))],
        out_shape=[jax.ShapeDtypeStruct((n, d), F32),
                   jax.ShapeDtypeStruct((n // tm, 2, D_FF), F32),
                   jax.ShapeDtypeStruct((n // tm, 2, D_FF), F32)],
        scratch_shapes=[pltpu.VMEM((tm, d), BF16),
                        pltpu.VMEM((nj, 2, c), F32),
                        pltpu.VMEM((nj, 2, c), F32)],
        compiler_params=_cparams("arbitrary", "arbitrary"),
        name="ffn_prompt",
    )(x, g.reshape(1, d), w_up, w_up, conv_w, conv_w, cb, cb, w_down)
    return out, jnp.concatenate([tg, tu], axis=-1)[tps - 1::tps]


def _ffn_sample_kernel(t_seq, x_ref, g_ref, wg_ref, wu_ref, cwg_ref, cwu_ref, cbg_ref, cbu_ref, wd_ref,
                       p1g_ref, p2g_ref, p1u_ref, p2u_ref, o_ref, ug_ref, uu_ref, h_ref):
    j = pl.program_id(0)

    @pl.when(j == 0)
    def _():
        h_ref[...] = _rms_rows(x_ref[...], g_ref[...]).astype(BF16)

    h = h_ref[...]
    ug = _dot(h, wg_ref[...])
    uu = _dot(h, wu_ref[...])
    t = lax.broadcasted_iota(jnp.int32, ug.shape, 0) & (t_seq - 1)
    cg = _causal_conv(ug, p1g_ref[...], p2g_ref[...], t == 0, t < 2, cwg_ref, cbg_ref)
    cu = _causal_conv(uu, p1u_ref[...], p2u_ref[...], t == 0, t < 2, cwu_ref, cbu_ref)
    part = _dot((_silu(cg) * cu).astype(BF16), wd_ref[...])
    ug_ref[...] = ug
    uu_ref[...] = uu

    @pl.when(j == 0)
    def _():
        o_ref[...] = x_ref[...] + part

    @pl.when(j > 0)
    def _():
        o_ref[...] += part


def ffn_sample(x, g, w_up, conv_w, conv_b, w_down, buf, t_seq):
    n, d = x.shape
    nseq = n // t_seq
    c = FFN_CHUNK
    nj = D_FF // c
    cb = conv_b.reshape(1, 2 * D_FF)
    reps = t_seq // 2
    prev1 = jnp.concatenate([buf[:, ::-1]] * reps, axis=1).reshape(n, 2 * D_FF)
    prev2 = jnp.concatenate([buf] * reps, axis=1).reshape(n, 2 * D_FF)
    full = lambda blk, off=0: pl.BlockSpec(blk, lambda j: (0, j + off))
    out, ug, uu = pl.pallas_call(
        functools.partial(_ffn_sample_kernel, t_seq),
        grid=(nj,),
        in_specs=[pl.BlockSpec((n, d), lambda j: (0, 0)),
                  pl.BlockSpec((1, d), lambda j: (0, 0)),
                  full((d, c)), full((d, c), nj),
                  full((CONV_W, c)), full((CONV_W, c), nj),
                  full((1, c)), full((1, c), nj),
                  pl.BlockSpec((c, d), lambda j: (j, 0)),
                  full((n, c)), full((n, c)), full((n, c), nj), full((n, c), nj)],
        out_specs=[pl.BlockSpec((n, d), lambda j: (0, 0)),
                   full((n, c)), full((n, c))],
        out_shape=[jax.ShapeDtypeStruct((n, d), F32),
                   jax.ShapeDtypeStruct((n, D_FF), F32),
                   jax.ShapeDtypeStruct((n, D_FF), F32)],
        scratch_shapes=[pltpu.VMEM((n, d), BF16)],
        compiler_params=_cparams("arbitrary"),
        name="ffn_sample",
    )(x, g.reshape(1, d), w_up, w_up, conv_w, conv_w, cb, cb, w_down, prev1, prev2, prev1, prev2)
    u = jnp.concatenate([ug, uu], axis=-1).reshape(nseq, t_seq, 2 * D_FF)
    return out, u[:, t_seq - (CONV_W - 1):]


def _head_norm_t(x, g):
    ms = jnp.mean(x * x, axis=0, keepdims=True)
    return x * lax.rsqrt(ms + NORM_EPS) * g


def _rope_t(x, cos, sin):
    half = HEAD_DIM // 2
    x1, x2 = x[0:half, :], x[half:HEAD_DIM, :]
    return jnp.concatenate([x1 * cos - x2 * sin, x1 * sin + x2 * cos], axis=0)


def _nsa_project_kernel(x_ref, g_ref, wq_ref, wkv_ref, wg_ref, cos_ref, sin_ref, cost_ref, sint_ref, gq_ref, gk_ref,
                        so_ref, qn_ref, qr_ref, gate_ref, rows_ref, win_ref, ksel_ref, kwin_ref):
    hn = _rms_rows(x_ref[...], g_ref[...]).astype(BF16)
    cos, sin = cos_ref[...], sin_ref[...]
    so = so_ref[...]
    scale = HEAD_DIM ** -0.5
    q_all = _dot(hn, wq_ref[...])
    for s in range(NSA_NQ // LANES):
        sl = slice(s * LANES, (s + 1) * LANES)
        q = _seg64_norm(q_all[:, sl], gq_ref[...], so)
        qn_ref[:, sl] = (q * scale).astype(BF16)
        qr_ref[:, sl] = (_rope_slab(q, cos, sin) * scale).astype(BF16)
    gate_ref[...] = 1.0 / (1.0 + jnp.exp(-_dot(hn, wg_ref[...])))
    kvt = _dot_nt(wkv_ref[...], hn)
    gw = NSA_GW
    cost, sint = cost_ref[...], sint_ref[...]
    rows_ref[0, 0:2 * gw, :] = kvt[0:2 * gw, :]
    rows_ref[0, 3 * gw:4 * gw, :] = kvt[3 * gw:4 * gw, :]
    win_ref[0, gw:2 * gw, :] = kvt[5 * gw:6 * gw, :]
    for g in range(NSA_GROUPS):
        hs = slice(g * HEAD_DIM, (g + 1) * HEAD_DIM)
        rows_ref[0, 2 * gw + g * HEAD_DIM:2 * gw + (g + 1) * HEAD_DIM, :] = _rope_t(
            _head_norm_t(kvt[2 * gw + g * HEAD_DIM:2 * gw + (g + 1) * HEAD_DIM, :], gk_ref[0]), cost, sint)
        win_ref[0, hs, :] = _rope_t(
            _head_norm_t(kvt[4 * gw + g * HEAD_DIM:4 * gw + (g + 1) * HEAD_DIM, :], gk_ref[1]), cost, sint)
    ksel_ref[0, 0] = rows_ref[0, 2 * gw:4 * gw, :].astype(BF16)
    wb = win_ref[0].astype(BF16)
    for u in range(kwin_ref.shape[1]):
        kwin_ref[0, u] = wb[:, u * LANES:(u + 1) * LANES]


def nsa_project(x, norm_g, w_in, qk_g, pos, nseq):
    n, d = x.shape
    t = n // nseq
    tm = min(SEL_TK, t)
    nt = t // tm
    wq = w_in[:, :NSA_NQ].astype(BF16)
    wkv_t = w_in[:, NSA_NQ:NSA_NQ + NSA_NKV].T.astype(BF16)
    wg = jnp.pad(w_in[:, NSA_NQ + NSA_NKV:], ((0, 0), (0, LANES - NSA_NGATE))).astype(BF16)
    cos, sin = _rope_angles(pos)
    cos_q, sin_q = jnp.tile(cos, (1, 4)), jnp.tile(jnp.concatenate([-sin, sin], axis=1), (1, 2))
    gq = jnp.tile(qk_g[0:1], (1, 2))
    gk = qk_g[1:3].reshape(2, HEAD_DIM, 1)
    so = _seg_ones()
    const = lambda a: pl.BlockSpec(a.shape, lambda b, i: (0,) * a.ndim)
    row = lambda w: pl.BlockSpec((tm, w), lambda b, i: (b * nt + i, 0))
    return pl.pallas_call(
        _nsa_project_kernel,
        grid=(nseq, nt),
        in_specs=[row(d), const(norm_g.reshape(1, d)), const(wq), const(wkv_t), const(wg),
                  pl.BlockSpec((tm, LANES), lambda b, i: (i, 0)), pl.BlockSpec((tm, LANES), lambda b, i: (i, 0)),
                  pl.BlockSpec((HEAD_DIM // 2, tm), lambda b, i: (0, i)),
                  pl.BlockSpec((HEAD_DIM // 2, tm), lambda b, i: (0, i)),
                  const(gq), const(gk), const(so)],
        out_specs=[row(NSA_NQ), row(NSA_NQ), row(LANES),
                   pl.BlockSpec((1, 4 * NSA_GW, tm), lambda b, i: (b, 0, i)),
                   pl.BlockSpec((1, 2 * NSA_GW, tm), lambda b, i: (b, 0, i)),
                   pl.BlockSpec((1, 1, 2 * NSA_GW, tm), lambda b, i: (b, i, 0, 0)),
                   pl.BlockSpec((1, tm // LANES, 2 * NSA_GW, LANES), lambda b, i: (b, i, 0, 0))],
        out_shape=[jax.ShapeDtypeStruct((n, NSA_NQ), BF16), jax.ShapeDtypeStruct((n, NSA_NQ), BF16),
                   jax.ShapeDtypeStruct((n, LANES), F32),
                   jax.ShapeDtypeStruct((nseq, 4 * NSA_GW, t), F32),
                   jax.ShapeDtypeStruct((nseq, 2 * NSA_GW, t), F32),
                   jax.ShapeDtypeStruct((nseq, nt, 2 * NSA_GW, tm), BF16),
                   jax.ShapeDtypeStruct((nseq, t // LANES, 2 * NSA_GW, LANES), BF16)],
        compiler_params=_cparams("parallel", "parallel"),
        name="nsa_project",
    )(x, norm_g.reshape(1, d), wq, wkv_t, wg, cos_q, sin_q, cos.T, sin.T, gq, gk, so)


def _nsa_compress_kernel(npp, pt_ref, *refs):
    page_refs = refs[:npp]
    pe_ref, w_ref, g_ref, so_ref, kc_ref, vc_ref, seqk_ref, seqv_ref, acc_ref = refs[npp:]
    p = pl.program_id(1)
    gw = NSA_GW
    for u in range(npp):
        lo = pl.multiple_of((p * npp + u) * gw, gw)
        seqk_ref[pl.ds(lo, gw), :] = page_refs[u][0:gw, :]
        seqv_ref[pl.ds(lo, gw), :] = page_refs[u][gw:2 * gw, :]

    @pl.when(p == pl.num_programs(1) - 1)
    def _():
        nrow = seqk_ref.shape[0] // HEAD_DIM
        acc_ref[...] = jnp.zeros_like(acc_ref)
        for dd in range(HEAD_DIM):
            a = jnp.concatenate([seqk_ref[pl.ds(dd, nrow, stride=HEAD_DIM), :],
                                 seqv_ref[pl.ds(dd, nrow, stride=HEAD_DIM), :]], axis=1) + pe_ref[dd:dd + 1, :]
            acc_ref[...] += _dot(a.astype(BF16), w_ref[dd])
        kc_ref[0] = _seg64_norm(acc_ref[:, 0:LANES], g_ref[...], so_ref[...])
        vc_ref[0] = acc_ref[:, LANES:2 * LANES]


def nsa_compress(pages_t, page_index, nseq, npg, pe, w_phi, g_c):
    npp = PAGES_PER_STEP
    nrow = npg * NSA_GROUPS
    pe_t = jnp.concatenate([jnp.tile(pe[0].T, (1, 2)), jnp.tile(pe[1].T, (1, 2))], axis=1)
    eye4 = jnp.eye(4, dtype=F32)
    w4 = jnp.stack([w_phi[0], w_phi[0], w_phi[1], w_phi[1]])
    w = jnp.einsum('ab,alde->dalbe', eye4, w4).reshape(HEAD_DIM, 2 * LANES, 2 * LANES).astype(BF16)
    g2 = jnp.tile(g_c.reshape(1, HEAD_DIM), (1, 2))
    so = _seg_ones()
    const = lambda a: pl.BlockSpec(a.shape, lambda b, p, pt: (0,) * a.ndim)
    page = lambda u: pl.BlockSpec((None, 2 * NSA_GW, PAGE_SIZE), lambda b, p, pt: page_index(b, p * npp + u, pt))
    out = pl.BlockSpec((1, nrow, LANES), lambda b, p, pt: (b, 0, 0))
    table = page_index.table
    kc, vc = pl.pallas_call(
        functools.partial(_nsa_compress_kernel, npp),
        grid_spec=pltpu.PrefetchScalarGridSpec(
            num_scalar_prefetch=1,
            grid=(nseq, npg // npp),
            in_specs=[page(u) for u in range(npp)] + [const(pe_t), const(w), const(g2), const(so)],
            out_specs=[out, out],
            scratch_shapes=[pltpu.VMEM((npg * NSA_GW, PAGE_SIZE), F32), pltpu.VMEM((npg * NSA_GW, PAGE_SIZE), F32),
                            pltpu.VMEM((nrow, 2 * LANES), F32)]),
        out_shape=[jax.ShapeDtypeStruct((nseq, nrow, LANES), F32)] * 2,
        compiler_params=_cparams("arbitrary", "arbitrary"),
        name="nsa_compress",
    )(table, *([pages_t] * npp), pe_t, w, g2, so)
    fix = lambda a: a.reshape(nseq, npg, NSA_GROUPS, 2, HEAD_DIM).transpose(0, 2, 1, 3, 4).reshape(
        nseq, NSA_GROUPS, 2 * npg, HEAD_DIM)
    return fix(kc), fix(vc)


class _PageIndex:
    def __init__(self, table, fn):
        self.table = table
        self._fn = fn

    def __call__(self, b, p, pt):
        return self._fn(b, p, pt)


def _stack_heads(q):
    head = lax.broadcasted_iota(jnp.int32, q.shape, 1) >> 6
    qf = q.astype(F32)
    return jnp.concatenate([jnp.where(head == r, qf, 0.0) for r in range(NSA_REP)], axis=0).astype(BF16)


def _unstack_heads(o4, tq):
    head = lax.broadcasted_iota(jnp.int32, (tq, o4.shape[1]), 1) >> 6
    out = jnp.zeros((tq, o4.shape[1]), F32)
    for r in range(NSA_REP):
        out = jnp.where(head == r, o4[r * tq:(r + 1) * tq, :], out)
    return out


def _tile_rows4(x):
    return jnp.concatenate([x] * NSA_REP, axis=0)


def _nsa_cmp_kernel(q0, ns_rows, q_ref, kc_ref, vc_ref, oc_ref, sel_ref):
    i = pl.program_id(2)
    tq = q_ref.shape[1]
    nc = kc_ref.shape[2]
    qst = _stack_heads(q_ref[0])
    kc, vc = kc_ref[0, 0], vc_ref[0, 0]
    base = q0 + i * tq
    qpos = base + (lax.broadcasted_iota(jnp.int32, (NSA_REP * tq, nc), 0) & (tq - 1))
    blk_end = (lax.broadcasted_iota(jnp.int32, (NSA_REP * tq, nc), 1) + 1) * NSA_BLOCK - 1
    ok = blk_end <= qpos
    s = jnp.where(ok, _dot_nt(qst, kc), NEG_INF)
    e = jnp.exp(s - jnp.max(s, axis=-1, keepdims=True))
    p = jnp.where(ok, e / jnp.sum(e, axis=-1, keepdims=True), 0.0)
    oc_ref[0] = _unstack_heads(_dot(p.astype(BF16), vc), tq)
    qpos_t = base + (lax.broadcasted_iota(jnp.int32, (nc, NSA_REP * tq), 1) & (tq - 1))
    blk_end_t = (lax.broadcasted_iota(jnp.int32, (nc, NSA_REP * tq), 0) + 1) * NSA_BLOCK - 1
    ok_t = blk_end_t <= qpos_t
    st = jnp.where(ok_t, _dot_nt(kc, qst), NEG_INF)
    et = jnp.exp(st - jnp.max(st, axis=0, keepdims=True))
    pt = jnp.where(ok_t, et / jnp.sum(et, axis=0, keepdims=True), 0.0)
    imp = pt[:, 0:tq]
    for r in range(1, NSA_REP):
        imp = imp + pt[:, r * tq:(r + 1) * tq]
    if ns_rows > nc:
        imp = jnp.concatenate([imp, jnp.zeros((ns_rows - nc, tq), F32)], axis=0)
    blk = lax.broadcasted_iota(jnp.int32, (ns_rows, tq), 0)
    cur = (base + lax.broadcasted_iota(jnp.int32, (ns_rows, tq), 1)) >> 6
    forced = (blk == 0) | (blk == cur) | (blk == cur - 1)
    imp = jnp.where(forced, NSA_FORCED, imp)
    imp = jnp.where(blk <= cur, imp, NEG_INF)
    taken = jnp.float32(-3e38)
    blk_f = blk.astype(F32)

    def pick(_, carry):
        imp, sel = carry
        m = jnp.max(imp, axis=0, keepdims=True)
        first = jnp.min(jnp.where(imp == m, blk_f, float(ns_rows)), axis=0, keepdims=True)
        hit = blk_f == first
        return jnp.where(hit, taken, imp), jnp.where(hit, 0.0, sel)

    _, sel = lax.fori_loop(0, NSA_TOPN, pick, (imp, jnp.full((ns_rows, tq), NEG_INF, F32)))
    sel_ref[0, 0] = sel.T.astype(BF16)


def _tile_lanes4(x):
    return jnp.tile(x, (1, 1, 1, NSA_REP)).astype(BF16)


def nsa_cmp_select(qn, kc, vc, q0, ns_rows):
    b, t, _ = qn.shape
    tq = 128
    nc = kc.shape[2]
    kv = pl.BlockSpec((1, 1, nc, 256), lambda bi, g, i: (bi, g, 0, 0))
    return pl.pallas_call(
        functools.partial(_nsa_cmp_kernel, q0, ns_rows),
        grid=(b, NSA_GROUPS, t // tq),
        in_specs=[pl.BlockSpec((1, tq, 256), lambda bi, g, i: (bi, i, g)), kv, kv],
        out_specs=[pl.BlockSpec((1, tq, 256), lambda bi, g, i: (bi, i, g)),
                   pl.BlockSpec((1, 1, tq, ns_rows), lambda bi, g, i: (bi, g, i, 0))],
        out_shape=[jax.ShapeDtypeStruct((b, t, 1024), F32),
                   jax.ShapeDtypeStruct((b, NSA_GROUPS, t, ns_rows), BF16)],
        compiler_params=_cparams("parallel", "parallel", "parallel"),
        name="nsa_cmp_select",
    )(qn, _tile_lanes4(kc), _tile_lanes4(vc))


def _nsa_window_kernel(q_ref, k_ref, v_ref, o_ref):
    i = pl.program_id(2)
    tq = q_ref.shape[1]
    ntile = (NSA_WINDOW + tq) // LANES
    span = ntile * LANES
    qst = _stack_heads(q_ref[0])
    j0 = jnp.maximum(i * tq - NSA_WINDOW, 0) // LANES
    kt = jnp.concatenate([k_ref[0, j0 + u] for u in range(ntile)], axis=1)
    vt = jnp.concatenate([v_ref[0, j0 + u] for u in range(ntile)], axis=1)
    qpos = i * tq + (lax.broadcasted_iota(jnp.int32, (NSA_REP * tq, span), 0) & (tq - 1))
    kpos = j0 * LANES + lax.broadcasted_iota(jnp.int32, (NSA_REP * tq, span), 1)
    ok = (kpos <= qpos) & (kpos > qpos - NSA_WINDOW)
    s = jnp.where(ok, _dot(qst, _tile_rows4(kt)), NEG_INF)
    e = jnp.exp(s - jnp.max(s, axis=-1, keepdims=True))
    p = e / jnp.sum(e, axis=-1, keepdims=True)
    o_ref[0] = _unstack_heads(_dot_nt(p.astype(BF16), _tile_rows4(vt)), tq)


def nsa_window_prompt(qr, kwin):
    b, t, _ = qr.shape
    tq = 128
    nt = kwin.shape[1]
    kv = lambda off: pl.BlockSpec((1, nt, HEAD_DIM, LANES), lambda bi, g, i: (bi, 0, off + g, 0))
    return pl.pallas_call(
        _nsa_window_kernel,
        grid=(b, NSA_GROUPS, t // tq),
        in_specs=[pl.BlockSpec((1, tq, 256), lambda bi, g, i: (bi, i, g)), kv(0), kv(NSA_GROUPS)],
        out_specs=pl.BlockSpec((1, tq, 256), lambda bi, g, i: (bi, i, g)),
        out_shape=jax.ShapeDtypeStruct((b, t, 1024), F32),
        compiler_params=_cparams("parallel", "parallel", "arbitrary"),
        name="nsa_window",
    )(qr, kwin, kwin)


def _nsa_selected_kernel(q_ref, sel_ref, k_ref, v_ref, e_ref, o_ref, m_ref, l_ref, acc_ref):
    i = pl.program_id(2)
    tq = q_ref.shape[1]
    tk = k_ref.shape[3]
    rows = NSA_REP * tq
    qst = _stack_heads(q_ref[0])
    sel = sel_ref[0, 0]
    _softmax_init(m_ref, l_ref, acc_ref)

    def tile(j, causal):
        bias = _dot(sel, e_ref[j])
        s = _dot(qst, _tile_rows4(k_ref[0, j])) + _tile_rows4(bias)
        if causal:
            qpos = i * tq + (lax.broadcasted_iota(jnp.int32, (rows, tk), 0) & (tq - 1))
            kpos = j * tk + lax.broadcasted_iota(jnp.int32, (rows, tk), 1)
            s = jnp.where(kpos <= qpos, s, NEG_INF)
        vt4 = _tile_rows4(v_ref[0, j])
        _softmax_update(s, lambda p: _dot_nt(p, vt4), m_ref, l_ref, acc_ref)

    nfull = (i * tq) // tk

    def body(j, c):
        tile(j, False)
        return c

    lax.fori_loop(0, nfull, body, 0)
    tile(nfull, True)
    o_ref[0] = _unstack_heads(acc_ref[...] / l_ref[...], tq)


def nsa_selected_prompt(qr, sel, ksel):
    b, t, _ = qr.shape
    tq = 128
    nt, _, tk = ksel.shape[1:]
    nblk = sel.shape[-1]
    e3 = (jnp.arange(nblk)[None, :, None] == (jnp.arange(nt)[:, None, None] * tk + jnp.arange(tk)[None, None, :]) // NSA_BLOCK
          ).astype(BF16)
    kv = lambda off: pl.BlockSpec((1, nt, HEAD_DIM, tk), lambda bi, g, i: (bi, 0, off + g, 0))
    return pl.pallas_call(
        _nsa_selected_kernel,
        grid=(b, NSA_GROUPS, t // tq),
        in_specs=[pl.BlockSpec((1, tq, 256), lambda bi, g, i: (bi, i, g)),
                  pl.BlockSpec((1, 1, tq, nblk), lambda bi, g, i: (bi, g, i, 0)),
                  kv(0), kv(NSA_GROUPS),
                  pl.BlockSpec(e3.shape, lambda bi, g, i: (0, 0, 0))],
        out_specs=pl.BlockSpec((1, tq, 256), lambda bi, g, i: (bi, i, g)),
        out_shape=jax.ShapeDtypeStruct((b, t, 1024), F32),
        scratch_shapes=[pltpu.VMEM((NSA_REP * tq, 1), F32), pltpu.VMEM((NSA_REP * tq, 1), F32),
                        pltpu.VMEM((NSA_REP * tq, 256), F32)],
        compiler_params=_cparams("parallel", "parallel", "arbitrary"),
        name="nsa_selected",
    )(qr, sel, ksel, ksel, e3)


def _nsa_out_kernel(oc_ref, os_ref, ow_ref, gate_ref, ex_ref, w_ref, r_ref, o_ref):
    gate = gate_ref[...]
    comb = (_split_dot(gate, ex_ref[0]) * oc_ref[...] + _split_dot(gate, ex_ref[1]) * os_ref[...]
            + _split_dot(gate, ex_ref[2]) * ow_ref[...])
    o_ref[...] = r_ref[...] + _dot(comb.astype(BF16), w_ref[...])


def nsa_out(oc, os_, ow, gates, w_o, res):
    n, d = res.shape
    tm = min(256, n)
    lane = jnp.arange(1024) // HEAD_DIM
    ex = jnp.stack([(jnp.arange(LANES)[:, None] == lane[None, :] * 3 + k) for k in range(3)]).astype(BF16)
    row = lambda w: pl.BlockSpec((tm, w), lambda i: (i, 0))
    return pl.pallas_call(
        _nsa_out_kernel,
        grid=(n // tm,),
        in_specs=[row(1024), row(1024), row(1024), row(LANES),
                  pl.BlockSpec(ex.shape, lambda i: (0, 0, 0)),
                  pl.BlockSpec(w_o.shape, lambda i: (0, 0)), row(d)],
        out_specs=row(d),
        out_shape=jax.ShapeDtypeStruct((n, d), F32),
        compiler_params=_cparams("parallel"),
        name="nsa_out",
    )(oc, os_, ow, gates, ex, w_o, res)


def _paged_attn_kernel(npp, kv_t, pt_ref, q_ref, *refs):
    k_refs, v_refs = refs[:npp], refs[npp:2 * npp]
    kn_ref, vn_ref, bp_ref, bn_ref, o_ref, m_ref, l_ref, acc_ref = refs[2 * npp:]
    p = pl.program_id(1)
    last = pl.num_programs(1) - 1

    @pl.when(p == 0)
    def _():
        _softmax_init(m_ref, l_ref, acc_ref)

    def step(k, v, bias):
        if kv_t:
            s = _dot(q_ref[0], k) + bias
            _softmax_update(s, lambda e: _dot_nt(e, v), m_ref, l_ref, acc_ref)
        else:
            s = _dot_nt(q_ref[0], k) + bias
            _softmax_update(s, lambda e: _dot(e, v), m_ref, l_ref, acc_ref)

    def load(r):
        x = r[...]
        return x if kv_t else x.reshape(-1, x.shape[-1])

    @pl.when(p < last)
    def _():
        axis = 1 if kv_t else 0
        step(jnp.concatenate([load(r) for r in k_refs], axis=axis).astype(BF16),
             jnp.concatenate([load(r) for r in v_refs], axis=axis).astype(BF16), bp_ref[0])

    @pl.when(p == last)
    def _():
        step(kn_ref[0], vn_ref[0], bn_ref[0])
        o_ref[0] = acc_ref[...] / l_ref[...]


def paged_attention(q, pages, table, npg, k_spec, v_spec, kv_t, k_new, v_new, bias_past, bias_new, past_per_step):
    nseq, rows, _ = q.shape
    npp = PAGES_PER_STEP
    lv = acc_w = v_new.shape[1] if kv_t else v_new.shape[2]
    seq = lambda a: pl.BlockSpec((1,) + a.shape[1:], lambda b, p, pt: (b, 0, 0))
    wpast = bias_past.shape[-1] if not past_per_step else bias_past.shape[-1] // (npg // npp)
    bp_spec = pl.BlockSpec((1, rows, wpast), lambda b, p, pt: (
        b if bias_past.shape[0] > 1 else 0, 0, jnp.minimum(p, npg // npp - 1) if past_per_step else 0))
    bn_spec = pl.BlockSpec((1,) + bias_new.shape[1:], lambda b, p, pt: (b if bias_new.shape[0] > 1 else 0, 0, 0))
    return pl.pallas_call(
        functools.partial(_paged_attn_kernel, npp, kv_t),
        grid_spec=pltpu.PrefetchScalarGridSpec(
            num_scalar_prefetch=1,
            grid=(nseq, npg // npp + 1),
            in_specs=[seq(q)] + [k_spec(u) for u in range(npp)] + [v_spec(u) for u in range(npp)]
                     + [seq(k_new), seq(v_new), bp_spec, bn_spec],
            out_specs=pl.BlockSpec((1, rows, acc_w), lambda b, p, pt: (b, 0, 0)),
            scratch_shapes=[pltpu.VMEM((rows, 1), F32), pltpu.VMEM((rows, 1), F32), pltpu.VMEM((rows, acc_w), F32)]),
        out_shape=jax.ShapeDtypeStruct((nseq, rows, lv), F32),
        compiler_params=_cparams("arbitrary", "arbitrary"),
        name="paged_attention",
    )(table, q, *([pages] * (2 * npp)), k_new, v_new, bias_past, bias_new)


def _diff_post_kernel(p_ref, cos_ref, sin_ref, g_ref, so_ref, q_ref, kvf_ref, kvb_ref):
    cos, sin = cos_ref[...], sin_ref[...]
    so = so_ref[...]
    scale = HEAD_DIM ** -0.5
    for s in range(DIFF_WIDTH // LANES):
        sl = slice(s * LANES, (s + 1) * LANES)
        q = _rope_slab(_seg64_norm(p_ref[:, sl], g_ref[0:1, :], so), cos, sin)
        q_ref[:, sl] = (q * scale).astype(BF16)
        k = _rope_slab(_seg64_norm(p_ref[:, DIFF_WIDTH + s * LANES:DIFF_WIDTH + (s + 1) * LANES], g_ref[1:2, :], so),
                       cos, sin)
        kvf_ref[:, sl] = k
        kvb_ref[:, sl] = k.astype(BF16)
    v = p_ref[:, 2 * DIFF_WIDTH:3 * DIFF_WIDTH]
    kvf_ref[:, DIFF_WIDTH:] = v
    kvb_ref[:, DIFF_WIDTH:] = v.astype(BF16)


def diff_post(proj, cos, sin, qk_g):
    n = proj.shape[0]
    tm = min(256, n)
    g2 = jnp.tile(qk_g, (1, 2))
    so = _seg_ones()
    row = lambda w: pl.BlockSpec((tm, w), lambda i: (i, 0))
    const = lambda a: pl.BlockSpec(a.shape, lambda i: (0, 0))
    return pl.pallas_call(
        _diff_post_kernel,
        grid=(n // tm,),
        in_specs=[row(3 * DIFF_WIDTH), row(LANES), row(LANES), const(g2), const(so)],
        out_specs=[row(DIFF_WIDTH), row(2 * DIFF_WIDTH), row(2 * DIFF_WIDTH)],
        out_shape=[jax.ShapeDtypeStruct((n, DIFF_WIDTH), BF16), jax.ShapeDtypeStruct((n, 2 * DIFF_WIDTH), F32),
                   jax.ShapeDtypeStruct((n, 2 * DIFF_WIDTH), BF16)],
        compiler_params=_cparams("parallel"),
        name="diff_post",
    )(proj, cos, sin, g2, so)


def _diff_lambda(lam_ref, lam_init):
    lf = lam_ref[...]
    a = jnp.sum(lf[0:1, :] * lf[1:2, :], axis=-1, keepdims=True)
    b = jnp.sum(lf[2:3, :] * lf[3:4, :], axis=-1, keepdims=True)
    return jnp.exp(a) - jnp.exp(b) + lam_init


DIFF_TQ = 256
DIFF_TK = 512


def _diff_flash_kernel(lam_init, tk, q_ref, k_ref, v_ref, lam_ref, o_ref, m_ref, l_ref, acc_ref):
    i = pl.program_id(2)
    tq = q_ref.shape[1]
    rows = 2 * tq
    q = q_ref[0].astype(F32)
    comp = lax.broadcasted_iota(jnp.int32, q.shape, 1) >> 6
    qst = jnp.concatenate([jnp.where(comp == c, q, 0.0) for c in range(2)], axis=0).astype(BF16)
    _softmax_init(m_ref, l_ref, acc_ref)

    def tile(j, causal):
        lo = pl.multiple_of(j * tk, tk)
        s = _dot_nt(qst, k_ref[0, pl.ds(lo, tk), :])
        if causal:
            qpos = i * tq + (lax.broadcasted_iota(jnp.int32, (rows, tk), 0) & (tq - 1))
            kpos = j * tk + lax.broadcasted_iota(jnp.int32, (rows, tk), 1)
            s = jnp.where(kpos <= qpos, s, NEG_INF)
        v = v_ref[0, pl.ds(lo, tk), :]
        _softmax_update(s, lambda p: _dot(p, v), m_ref, l_ref, acc_ref)

    nfull = (i * tq) // tk

    def body(j, c):
        tile(j, False)
        return c

    lax.fori_loop(0, nfull, body, 0)
    tile(nfull, True)
    o = acc_ref[...] / l_ref[...]
    o_ref[0] = o[0:tq, :] - _diff_lambda(lam_ref, lam_init) * o[tq:rows, :]


def diff_flash_prompt(q, kvb, lam, lam_init):
    b, t, _ = q.shape
    tq = min(DIFF_TQ, t)
    tk = min(DIFF_TK, t)
    return pl.pallas_call(
        functools.partial(_diff_flash_kernel, lam_init, tk),
        grid=(b, DIFF_HEADS, t // tq),
        in_specs=[pl.BlockSpec((1, tq, LANES), lambda bi, h, i: (bi, i, h)),
                  pl.BlockSpec((1, t, LANES), lambda bi, h, i: (bi, 0, h)),
                  pl.BlockSpec((1, t, LANES), lambda bi, h, i: (bi, 0, DIFF_HEADS + h)),
                  pl.BlockSpec(lam.shape, lambda bi, h, i: (0, 0))],
        out_specs=pl.BlockSpec((1, tq, LANES), lambda bi, h, i: (bi, i, h)),
        out_shape=jax.ShapeDtypeStruct((b, t, DIFF_WIDTH), F32),
        scratch_shapes=[pltpu.VMEM((2 * tq, 1), F32), pltpu.VMEM((2 * tq, 1), F32), pltpu.VMEM((2 * tq, LANES), F32)],
        compiler_params=_cparams("parallel", "parallel", "arbitrary"),
        name="diff_flash",
    )(q, kvb, kvb, lam)


def _diff_out_kernel(lam_init, two, *refs):
    if two:
        o0_ref, o1_ref, lam_ref, g_ref, w_ref, r_ref, out_ref, h_ref = refs
        o = o0_ref[...] - _diff_lambda(lam_ref, lam_init) * o1_ref[...]
    else:
        o0_ref, g_ref, w_ref, r_ref, out_ref, h_ref = refs
        o = o0_ref[...]
    for s in range(DIFF_HEADS):
        sl = slice(s * LANES, (s + 1) * LANES)
        h_ref[:, sl] = (_rms_rows(o[:, sl], g_ref[...]) * (1.0 - lam_init)).astype(BF16)
    out_ref[...] = r_ref[...] + _dot(h_ref[...], w_ref[...])


def diff_out(o, sub_g, w_o, res, lam_init, o1=None, lam=None):
    n, d = res.shape
    tm = min(512, n)
    two = o1 is not None
    row = lambda w: pl.BlockSpec((tm, w), lambda i: (i, 0))
    const = lambda a: pl.BlockSpec(a.shape, lambda i: (0, 0))
    g = sub_g.reshape(1, LANES)
    ins = [o, o1, lam, g, w_o, res] if two else [o, g, w_o, res]
    specs = ([row(DIFF_WIDTH), row(DIFF_WIDTH), const(lam)] if two else [row(DIFF_WIDTH)]) + [const(g), const(w_o), row(d)]
    return pl.pallas_call(
        functools.partial(_diff_out_kernel, lam_init, two),
        grid=(n // tm,),
        in_specs=specs,
        out_specs=row(d),
        out_shape=jax.ShapeDtypeStruct((n, d), F32),
        scratch_shapes=[pltpu.VMEM((tm, DIFF_WIDTH), BF16)],
        compiler_params=_cparams("parallel"),
        name="diff_out",
    )(*ins)


def _gla_kernel(chunk, nchunk, t_valid, p_ref, s0_ref, wa_ref, ba_ref, g_ref, tri_ref, y_ref, sfin_ref, st_ref):
    step = pl.program_id(1)

    @pl.when(step == 0)
    def _():
        st_ref[...] = s0_ref[0]

    nk = GLA_HEADS * GLA_DK
    nv = GLA_HEADS * GLA_DV
    tri = tri_ref[...]
    causal = lax.broadcasted_iota(jnp.int32, (chunk, chunk), 0) >= lax.broadcasted_iota(jnp.int32, (chunk, chunk), 1)
    for ci in range(nchunk):
        r0 = ci * chunk
        rows = slice(r0, r0 + chunk)
        a1 = p_ref[0, rows, GLA_MAIN:GLA_MAIN + LANES].astype(BF16)
        z = _dot(a1, wa_ref[...]) + ba_ref[...]
        log_a = (jnp.minimum(z, 0.0) - jnp.log(1.0 + jnp.exp(-jnp.abs(z)))) * (1.0 / GLA_TAU)
        if t_valid < chunk:
            log_a = jnp.where(lax.broadcasted_iota(jnp.int32, log_a.shape, 0) < t_valid, log_a, 0.0)
        cum = _split_dot_left(tri, log_a)
        for h in range(GLA_HEADS):
            ksl = slice(h * GLA_DK, (h + 1) * GLA_DK)
            vsl = slice(h * GLA_DV, (h + 1) * GLA_DV)
            q = p_ref[0, rows, h * GLA_DK:(h + 1) * GLA_DK] * (GLA_DK ** -0.5)
            k = p_ref[0, rows, nk + h * GLA_DK:nk + (h + 1) * GLA_DK]
            v = p_ref[0, rows, 2 * nk + h * GLA_DV:2 * nk + (h + 1) * GLA_DV]
            r = p_ref[0, rows, 2 * nk + nv + h * GLA_DV:2 * nk + nv + (h + 1) * GLA_DV]
            cm = cum[:, ksl]
            last = cm[chunk - 1:chunk, :]
            qe = (q * jnp.exp(cm)).astype(BF16)
            ke = (k * jnp.exp(-cm)).astype(BF16)
            vb = v.astype(BF16)
            att = jnp.where(causal, _dot_nt(qe, ke), 0.0)
            st = st_ref[h]
            o = _dot_nt(qe, st.astype(BF16)) + _dot(att.astype(BF16), vb)
            kd = (k * jnp.exp(last - cm)).astype(BF16)
            st_ref[h] = st * jnp.exp(last) + _dot(v.T.astype(BF16), kd)
            y_ref[0, rows, vsl] = _rms_rows(o, g_ref[...]) * _silu(r)

    @pl.when(step == pl.num_programs(1) - 1)
    def _():
        sfin_ref[0] = st_ref[...]


def gla_core(proj, s0_t, w_a2, b_a, out_g, chunk, nchunk, t_valid):
    b, t, _ = proj.shape
    rows = chunk * nchunk
    wa = jnp.zeros((LANES, GLA_HEADS * GLA_DK), F32).at[:GLA_RANK].set(w_a2).astype(BF16)
    ba = b_a.reshape(1, -1)
    g = out_g.reshape(1, GLA_DV)
    tri = (jnp.arange(chunk)[:, None] >= jnp.arange(chunk)[None, :]).astype(BF16)
    const = lambda a: pl.BlockSpec(a.shape, lambda bi, s: (0,) * a.ndim)
    st_spec = pl.BlockSpec((1, GLA_HEADS, GLA_DV, GLA_DK), lambda bi, s: (bi, 0, 0, 0))
    return pl.pallas_call(
        functools.partial(_gla_kernel, chunk, nchunk, t_valid),
        grid=(b, t // rows),
        in_specs=[pl.BlockSpec((1, rows, GLA_IN_PAD), lambda bi, s: (bi, s, 0)), st_spec,
                  const(wa), const(ba), const(g), const(tri)],
        out_specs=[pl.BlockSpec((1, rows, GLA_HEADS * GLA_DV), lambda bi, s: (bi, s, 0)), st_spec],
        out_shape=[jax.ShapeDtypeStruct((b, t, GLA_HEADS * GLA_DV), F32),
                   jax.ShapeDtypeStruct((b, GLA_HEADS, GLA_DV, GLA_DK), F32)],
        scratch_shapes=[pltpu.VMEM((GLA_HEADS, GLA_DV, GLA_DK), F32)],
        compiler_params=_cparams("parallel", "arbitrary"),
        name="gla_core",
    )(proj, s0_t, wa, ba, g, tri)


def _pad_cols(w, width):
    return jnp.pad(w, ((0, 0), (0, width - w.shape[1]))).astype(BF16)


def _pad_axis(x, axis, size):
    pad = [(0, 0)] * x.ndim
    pad[axis] = (0, size - x.shape[axis])
    return jnp.pad(x, pad)


def _group_diag(q, t_seq):
    nseq = q.shape[0] // t_seq
    qg = q.reshape(nseq, t_seq, NSA_GROUPS, NSA_REP, HEAD_DIM).transpose(0, 2, 1, 3, 4)
    eye = jnp.eye(NSA_GROUPS, dtype=q.dtype)
    out = qg[:, :, :, :, None, :] * eye[None, :, None, None, :, None]
    return out.reshape(nseq, NSA_GROUPS * t_seq * NSA_REP, NSA_GROUPS * HEAD_DIM)


def _group_undiag(o, t_seq):
    nseq = o.shape[0]
    o6 = o.reshape(nseq, NSA_GROUPS, t_seq, NSA_REP, NSA_GROUPS, HEAD_DIM)
    od = jnp.stack([o6[:, g, :, :, g, :] for g in range(NSA_GROUPS)], axis=1)
    return od.transpose(0, 2, 1, 3, 4).reshape(nseq * t_seq, NSA_HEADS * HEAD_DIM)


def _seq_cols(x_t, nseq, t_seq):
    r = x_t.shape[0]
    return _pad_axis(x_t.reshape(r, nseq, t_seq).transpose(1, 0, 2), 2, PAGE_SIZE)


def _nsa_layer(s, xp, xs, seq_p, t_s, past_len, cache_t, state_win_t, page_table, norm_g, w_in, qk_g, pe, w_phi, w_o):
    bp = xp.shape[0] // seq_p
    bs = xs.shape[0] // t_s
    w_o_b = w_o.astype(BF16)
    npg = past_len // PAGE_SIZE

    qn, qr, gates, rows_t, win_t, ksel, kwin = nsa_project(xp, norm_g, w_in, qk_g, jnp.arange(seq_p), bp)
    own = _PageIndex(jnp.zeros((1, 1), jnp.int32), lambda b, p, pt: (b, 0, p))
    kc, vc = nsa_compress(rows_t, own, bp, seq_p // PAGE_SIZE, pe, w_phi, qk_g[3])
    qn3, qr3 = qn.reshape(bp, seq_p, 1024), qr.reshape(bp, seq_p, 1024)
    oc, sel = nsa_cmp_select(qn3, kc, vc, 0, seq_p // NSA_BLOCK)
    os_ = nsa_selected_prompt(qr3, sel, ksel)
    ow = nsa_window_prompt(qr3, kwin)
    xp_new = nsa_out(oc.reshape(-1, 1024), os_.reshape(-1, 1024), ow.reshape(-1, 1024), gates, w_o_b, xp)
    kv_p = rows_t.reshape(bp, 4, NSA_GROUPS, HEAD_DIM, seq_p).transpose(0, 4, 1, 2, 3)
    wlen = min(NSA_WINDOW, seq_p)
    win_p = win_t[:, :, seq_p - wlen:].reshape(bp, 2, NSA_GROUPS, HEAD_DIM, wlen).transpose(0, 4, 1, 2, 3)

    pos_s = past_len + jnp.arange(t_s)
    qn, qr, gates, rows_t, win_t, _, _ = nsa_project(xs, norm_g, w_in, qk_g, jnp.tile(pos_s, bs), 1)
    rows_t, win_t = rows_t[0], win_t[0]
    layer_pages = cache_t.shape[0] // (state_win_t.shape[0] // bs)
    table = page_table + s * layer_pages
    paged = _PageIndex(table, lambda b, p, pt: (pt[b, p], 0, 0))
    kc, vc = nsa_compress(cache_t, paged, bs, npg, pe, w_phi, qk_g[3])
    qn_pad = _pad_axis(qn.reshape(bs, t_s, 1024), 1, 128)
    ns = -(-(past_len + t_s) // NSA_BLOCK)
    oc, sel = nsa_cmp_select(qn_pad, kc, vc, past_len, 256)
    oc = oc[:, :t_s].reshape(bs * t_s, 1024)
    nrow = NSA_GROUPS * t_s * NSA_REP
    sel_f = sel[:, :, :t_s, :ns].astype(F32)
    by_row = lambda a: jnp.broadcast_to(a[:, :, :, None, :], a.shape[:3] + (NSA_REP, a.shape[-1])).reshape(
        a.shape[0], nrow, a.shape[-1])
    past_bias = by_row(jnp.repeat(sel_f[..., :past_len // NSA_BLOCK], NSA_BLOCK, axis=-1))
    new_ok = jnp.arange(PAGE_SIZE)[None, :] <= jnp.arange(t_s)[:, None]
    new_bias = by_row(jnp.where(new_ok[None, None], sel_f[..., past_len // NSA_BLOCK][..., None], NEG_INF))
    q_bd = _group_diag(qr, t_s)
    npp = PAGES_PER_STEP
    kpage = lambda blk: (lambda u: pl.BlockSpec((None, NSA_GW, PAGE_SIZE),
                                                lambda b, p, pt: (pt[b, jnp.minimum(p * npp + u, npg - 1)], blk, 0)))
    k_new = _seq_cols(rows_t[2 * NSA_GW:3 * NSA_GW], bs, t_s).astype(BF16)
    v_new = _seq_cols(rows_t[3 * NSA_GW:4 * NSA_GW], bs, t_s).astype(BF16)
    os_ = _group_undiag(paged_attention(q_bd, cache_t, table, npg, kpage(2), kpage(3), True, k_new, v_new,
                                        past_bias, new_bias, True), t_s)
    wbuf = state_win_t.shape[-1]
    wpg = wbuf // PAGE_SIZE
    wpage = lambda blk: (lambda u: pl.BlockSpec((None, NSA_GW, PAGE_SIZE),
                                                lambda b, p, pt: (s * bs + b, blk, jnp.minimum(p * npp + u, wpg - 1))))
    kidx = jnp.arange(wbuf + PAGE_SIZE)
    wpos = jnp.where(kidx < wbuf, past_len - wbuf + kidx, past_len + kidx - wbuf)
    w_ok = ((wpos[None, :] <= pos_s[:, None]) & (wpos[None, :] > pos_s[:, None] - NSA_WINDOW) & (wpos[None, :] >= 0)
            & (kidx[None, :] < wbuf + t_s))
    wbias = by_row(jnp.broadcast_to(jnp.where(w_ok, 0.0, NEG_INF).astype(F32)[None, None],
                                    (1, NSA_GROUPS, t_s, wbuf + PAGE_SIZE)))
    kw_new = _seq_cols(win_t[0:NSA_GW], bs, t_s).astype(BF16)
    vw_new = _seq_cols(win_t[NSA_GW:2 * NSA_GW], bs, t_s).astype(BF16)
    ow = _group_undiag(paged_attention(q_bd, state_win_t, jnp.zeros((1, 1), jnp.int32), wpg, wpage(0), wpage(1), True,
                                       kw_new, vw_new, wbias[:, :, :wbuf], wbias[:, :, wbuf:], True), t_s)
    xs_new = nsa_out(oc, os_, ow, gates, w_o_b, xs)
    kv_s = rows_t.reshape(4, NSA_GROUPS, HEAD_DIM, bs, t_s).transpose(3, 4, 0, 1, 2)
    win_new = win_t.reshape(2 * NSA_GW, bs, t_s).transpose(1, 0, 2)
    win_all = jnp.concatenate([state_win_t[s * bs:(s + 1) * bs], win_new], axis=2)
    wlen = min(NSA_WINDOW, win_all.shape[2])
    win_s = win_all[:, :, win_all.shape[2] - wlen:].reshape(bs, 2, NSA_GROUPS, HEAD_DIM, wlen).transpose(0, 4, 1, 2, 3)
    return xp_new, xs_new, kv_p, kv_s, win_p, win_s


def _diff_layer(layer, xp, xs, seq_p, t_s, past_len, cache, page_table, norm_g, w_in, qk_g, lam, sub_g, w_o):
    bp = xp.shape[0] // seq_p
    bs = xs.shape[0] // t_s
    npg = past_len // PAGE_SIZE
    lam_init = 0.8 - 0.6 * math.exp(-0.3 * layer)
    w_in_b = w_in.astype(BF16)
    w_o_b = w_o.astype(BF16)
    cos_p, sin_p = _rope_tables(jnp.arange(seq_p))
    proj = norm_matmul(xp, norm_g, w_in_b, 768)
    q, kvf, kvb = diff_post(proj, jnp.tile(cos_p, (bp, 1)), jnp.tile(sin_p, (bp, 1)), qk_g)
    o = diff_flash_prompt(q.reshape(bp, seq_p, DIFF_WIDTH), kvb.reshape(bp, seq_p, 2 * DIFF_WIDTH), lam, lam_init)
    xp_new = diff_out(o.reshape(-1, DIFF_WIDTH), sub_g, w_o_b, xp, lam_init)
    kv_p = kvf.reshape(bp, seq_p, 2, DIFF_HEADS, 2 * HEAD_DIM)
    pos_s = past_len + jnp.arange(t_s)
    cos_s, sin_s = _rope_tables(pos_s)
    proj = norm_matmul(xs, norm_g, w_in_b, 768)
    q, kvf, kvb = diff_post(proj, jnp.tile(cos_s, (bs, 1)), jnp.tile(sin_s, (bs, 1)), qk_g)
    nrow = DIFF_HEADS * 2 * t_s
    q5 = q.reshape(bs, t_s, DIFF_HEADS, 2, HEAD_DIM).transpose(0, 2, 3, 1, 4)
    q_rows = (q5[:, :, :, :, None, :] * jnp.eye(2, dtype=q.dtype)[None, None, :, None, :, None]).reshape(
        bs, nrow, 2 * HEAD_DIM)
    kv5 = kvb.reshape(bs, t_s, 2, DIFF_HEADS, 2 * HEAD_DIM)
    k_new = _pad_axis(kv5[:, :, 0], 1, PAGE_SIZE).reshape(bs, PAGE_SIZE * DIFF_HEADS, 2 * HEAD_DIM)
    v_new = _pad_axis(kv5[:, :, 1], 1, PAGE_SIZE).reshape(bs, PAGE_SIZE * DIFF_HEADS, 2 * HEAD_DIM)
    npp = PAGES_PER_STEP
    row_h = jnp.arange(nrow) // (2 * t_s)
    row_t = jnp.arange(nrow) % t_s
    slot_h = jnp.arange(PAGE_SIZE * DIFF_HEADS) % DIFF_HEADS
    slot_tok = jnp.arange(PAGE_SIZE * DIFF_HEADS) // DIFF_HEADS
    same_head = row_h[:, None] == slot_h[None, :]
    bias_page = jnp.where(same_head, 0.0, NEG_INF).astype(F32)
    bias_past = jnp.tile(bias_page, (1, npp))[None]
    bias_new = jnp.where(same_head & (slot_tok[None, :] <= row_t[:, None]), 0.0, NEG_INF).astype(F32)[None]
    page = lambda slot: (lambda u: pl.BlockSpec(
        (None, PAGE_SIZE, None, DIFF_HEADS, 2 * HEAD_DIM),
        lambda b, p, pt: (pt[b, jnp.minimum(p * npp + u, npg - 1)], 0, slot, 0, 0)))
    o = paged_attention(q_rows, cache, page_table, npg, page(0), page(1), False, k_new, v_new,
                        bias_past, bias_new, False)
    od = o.reshape(bs, DIFF_HEADS, 2, t_s, 2 * HEAD_DIM).transpose(2, 0, 3, 1, 4).reshape(2, bs * t_s, DIFF_WIDTH)
    xs_new = diff_out(od[0], sub_g, w_o_b, xs, lam_init, o1=od[1], lam=lam)
    kv_s = kvf.reshape(bs, t_s, 2, DIFF_HEADS, 2 * HEAD_DIM)
    return xp_new, xs_new, kv_p, kv_s


def _gla_layer(xp, xs, seq_p, t_s, state, norm_g, w_in, w_a2, b_a, out_g, w_o):
    bp = xp.shape[0] // seq_p
    bs = xs.shape[0] // t_s
    w_in_b = _pad_cols(w_in, GLA_IN_PAD)
    w_o_b = w_o.astype(BF16)
    proj = norm_matmul(xp, norm_g, w_in_b, 640).reshape(bp, seq_p, GLA_IN_PAD)
    chunk = min(GLA_CHUNK, seq_p)
    s0 = jnp.zeros((bp, GLA_HEADS, GLA_DV, GLA_DK), F32)
    y, st = gla_core(proj, s0, w_a2, b_a, out_g, chunk, 4 if seq_p % (4 * chunk) == 0 else 1, chunk)
    xp_new = matmul_residual(y.reshape(-1, GLA_HEADS * GLA_DV), w_o_b, xp)
    st_p = st.transpose(0, 1, 3, 2)
    chunk_s = GLA_CHUNK
    proj = norm_matmul(xs, norm_g, w_in_b, 640).reshape(bs, t_s, GLA_IN_PAD)
    proj = _pad_axis(proj, 1, chunk_s)
    y, st = gla_core(proj, state.transpose(0, 1, 3, 2), w_a2, b_a, out_g, chunk_s, 1, t_s)
    xs_new = matmul_residual(y[:, :t_s].reshape(-1, GLA_HEADS * GLA_DV), w_o_b, xs)
    st_s = st.transpose(0, 1, 3, 2)
    return xp_new, xs_new, st_p, st_s


def kernel(x_prompt, x_sample, cache_nsa_kv, state_nsa_win, cache_diff_kv, state_gla, state_ffn, page_table, norm_g, ffn_w_up, ffn_conv_w, ffn_conv_b, ffn_w_down, nsa_w_in, nsa_qk_g, nsa_pe, nsa_w_phi, nsa_w_o, diff_w_in, diff_qk_g, diff_lam, diff_sub_g, diff_w_o, gla_w_in, gla_w_a2, gla_b_a, gla_out_g, gla_w_o):
    bp, seq_p, d = x_prompt.shape
    bs, t_s, _ = x_sample.shape
    past_len = page_table.shape[1] * PAGE_SIZE
    xp = x_prompt.reshape(bp * seq_p, d)
    xs = x_sample.reshape(bs * t_s, d)
    cache_t = cache_nsa_kv.transpose(0, 1, 3, 4, 5, 2).reshape(-1, 4 * NSA_GW, PAGE_SIZE)
    win_t = state_nsa_win.transpose(0, 1, 3, 4, 5, 2).reshape(-1, 2 * NSA_GW, state_nsa_win.shape[2])
    nsa_kv_p, nsa_kv_s, nsa_win_p, nsa_win_s = [], [], [], []
    diff_kv_p, diff_kv_s, gla_p, gla_s, ffn_p, ffn_s = [], [], [], [], [], []
    for i in range(DEPTH):
        kind, s = i % N_MIXERS, i // N_MIXERS
        if kind == 0:
            xp, xs, kvp, kvs, wp, ws = _nsa_layer(
                s, xp, xs, seq_p, t_s, past_len, cache_t, win_t, page_table,
                norm_g[i, 0], nsa_w_in[s], nsa_qk_g[s], nsa_pe[s], nsa_w_phi[s], nsa_w_o[s])
            nsa_kv_p.append(kvp); nsa_kv_s.append(kvs); nsa_win_p.append(wp); nsa_win_s.append(ws)
        elif kind == 1:
            xp, xs, kvp, kvs = _diff_layer(
                i, xp, xs, seq_p, t_s, past_len, cache_diff_kv[s], page_table, norm_g[i, 0], diff_w_in[s],
                diff_qk_g[s], diff_lam[s], diff_sub_g[s], diff_w_o[s])
            diff_kv_p.append(kvp); diff_kv_s.append(kvs)
        else:
            xp, xs, stp, sts = _gla_layer(xp, xs, seq_p, t_s, state_gla[s], norm_g[i, 0], gla_w_in[s], gla_w_a2[s],
                                          gla_b_a[s], gla_out_g[s], gla_w_o[s])
            gla_p.append(stp); gla_s.append(sts)
        w_up_b = ffn_w_up[i].astype(BF16)
        w_dn_b = ffn_w_down[i].astype(BF16)
        xp, tail_p = ffn_prompt(xp, norm_g[i, 1], w_up_b, ffn_conv_w[i], ffn_conv_b[i], w_dn_b, seq_p)
        xs, tail_s = ffn_sample(xs, norm_g[i, 1], w_up_b, ffn_conv_w[i], ffn_conv_b[i], w_dn_b, state_ffn[i], t_s)
        ffn_p.append(tail_p); ffn_s.append(tail_s)
    return (xp.reshape(bp, seq_p, d), xs.reshape(bs, t_s, d),
            jnp.stack(nsa_kv_p), jnp.stack(nsa_kv_s), jnp.stack(nsa_win_p), jnp.stack(nsa_win_s),
            jnp.stack(diff_kv_p), jnp.stack(diff_kv_s), jnp.stack(gla_p), jnp.stack(gla_s),
            jnp.stack(ffn_p), jnp.stack(ffn_s))
```

```python
import functools
import math

import jax
import jax.numpy as jnp
from jax import lax
from jax.experimental import pallas as pl
from jax.experimental.pallas import tpu as pltpu

F32 = jnp.float32
BF16 = jnp.bfloat16

D_MODEL = 1024
HEAD_DIM = 64
ROPE_THETA = 10000.0
NORM_EPS = 1e-6
NEG_INF = -1e30
DEPTH = 4
N_MIXERS = 3
PAGE_SIZE = 128
NSA_HEADS = 16
NSA_GROUPS = 4
NSA_REP = 4
NSA_BLOCK = 64
NSA_TOPN = 16
NSA_WINDOW = 512
NSA_FORCED = 1e9
NSA_NQ = NSA_HEADS * HEAD_DIM
NSA_NKV = 6 * NSA_GROUPS * HEAD_DIM
NSA_NGATE = 3 * NSA_HEADS
NSA_GW = NSA_GROUPS * HEAD_DIM
DIFF_HEADS = 8
DIFF_WIDTH = 1024
GLA_HEADS = 4
GLA_DK = 128
GLA_DV = 256
GLA_RANK = 16
GLA_TAU = 16.0
GLA_CHUNK = 64
GLA_MAIN = 2 * GLA_HEADS * GLA_DK + 2 * GLA_HEADS * GLA_DV
GLA_IN_PAD = GLA_MAIN + 128
D_FF = 2816
CONV_W = 3
LANES = 128
VMEM_LIMIT = 56 * 1024 * 1024
SEL_TK = 512
PAGES_PER_STEP = 4
CMP_PITCH = HEAD_DIM + 8


def _cparams(*sem):
    return pltpu.CompilerParams(dimension_semantics=sem, vmem_limit_bytes=VMEM_LIMIT)


def _dot(a, b):
    return jnp.dot(a, b, preferred_element_type=F32)


def _dot_nt(a, b):
    return lax.dot_general(a, b, (((1,), (1,)), ((), ())), preferred_element_type=F32)


def _split_dot(x, m):
    hi = x.astype(BF16)
    lo = (x - hi.astype(F32)).astype(BF16)
    return _dot(hi, m) + _dot(lo, m)


def _split_dot_left(m, x):
    hi = x.astype(BF16)
    lo = (x - hi.astype(F32)).astype(BF16)
    return _dot(m, hi) + _dot(m, lo)


def _rms_rows(x, g):
    ms = jnp.mean(x * x, axis=-1, keepdims=True)
    return x * lax.rsqrt(ms + NORM_EPS) * g


def _seg64_norm(x, g, seg_ones):
    ms = _split_dot(x * x, seg_ones) * (1.0 / HEAD_DIM)
    return x * lax.rsqrt(ms + NORM_EPS) * g


def _rope_slab(x, cos, sin_signed):
    lane = lax.broadcasted_iota(jnp.int32, x.shape, 1)
    first = (lane & 63) < 32
    partner = jnp.where(first, pltpu.roll(x, 96, 1), pltpu.roll(x, 32, 1))
    return x * cos + partner * sin_signed


def _rope_angles(pos):
    half = HEAD_DIM // 2
    inv = ROPE_THETA ** (-jnp.arange(half, dtype=F32) / half)
    ang = pos.astype(F32)[:, None] * inv
    return jnp.cos(ang), jnp.sin(ang)


def _rope_tables(pos):
    cos, sin = _rope_angles(pos)
    return jnp.tile(cos, (1, 4)), jnp.tile(jnp.concatenate([-sin, sin], axis=1), (1, 2))


def _seg_ones():
    i = jnp.arange(LANES)
    return (i[:, None] // HEAD_DIM == i[None, :] // HEAD_DIM).astype(BF16)


def _lanes(x, width):
    return x if width == LANES else jnp.tile(x, (1, width // LANES))


def _softmax_update(s, v_dot, m_ref, l_ref, acc_ref):
    m = m_ref[...]
    m_new = jnp.maximum(m, jnp.max(s, axis=-1, keepdims=True))
    alpha = jnp.exp(m - m_new)
    p = jnp.exp(s - _lanes(m_new, s.shape[1]))
    m_ref[...] = m_new
    l_ref[...] = alpha * l_ref[...] + jnp.sum(p, axis=-1, keepdims=True)
    acc_ref[...] = _lanes(alpha, acc_ref.shape[-1]) * acc_ref[...] + v_dot(p.astype(BF16))


def _softmax_result(l_ref, acc_ref):
    return acc_ref[...] / _lanes(l_ref[...], acc_ref.shape[-1])


def _softmax_init(m_ref, l_ref, acc_ref):
    m_ref[...] = jnp.full_like(m_ref, NEG_INF)
    l_ref[...] = jnp.zeros_like(l_ref)
    acc_ref[...] = jnp.zeros_like(acc_ref)


def _norm_matmul_kernel(x_ref, g_ref, w_ref, o_ref, h_ref):
    @pl.when(pl.program_id(1) == 0)
    def _():
        h_ref[...] = _rms_rows(x_ref[...], g_ref[...]).astype(BF16)

    o_ref[...] = _dot(h_ref[...], w_ref[...])


def norm_matmul(x, g, w, tn):
    n, d = x.shape
    nout = w.shape[1]
    tm = min(512, n)
    return pl.pallas_call(
        _norm_matmul_kernel,
        grid=(n // tm, nout // tn),
        in_specs=[pl.BlockSpec((tm, d), lambda i, j: (i, 0)),
                  pl.BlockSpec((1, d), lambda i, j: (0, 0)),
                  pl.BlockSpec((d, tn), lambda i, j: (0, j))],
        out_specs=pl.BlockSpec((tm, tn), lambda i, j: (i, j)),
        out_shape=jax.ShapeDtypeStruct((n, nout), F32),
        scratch_shapes=[pltpu.VMEM((tm, d), BF16)],
        compiler_params=_cparams("parallel", "arbitrary"),
        name="norm_matmul",
    )(x, g.reshape(1, d), w)


def _matmul_res_kernel(a_ref, w_ref, r_ref, o_ref):
    o_ref[...] = r_ref[...] + _dot(a_ref[...].astype(BF16), w_ref[...])


def matmul_residual(a, w, res):
    n, k = a.shape
    d = w.shape[1]
    tm = min(512, n)
    return pl.pallas_call(
        _matmul_res_kernel,
        grid=(n // tm,),
        in_specs=[pl.BlockSpec((tm, k), lambda i: (i, 0)),
                  pl.BlockSpec((k, d), lambda i: (0, 0)),
                  pl.BlockSpec((tm, d), lambda i: (i, 0))],
        out_specs=pl.BlockSpec((tm, d), lambda i: (i, 0)),
        out_shape=jax.ShapeDtypeStruct((n, d), F32),
        compiler_params=_cparams("parallel"),
        name="matmul_residual",
    )(a, w, res)


FFN_CHUNK = 256


def _silu(x):
    return x / (1.0 + jnp.exp(-x))


def _causal_conv(u, prev1, prev2, use1, use2, cw_ref, cb_ref):
    u1 = jnp.where(use1, prev1, pltpu.roll(u, 1, 0))
    u2 = jnp.where(use2, prev2, pltpu.roll(u, 2, 0))
    return cb_ref[...] + cw_ref[0:1, :] * u2 + cw_ref[1:2, :] * u1 + cw_ref[2:3, :] * u


def _ffn_prompt_kernel(tiles_per_seq, x_ref, g_ref, wg_ref, wu_ref, cwg_ref, cwu_ref, cbg_ref, cbu_ref,
                       wd_ref, o_ref, tg_ref, tu_ref, cg_ref, cu_ref):
    i = pl.program_id(0)
    tm = x_ref.shape[0]
    nj, _, c = wg_ref.shape
    x = x_ref[...]
    h = _rms_rows(x, g_ref[...]).astype(BF16)
    row = lax.broadcasted_iota(jnp.int32, (tm, c), 0)

    @pl.when((i % tiles_per_seq) == 0)
    def _():
        cg_ref[...] = jnp.zeros_like(cg_ref)
        cu_ref[...] = jnp.zeros_like(cu_ref)

    def conv(u, prev, cw_ref, cb_ref):
        prev2 = jnp.where(row == 0, prev[0:1, :], prev[1:2, :])
        return _causal_conv(u, prev[1:2, :], prev2, row == 0, row < 2, cw_ref, cb_ref)

    acc = x
    for j in range(nj):
        ug = _dot(h, wg_ref[j])
        uu = _dot(h, wu_ref[j])
        act = _silu(conv(ug, cg_ref[j], cwg_ref.at[j], cbg_ref.at[j])) * conv(uu, cu_ref[j], cwu_ref.at[j], cbu_ref.at[j])
        acc = acc + _dot(act.astype(BF16), wd_ref[j])
        cg_ref[j] = ug[tm - 2:tm, :]
        cu_ref[j] = uu[tm - 2:tm, :]
        tg_ref[0, :, j * c:(j + 1) * c] = ug[tm - 2:tm, :]
        tu_ref[0, :, j * c:(j + 1) * c] = uu[tm - 2:tm, :]
    o_ref[...] = acc


def _resident(a):
    return pl.BlockSpec(a.shape, lambda i: (0,) * a.ndim, pipeline_mode=pl.Buffered(1))


def ffn_prompt(x, g, w_up, conv_w, conv_b, w_down, seq_len):
    n, d = x.shape
    c = FFN_CHUNK
    nj = D_FF // c
    tm = min(512, seq_len)
    tps = seq_len // tm
    chunks = lambda a: a.reshape(a.shape[0], 2, nj, c).transpose(1, 2, 0, 3)
    wg, wu = chunks(w_up)
    cwg, cwu = chunks(conv_w)
    cbg, cbu = chunks(conv_b.reshape(1, 2 * D_FF))
    wd = w_down.reshape(nj, c, d)
    g2 = g.reshape(1, d)
    consts = [g2, wg, wu, cwg, cwu, cbg, cbu, wd]
    out, tg, tu = pl.pallas_call(
        functools.partial(_ffn_prompt_kernel, tps),
        grid=(n // tm,),
        in_specs=[pl.BlockSpec((tm, d), lambda i: (i, 0))] + [_resident(a) for a in consts],
        out_specs=[pl.BlockSpec((tm, d), lambda i: (i, 0)),
                   pl.BlockSpec((1, 2, D_FF), lambda i: (i, 0, 0)),
                   pl.BlockSpec((1, 2, D_FF), lambda i: (i, 0, 0))],
        out_shape=[jax.ShapeDtypeStruct((n, d), F32),
                   jax.ShapeDtypeStruct((n // tm, 2, D_FF), F32),
                   jax.ShapeDtypeStruct((n // tm, 2, D_FF), F32)],
        scratch_shapes=[pltpu.VMEM((nj, 2, c), F32),
                        pltpu.VMEM((nj, 2, c), F32)],
        compiler_params=_cparams("arbitrary"),
        name="ffn_prompt",
    )(x, *consts)
    return out, jnp.concatenate([tg, tu], axis=-1)[tps - 1::tps]


def _ffn_sample_kernel(t_seq, x_ref, g_ref, wg_ref, wu_ref, cwg_ref, cwu_ref, cbg_ref, cbu_ref, wd_ref,
                       p1g_ref, p2g_ref, p1u_ref, p2u_ref, o_ref, ug_ref, uu_ref, h_ref):
    j = pl.program_id(0)

    @pl.when(j == 0)
    def _():
        h_ref[...] = _rms_rows(x_ref[...], g_ref[...]).astype(BF16)

    h = h_ref[...]
    ug = _dot(h, wg_ref[...])
    uu = _dot(h, wu_ref[...])
    t = lax.broadcasted_iota(jnp.int32, ug.shape, 0) & (t_seq - 1)
    cg = _causal_conv(ug, p1g_ref[...], p2g_ref[...], t == 0, t < 2, cwg_ref, cbg_ref)
    cu = _causal_conv(uu, p1u_ref[...], p2u_ref[...], t == 0, t < 2, cwu_ref, cbu_ref)
    part = _dot((_silu(cg) * cu).astype(BF16), wd_ref[...])
    ug_ref[...] = ug
    uu_ref[...] = uu

    @pl.when(j == 0)
    def _():
        o_ref[...] = x_ref[...] + part

    @pl.when(j > 0)
    def _():
        o_ref[...] += part


def ffn_sample(x, g, w_up, conv_w, conv_b, w_down, buf, t_seq):
    n, d = x.shape
    nseq = n // t_seq
    c = FFN_CHUNK
    nj = D_FF // c
    cb = conv_b.reshape(1, 2 * D_FF)
    reps = t_seq // 2
    prev1 = jnp.concatenate([buf[:, ::-1]] * reps, axis=1).reshape(n, 2 * D_FF)
    prev2 = jnp.concatenate([buf] * reps, axis=1).reshape(n, 2 * D_FF)
    full = lambda blk, off=0: pl.BlockSpec(blk, lambda j: (0, j + off))
    out, ug, uu = pl.pallas_call(
        functools.partial(_ffn_sample_kernel, t_seq),
        grid=(nj,),
        in_specs=[pl.BlockSpec((n, d), lambda j: (0, 0)),
                  pl.BlockSpec((1, d), lambda j: (0, 0)),
                  full((d, c)), full((d, c), nj),
                  full((CONV_W, c)), full((CONV_W, c), nj),
                  full((1, c)), full((1, c), nj),
                  pl.BlockSpec((c, d), lambda j: (j, 0)),
                  full((n, c)), full((n, c)), full((n, c), nj), full((n, c), nj)],
        out_specs=[pl.BlockSpec((n, d), lambda j: (0, 0)),
                   full((n, c)), full((n, c))],
        out_shape=[jax.ShapeDtypeStruct((n, d), F32),
                   jax.ShapeDtypeStruct((n, D_FF), F32),
                   jax.ShapeDtypeStruct((n, D_FF), F32)],
        scratch_shapes=[pltpu.VMEM((n, d), BF16)],
        compiler_params=_cparams("arbitrary"),
        name="ffn_sample",
    )(x, g.reshape(1, d), w_up, w_up, conv_w, conv_w, cb, cb, w_down, prev1, prev2, prev1, prev2)
    u = jnp.concatenate([ug, uu], axis=-1).reshape(nseq, t_seq, 2 * D_FF)
    return out, u[:, t_seq - (CONV_W - 1):]


def _head_norm_t(x, g):
    ms = jnp.mean(x * x, axis=0, keepdims=True)
    return x * lax.rsqrt(ms + NORM_EPS) * g


def _rope_t(x, cos, sin):
    half = HEAD_DIM // 2
    x1, x2 = x[0:half, :], x[half:HEAD_DIM, :]
    return jnp.concatenate([x1 * cos - x2 * sin, x1 * sin + x2 * cos], axis=0)


def _nsa_project_kernel(x_ref, g_ref, wq_ref, wkv_ref, wg_ref, cos_ref, sin_ref, cost_ref, sint_ref, gq_ref, gk_ref,
                        so_ref, qn_ref, qr_ref, gate_ref, rows_ref, win_ref, ksel_ref, kwin_ref):
    hn = _rms_rows(x_ref[...], g_ref[...]).astype(BF16)
    cos, sin = cos_ref[...], sin_ref[...]
    so = so_ref[...]
    scale = HEAD_DIM ** -0.5
    q_all = _dot(hn, wq_ref[...])
    for s in range(NSA_NQ // LANES):
        sl = slice(s * LANES, (s + 1) * LANES)
        q = _seg64_norm(q_all[:, sl], gq_ref[...], so)
        qn_ref[:, sl] = (q * scale).astype(BF16)
        qr_ref[:, sl] = (_rope_slab(q, cos, sin) * scale).astype(BF16)
    gate_ref[...] = 1.0 / (1.0 + jnp.exp(-_dot(hn, wg_ref[...])))
    kvt = _dot_nt(wkv_ref[...], hn)
    gw = NSA_GW
    cost, sint = cost_ref[...], sint_ref[...]
    rows_ref[0, 0:2 * gw, :] = kvt[0:2 * gw, :]
    rows_ref[0, 3 * gw:4 * gw, :] = kvt[3 * gw:4 * gw, :]
    win_ref[0, gw:2 * gw, :] = kvt[5 * gw:6 * gw, :]
    for g in range(NSA_GROUPS):
        hs = slice(g * HEAD_DIM, (g + 1) * HEAD_DIM)
        rows_ref[0, 2 * gw + g * HEAD_DIM:2 * gw + (g + 1) * HEAD_DIM, :] = _rope_t(
            _head_norm_t(kvt[2 * gw + g * HEAD_DIM:2 * gw + (g + 1) * HEAD_DIM, :], gk_ref[0]), cost, sint)
        win_ref[0, hs, :] = _rope_t(
            _head_norm_t(kvt[4 * gw + g * HEAD_DIM:4 * gw + (g + 1) * HEAD_DIM, :], gk_ref[1]), cost, sint)
    ksel_ref[0, 0] = rows_ref[0, 2 * gw:4 * gw, :].astype(BF16)
    wb = win_ref[0].astype(BF16)
    for u in range(kwin_ref.shape[1]):
        kwin_ref[0, u] = wb[:, u * LANES:(u + 1) * LANES]


def nsa_project(x, norm_g, w_in, qk_g, pos, nseq):
    n, d = x.shape
    t = n // nseq
    tm = min(SEL_TK, t)
    nt = t // tm
    wq = w_in[:, :NSA_NQ].astype(BF16)
    wkv_t = w_in[:, NSA_NQ:NSA_NQ + NSA_NKV].T.astype(BF16)
    wg = jnp.pad(w_in[:, NSA_NQ + NSA_NKV:], ((0, 0), (0, LANES - NSA_NGATE))).astype(BF16)
    cos, sin = _rope_angles(pos)
    cos_q, sin_q = jnp.tile(cos, (1, 4)), jnp.tile(jnp.concatenate([-sin, sin], axis=1), (1, 2))
    gq = jnp.tile(qk_g[0:1], (1, 2))
    gk = qk_g[1:3].reshape(2, HEAD_DIM, 1)
    so = _seg_ones()
    const = lambda a: pl.BlockSpec(a.shape, lambda b, i: (0,) * a.ndim)
    row = lambda w: pl.BlockSpec((tm, w), lambda b, i: (b * nt + i, 0))
    return pl.pallas_call(
        _nsa_project_kernel,
        grid=(nseq, nt),
        in_specs=[row(d), const(norm_g.reshape(1, d)), const(wq), const(wkv_t), const(wg),
                  pl.BlockSpec((tm, LANES), lambda b, i: (i, 0)), pl.BlockSpec((tm, LANES), lambda b, i: (i, 0)),
                  pl.BlockSpec((HEAD_DIM // 2, tm), lambda b, i: (0, i)),
                  pl.BlockSpec((HEAD_DIM // 2, tm), lambda b, i: (0, i)),
                  const(gq), const(gk), const(so)],
        out_specs=[row(NSA_NQ), row(NSA_NQ), row(LANES),
                   pl.BlockSpec((1, 4 * NSA_GW, tm), lambda b, i: (b, 0, i)),
                   pl.BlockSpec((1, 2 * NSA_GW, tm), lambda b, i: (b, 0, i)),
                   pl.BlockSpec((1, 1, 2 * NSA_GW, tm), lambda b, i: (b, i, 0, 0)),
                   pl.BlockSpec((1, tm // LANES, 2 * NSA_GW, LANES), lambda b, i: (b, i, 0, 0))],
        out_shape=[jax.ShapeDtypeStruct((n, NSA_NQ), BF16), jax.ShapeDtypeStruct((n, NSA_NQ), BF16),
                   jax.ShapeDtypeStruct((n, LANES), F32),
                   jax.ShapeDtypeStruct((nseq, 4 * NSA_GW, t), F32),
                   jax.ShapeDtypeStruct((nseq, 2 * NSA_GW, t), F32),
                   jax.ShapeDtypeStruct((nseq, nt, 2 * NSA_GW, tm), BF16),
                   jax.ShapeDtypeStruct((nseq, t // LANES, 2 * NSA_GW, LANES), BF16)],
        compiler_params=_cparams("parallel", "parallel"),
        name="nsa_project",
    )(x, norm_g.reshape(1, d), wq, wkv_t, wg, cos_q, sin_q, cos.T, sin.T, gq, gk, so)


def _nsa_compress_kernel(npp, pt_ref, *refs):
    page_refs = refs[:npp]
    pe_ref, w_ref, g_ref, so_ref, kc_ref, vc_ref, seqk_ref, seqv_ref, acc_ref = refs[npp:]
    p = pl.program_id(1)
    gw = NSA_GW
    pitch = CMP_PITCH
    for u in range(npp):
        for g in range(NSA_GROUPS):
            lo = pl.multiple_of(((p * npp + u) * NSA_GROUPS + g) * pitch, 8)
            seqk_ref[pl.ds(lo, HEAD_DIM), :] = page_refs[u][g * HEAD_DIM:(g + 1) * HEAD_DIM, :]
            seqv_ref[pl.ds(lo, HEAD_DIM), :] = page_refs[u][gw + g * HEAD_DIM:gw + (g + 1) * HEAD_DIM, :]

    @pl.when(p == pl.num_programs(1) - 1)
    def _():
        nrow = seqk_ref.shape[0] // pitch
        acc_ref[...] = jnp.zeros_like(acc_ref)
        for dd in range(HEAD_DIM):
            a = jnp.concatenate([seqk_ref[pl.ds(dd, nrow, stride=pitch), :],
                                 seqv_ref[pl.ds(dd, nrow, stride=pitch), :]], axis=1) + pe_ref[dd:dd + 1, :]
            acc_ref[...] += _dot(a.astype(BF16), w_ref[dd])
        kc_ref[0] = _seg64_norm(acc_ref[:, 0:LANES], g_ref[...], so_ref[...])
        vc_ref[0] = acc_ref[:, LANES:2 * LANES]


def nsa_compress(pages_t, page_index, nseq, npg, pe, w_phi, g_c):
    npp = PAGES_PER_STEP
    nrow = npg * NSA_GROUPS
    pe_t = jnp.concatenate([jnp.tile(pe[0].T, (1, 2)), jnp.tile(pe[1].T, (1, 2))], axis=1)
    eye4 = jnp.eye(4, dtype=F32)
    w4 = jnp.stack([w_phi[0], w_phi[0], w_phi[1], w_phi[1]])
    w = jnp.einsum('ab,alde->dalbe', eye4, w4).reshape(HEAD_DIM, 2 * LANES, 2 * LANES).astype(BF16)
    g2 = jnp.tile(g_c.reshape(1, HEAD_DIM), (1, 2))
    so = _seg_ones()
    const = lambda a: pl.BlockSpec(a.shape, lambda b, p, pt: (0,) * a.ndim)
    page = lambda u: pl.BlockSpec((None, 2 * NSA_GW, PAGE_SIZE), lambda b, p, pt: page_index(b, p * npp + u, pt))
    out = pl.BlockSpec((1, nrow, LANES), lambda b, p, pt: (b, 0, 0))
    table = page_index.table
    kc, vc = pl.pallas_call(
        functools.partial(_nsa_compress_kernel, npp),
        grid_spec=pltpu.PrefetchScalarGridSpec(
            num_scalar_prefetch=1,
            grid=(nseq, npg // npp),
            in_specs=[page(u) for u in range(npp)] + [const(pe_t), const(w), const(g2), const(so)],
            out_specs=[out, out],
            scratch_shapes=[pltpu.VMEM((nrow * CMP_PITCH, PAGE_SIZE), F32), pltpu.VMEM((nrow * CMP_PITCH, PAGE_SIZE), F32),
                            pltpu.VMEM((nrow, 2 * LANES), F32)]),
        out_shape=[jax.ShapeDtypeStruct((nseq, nrow, LANES), F32)] * 2,
        compiler_params=_cparams("arbitrary", "arbitrary"),
        name="nsa_compress",
    )(table, *([pages_t] * npp), pe_t, w, g2, so)
    fix = lambda a: a.reshape(nseq, npg, NSA_GROUPS, 2, HEAD_DIM).transpose(0, 2, 1, 3, 4).reshape(
        nseq, NSA_GROUPS, 2 * npg, HEAD_DIM)
    return fix(kc), fix(vc)


class _PageIndex:
    def __init__(self, table, fn):
        self.table = table
        self._fn = fn

    def __call__(self, b, p, pt):
        return self._fn(b, p, pt)


def _stack_heads(q):
    head = lax.broadcasted_iota(jnp.int32, q.shape, 1) >> 6
    qf = q.astype(F32)
    return jnp.concatenate([jnp.where(head == r, qf, 0.0) for r in range(NSA_REP)], axis=0).astype(BF16)


def _unstack_heads(o4, tq):
    head = lax.broadcasted_iota(jnp.int32, (tq, o4.shape[1]), 1) >> 6
    out = jnp.zeros((tq, o4.shape[1]), F32)
    for r in range(NSA_REP):
        out = jnp.where(head == r, o4[r * tq:(r + 1) * tq, :], out)
    return out


def _tile_rows4(x):
    return jnp.concatenate([x] * NSA_REP, axis=0)


def _nsa_cmp_kernel(q0, ns_rows, q_ref, kc_ref, vc_ref, oc_ref, sel_ref):
    i = pl.program_id(2)
    tq = q_ref.shape[1]
    nc = kc_ref.shape[2]
    qst = _stack_heads(q_ref[0])
    kc, vc = kc_ref[0, 0], vc_ref[0, 0]
    base = q0 + i * tq
    qpos = base + (lax.broadcasted_iota(jnp.int32, (NSA_REP * tq, nc), 0) & (tq - 1))
    blk_end = (lax.broadcasted_iota(jnp.int32, (NSA_REP * tq, nc), 1) + 1) * NSA_BLOCK - 1
    ok = blk_end <= qpos
    s = jnp.where(ok, _dot_nt(qst, kc), NEG_INF)
    e = jnp.exp(s - jnp.max(s, axis=-1, keepdims=True))
    p = jnp.where(ok, e / jnp.sum(e, axis=-1, keepdims=True), 0.0)
    oc_ref[0] = _unstack_heads(_dot(p.astype(BF16), vc), tq)
    qpos_t = base + (lax.broadcasted_iota(jnp.int32, (nc, NSA_REP * tq), 1) & (tq - 1))
    blk_end_t = (lax.broadcasted_iota(jnp.int32, (nc, NSA_REP * tq), 0) + 1) * NSA_BLOCK - 1
    ok_t = blk_end_t <= qpos_t
    st = jnp.where(ok_t, _dot_nt(kc, qst), NEG_INF)
    et = jnp.exp(st - jnp.max(st, axis=0, keepdims=True))
    pt = jnp.where(ok_t, et / jnp.sum(et, axis=0, keepdims=True), 0.0)
    imp = pt[:, 0:tq]
    for r in range(1, NSA_REP):
        imp = imp + pt[:, r * tq:(r + 1) * tq]
    if ns_rows > nc:
        imp = jnp.concatenate([imp, jnp.zeros((ns_rows - nc, tq), F32)], axis=0)
    blk = lax.broadcasted_iota(jnp.int32, (ns_rows, tq), 0)
    cur = (base + lax.broadcasted_iota(jnp.int32, (ns_rows, tq), 1)) >> 6
    forced = (blk == 0) | (blk == cur) | (blk == cur - 1)
    imp = jnp.where(forced, NSA_FORCED, imp)
    imp = jnp.where(blk <= cur, imp, NEG_INF)
    taken = jnp.float32(-3e38)
    blk_f = blk.astype(F32)

    def pick(_, carry):
        imp, sel = carry
        m = jnp.max(imp, axis=0, keepdims=True)
        first = jnp.min(jnp.where(imp == m, blk_f, float(ns_rows)), axis=0, keepdims=True)
        hit = blk_f == first
        return jnp.where(hit, taken, imp), jnp.where(hit, 0.0, sel)

    _, sel = lax.fori_loop(0, NSA_TOPN, pick, (imp, jnp.full((ns_rows, tq), NEG_INF, F32)))
    sel_ref[0, 0] = sel.T.astype(BF16)


def _tile_lanes4(x):
    return jnp.tile(x, (1, 1, 1, NSA_REP)).astype(BF16)


def nsa_cmp_select(qn, kc, vc, q0, ns_rows):
    b, t, _ = qn.shape
    tq = min(256, t)
    nc = kc.shape[2]
    kv = pl.BlockSpec((1, 1, nc, 256), lambda bi, g, i: (bi, g, 0, 0))
    return pl.pallas_call(
        functools.partial(_nsa_cmp_kernel, q0, ns_rows),
        grid=(b, NSA_GROUPS, t // tq),
        in_specs=[pl.BlockSpec((1, tq, 256), lambda bi, g, i: (bi, i, g)), kv, kv],
        out_specs=[pl.BlockSpec((1, tq, 256), lambda bi, g, i: (bi, i, g)),
                   pl.BlockSpec((1, 1, tq, ns_rows), lambda bi, g, i: (bi, g, i, 0))],
        out_shape=[jax.ShapeDtypeStruct((b, t, 1024), F32),
                   jax.ShapeDtypeStruct((b, NSA_GROUPS, t, ns_rows), BF16)],
        compiler_params=_cparams("parallel", "parallel", "parallel"),
        name="nsa_cmp_select",
    )(qn, _tile_lanes4(kc), _tile_lanes4(vc))


WIN_SUB = 2


def _nsa_window_kernel(nsub, q_ref, k_ref, v_ref, o_ref):
    i = pl.program_id(2)
    tq = q_ref.shape[1] // nsub
    ntile = (NSA_WINDOW + nsub * tq) // LANES
    span = ntile * LANES
    j0 = jnp.maximum(i * nsub * tq - NSA_WINDOW, 0) // LANES
    kt4 = _tile_rows4(jnp.concatenate([k_ref[0, j0 + u] for u in range(ntile)], axis=1))
    vt4 = _tile_rows4(jnp.concatenate([v_ref[0, j0 + u] for u in range(ntile)], axis=1))
    kpos = j0 * LANES + lax.broadcasted_iota(jnp.int32, (NSA_REP * tq, span), 1)
    for a in range(nsub):
        qst = _stack_heads(q_ref[0, a * tq:(a + 1) * tq, :])
        qpos = (i * nsub + a) * tq + (lax.broadcasted_iota(jnp.int32, (NSA_REP * tq, span), 0) & (tq - 1))
        ok = (kpos <= qpos) & (kpos > qpos - NSA_WINDOW)
        s = jnp.where(ok, _dot(qst, kt4), NEG_INF)
        e = jnp.exp(s - jnp.max(s, axis=-1, keepdims=True))
        p = e / jnp.sum(e, axis=-1, keepdims=True)
        o_ref[0, a * tq:(a + 1) * tq, :] = _unstack_heads(_dot_nt(p.astype(BF16), vt4), tq)


def nsa_window_prompt(qr, kwin):
    b, t, _ = qr.shape
    tq = WIN_SUB * 128
    nt = kwin.shape[1]
    kv = lambda off: pl.BlockSpec((1, nt, HEAD_DIM, LANES), lambda bi, g, i: (bi, 0, off + g, 0))
    return pl.pallas_call(
        functools.partial(_nsa_window_kernel, WIN_SUB),
        grid=(b, NSA_GROUPS, t // tq),
        in_specs=[pl.BlockSpec((1, tq, 256), lambda bi, g, i: (bi, i, g)), kv(0), kv(NSA_GROUPS)],
        out_specs=pl.BlockSpec((1, tq, 256), lambda bi, g, i: (bi, i, g)),
        out_shape=jax.ShapeDtypeStruct((b, t, 1024), F32),
        compiler_params=_cparams("parallel", "parallel", "arbitrary"),
        name="nsa_window",
    )(qr, kwin, kwin)


SEL_SUB = 2


def _nsa_selected_kernel(q_ref, sel_ref, k_ref, v_ref, e_ref, o_ref, m_ref, l_ref, acc_ref):
    i = pl.program_id(2)
    nsub = m_ref.shape[0]
    tq = q_ref.shape[1] // nsub
    tk = k_ref.shape[3]
    rows = NSA_REP * tq
    qst = [_stack_heads(q_ref[0, a * tq:(a + 1) * tq, :]) for a in range(nsub)]
    sel = [sel_ref[0, 0, a * tq:(a + 1) * tq, :] for a in range(nsub)]
    _softmax_init(m_ref, l_ref, acc_ref)

    def tile(j, causal):
        kt4 = _tile_rows4(k_ref[0, j])
        vt4 = _tile_rows4(v_ref[0, j])
        e = e_ref[j]
        for a in range(nsub):
            bias = _dot(sel[a], e)
            s = _dot(qst[a], kt4) + _tile_rows4(bias)
            if causal:
                qpos = (i * nsub + a) * tq + (lax.broadcasted_iota(jnp.int32, (rows, tk), 0) & (tq - 1))
                kpos = j * tk + lax.broadcasted_iota(jnp.int32, (rows, tk), 1)
                s = jnp.where(kpos <= qpos, s, NEG_INF)
            _softmax_update(s, lambda p: _dot_nt(p, vt4), m_ref.at[a], l_ref.at[a], acc_ref.at[a])

    nfull = (i * nsub * tq) // tk

    def body(j, c):
        tile(j, False)
        return c

    lax.fori_loop(0, nfull, body, 0)
    tile(nfull, True)
    for a in range(nsub):
        o_ref[0, a * tq:(a + 1) * tq, :] = _unstack_heads(_softmax_result(l_ref.at[a], acc_ref.at[a]), tq)


def nsa_selected_prompt(qr, sel, ksel):
    b, t, _ = qr.shape
    sub = 128
    tq = SEL_SUB * sub
    nt, _, tk = ksel.shape[1:]
    assert tk % tq == 0
    nblk = sel.shape[-1]
    e3 = (jnp.arange(nblk)[None, :, None] == (jnp.arange(nt)[:, None, None] * tk + jnp.arange(tk)[None, None, :]) // NSA_BLOCK
          ).astype(BF16)
    kv = lambda off: pl.BlockSpec((1, nt, HEAD_DIM, tk), lambda bi, g, i: (bi, 0, off + g, 0))
    return pl.pallas_call(
        _nsa_selected_kernel,
        grid=(b, NSA_GROUPS, t // tq),
        in_specs=[pl.BlockSpec((1, tq, 256), lambda bi, g, i: (bi, i, g)),
                  pl.BlockSpec((1, 1, tq, nblk), lambda bi, g, i: (bi, g, i, 0)),
                  kv(0), kv(NSA_GROUPS),
                  pl.BlockSpec(e3.shape, lambda bi, g, i: (0, 0, 0))],
        out_specs=pl.BlockSpec((1, tq, 256), lambda bi, g, i: (bi, i, g)),
        out_shape=jax.ShapeDtypeStruct((b, t, 1024), F32),
        scratch_shapes=[pltpu.VMEM((SEL_SUB, NSA_REP * sub, LANES), F32), pltpu.VMEM((SEL_SUB, NSA_REP * sub, LANES), F32),
                        pltpu.VMEM((SEL_SUB, NSA_REP * sub, 256), F32)],
        compiler_params=_cparams("parallel", "parallel", "arbitrary"),
        name="nsa_selected",
    )(qr, sel, ksel, ksel, e3)


def _nsa_out_kernel(oc_ref, os_ref, ow_ref, gate_ref, ex_ref, w_ref, r_ref, o_ref):
    gate = gate_ref[...]
    comb = (_split_dot(gate, ex_ref[0]) * oc_ref[...] + _split_dot(gate, ex_ref[1]) * os_ref[...]
            + _split_dot(gate, ex_ref[2]) * ow_ref[...])
    o_ref[...] = r_ref[...] + _dot(comb.astype(BF16), w_ref[...])


def nsa_out(oc, os_, ow, gates, w_o, res):
    n, d = res.shape
    tm = min(256, n)
    lane = jnp.arange(1024) // HEAD_DIM
    ex = jnp.stack([(jnp.arange(LANES)[:, None] == lane[None, :] * 3 + k) for k in range(3)]).astype(BF16)
    row = lambda w: pl.BlockSpec((tm, w), lambda i: (i, 0))
    return pl.pallas_call(
        _nsa_out_kernel,
        grid=(n // tm,),
        in_specs=[row(1024), row(1024), row(1024), row(LANES),
                  pl.BlockSpec(ex.shape, lambda i: (0, 0, 0)),
                  pl.BlockSpec(w_o.shape, lambda i: (0, 0)), row(d)],
        out_specs=row(d),
        out_shape=jax.ShapeDtypeStruct((n, d), F32),
        compiler_params=_cparams("parallel"),
        name="nsa_out",
    )(oc, os_, ow, gates, ex, w_o, res)


def _paged_attn_kernel(npp, kv_t, pt_ref, q_ref, *refs):
    k_refs, v_refs = refs[:npp], refs[npp:2 * npp]
    kn_ref, vn_ref, bp_ref, bn_ref, o_ref, m_ref, l_ref, acc_ref = refs[2 * npp:]
    p = pl.program_id(1)
    last = pl.num_programs(1) - 1

    @pl.when(p == 0)
    def _():
        _softmax_init(m_ref, l_ref, acc_ref)

    def step(k, v, bias):
        if kv_t:
            s = _dot(q_ref[0], k) + bias
            _softmax_update(s, lambda e: _dot_nt(e, v), m_ref, l_ref, acc_ref)
        else:
            s = _dot_nt(q_ref[0], k) + bias
            _softmax_update(s, lambda e: _dot(e, v), m_ref, l_ref, acc_ref)

    def load(r):
        x = r[...]
        return x if kv_t else x.reshape(-1, x.shape[-1])

    @pl.when(p < last)
    def _():
        axis = 1 if kv_t else 0
        step(jnp.concatenate([load(r) for r in k_refs], axis=axis).astype(BF16),
             jnp.concatenate([load(r) for r in v_refs], axis=axis).astype(BF16), bp_ref[0])

    @pl.when(p == last)
    def _():
        step(kn_ref[0], vn_ref[0], bn_ref[0])
        o_ref[0] = _softmax_result(l_ref, acc_ref)


def paged_attention(q, pages, table, npg, k_spec, v_spec, kv_t, k_new, v_new, bias_past, bias_new, past_per_step):
    nseq, rows, _ = q.shape
    npp = PAGES_PER_STEP
    lv = acc_w = v_new.shape[1] if kv_t else v_new.shape[2]
    seq = lambda a: pl.BlockSpec((1,) + a.shape[1:], lambda b, p, pt: (b, 0, 0))
    wpast = bias_past.shape[-1] if not past_per_step else bias_past.shape[-1] // (npg // npp)
    bp_spec = pl.BlockSpec((1, rows, wpast), lambda b, p, pt: (
        b if bias_past.shape[0] > 1 else 0, 0, jnp.minimum(p, npg // npp - 1) if past_per_step else 0))
    bn_spec = pl.BlockSpec((1,) + bias_new.shape[1:], lambda b, p, pt: (b if bias_new.shape[0] > 1 else 0, 0, 0))
    return pl.pallas_call(
        functools.partial(_paged_attn_kernel, npp, kv_t),
        grid_spec=pltpu.PrefetchScalarGridSpec(
            num_scalar_prefetch=1,
            grid=(nseq, npg // npp + 1),
            in_specs=[seq(q)] + [k_spec(u) for u in range(npp)] + [v_spec(u) for u in range(npp)]
                     + [seq(k_new), seq(v_new), bp_spec, bn_spec],
            out_specs=pl.BlockSpec((1, rows, acc_w), lambda b, p, pt: (b, 0, 0)),
            scratch_shapes=[pltpu.VMEM((rows, LANES), F32), pltpu.VMEM((rows, LANES), F32),
                            pltpu.VMEM((rows, acc_w), F32)]),
        out_shape=jax.ShapeDtypeStruct((nseq, rows, lv), F32),
        compiler_params=_cparams("arbitrary", "arbitrary"),
        name="paged_attention",
    )(table, q, *([pages] * (2 * npp)), k_new, v_new, bias_past, bias_new)


def _diff_post_kernel(p_ref, cos_ref, sin_ref, g_ref, so_ref, q_ref, kvf_ref, kvb_ref):
    cos, sin = cos_ref[...], sin_ref[...]
    so = so_ref[...]
    scale = HEAD_DIM ** -0.5
    for s in range(DIFF_WIDTH // LANES):
        sl = slice(s * LANES, (s + 1) * LANES)
        q = _rope_slab(_seg64_norm(p_ref[:, sl], g_ref[0:1, :], so), cos, sin)
        q_ref[:, sl] = (q * scale).astype(BF16)
        k = _rope_slab(_seg64_norm(p_ref[:, DIFF_WIDTH + s * LANES:DIFF_WIDTH + (s + 1) * LANES], g_ref[1:2, :], so),
                       cos, sin)
        kvf_ref[:, sl] = k
        kvb_ref[:, sl] = k.astype(BF16)
    v = p_ref[:, 2 * DIFF_WIDTH:3 * DIFF_WIDTH]
    kvf_ref[:, DIFF_WIDTH:] = v
    kvb_ref[:, DIFF_WIDTH:] = v.astype(BF16)


def diff_post(proj, cos, sin, qk_g):
    n = proj.shape[0]
    tm = min(256, n)
    g2 = jnp.tile(qk_g, (1, 2))
    so = _seg_ones()
    row = lambda w: pl.BlockSpec((tm, w), lambda i: (i, 0))
    const = lambda a: pl.BlockSpec(a.shape, lambda i: (0, 0))
    return pl.pallas_call(
        _diff_post_kernel,
        grid=(n // tm,),
        in_specs=[row(3 * DIFF_WIDTH), row(LANES), row(LANES), const(g2), const(so)],
        out_specs=[row(DIFF_WIDTH), row(2 * DIFF_WIDTH), row(2 * DIFF_WIDTH)],
        out_shape=[jax.ShapeDtypeStruct((n, DIFF_WIDTH), BF16), jax.ShapeDtypeStruct((n, 2 * DIFF_WIDTH), F32),
                   jax.ShapeDtypeStruct((n, 2 * DIFF_WIDTH), BF16)],
        compiler_params=_cparams("parallel"),
        name="diff_post",
    )(proj, cos, sin, g2, so)


def _diff_lambda(lam_ref, lam_init):
    lf = lam_ref[...]
    a = jnp.sum(lf[0:1, :] * lf[1:2, :], axis=-1, keepdims=True)
    b = jnp.sum(lf[2:3, :] * lf[3:4, :], axis=-1, keepdims=True)
    return jnp.exp(a) - jnp.exp(b) + lam_init


DIFF_SUB = 2
DIFF_SUB_TQ = 256
DIFF_TK = 512


def _diff_flash_kernel(lam_init, tk, q_ref, k_ref, v_ref, lam_ref, o_ref, m_ref, l_ref, acc_ref):
    i = pl.program_id(2)
    nsub = m_ref.shape[0]
    tq = q_ref.shape[1] // nsub
    rows = 2 * tq

    def stack_components(q):
        comp = lax.broadcasted_iota(jnp.int32, q.shape, 1) >> 6
        return jnp.concatenate([jnp.where(comp == c, q, 0.0) for c in range(2)], axis=0).astype(BF16)

    qst = [stack_components(q_ref[0, a * tq:(a + 1) * tq, :].astype(F32)) for a in range(nsub)]
    _softmax_init(m_ref, l_ref, acc_ref)

    def tile(j, causal):
        lo = pl.multiple_of(j * tk, tk)
        k = k_ref[0, pl.ds(lo, tk), :]
        v = v_ref[0, pl.ds(lo, tk), :]
        for a in range(nsub):
            s = _dot_nt(qst[a], k)
            if causal:
                qpos = (i * nsub + a) * tq + (lax.broadcasted_iota(jnp.int32, (rows, tk), 0) & (tq - 1))
                kpos = j * tk + lax.broadcasted_iota(jnp.int32, (rows, tk), 1)
                s = jnp.where(kpos <= qpos, s, NEG_INF)
            _softmax_update(s, lambda p: _dot(p, v), m_ref.at[a], l_ref.at[a], acc_ref.at[a])

    nfull = (i * nsub * tq) // tk

    def body(j, c):
        tile(j, False)
        return c

    lax.fori_loop(0, nfull, body, 0)
    tile(nfull, True)
    lam = _diff_lambda(lam_ref, lam_init)
    for a in range(nsub):
        o = _softmax_result(l_ref.at[a], acc_ref.at[a])
        o_ref[0, a * tq:(a + 1) * tq, :] = o[0:tq, :] - lam * o[tq:rows, :]


def diff_flash_prompt(q, kvb, lam, lam_init):
    b, t, _ = q.shape
    sub = DIFF_SUB_TQ
    tq = DIFF_SUB * sub
    tk = DIFF_TK
    assert tk % tq == 0 and t % tk == 0
    return pl.pallas_call(
        functools.partial(_diff_flash_kernel, lam_init, tk),
        grid=(b, DIFF_HEADS, t // tq),
        in_specs=[pl.BlockSpec((1, tq, LANES), lambda bi, h, i: (bi, i, h)),
                  pl.BlockSpec((1, t, LANES), lambda bi, h, i: (bi, 0, h)),
                  pl.BlockSpec((1, t, LANES), lambda bi, h, i: (bi, 0, DIFF_HEADS + h)),
                  pl.BlockSpec(lam.shape, lambda bi, h, i: (0, 0))],
        out_specs=pl.BlockSpec((1, tq, LANES), lambda bi, h, i: (bi, i, h)),
        out_shape=jax.ShapeDtypeStruct((b, t, DIFF_WIDTH), F32),
        scratch_shapes=[pltpu.VMEM((DIFF_SUB, 2 * sub, LANES), F32)] * 3,
        compiler_params=_cparams("parallel", "parallel", "arbitrary"),
        name="diff_flash",
    )(q, kvb, kvb, lam)


def _diff_out_kernel(lam_init, two, *refs):
    if two:
        o0_ref, o1_ref, lam_ref, g_ref, w_ref, r_ref, out_ref, h_ref = refs
        o = o0_ref[...] - _diff_lambda(lam_ref, lam_init) * o1_ref[...]
    else:
        o0_ref, g_ref, w_ref, r_ref, out_ref, h_ref = refs
        o = o0_ref[...]
    for s in range(DIFF_HEADS):
        sl = slice(s * LANES, (s + 1) * LANES)
        h_ref[:, sl] = (_rms_rows(o[:, sl], g_ref[...]) * (1.0 - lam_init)).astype(BF16)
    out_ref[...] = r_ref[...] + _dot(h_ref[...], w_ref[...])


def diff_out(o, sub_g, w_o, res, lam_init, o1=None, lam=None):
    n, d = res.shape
    tm = min(512, n)
    two = o1 is not None
    row = lambda w: pl.BlockSpec((tm, w), lambda i: (i, 0))
    const = lambda a: pl.BlockSpec(a.shape, lambda i: (0, 0))
    g = sub_g.reshape(1, LANES)
    ins = [o, o1, lam, g, w_o, res] if two else [o, g, w_o, res]
    specs = ([row(DIFF_WIDTH), row(DIFF_WIDTH), const(lam)] if two else [row(DIFF_WIDTH)]) + [const(g), const(w_o), row(d)]
    return pl.pallas_call(
        functools.partial(_diff_out_kernel, lam_init, two),
        grid=(n // tm,),
        in_specs=specs,
        out_specs=row(d),
        out_shape=jax.ShapeDtypeStruct((n, d), F32),
        scratch_shapes=[pltpu.VMEM((tm, DIFF_WIDTH), BF16)],
        compiler_params=_cparams("parallel"),
        name="diff_out",
    )(*ins)


def _gla_kernel(chunk, nchunk, t_valid, p_ref, s0_ref, wa_ref, ba_ref, g_ref, tri_ref, y_ref, sfin_ref, st_ref):
    step = pl.program_id(1)

    @pl.when(step == 0)
    def _():
        st_ref[...] = s0_ref[0]

    nk = GLA_HEADS * GLA_DK
    nv = GLA_HEADS * GLA_DV
    tri = tri_ref[...]
    causal = lax.broadcasted_iota(jnp.int32, (chunk, chunk), 0) >= lax.broadcasted_iota(jnp.int32, (chunk, chunk), 1)
    for ci in range(nchunk):
        r0 = ci * chunk
        rows = slice(r0, r0 + chunk)
        a1 = p_ref[0, rows, GLA_MAIN:GLA_MAIN + LANES].astype(BF16)
        z = _dot(a1, wa_ref[...]) + ba_ref[...]
        log_a = (jnp.minimum(z, 0.0) - jnp.log(1.0 + jnp.exp(-jnp.abs(z)))) * (1.0 / GLA_TAU)
        if t_valid < chunk:
            log_a = jnp.where(lax.broadcasted_iota(jnp.int32, log_a.shape, 0) < t_valid, log_a, 0.0)
        cum = _split_dot_left(tri, log_a)
        for h in range(GLA_HEADS):
            ksl = slice(h * GLA_DK, (h + 1) * GLA_DK)
            vsl = slice(h * GLA_DV, (h + 1) * GLA_DV)
            q = p_ref[0, rows, h * GLA_DK:(h + 1) * GLA_DK] * (GLA_DK ** -0.5)
            k = p_ref[0, rows, nk + h * GLA_DK:nk + (h + 1) * GLA_DK]
            v = p_ref[0, rows, 2 * nk + h * GLA_DV:2 * nk + (h + 1) * GLA_DV]
            r = p_ref[0, rows, 2 * nk + nv + h * GLA_DV:2 * nk + nv + (h + 1) * GLA_DV]
            cm = cum[:, ksl]
            last = cm[chunk - 1:chunk, :]
            qe = (q * jnp.exp(cm)).astype(BF16)
            ke = (k * jnp.exp(-cm)).astype(BF16)
            vb = v.astype(BF16)
            att = jnp.where(causal, _dot_nt(qe, ke), 0.0)
            st = st_ref[h]
            o = _dot_nt(qe, st.astype(BF16)) + _dot(att.astype(BF16), vb)
            kd = (k * jnp.exp(last - cm)).astype(BF16)
            st_ref[h] = st * jnp.exp(last) + _dot(v.T.astype(BF16), kd)
            y_ref[0, rows, vsl] = _rms_rows(o, g_ref[...]) * _silu(r)

    @pl.when(step == pl.num_programs(1) - 1)
    def _():
        sfin_ref[0] = st_ref[...]


def gla_core(proj, s0_t, w_a2, b_a, out_g, chunk, nchunk, t_valid):
    b, t, _ = proj.shape
    rows = chunk * nchunk
    wa = jnp.zeros((LANES, GLA_HEADS * GLA_DK), F32).at[:GLA_RANK].set(w_a2).astype(BF16)
    ba = b_a.reshape(1, -1)
    g = out_g.reshape(1, GLA_DV)
    tri = (jnp.arange(chunk)[:, None] >= jnp.arange(chunk)[None, :]).astype(BF16)
    const = lambda a: pl.BlockSpec(a.shape, lambda bi, s: (0,) * a.ndim)
    st_spec = pl.BlockSpec((1, GLA_HEADS, GLA_DV, GLA_DK), lambda bi, s: (bi, 0, 0, 0))
    return pl.pallas_call(
        functools.partial(_gla_kernel, chunk, nchunk, t_valid),
        grid=(b, t // rows),
        in_specs=[pl.BlockSpec((1, rows, GLA_IN_PAD), lambda bi, s: (bi, s, 0)), st_spec,
                  const(wa), const(ba), const(g), const(tri)],
        out_specs=[pl.BlockSpec((1, rows, GLA_HEADS * GLA_DV), lambda bi, s: (bi, s, 0)), st_spec],
        out_shape=[jax.ShapeDtypeStruct((b, t, GLA_HEADS * GLA_DV), F32),
                   jax.ShapeDtypeStruct((b, GLA_HEADS, GLA_DV, GLA_DK), F32)],
        scratch_shapes=[pltpu.VMEM((GLA_HEADS, GLA_DV, GLA_DK), F32)],
        compiler_params=_cparams("parallel", "arbitrary"),
        name="gla_core",
    )(proj, s0_t, wa, ba, g, tri)


def _pad_cols(w, width):
    return jnp.pad(w, ((0, 0), (0, width - w.shape[1]))).astype(BF16)


def _pad_axis(x, axis, size):
    pad = [(0, 0)] * x.ndim
    pad[axis] = (0, size - x.shape[axis])
    return jnp.pad(x, pad)


def _group_diag(q, t_seq):
    nseq = q.shape[0] // t_seq
    qg = q.reshape(nseq, t_seq, NSA_GROUPS, NSA_REP, HEAD_DIM).transpose(0, 2, 1, 3, 4)
    eye = jnp.eye(NSA_GROUPS, dtype=q.dtype)
    out = qg[:, :, :, :, None, :] * eye[None, :, None, None, :, None]
    return out.reshape(nseq, NSA_GROUPS * t_seq * NSA_REP, NSA_GROUPS * HEAD_DIM)


def _group_undiag(o, t_seq):
    nseq = o.shape[0]
    o6 = o.reshape(nseq, NSA_GROUPS, t_seq, NSA_REP, NSA_GROUPS, HEAD_DIM)
    od = jnp.stack([o6[:, g, :, :, g, :] for g in range(NSA_GROUPS)], axis=1)
    return od.transpose(0, 2, 1, 3, 4).reshape(nseq * t_seq, NSA_HEADS * HEAD_DIM)


def _seq_cols(x_t, nseq, t_seq):
    r = x_t.shape[0]
    return _pad_axis(x_t.reshape(r, nseq, t_seq).transpose(1, 0, 2), 2, PAGE_SIZE)


def _nsa_layer(s, xp, xs, seq_p, t_s, past_len, cache_t, state_win_t, page_table, norm_g, w_in, qk_g, pe, w_phi, w_o):
    bp = xp.shape[0] // seq_p
    bs = xs.shape[0] // t_s
    w_o_b = w_o.astype(BF16)
    npg = past_len // PAGE_SIZE

    qn, qr, gates, rows_t, win_t, ksel, kwin = nsa_project(xp, norm_g, w_in, qk_g, jnp.arange(seq_p), bp)
    own = _PageIndex(jnp.zeros((1, 1), jnp.int32), lambda b, p, pt: (b, 0, p))
    kc, vc = nsa_compress(rows_t, own, bp, seq_p // PAGE_SIZE, pe, w_phi, qk_g[3])
    qn3, qr3 = qn.reshape(bp, seq_p, 1024), qr.reshape(bp, seq_p, 1024)
    oc, sel = nsa_cmp_select(qn3, kc, vc, 0, seq_p // NSA_BLOCK)
    os_ = nsa_selected_prompt(qr3, sel, ksel)
    ow = nsa_window_prompt(qr3, kwin)
    xp_new = nsa_out(oc.reshape(-1, 1024), os_.reshape(-1, 1024), ow.reshape(-1, 1024), gates, w_o_b, xp)
    kv_p = rows_t.reshape(bp, 4, NSA_GROUPS, HEAD_DIM, seq_p).transpose(0, 4, 1, 2, 3)
    wlen = min(NSA_WINDOW, seq_p)
    win_p = win_t[:, :, seq_p - wlen:].reshape(bp, 2, NSA_GROUPS, HEAD_DIM, wlen).transpose(0, 4, 1, 2, 3)

    pos_s = past_len + jnp.arange(t_s)
    qn, qr, gates, rows_t, win_t, _, _ = nsa_project(xs, norm_g, w_in, qk_g, jnp.tile(pos_s, bs), 1)
    rows_t, win_t = rows_t[0], win_t[0]
    layer_pages = cache_t.shape[0] // (state_win_t.shape[0] // bs)
    table = page_table + s * layer_pages
    paged = _PageIndex(table, lambda b, p, pt: (pt[b, p], 0, 0))
    kc, vc = nsa_compress(cache_t, paged, bs, npg, pe, w_phi, qk_g[3])
    qn_pad = _pad_axis(qn.reshape(bs, t_s, 1024), 1, 128)
    ns = -(-(past_len + t_s) // NSA_BLOCK)
    oc, sel = nsa_cmp_select(qn_pad, kc, vc, past_len, 256)
    oc = oc[:, :t_s].reshape(bs * t_s, 1024)
    nrow = NSA_GROUPS * t_s * NSA_REP
    sel_f = sel[:, :, :t_s, :ns].astype(F32)
    by_row = lambda a: jnp.broadcast_to(a[:, :, :, None, :], a.shape[:3] + (NSA_REP, a.shape[-1])).reshape(
        a.shape[0], nrow, a.shape[-1])
    past_bias = by_row(jnp.repeat(sel_f[..., :past_len // NSA_BLOCK], NSA_BLOCK, axis=-1))
    new_ok = jnp.arange(PAGE_SIZE)[None, :] <= jnp.arange(t_s)[:, None]
    new_bias = by_row(jnp.where(new_ok[None, None], sel_f[..., past_len // NSA_BLOCK][..., None], NEG_INF))
    q_bd = _group_diag(qr, t_s)
    npp = PAGES_PER_STEP
    kpage = lambda blk: (lambda u: pl.BlockSpec((None, NSA_GW, PAGE_SIZE),
                                                lambda b, p, pt: (pt[b, jnp.minimum(p * npp + u, npg - 1)], blk, 0)))
    k_new = _seq_cols(rows_t[2 * NSA_GW:3 * NSA_GW], bs, t_s).astype(BF16)
    v_new = _seq_cols(rows_t[3 * NSA_GW:4 * NSA_GW], bs, t_s).astype(BF16)
    os_ = _group_undiag(paged_attention(q_bd, cache_t, table, npg, kpage(2), kpage(3), True, k_new, v_new,
                                        past_bias, new_bias, True), t_s)
    wbuf = state_win_t.shape[-1]
    wpg = wbuf // PAGE_SIZE
    wpage = lambda blk: (lambda u: pl.BlockSpec((None, NSA_GW, PAGE_SIZE),
                                                lambda b, p, pt: (s * bs + b, blk, jnp.minimum(p * npp + u, wpg - 1))))
    kidx = jnp.arange(wbuf + PAGE_SIZE)
    wpos = jnp.where(kidx < wbuf, past_len - wbuf + kidx, past_len + kidx - wbuf)
    w_ok = ((wpos[None, :] <= pos_s[:, None]) & (wpos[None, :] > pos_s[:, None] - NSA_WINDOW) & (wpos[None, :] >= 0)
            & (kidx[None, :] < wbuf + t_s))
    wbias = by_row(jnp.broadcast_to(jnp.where(w_ok, 0.0, NEG_INF).astype(F32)[None, None],
                                    (1, NSA_GROUPS, t_s, wbuf + PAGE_SIZE)))
    kw_new = _seq_cols(win_t[0:NSA_GW], bs, t_s).astype(BF16)
    vw_new = _seq_cols(win_t[NSA_GW:2 * NSA_GW], bs, t_s).astype(BF16)
    ow = _group_undiag(paged_attention(q_bd, state_win_t, jnp.zeros((1, 1), jnp.int32), wpg, wpage(0), wpage(1), True,
                                       kw_new, vw_new, wbias[:, :, :wbuf], wbias[:, :, wbuf:], True), t_s)
    xs_new = nsa_out(oc, os_, ow, gates, w_o_b, xs)
    kv_s = rows_t.reshape(4, NSA_GROUPS, HEAD_DIM, bs, t_s).transpose(3, 4, 0, 1, 2)
    win_new = win_t.reshape(2 * NSA_GW, bs, t_s).transpose(1, 0, 2)
    win_all = jnp.concatenate([state_win_t[s * bs:(s + 1) * bs], win_new], axis=2)
    wlen = min(NSA_WINDOW, win_all.shape[2])
    win_s = win_all[:, :, win_all.shape[2] - wlen:].reshape(bs, 2, NSA_GROUPS, HEAD_DIM, wlen).transpose(0, 4, 1, 2, 3)
    return xp_new, xs_new, kv_p, kv_s, win_p, win_s


def _diff_layer(layer, xp, xs, seq_p, t_s, past_len, cache, page_table, norm_g, w_in, qk_g, lam, sub_g, w_o):
    bp = xp.shape[0] // seq_p
    bs = xs.shape[0] // t_s
    npg = past_len // PAGE_SIZE
    lam_init = 0.8 - 0.6 * math.exp(-0.3 * layer)
    w_in_b = w_in.astype(BF16)
    w_o_b = w_o.astype(BF16)
    cos_p, sin_p = _rope_tables(jnp.arange(seq_p))
    proj = norm_matmul(xp, norm_g, w_in_b, 768)
    q, kvf, kvb = diff_post(proj, jnp.tile(cos_p, (bp, 1)), jnp.tile(sin_p, (bp, 1)), qk_g)
    o = diff_flash_prompt(q.reshape(bp, seq_p, DIFF_WIDTH), kvb.reshape(bp, seq_p, 2 * DIFF_WIDTH), lam, lam_init)
    xp_new = diff_out(o.reshape(-1, DIFF_WIDTH), sub_g, w_o_b, xp, lam_init)
    kv_p = kvf.reshape(bp, seq_p, 2, DIFF_HEADS, 2 * HEAD_DIM)
    pos_s = past_len + jnp.arange(t_s)
    cos_s, sin_s = _rope_tables(pos_s)
    proj = norm_matmul(xs, norm_g, w_in_b, 768)
    q, kvf, kvb = diff_post(proj, jnp.tile(cos_s, (bs, 1)), jnp.tile(sin_s, (bs, 1)), qk_g)
    nrow = DIFF_HEADS * 2 * t_s
    q5 = q.reshape(bs, t_s, DIFF_HEADS, 2, HEAD_DIM).transpose(0, 2, 3, 1, 4)
    q_rows = (q5[:, :, :, :, None, :] * jnp.eye(2, dtype=q.dtype)[None, None, :, None, :, None]).reshape(
        bs, nrow, 2 * HEAD_DIM)
    kv5 = kvb.reshape(bs, t_s, 2, DIFF_HEADS, 2 * HEAD_DIM)
    k_new = _pad_axis(kv5[:, :, 0], 1, PAGE_SIZE).reshape(bs, PAGE_SIZE * DIFF_HEADS, 2 * HEAD_DIM)
    v_new = _pad_axis(kv5[:, :, 1], 1, PAGE_SIZE).reshape(bs, PAGE_SIZE * DIFF_HEADS, 2 * HEAD_DIM)
    npp = PAGES_PER_STEP
    row_h = jnp.arange(nrow) // (2 * t_s)
    row_t = jnp.arange(nrow) % t_s
    slot_h = jnp.arange(PAGE_SIZE * DIFF_HEADS) % DIFF_HEADS
    slot_tok = jnp.arange(PAGE_SIZE * DIFF_HEADS) // DIFF_HEADS
    same_head = row_h[:, None] == slot_h[None, :]
    bias_page = jnp.where(same_head, 0.0, NEG_INF).astype(F32)
    bias_past = jnp.tile(bias_page, (1, npp))[None]
    bias_new = jnp.where(same_head & (slot_tok[None, :] <= row_t[:, None]), 0.0, NEG_INF).astype(F32)[None]
    page = lambda slot: (lambda u: pl.BlockSpec(
        (None, PAGE_SIZE, None, DIFF_HEADS, 2 * HEAD_DIM),
        lambda b, p, pt: (pt[b, jnp.minimum(p * npp + u, npg - 1)], 0, slot, 0, 0)))
    o = paged_attention(q_rows, cache, page_table, npg, page(0), page(1), False, k_new, v_new,
                        bias_past, bias_new, False)
    od = o.reshape(bs, DIFF_HEADS, 2, t_s, 2 * HEAD_DIM).transpose(2, 0, 3, 1, 4).reshape(2, bs * t_s, DIFF_WIDTH)
    xs_new = diff_out(od[0], sub_g, w_o_b, xs, lam_init, o1=od[1], lam=lam)
    kv_s = kvf.reshape(bs, t_s, 2, DIFF_HEADS, 2 * HEAD_DIM)
    return xp_new, xs_new, kv_p, kv_s


def _gla_layer(xp, xs, seq_p, t_s, state, norm_g, w_in, w_a2, b_a, out_g, w_o):
    bp = xp.shape[0] // seq_p
    bs = xs.shape[0] // t_s
    w_in_b = _pad_cols(w_in, GLA_IN_PAD)
    w_o_b = w_o.astype(BF16)
    proj = norm_matmul(xp, norm_g, w_in_b, 640).reshape(bp, seq_p, GLA_IN_PAD)
    chunk = min(GLA_CHUNK, seq_p)
    s0 = jnp.zeros((bp, GLA_HEADS, GLA_DV, GLA_DK), F32)
    y, st = gla_core(proj, s0, w_a2, b_a, out_g, chunk, 4 if seq_p % (4 * chunk) == 0 else 1, chunk)
    xp_new = matmul_residual(y.reshape(-1, GLA_HEADS * GLA_DV), w_o_b, xp)
    st_p = st.transpose(0, 1, 3, 2)
    chunk_s = GLA_CHUNK
    proj = norm_matmul(xs, norm_g, w_in_b, 640).reshape(bs, t_s, GLA_IN_PAD)
    proj = _pad_axis(proj, 1, chunk_s)
    y, st = gla_core(proj, state.transpose(0, 1, 3, 2), w_a2, b_a, out_g, chunk_s, 1, t_s)
    xs_new = matmul_residual(y[:, :t_s].reshape(-1, GLA_HEADS * GLA_DV), w_o_b, xs)
    st_s = st.transpose(0, 1, 3, 2)
    return xp_new, xs_new, st_p, st_s


def kernel(x_prompt, x_sample, cache_nsa_kv, state_nsa_win, cache_diff_kv, state_gla, state_ffn, page_table, norm_g, ffn_w_up, ffn_conv_w, ffn_conv_b, ffn_w_down, nsa_w_in, nsa_qk_g, nsa_pe, nsa_w_phi, nsa_w_o, diff_w_in, diff_qk_g, diff_lam, diff_sub_g, diff_w_o, gla_w_in, gla_w_a2, gla_b_a, gla_out_g, gla_w_o):
    bp, seq_p, d = x_prompt.shape
    bs, t_s, _ = x_sample.shape
    past_len = page_table.shape[1] * PAGE_SIZE
    xp = x_prompt.reshape(bp * seq_p, d)
    xs = x_sample.reshape(bs * t_s, d)
    cache_t = cache_nsa_kv.transpose(0, 1, 3, 4, 5, 2).reshape(-1, 4 * NSA_GW, PAGE_SIZE)
    win_t = state_nsa_win.transpose(0, 1, 3, 4, 5, 2).reshape(-1, 2 * NSA_GW, state_nsa_win.shape[2])
    nsa_kv_p, nsa_kv_s, nsa_win_p, nsa_win_s = [], [], [], []
    diff_kv_p, diff_kv_s, gla_p, gla_s, ffn_p, ffn_s = [], [], [], [], [], []
    for i in range(DEPTH):
        kind, s = i % N_MIXERS, i // N_MIXERS
        if kind == 0:
            xp, xs, kvp, kvs, wp, ws = _nsa_layer(
                s, xp, xs, seq_p, t_s, past_len, cache_t, win_t, page_table,
                norm_g[i, 0], nsa_w_in[s], nsa_qk_g[s], nsa_pe[s], nsa_w_phi[s], nsa_w_o[s])
            nsa_kv_p.append(kvp); nsa_kv_s.append(kvs); nsa_win_p.append(wp); nsa_win_s.append(ws)
        elif kind == 1:
            xp, xs, kvp, kvs = _diff_layer(
                i, xp, xs, seq_p, t_s, past_len, cache_diff_kv[s], page_table, norm_g[i, 0], diff_w_in[s],
                diff_qk_g[s], diff_lam[s], diff_sub_g[s], diff_w_o[s])
            diff_kv_p.append(kvp); diff_kv_s.append(kvs)
        else:
            xp, xs, stp, sts = _gla_layer(xp, xs, seq_p, t_s, state_gla[s], norm_g[i, 0], gla_w_in[s], gla_w_a2[s],
                                          gla_b_a[s], gla_out_g[s], gla_w_o[s])
            gla_p.append(stp); gla_s.append(sts)
        w_up_b = ffn_w_up[i].astype(BF16)
        w_dn_b = ffn_w_down[i].astype(BF16)
        xp, tail_p = ffn_prompt(xp, norm_g[i, 1], w_up_b, ffn_conv_w[i], ffn_conv_b[i], w_dn_b, seq_p)
        xs, tail_s = ffn_sample(xs, norm_g[i, 1], w_up_b, ffn_conv_w[i], ffn_conv_b[i], w_dn_b, state_ffn[i], t_s)
        ffn_p.append(tail_p); ffn_s.append(tail_s)
    return (xp.reshape(bp, seq_p, d), xs.reshape(bs, t_s, d),
            jnp.stack(nsa_kv_p), jnp.stack(nsa_kv_s), jnp.stack(nsa_win_p), jnp.stack(nsa_win_s),
            jnp.stack(diff_kv_p), jnp.stack(diff_kv_s), jnp.stack(gla_p), jnp.stack(gla_s),
            jnp.stack(ffn_p), jnp.stack(ffn_s))
```

```python
import functools
import math

import jax
import jax.numpy as jnp
from jax import lax
from jax.experimental import pallas as pl
from jax.experimental.pallas import tpu as pltpu

F32 = jnp.float32
BF16 = jnp.bfloat16

D_MODEL = 1024
HEAD_DIM = 64
ROPE_THETA = 10000.0
NORM_EPS = 1e-6
NEG_INF = -1e30
DEPTH = 4
N_MIXERS = 3
PAGE_SIZE = 128
NSA_HEADS = 16
NSA_GROUPS = 4
NSA_REP = 4
NSA_BLOCK = 64
NSA_TOPN = 16
NSA_WINDOW = 512
NSA_FORCED = 1e9
NSA_NQ = NSA_HEADS * HEAD_DIM
NSA_NKV = 6 * NSA_GROUPS * HEAD_DIM
NSA_NGATE = 3 * NSA_HEADS
NSA_GW = NSA_GROUPS * HEAD_DIM
DIFF_HEADS = 8
DIFF_WIDTH = 1024
GLA_HEADS = 4
GLA_DK = 128
GLA_DV = 256
GLA_RANK = 16
GLA_TAU = 16.0
GLA_CHUNK = 64
GLA_MAIN = 2 * GLA_HEADS * GLA_DK + 2 * GLA_HEADS * GLA_DV
GLA_IN_PAD = GLA_MAIN + 128
D_FF = 2816
CONV_W = 3
LANES = 128
VMEM_LIMIT = 56 * 1024 * 1024
SEL_TK = 512
NSA_PAGES_PER_STEP = 16
DIFF_PAGES_PER_STEP = 8
CMP_PITCH = HEAD_DIM + 8


def _cparams(*sem):
    return pltpu.CompilerParams(dimension_semantics=sem, vmem_limit_bytes=VMEM_LIMIT)


def _dot(a, b):
    return jnp.dot(a, b, preferred_element_type=F32)


def _dot_nt(a, b):
    return lax.dot_general(a, b, (((1,), (1,)), ((), ())), preferred_element_type=F32)


def _split_dot(x, m):
    hi = x.astype(BF16)
    lo = (x - hi.astype(F32)).astype(BF16)
    return _dot(hi, m) + _dot(lo, m)


def _split_dot_left(m, x):
    hi = x.astype(BF16)
    lo = (x - hi.astype(F32)).astype(BF16)
    return _dot(m, hi) + _dot(m, lo)


def _rms_rows(x, g):
    ms = jnp.mean(x * x, axis=-1, keepdims=True)
    return x * lax.rsqrt(ms + NORM_EPS) * g


def _seg64_norm(x, g, seg_ones):
    ms = _split_dot(x * x, seg_ones) * (1.0 / HEAD_DIM)
    return x * lax.rsqrt(ms + NORM_EPS) * g


def _rope_slab(x, cos, sin_signed):
    lane = lax.broadcasted_iota(jnp.int32, x.shape, 1)
    first = (lane & 63) < 32
    partner = jnp.where(first, pltpu.roll(x, 96, 1), pltpu.roll(x, 32, 1))
    return x * cos + partner * sin_signed


def _rope_angles(pos):
    half = HEAD_DIM // 2
    inv = ROPE_THETA ** (-jnp.arange(half, dtype=F32) / half)
    ang = pos.astype(F32)[:, None] * inv
    return jnp.cos(ang), jnp.sin(ang)


def _rope_tables(pos):
    cos, sin = _rope_angles(pos)
    return jnp.tile(cos, (1, 4)), jnp.tile(jnp.concatenate([-sin, sin], axis=1), (1, 2))


def _seg_ones():
    i = jnp.arange(LANES)
    return (i[:, None] // HEAD_DIM == i[None, :] // HEAD_DIM).astype(BF16)


def _lanes(x, width):
    return x if width == LANES else jnp.tile(x, (1, width // LANES))


def _softmax_update(s, v_dot, m_ref, l_ref, acc_ref):
    m = m_ref[...]
    m_new = jnp.maximum(m, jnp.max(s, axis=-1, keepdims=True))
    alpha = jnp.exp2(m - m_new)
    p = jnp.exp2(s - _lanes(m_new, s.shape[1]))
    m_ref[...] = m_new
    l_ref[...] = alpha * l_ref[...] + jnp.sum(p, axis=-1, keepdims=True)
    acc_ref[...] = _lanes(alpha, acc_ref.shape[-1]) * acc_ref[...] + v_dot(p.astype(BF16))


def _softmax_result(l_ref, acc_ref):
    return acc_ref[...] / _lanes(l_ref[...], acc_ref.shape[-1])


def _softmax_update_vsum(s, v_dot, m_ref, acc_ref):
    m = m_ref[...]
    m_new = jnp.maximum(m, jnp.max(s, axis=-1, keepdims=True))
    alpha = jnp.exp2(m - m_new)
    p = jnp.exp2(s - _lanes(m_new, s.shape[1]))
    m_ref[...] = m_new
    acc_ref[...] = _lanes(alpha, acc_ref.shape[-1]) * acc_ref[...] + v_dot(p.astype(BF16))


def _softmax_init(m_ref, acc_ref, l_ref=None):
    m_ref[...] = jnp.full_like(m_ref, NEG_INF)
    acc_ref[...] = jnp.zeros_like(acc_ref)
    if l_ref is not None:
        l_ref[...] = jnp.zeros_like(l_ref)


QK_SCALE = HEAD_DIM ** -0.5
QK_SCALE_LOG2 = QK_SCALE * math.log2(math.e)


def _norm_matmul_kernel(x_ref, g_ref, w_ref, o_ref, h_ref):
    @pl.when(pl.program_id(1) == 0)
    def _():
        h_ref[...] = _rms_rows(x_ref[...], g_ref[...]).astype(BF16)

    o_ref[...] = _dot(h_ref[...], w_ref[...])


def norm_matmul(x, g, w, tn):
    n, d = x.shape
    nout = w.shape[1]
    tm = min(512, n)
    return pl.pallas_call(
        _norm_matmul_kernel,
        grid=(n // tm, nout // tn),
        in_specs=[pl.BlockSpec((tm, d), lambda i, j: (i, 0)),
                  pl.BlockSpec((1, d), lambda i, j: (0, 0)),
                  pl.BlockSpec((d, tn), lambda i, j: (0, j))],
        out_specs=pl.BlockSpec((tm, tn), lambda i, j: (i, j)),
        out_shape=jax.ShapeDtypeStruct((n, nout), F32),
        scratch_shapes=[pltpu.VMEM((tm, d), BF16)],
        compiler_params=_cparams("parallel", "arbitrary"),
        name="norm_matmul",
    )(x, g.reshape(1, d), w)


def _matmul_res_kernel(a_ref, w_ref, r_ref, o_ref):
    o_ref[...] = r_ref[...] + _dot(a_ref[...].astype(BF16), w_ref[...])


def matmul_residual(a, w, res):
    n, k = a.shape
    d = w.shape[1]
    tm = min(512, n)
    return pl.pallas_call(
        _matmul_res_kernel,
        grid=(n // tm,),
        in_specs=[pl.BlockSpec((tm, k), lambda i: (i, 0)),
                  pl.BlockSpec((k, d), lambda i: (0, 0)),
                  pl.BlockSpec((tm, d), lambda i: (i, 0))],
        out_specs=pl.BlockSpec((tm, d), lambda i: (i, 0)),
        out_shape=jax.ShapeDtypeStruct((n, d), F32),
        compiler_params=_cparams("parallel"),
        name="matmul_residual",
    )(a, w, res)


FFN_CHUNK = 256


def _silu(x):
    return x / (1.0 + jnp.exp(-x))


def _causal_conv(u, prev1, prev2, use1, use2, cw_ref, cb_ref):
    u1 = jnp.where(use1, prev1, pltpu.roll(u, 1, 0))
    u2 = jnp.where(use2, prev2, pltpu.roll(u, 2, 0))
    return cb_ref[...] + cw_ref[0:1, :] * u2 + cw_ref[1:2, :] * u1 + cw_ref[2:3, :] * u


def _ffn_prompt_kernel(tiles_per_seq, x_ref, g_ref, wg_ref, wu_ref, cwg_ref, cwu_ref, cbg_ref, cbu_ref,
                       wd_ref, o_ref, tg_ref, tu_ref, cg_ref, cu_ref):
    i = pl.program_id(0)
    tm = x_ref.shape[0]
    nj, _, c = wg_ref.shape
    x = x_ref[...]
    h = _rms_rows(x, g_ref[...]).astype(BF16)
    row = lax.broadcasted_iota(jnp.int32, (tm, c), 0)

    @pl.when((i % tiles_per_seq) == 0)
    def _():
        cg_ref[...] = jnp.zeros_like(cg_ref)
        cu_ref[...] = jnp.zeros_like(cu_ref)

    def conv(u, prev, cw_ref, cb_ref):
        prev2 = jnp.where(row == 0, prev[0:1, :], prev[1:2, :])
        return _causal_conv(u, prev[1:2, :], prev2, row == 0, row < 2, cw_ref, cb_ref)

    acc = x
    for j in range(nj):
        ug = _dot(h, wg_ref[j])
        uu = _dot(h, wu_ref[j])
        act = _silu(conv(ug, cg_ref[j], cwg_ref.at[j], cbg_ref.at[j])) * conv(uu, cu_ref[j], cwu_ref.at[j], cbu_ref.at[j])
        acc = acc + _dot(act.astype(BF16), wd_ref[j])
        cg_ref[j] = ug[tm - 2:tm, :]
        cu_ref[j] = uu[tm - 2:tm, :]
        tg_ref[0, :, j * c:(j + 1) * c] = ug[tm - 2:tm, :]
        tu_ref[0, :, j * c:(j + 1) * c] = uu[tm - 2:tm, :]
    o_ref[...] = acc


def _resident(a):
    return pl.BlockSpec(a.shape, lambda i: (0,) * a.ndim, pipeline_mode=pl.Buffered(1))


def ffn_prompt(x, g, w_up, conv_w, conv_b, w_down, seq_len):
    n, d = x.shape
    c = FFN_CHUNK
    nj = D_FF // c
    tm = min(512, seq_len)
    tps = seq_len // tm
    chunks = lambda a: a.reshape(a.shape[0], 2, nj, c).transpose(1, 2, 0, 3)
    wg, wu = chunks(w_up)
    cwg, cwu = chunks(conv_w)
    cbg, cbu = chunks(conv_b.reshape(1, 2 * D_FF))
    wd = w_down.reshape(nj, c, d)
    g2 = g.reshape(1, d)
    consts = [g2, wg, wu, cwg, cwu, cbg, cbu, wd]
    out, tg, tu = pl.pallas_call(
        functools.partial(_ffn_prompt_kernel, tps),
        grid=(n // tm,),
        in_specs=[pl.BlockSpec((tm, d), lambda i: (i, 0))] + [_resident(a) for a in consts],
        out_specs=[pl.BlockSpec((tm, d), lambda i: (i, 0)),
                   pl.BlockSpec((1, 2, D_FF), lambda i: (i, 0, 0)),
                   pl.BlockSpec((1, 2, D_FF), lambda i: (i, 0, 0))],
        out_shape=[jax.ShapeDtypeStruct((n, d), F32),
                   jax.ShapeDtypeStruct((n // tm, 2, D_FF), F32),
                   jax.ShapeDtypeStruct((n // tm, 2, D_FF), F32)],
        scratch_shapes=[pltpu.VMEM((nj, 2, c), F32),
                        pltpu.VMEM((nj, 2, c), F32)],
        compiler_params=_cparams("arbitrary"),
        name="ffn_prompt",
    )(x, *consts)
    return out, jnp.concatenate([tg, tu], axis=-1)[tps - 1::tps]


def _ffn_sample_kernel(t_seq, x_ref, g_ref, wg_ref, wu_ref, cwg_ref, cwu_ref, cbg_ref, cbu_ref, wd_ref,
                       p1g_ref, p2g_ref, p1u_ref, p2u_ref, o_ref, ug_ref, uu_ref, h_ref):
    j = pl.program_id(0)

    @pl.when(j == 0)
    def _():
        h_ref[...] = _rms_rows(x_ref[...], g_ref[...]).astype(BF16)

    h = h_ref[...]
    ug = _dot(h, wg_ref[...])
    uu = _dot(h, wu_ref[...])
    t = lax.broadcasted_iota(jnp.int32, ug.shape, 0) & (t_seq - 1)
    cg = _causal_conv(ug, p1g_ref[...], p2g_ref[...], t == 0, t < 2, cwg_ref, cbg_ref)
    cu = _causal_conv(uu, p1u_ref[...], p2u_ref[...], t == 0, t < 2, cwu_ref, cbu_ref)
    part = _dot((_silu(cg) * cu).astype(BF16), wd_ref[...])
    ug_ref[...] = ug
    uu_ref[...] = uu

    @pl.when(j == 0)
    def _():
        o_ref[...] = x_ref[...] + part

    @pl.when(j > 0)
    def _():
        o_ref[...] += part


def ffn_sample(x, g, w_up, conv_w, conv_b, w_down, buf, t_seq):
    n, d = x.shape
    nseq = n // t_seq
    c = FFN_CHUNK
    nj = D_FF // c
    cb = conv_b.reshape(1, 2 * D_FF)
    reps = t_seq // 2
    prev1 = jnp.concatenate([buf[:, ::-1]] * reps, axis=1).reshape(n, 2 * D_FF)
    prev2 = jnp.concatenate([buf] * reps, axis=1).reshape(n, 2 * D_FF)
    full = lambda blk, off=0: pl.BlockSpec(blk, lambda j: (0, j + off))
    out, ug, uu = pl.pallas_call(
        functools.partial(_ffn_sample_kernel, t_seq),
        grid=(nj,),
        in_specs=[pl.BlockSpec((n, d), lambda j: (0, 0)),
                  pl.BlockSpec((1, d), lambda j: (0, 0)),
                  full((d, c)), full((d, c), nj),
                  full((CONV_W, c)), full((CONV_W, c), nj),
                  full((1, c)), full((1, c), nj),
                  pl.BlockSpec((c, d), lambda j: (j, 0)),
                  full((n, c)), full((n, c)), full((n, c), nj), full((n, c), nj)],
        out_specs=[pl.BlockSpec((n, d), lambda j: (0, 0)),
                   full((n, c)), full((n, c))],
        out_shape=[jax.ShapeDtypeStruct((n, d), F32),
                   jax.ShapeDtypeStruct((n, D_FF), F32),
                   jax.ShapeDtypeStruct((n, D_FF), F32)],
        scratch_shapes=[pltpu.VMEM((n, d), BF16)],
        compiler_params=_cparams("arbitrary"),
        name="ffn_sample",
    )(x, g.reshape(1, d), w_up, w_up, conv_w, conv_w, cb, cb, w_down, prev1, prev2, prev1, prev2)
    u = jnp.concatenate([ug, uu], axis=-1).reshape(nseq, t_seq, 2 * D_FF)
    return out, u[:, t_seq - (CONV_W - 1):]


def _head_norm_t(x, g):
    ms = jnp.mean(x * x, axis=0, keepdims=True)
    return x * lax.rsqrt(ms + NORM_EPS) * g


def _rope_t(x, cos, sin):
    half = HEAD_DIM // 2
    x1, x2 = x[0:half, :], x[half:HEAD_DIM, :]
    return jnp.concatenate([x1 * cos - x2 * sin, x1 * sin + x2 * cos], axis=0)


def _nsa_project_kernel(x_ref, g_ref, wq_ref, wkv_ref, wg_ref, cos_ref, sin_ref, cost_ref, sint_ref, gq_ref, gk_ref,
                        so_ref, qn_ref, qr_ref, gate_ref, rows_ref, win_ref, ksel_ref, kwin_ref):
    hn = _rms_rows(x_ref[...], g_ref[...]).astype(BF16)
    cos, sin = cos_ref[...], sin_ref[...]
    so = so_ref[...]
    q_all = _dot(hn, wq_ref[...])
    for s in range(NSA_NQ // LANES):
        sl = slice(s * LANES, (s + 1) * LANES)
        q = _seg64_norm(q_all[:, sl], gq_ref[...], so)
        qn_ref[:, sl] = (q * QK_SCALE).astype(BF16)
        qr_ref[:, sl] = (_rope_slab(q, cos, sin) * QK_SCALE_LOG2).astype(BF16)
    gate_ref[...] = 1.0 / (1.0 + jnp.exp(-_dot(hn, wg_ref[...])))
    kvt = _dot_nt(wkv_ref[...], hn)
    gw = NSA_GW
    cost, sint = cost_ref[...], sint_ref[...]
    rows_ref[0, 0:2 * gw, :] = kvt[0:2 * gw, :]
    rows_ref[0, 3 * gw:4 * gw, :] = kvt[3 * gw:4 * gw, :]
    win_ref[0, gw:2 * gw, :] = kvt[5 * gw:6 * gw, :]
    for g in range(NSA_GROUPS):
        hs = slice(g * HEAD_DIM, (g + 1) * HEAD_DIM)
        rows_ref[0, 2 * gw + g * HEAD_DIM:2 * gw + (g + 1) * HEAD_DIM, :] = _rope_t(
            _head_norm_t(kvt[2 * gw + g * HEAD_DIM:2 * gw + (g + 1) * HEAD_DIM, :], gk_ref[0]), cost, sint)
        win_ref[0, hs, :] = _rope_t(
            _head_norm_t(kvt[4 * gw + g * HEAD_DIM:4 * gw + (g + 1) * HEAD_DIM, :], gk_ref[1]), cost, sint)
    tm = kvt.shape[1]
    for g in range(NSA_GROUPS):
        hs = slice(g * HEAD_DIM, (g + 1) * HEAD_DIM)
        ksel_ref[0, 0, g, 0:HEAD_DIM, :] = rows_ref[0, 2 * gw + g * HEAD_DIM:2 * gw + (g + 1) * HEAD_DIM, :].astype(BF16)
        ksel_ref[0, 0, g, HEAD_DIM:2 * HEAD_DIM, :] = jnp.zeros((HEAD_DIM, tm), BF16)
        ksel_ref[0, 0, g, 2 * HEAD_DIM:3 * HEAD_DIM, :] = kvt[3 * gw + g * HEAD_DIM:3 * gw + (g + 1) * HEAD_DIM, :].astype(BF16)
        ksel_ref[0, 0, g, 3 * HEAD_DIM:4 * HEAD_DIM, :] = jnp.ones((HEAD_DIM, tm), BF16)
    wb = win_ref[0].astype(BF16)
    for u in range(kwin_ref.shape[1]):
        kwin_ref[0, u] = wb[:, u * LANES:(u + 1) * LANES]


def nsa_project(x, norm_g, w_in, qk_g, pos, nseq):
    n, d = x.shape
    t = n // nseq
    tm = min(SEL_TK, t)
    nt = t // tm
    wq = w_in[:, :NSA_NQ].astype(BF16)
    wkv_t = w_in[:, NSA_NQ:NSA_NQ + NSA_NKV].T.astype(BF16)
    wg = jnp.pad(w_in[:, NSA_NQ + NSA_NKV:], ((0, 0), (0, LANES - NSA_NGATE))).astype(BF16)
    cos, sin = _rope_angles(pos)
    cos_q, sin_q = jnp.tile(cos, (1, 4)), jnp.tile(jnp.concatenate([-sin, sin], axis=1), (1, 2))
    gq = jnp.tile(qk_g[0:1], (1, 2))
    gk = qk_g[1:3].reshape(2, HEAD_DIM, 1)
    so = _seg_ones()
    const = lambda a: pl.BlockSpec(a.shape, lambda b, i: (0,) * a.ndim)
    row = lambda w: pl.BlockSpec((tm, w), lambda b, i: (b * nt + i, 0))
    return pl.pallas_call(
        _nsa_project_kernel,
        grid=(nseq, nt),
        in_specs=[row(d), const(norm_g.reshape(1, d)), const(wq), const(wkv_t), const(wg),
                  pl.BlockSpec((tm, LANES), lambda b, i: (i, 0)), pl.BlockSpec((tm, LANES), lambda b, i: (i, 0)),
                  pl.BlockSpec((HEAD_DIM // 2, tm), lambda b, i: (0, i)),
                  pl.BlockSpec((HEAD_DIM // 2, tm), lambda b, i: (0, i)),
                  const(gq), const(gk), const(so)],
        out_specs=[row(NSA_NQ), row(NSA_NQ), row(LANES),
                   pl.BlockSpec((1, 4 * NSA_GW, tm), lambda b, i: (b, 0, i)),
                   pl.BlockSpec((1, 2 * NSA_GW, tm), lambda b, i: (b, 0, i)),
                   pl.BlockSpec((1, 1, NSA_GROUPS, 4 * HEAD_DIM, tm), lambda b, i: (b, i, 0, 0, 0)),
                   pl.BlockSpec((1, tm // LANES, 2 * NSA_GW, LANES), lambda b, i: (b, i, 0, 0))],
        out_shape=[jax.ShapeDtypeStruct((n, NSA_NQ), BF16), jax.ShapeDtypeStruct((n, NSA_NQ), BF16),
                   jax.ShapeDtypeStruct((n, LANES), F32),
                   jax.ShapeDtypeStruct((nseq, 4 * NSA_GW, t), F32),
                   jax.ShapeDtypeStruct((nseq, 2 * NSA_GW, t), F32),
                   jax.ShapeDtypeStruct((nseq, nt, NSA_GROUPS, 4 * HEAD_DIM, tm), BF16),
                   jax.ShapeDtypeStruct((nseq, t // LANES, 2 * NSA_GW, LANES), BF16)],
        compiler_params=_cparams("parallel", "parallel"),
        name="nsa_project",
    )(x, norm_g.reshape(1, d), wq, wkv_t, wg, cos_q, sin_q, cos.T, sin.T, gq, gk, so)


def _nsa_compress_kernel(npp, pt_ref, *refs):
    page_refs = refs[:npp]
    pe_ref, w_ref, g_ref, so_ref, kc_ref, vc_ref, seqk_ref, seqv_ref, acc_ref = refs[npp:]
    p = pl.program_id(1)
    gw = NSA_GW
    pitch = CMP_PITCH
    for u in range(npp):
        for g in range(NSA_GROUPS):
            lo = pl.multiple_of(((p * npp + u) * NSA_GROUPS + g) * pitch, 8)
            seqk_ref[pl.ds(lo, HEAD_DIM), :] = page_refs[u][g * HEAD_DIM:(g + 1) * HEAD_DIM, :]
            seqv_ref[pl.ds(lo, HEAD_DIM), :] = page_refs[u][gw + g * HEAD_DIM:gw + (g + 1) * HEAD_DIM, :]

    @pl.when(p == pl.num_programs(1) - 1)
    def _():
        nrow = seqk_ref.shape[0] // pitch
        acc_ref[...] = jnp.zeros_like(acc_ref)
        for dd in range(HEAD_DIM):
            a = jnp.concatenate([seqk_ref[pl.ds(dd, nrow, stride=pitch), :],
                                 seqv_ref[pl.ds(dd, nrow, stride=pitch), :]], axis=1) + pe_ref[dd:dd + 1, :]
            acc_ref[...] += _dot(a.astype(BF16), w_ref[dd])
        kc_ref[0] = _seg64_norm(acc_ref[:, 0:LANES], g_ref[...], so_ref[...])
        vc_ref[0] = acc_ref[:, LANES:2 * LANES]


def nsa_compress(pages_t, page_index, nseq, npg, pe, w_phi, g_c):
    npp = min(NSA_PAGES_PER_STEP, npg)
    nrow = npg * NSA_GROUPS
    pe_t = jnp.concatenate([jnp.tile(pe[0].T, (1, 2)), jnp.tile(pe[1].T, (1, 2))], axis=1)
    eye4 = jnp.eye(4, dtype=F32)
    w4 = jnp.stack([w_phi[0], w_phi[0], w_phi[1], w_phi[1]])
    w = jnp.einsum('ab,alde->dalbe', eye4, w4).reshape(HEAD_DIM, 2 * LANES, 2 * LANES).astype(BF16)
    g2 = jnp.tile(g_c.reshape(1, HEAD_DIM), (1, 2))
    so = _seg_ones()
    const = lambda a: pl.BlockSpec(a.shape, lambda b, p, pt: (0,) * a.ndim, pipeline_mode=pl.Buffered(1))
    page = lambda u: pl.BlockSpec((None, 2 * NSA_GW, PAGE_SIZE), lambda b, p, pt: page_index(b, p * npp + u, pt))
    out = pl.BlockSpec((1, nrow, LANES), lambda b, p, pt: (b, 0, 0))
    table = page_index.table
    kc, vc = pl.pallas_call(
        functools.partial(_nsa_compress_kernel, npp),
        grid_spec=pltpu.PrefetchScalarGridSpec(
            num_scalar_prefetch=1,
            grid=(nseq, npg // npp),
            in_specs=[page(u) for u in range(npp)] + [const(pe_t), const(w), const(g2), const(so)],
            out_specs=[out, out],
            scratch_shapes=[pltpu.VMEM((nrow * CMP_PITCH, PAGE_SIZE), F32), pltpu.VMEM((nrow * CMP_PITCH, PAGE_SIZE), F32),
                            pltpu.VMEM((nrow, 2 * LANES), F32)]),
        out_shape=[jax.ShapeDtypeStruct((nseq, nrow, LANES), F32)] * 2,
        compiler_params=_cparams("arbitrary", "arbitrary"),
        name="nsa_compress",
    )(table, *([pages_t] * npp), pe_t, w, g2, so)
    fix = lambda a: a.reshape(nseq, npg, NSA_GROUPS, 2, HEAD_DIM).transpose(0, 2, 1, 3, 4).reshape(
        nseq, NSA_GROUPS, 2 * npg, HEAD_DIM)
    return fix(kc), fix(vc)


class _PageIndex:
    def __init__(self, table, fn):
        self.table = table
        self._fn = fn

    def __call__(self, b, p, pt):
        return self._fn(b, p, pt)


def _stack_heads(q):
    head = lax.broadcasted_iota(jnp.int32, q.shape, 1) >> 6
    qf = q.astype(F32)
    return jnp.concatenate([jnp.where(head == r, qf, 0.0) for r in range(NSA_REP)], axis=0).astype(BF16)


def _unstack_heads(o4, tq):
    head = lax.broadcasted_iota(jnp.int32, (tq, o4.shape[1]), 1) >> 6
    out = jnp.zeros((tq, o4.shape[1]), F32)
    for r in range(NSA_REP):
        out = jnp.where(head == r, o4[r * tq:(r + 1) * tq, :], out)
    return out


def _tile_rows4(x):
    return jnp.concatenate([x] * NSA_REP, axis=0)


def _nsa_cmp_kernel(q0, ns_rows, q_ref, kc_ref, vc_ref, oc_ref, sel_ref):
    i = pl.program_id(2)
    tq = q_ref.shape[1]
    nc = kc_ref.shape[2]
    qst = _stack_heads(q_ref[0])
    kc, vc = kc_ref[0, 0], vc_ref[0, 0]
    base = q0 + i * tq
    qpos = base + (lax.broadcasted_iota(jnp.int32, (NSA_REP * tq, nc), 0) & (tq - 1))
    blk_end = (lax.broadcasted_iota(jnp.int32, (NSA_REP * tq, nc), 1) + 1) * NSA_BLOCK - 1
    ok = blk_end <= qpos
    s = jnp.where(ok, _dot_nt(qst, kc), NEG_INF)
    e = jnp.exp(s - jnp.max(s, axis=-1, keepdims=True))
    p = jnp.where(ok, e / jnp.sum(e, axis=-1, keepdims=True), 0.0)
    oc_ref[0] = _unstack_heads(_dot(p.astype(BF16), vc), tq)
    qpos_t = base + (lax.broadcasted_iota(jnp.int32, (nc, NSA_REP * tq), 1) & (tq - 1))
    blk_end_t = (lax.broadcasted_iota(jnp.int32, (nc, NSA_REP * tq), 0) + 1) * NSA_BLOCK - 1
    ok_t = blk_end_t <= qpos_t
    st = jnp.where(ok_t, _dot_nt(kc, qst), NEG_INF)
    et = jnp.exp(st - jnp.max(st, axis=0, keepdims=True))
    pt = jnp.where(ok_t, et / jnp.sum(et, axis=0, keepdims=True), 0.0)
    imp = pt[:, 0:tq]
    for r in range(1, NSA_REP):
        imp = imp + pt[:, r * tq:(r + 1) * tq]
    if ns_rows > nc:
        imp = jnp.concatenate([imp, jnp.zeros((ns_rows - nc, tq), F32)], axis=0)
    blk = lax.broadcasted_iota(jnp.int32, (ns_rows, tq), 0)
    cur = (base + lax.broadcasted_iota(jnp.int32, (ns_rows, tq), 1)) >> 6
    forced = (blk == 0) | (blk == cur) | (blk == cur - 1)
    imp = jnp.where(forced, NSA_FORCED, imp)
    imp = jnp.where(blk <= cur, imp, NEG_INF)
    taken = jnp.float32(-3e38)
    blk_f = blk.astype(F32)

    def pick(_, carry):
        imp, sel = carry
        m = jnp.max(imp, axis=0, keepdims=True)
        first = jnp.min(jnp.where(imp == m, blk_f, float(ns_rows)), axis=0, keepdims=True)
        hit = blk_f == first
        return jnp.where(hit, taken, imp), jnp.where(hit, 0.0, sel)

    _, sel = lax.fori_loop(0, NSA_TOPN, pick, (imp, jnp.full((ns_rows, tq), NEG_INF, F32)))
    sel_ref[0, 0] = sel.T.astype(BF16)


def _tile_lanes4(x):
    return jnp.tile(x, (1, 1, 1, NSA_REP)).astype(BF16)


def nsa_cmp_select(qn, kc, vc, q0, ns_rows):
    b, t, _ = qn.shape
    tq = min(256, t)
    nc = kc.shape[2]
    kv = pl.BlockSpec((1, 1, nc, 256), lambda bi, g, i: (bi, g, 0, 0))
    return pl.pallas_call(
        functools.partial(_nsa_cmp_kernel, q0, ns_rows),
        grid=(b, NSA_GROUPS, t // tq),
        in_specs=[pl.BlockSpec((1, tq, 256), lambda bi, g, i: (bi, i, g)), kv, kv],
        out_specs=[pl.BlockSpec((1, tq, 256), lambda bi, g, i: (bi, i, g)),
                   pl.BlockSpec((1, 1, tq, ns_rows), lambda bi, g, i: (bi, g, i, 0))],
        out_shape=[jax.ShapeDtypeStruct((b, t, 1024), F32),
                   jax.ShapeDtypeStruct((b, NSA_GROUPS, t, ns_rows), BF16)],
        compiler_params=_cparams("parallel", "parallel", "parallel"),
        name="nsa_cmp_select",
    )(qn, _tile_lanes4(kc), _tile_lanes4(vc))


WIN_SUB = 2


def _nsa_window_kernel(nsub, q_ref, k_ref, v_ref, o_ref):
    i = pl.program_id(2)
    tq = q_ref.shape[1] // nsub
    ntile = (NSA_WINDOW + nsub * tq) // LANES
    span = ntile * LANES
    j0 = jnp.maximum(i * nsub * tq - NSA_WINDOW, 0) // LANES
    kt4 = _tile_rows4(jnp.concatenate([k_ref[0, j0 + u] for u in range(ntile)], axis=1))
    vt4 = _tile_rows4(jnp.concatenate([v_ref[0, j0 + u] for u in range(ntile)], axis=1))
    kpos = j0 * LANES + lax.broadcasted_iota(jnp.int32, (NSA_REP * tq, span), 1)
    for a in range(nsub):
        qst = _stack_heads(q_ref[0, a * tq:(a + 1) * tq, :])
        qpos = (i * nsub + a) * tq + (lax.broadcasted_iota(jnp.int32, (NSA_REP * tq, span), 0) & (tq - 1))
        ok = (kpos <= qpos) & (kpos > qpos - NSA_WINDOW)
        s = jnp.where(ok, _dot(qst, kt4), NEG_INF)
        e = jnp.exp2(s - jnp.max(s, axis=-1, keepdims=True))
        p = e / jnp.sum(e, axis=-1, keepdims=True)
        o_ref[0, a * tq:(a + 1) * tq, :] = _unstack_heads(_dot_nt(p.astype(BF16), vt4), tq)


def nsa_window_prompt(qr, kwin):
    b, t, _ = qr.shape
    tq = WIN_SUB * 128
    nt = kwin.shape[1]
    kv = lambda off: pl.BlockSpec((1, nt, HEAD_DIM, LANES), lambda bi, g, i: (bi, 0, off + g, 0))
    return pl.pallas_call(
        functools.partial(_nsa_window_kernel, WIN_SUB),
        grid=(b, NSA_GROUPS, t // tq),
        in_specs=[pl.BlockSpec((1, tq, 256), lambda bi, g, i: (bi, i, g)), kv(0), kv(NSA_GROUPS)],
        out_specs=pl.BlockSpec((1, tq, 256), lambda bi, g, i: (bi, i, g)),
        out_shape=jax.ShapeDtypeStruct((b, t, 1024), F32),
        compiler_params=_cparams("parallel", "parallel", "arbitrary"),
        name="nsa_window",
    )(qr, kwin, kwin)


SEL_SUB = 4


def _nsa_selected_kernel(q_ref, sel_ref, kv_ref, e_ref, o_ref, m_ref, acc_ref):
    i = pl.program_id(2)
    nsub = m_ref.shape[0]
    tq = q_ref.shape[1] // nsub
    tk = kv_ref.shape[4]
    rows = NSA_REP * tq
    lane = lax.broadcasted_iota(jnp.int32, (tq, LANES), 1)

    def lhs(a):
        q = q_ref[0, a * tq:(a + 1) * tq, :].astype(F32)
        sel = sel_ref[0, 0, a * tq:(a + 1) * tq, :].astype(F32)
        blocks = []
        for r in range(NSA_REP):
            slab = q[:, (r // 2) * LANES:(r // 2 + 1) * LANES]
            slab = pltpu.roll(slab, HEAD_DIM, 1) if r % 2 else slab
            blocks.append(jnp.concatenate([jnp.where(lane < HEAD_DIM, slab, 0.0), sel], axis=1))
        return jnp.concatenate(blocks, axis=0).astype(BF16)

    qs = [lhs(a) for a in range(nsub)]
    _softmax_init(m_ref, acc_ref)

    def tile(j, causal):
        k_aug = jnp.concatenate([kv_ref[0, j, 0, 0:2 * HEAD_DIM, :], e_ref[j]], axis=0)
        v_aug = kv_ref[0, j, 0, 2 * HEAD_DIM:4 * HEAD_DIM, :]
        for a in range(nsub):
            s = _dot(qs[a], k_aug)
            if causal:
                qpos = (i * nsub + a) * tq + (lax.broadcasted_iota(jnp.int32, (rows, tk), 0) & (tq - 1))
                kpos = j * tk + lax.broadcasted_iota(jnp.int32, (rows, tk), 1)
                s = jnp.where(kpos <= qpos, s, NEG_INF)
            _softmax_update_vsum(s, lambda p: _dot_nt(p, v_aug), m_ref.at[a], acc_ref.at[a])

    nfull = (i * nsub * tq) // tk

    def body(j, c):
        tile(j, False)
        return c

    lax.fori_loop(0, nfull, body, 0)
    tile(nfull, True)
    for a in range(nsub):
        acc = acc_ref[a]
        o4 = acc / pltpu.roll(acc, HEAD_DIM, 1)
        for half in range(NSA_REP // 2):
            even = o4[(2 * half) * tq:(2 * half + 1) * tq, :]
            odd = pltpu.roll(o4[(2 * half + 1) * tq:(2 * half + 2) * tq, :], HEAD_DIM, 1)
            o_ref[0, a * tq:(a + 1) * tq, half * LANES:(half + 1) * LANES] = jnp.where(lane < HEAD_DIM, even, odd)


def nsa_selected_prompt(qr, sel, ksel):
    b, t, _ = qr.shape
    sub = 128
    tq = SEL_SUB * sub
    nt, _, _, tk = ksel.shape[1:]
    assert tk % tq == 0
    nblk = sel.shape[-1]
    e3 = (jnp.arange(nblk)[None, :, None] == (jnp.arange(nt)[:, None, None] * tk + jnp.arange(tk)[None, None, :]) // NSA_BLOCK
          ).astype(BF16)
    return pl.pallas_call(
        _nsa_selected_kernel,
        grid=(b, NSA_GROUPS, t // tq),
        in_specs=[pl.BlockSpec((1, tq, 256), lambda bi, g, i: (bi, i, g)),
                  pl.BlockSpec((1, 1, tq, nblk), lambda bi, g, i: (bi, g, i, 0)),
                  pl.BlockSpec((1, nt, 1, 4 * HEAD_DIM, tk), lambda bi, g, i: (bi, 0, g, 0, 0)),
                  pl.BlockSpec(e3.shape, lambda bi, g, i: (0, 0, 0))],
        out_specs=pl.BlockSpec((1, tq, 256), lambda bi, g, i: (bi, i, g)),
        out_shape=jax.ShapeDtypeStruct((b, t, 1024), F32),
        scratch_shapes=[pltpu.VMEM((SEL_SUB, NSA_REP * sub, LANES), F32),
                        pltpu.VMEM((SEL_SUB, NSA_REP * sub, LANES), F32)],
        compiler_params=_cparams("parallel", "parallel", "arbitrary"),
        name="nsa_selected",
    )(qr, sel, ksel, e3)


def _nsa_out_kernel(oc_ref, os_ref, ow_ref, gate_ref, ex_ref, w_ref, r_ref, o_ref):
    gate = gate_ref[...]
    comb = (_split_dot(gate, ex_ref[0]) * oc_ref[...] + _split_dot(gate, ex_ref[1]) * os_ref[...]
            + _split_dot(gate, ex_ref[2]) * ow_ref[...])
    o_ref[...] = r_ref[...] + _dot(comb.astype(BF16), w_ref[...])


def nsa_out(oc, os_, ow, gates, w_o, res):
    n, d = res.shape
    tm = min(256, n)
    lane = jnp.arange(1024) // HEAD_DIM
    ex = jnp.stack([(jnp.arange(LANES)[:, None] == lane[None, :] * 3 + k) for k in range(3)]).astype(BF16)
    row = lambda w: pl.BlockSpec((tm, w), lambda i: (i, 0))
    return pl.pallas_call(
        _nsa_out_kernel,
        grid=(n // tm,),
        in_specs=[row(1024), row(1024), row(1024), row(LANES),
                  pl.BlockSpec(ex.shape, lambda i: (0, 0, 0)),
                  pl.BlockSpec(w_o.shape, lambda i: (0, 0)), row(d)],
        out_specs=row(d),
        out_shape=jax.ShapeDtypeStruct((n, d), F32),
        compiler_params=_cparams("parallel"),
        name="nsa_out",
    )(oc, os_, ow, gates, ex, w_o, res)


def _paged_attn_kernel(npp, kv_t, pt_ref, q_ref, *refs):
    k_refs, v_refs = refs[:npp], refs[npp:2 * npp]
    kn_ref, vn_ref, bp_ref, bn_ref, o_ref, m_ref, l_ref, acc_ref = refs[2 * npp:]
    p = pl.program_id(1)
    last = pl.num_programs(1) - 1

    @pl.when(p == 0)
    def _():
        _softmax_init(m_ref, acc_ref, l_ref)

    def step(k, v, bias):
        if kv_t:
            s = _dot(q_ref[0], k) + bias
            _softmax_update(s, lambda e: _dot_nt(e, v), m_ref, l_ref, acc_ref)
        else:
            s = _dot_nt(q_ref[0], k) + bias
            _softmax_update(s, lambda e: _dot(e, v), m_ref, l_ref, acc_ref)

    def load(r):
        x = r[...]
        return x if kv_t else x.reshape(-1, x.shape[-1])

    @pl.when(p < last)
    def _():
        axis = 1 if kv_t else 0
        step(jnp.concatenate([load(r) for r in k_refs], axis=axis).astype(BF16),
             jnp.concatenate([load(r) for r in v_refs], axis=axis).astype(BF16), bp_ref[0])

    @pl.when(p == last)
    def _():
        step(kn_ref[0], vn_ref[0], bn_ref[0])
        o_ref[0] = _softmax_result(l_ref, acc_ref)


def paged_attention(q, pages, table, npg, npp, k_spec, v_spec, kv_t, k_new, v_new, bias_past, bias_new, past_per_step):
    nseq, rows, _ = q.shape
    lv = acc_w = v_new.shape[1] if kv_t else v_new.shape[2]
    seq = lambda a: pl.BlockSpec((1,) + a.shape[1:], lambda b, p, pt: (b, 0, 0))
    wpast = bias_past.shape[-1] if not past_per_step else bias_past.shape[-1] // (npg // npp)
    bp_spec = pl.BlockSpec((1, rows, wpast), lambda b, p, pt: (
        b if bias_past.shape[0] > 1 else 0, 0, jnp.minimum(p, npg // npp - 1) if past_per_step else 0))
    bn_spec = pl.BlockSpec((1,) + bias_new.shape[1:], lambda b, p, pt: (b if bias_new.shape[0] > 1 else 0, 0, 0))
    return pl.pallas_call(
        functools.partial(_paged_attn_kernel, npp, kv_t),
        grid_spec=pltpu.PrefetchScalarGridSpec(
            num_scalar_prefetch=1,
            grid=(nseq, npg // npp + 1),
            in_specs=[seq(q)] + [k_spec(u) for u in range(npp)] + [v_spec(u) for u in range(npp)]
                     + [seq(k_new), seq(v_new), bp_spec, bn_spec],
            out_specs=pl.BlockSpec((1, rows, acc_w), lambda b, p, pt: (b, 0, 0)),
            scratch_shapes=[pltpu.VMEM((rows, LANES), F32), pltpu.VMEM((rows, LANES), F32),
                            pltpu.VMEM((rows, acc_w), F32)]),
        out_shape=jax.ShapeDtypeStruct((nseq, rows, lv), F32),
        compiler_params=_cparams("arbitrary", "arbitrary"),
        name="paged_attention",
    )(table, q, *([pages] * (2 * npp)), k_new, v_new, bias_past, bias_new)


def _diff_post_kernel(p_ref, cos_ref, sin_ref, g_ref, so_ref, q_ref, kvf_ref, kvb_ref):
    cos, sin = cos_ref[...], sin_ref[...]
    so = so_ref[...]
    for s in range(DIFF_WIDTH // LANES):
        sl = slice(s * LANES, (s + 1) * LANES)
        q = _rope_slab(_seg64_norm(p_ref[:, sl], g_ref[0:1, :], so), cos, sin)
        q_ref[:, sl] = (q * QK_SCALE_LOG2).astype(BF16)
        k = _rope_slab(_seg64_norm(p_ref[:, DIFF_WIDTH + s * LANES:DIFF_WIDTH + (s + 1) * LANES], g_ref[1:2, :], so),
                       cos, sin)
        kvf_ref[:, sl] = k
        kvb_ref[:, sl] = k.astype(BF16)
    v = p_ref[:, 2 * DIFF_WIDTH:3 * DIFF_WIDTH]
    kvf_ref[:, DIFF_WIDTH:] = v
    kvb_ref[:, DIFF_WIDTH:] = v.astype(BF16)


def diff_post(proj, cos, sin, qk_g):
    n = proj.shape[0]
    tm = min(256, n)
    g2 = jnp.tile(qk_g, (1, 2))
    so = _seg_ones()
    row = lambda w: pl.BlockSpec((tm, w), lambda i: (i, 0))
    const = lambda a: pl.BlockSpec(a.shape, lambda i: (0, 0))
    return pl.pallas_call(
        _diff_post_kernel,
        grid=(n // tm,),
        in_specs=[row(3 * DIFF_WIDTH), row(LANES), row(LANES), const(g2), const(so)],
        out_specs=[row(DIFF_WIDTH), row(2 * DIFF_WIDTH), row(2 * DIFF_WIDTH)],
        out_shape=[jax.ShapeDtypeStruct((n, DIFF_WIDTH), BF16), jax.ShapeDtypeStruct((n, 2 * DIFF_WIDTH), F32),
                   jax.ShapeDtypeStruct((n, 2 * DIFF_WIDTH), BF16)],
        compiler_params=_cparams("parallel"),
        name="diff_post",
    )(proj, cos, sin, g2, so)


def _diff_lambda(lam_ref, lam_init):
    lf = lam_ref[...]
    a = jnp.sum(lf[0:1, :] * lf[1:2, :], axis=-1, keepdims=True)
    b = jnp.sum(lf[2:3, :] * lf[3:4, :], axis=-1, keepdims=True)
    return jnp.exp(a) - jnp.exp(b) + lam_init


DIFF_SUB = 4
DIFF_SUB_TQ = 256
DIFF_TK = 1024


def _diff_flash_kernel(lam_init, tk, q_ref, k_ref, v_ref, lam_ref, o_ref, m_ref, acc_ref):
    i = pl.program_id(2)
    nsub = m_ref.shape[0]
    tq = q_ref.shape[1] // nsub
    rows = 2 * tq

    def stack_components(q):
        comp = lax.broadcasted_iota(jnp.int32, q.shape, 1) >> 6
        return jnp.concatenate([jnp.where(comp == c, q, 0.0) for c in range(2)], axis=0).astype(BF16)

    qst = [stack_components(q_ref[0, a * tq:(a + 1) * tq, :].astype(F32)) for a in range(nsub)]
    _softmax_init(m_ref, acc_ref)
    ones = jnp.ones((tk, LANES), BF16)

    def tile(j, causal):
        lo = pl.multiple_of(j * tk, tk)
        k = k_ref[0, pl.ds(lo, tk), :]
        v_aug = jnp.concatenate([v_ref[0, pl.ds(lo, tk), :], ones], axis=1)
        for a in range(nsub):
            s = _dot_nt(qst[a], k)
            if causal:
                qpos = (i * nsub + a) * tq + (lax.broadcasted_iota(jnp.int32, (rows, tk), 0) & (tq - 1))
                kpos = j * tk + lax.broadcasted_iota(jnp.int32, (rows, tk), 1)
                s = jnp.where(kpos <= qpos, s, NEG_INF)
            _softmax_update_vsum(s, lambda p: _dot(p, v_aug), m_ref.at[a], acc_ref.at[a])

    nfull = (i * nsub * tq) // tk

    def body(j, c):
        tile(j, False)
        return c

    lax.fori_loop(0, nfull, body, 0)
    tile(nfull, True)
    lam = _diff_lambda(lam_ref, lam_init)
    for a in range(nsub):
        acc = acc_ref[a]
        o = acc[:, 0:LANES] / acc[:, LANES:2 * LANES]
        o_ref[0, a * tq:(a + 1) * tq, :] = o[0:tq, :] - lam * o[tq:rows, :]


def diff_flash_prompt(q, kvb, lam, lam_init):
    b, t, _ = q.shape
    sub = DIFF_SUB_TQ
    tq = DIFF_SUB * sub
    tk = DIFF_TK
    assert tk % tq == 0 and t % tk == 0
    return pl.pallas_call(
        functools.partial(_diff_flash_kernel, lam_init, tk),
        grid=(b, DIFF_HEADS, t // tq),
        in_specs=[pl.BlockSpec((1, tq, LANES), lambda bi, h, i: (bi, i, h)),
                  pl.BlockSpec((1, t, LANES), lambda bi, h, i: (bi, 0, h)),
                  pl.BlockSpec((1, t, LANES), lambda bi, h, i: (bi, 0, DIFF_HEADS + h)),
                  pl.BlockSpec(lam.shape, lambda bi, h, i: (0, 0))],
        out_specs=pl.BlockSpec((1, tq, LANES), lambda bi, h, i: (bi, i, h)),
        out_shape=jax.ShapeDtypeStruct((b, t, DIFF_WIDTH), F32),
        scratch_shapes=[pltpu.VMEM((DIFF_SUB, 2 * sub, LANES), F32), pltpu.VMEM((DIFF_SUB, 2 * sub, 2 * LANES), F32)],
        compiler_params=_cparams("parallel", "parallel", "arbitrary"),
        name="diff_flash",
    )(q, kvb, kvb, lam)


def _diff_out_kernel(lam_init, two, *refs):
    if two:
        o0_ref, o1_ref, lam_ref, g_ref, w_ref, r_ref, out_ref, h_ref = refs
        o = o0_ref[...] - _diff_lambda(lam_ref, lam_init) * o1_ref[...]
    else:
        o0_ref, g_ref, w_ref, r_ref, out_ref, h_ref = refs
        o = o0_ref[...]
    for s in range(DIFF_HEADS):
        sl = slice(s * LANES, (s + 1) * LANES)
        h_ref[:, sl] = (_rms_rows(o[:, sl], g_ref[...]) * (1.0 - lam_init)).astype(BF16)
    out_ref[...] = r_ref[...] + _dot(h_ref[...], w_ref[...])


def diff_out(o, sub_g, w_o, res, lam_init, o1=None, lam=None):
    n, d = res.shape
    tm = min(512, n)
    two = o1 is not None
    row = lambda w: pl.BlockSpec((tm, w), lambda i: (i, 0))
    const = lambda a: pl.BlockSpec(a.shape, lambda i: (0, 0))
    g = sub_g.reshape(1, LANES)
    ins = [o, o1, lam, g, w_o, res] if two else [o, g, w_o, res]
    specs = ([row(DIFF_WIDTH), row(DIFF_WIDTH), const(lam)] if two else [row(DIFF_WIDTH)]) + [const(g), const(w_o), row(d)]
    return pl.pallas_call(
        functools.partial(_diff_out_kernel, lam_init, two),
        grid=(n // tm,),
        in_specs=specs,
        out_specs=row(d),
        out_shape=jax.ShapeDtypeStruct((n, d), F32),
        scratch_shapes=[pltpu.VMEM((tm, DIFF_WIDTH), BF16)],
        compiler_params=_cparams("parallel"),
        name="diff_out",
    )(*ins)


def _gla_kernel(chunk, nchunk, t_valid, p_ref, s0_ref, wa_ref, ba_ref, g_ref, tri_ref, y_ref, sfin_ref, st_ref):
    step = pl.program_id(1)

    @pl.when(step == 0)
    def _():
        st_ref[...] = s0_ref[0]

    nk = GLA_HEADS * GLA_DK
    nv = GLA_HEADS * GLA_DV
    tri = tri_ref[...]
    causal = lax.broadcasted_iota(jnp.int32, (chunk, chunk), 0) >= lax.broadcasted_iota(jnp.int32, (chunk, chunk), 1)
    for ci in range(nchunk):
        r0 = ci * chunk
        rows = slice(r0, r0 + chunk)
        a1 = p_ref[0, rows, GLA_MAIN:GLA_MAIN + LANES].astype(BF16)
        z = _dot(a1, wa_ref[...]) + ba_ref[...]
        log_a = (jnp.minimum(z, 0.0) - jnp.log(1.0 + jnp.exp(-jnp.abs(z)))) * (1.0 / GLA_TAU)
        if t_valid < chunk:
            log_a = jnp.where(lax.broadcasted_iota(jnp.int32, log_a.shape, 0) < t_valid, log_a, 0.0)
        cum = _split_dot_left(tri, log_a)
        for h in range(GLA_HEADS):
            ksl = slice(h * GLA_DK, (h + 1) * GLA_DK)
            vsl = slice(h * GLA_DV, (h + 1) * GLA_DV)
            q = p_ref[0, rows, h * GLA_DK:(h + 1) * GLA_DK] * (GLA_DK ** -0.5)
            k = p_ref[0, rows, nk + h * GLA_DK:nk + (h + 1) * GLA_DK]
            v = p_ref[0, rows, 2 * nk + h * GLA_DV:2 * nk + (h + 1) * GLA_DV]
            r = p_ref[0, rows, 2 * nk + nv + h * GLA_DV:2 * nk + nv + (h + 1) * GLA_DV]
            cm = cum[:, ksl]
            last = cm[chunk - 1:chunk, :]
            qe = (q * jnp.exp(cm)).astype(BF16)
            ke = (k * jnp.exp(-cm)).astype(BF16)
            vb = v.astype(BF16)
            att = jnp.where(causal, _dot_nt(qe, ke), 0.0)
            st = st_ref[h]
            o = _dot_nt(qe, st.astype(BF16)) + _dot(att.astype(BF16), vb)
            kd = (k * jnp.exp(last - cm)).astype(BF16)
            st_ref[h] = st * jnp.exp(last) + _dot(v.T.astype(BF16), kd)
            y_ref[0, rows, vsl] = _rms_rows(o, g_ref[...]) * _silu(r)

    @pl.when(step == pl.num_programs(1) - 1)
    def _():
        sfin_ref[0] = st_ref[...]


def gla_core(proj, s0_t, w_a2, b_a, out_g, chunk, nchunk, t_valid):
    b, t, _ = proj.shape
    rows = chunk * nchunk
    wa = jnp.zeros((LANES, GLA_HEADS * GLA_DK), F32).at[:GLA_RANK].set(w_a2).astype(BF16)
    ba = b_a.reshape(1, -1)
    g = out_g.reshape(1, GLA_DV)
    tri = (jnp.arange(chunk)[:, None] >= jnp.arange(chunk)[None, :]).astype(BF16)
    const = lambda a: pl.BlockSpec(a.shape, lambda bi, s: (0,) * a.ndim)
    st_spec = pl.BlockSpec((1, GLA_HEADS, GLA_DV, GLA_DK), lambda bi, s: (bi, 0, 0, 0))
    return pl.pallas_call(
        functools.partial(_gla_kernel, chunk, nchunk, t_valid),
        grid=(b, t // rows),
        in_specs=[pl.BlockSpec((1, rows, GLA_IN_PAD), lambda bi, s: (bi, s, 0)), st_spec,
                  const(wa), const(ba), const(g), const(tri)],
        out_specs=[pl.BlockSpec((1, rows, GLA_HEADS * GLA_DV), lambda bi, s: (bi, s, 0)), st_spec],
        out_shape=[jax.ShapeDtypeStruct((b, t, GLA_HEADS * GLA_DV), F32),
                   jax.ShapeDtypeStruct((b, GLA_HEADS, GLA_DV, GLA_DK), F32)],
        scratch_shapes=[pltpu.VMEM((GLA_HEADS, GLA_DV, GLA_DK), F32)],
        compiler_params=_cparams("parallel", "arbitrary"),
        name="gla_core",
    )(proj, s0_t, wa, ba, g, tri)


def _pad_cols(w, width):
    return jnp.pad(w, ((0, 0), (0, width - w.shape[1]))).astype(BF16)


def _pad_axis(x, axis, size):
    pad = [(0, 0)] * x.ndim
    pad[axis] = (0, size - x.shape[axis])
    return jnp.pad(x, pad)


def _group_diag(q, t_seq):
    nseq = q.shape[0] // t_seq
    qg = q.reshape(nseq, t_seq, NSA_GROUPS, NSA_REP, HEAD_DIM).transpose(0, 2, 1, 3, 4)
    eye = jnp.eye(NSA_GROUPS, dtype=q.dtype)
    out = qg[:, :, :, :, None, :] * eye[None, :, None, None, :, None]
    return out.reshape(nseq, NSA_GROUPS * t_seq * NSA_REP, NSA_GROUPS * HEAD_DIM)


def _group_undiag(o, t_seq):
    nseq = o.shape[0]
    o6 = o.reshape(nseq, NSA_GROUPS, t_seq, NSA_REP, NSA_GROUPS, HEAD_DIM)
    od = jnp.stack([o6[:, g, :, :, g, :] for g in range(NSA_GROUPS)], axis=1)
    return od.transpose(0, 2, 1, 3, 4).reshape(nseq * t_seq, NSA_HEADS * HEAD_DIM)


def _seq_cols(x_t, nseq, t_seq):
    r = x_t.shape[0]
    return _pad_axis(x_t.reshape(r, nseq, t_seq).transpose(1, 0, 2), 2, PAGE_SIZE)


def _nsa_layer(s, xp, xs, seq_p, t_s, past_len, cache_t, state_win_t, page_table, norm_g, w_in, qk_g, pe, w_phi, w_o):
    bp = xp.shape[0] // seq_p
    bs = xs.shape[0] // t_s
    w_o_b = w_o.astype(BF16)
    npg = past_len // PAGE_SIZE

    qn, qr, gates, rows_t, win_t, ksel, kwin = nsa_project(xp, norm_g, w_in, qk_g, jnp.arange(seq_p), bp)
    own = _PageIndex(jnp.zeros((1, 1), jnp.int32), lambda b, p, pt: (b, 0, p))
    kc, vc = nsa_compress(rows_t, own, bp, seq_p // PAGE_SIZE, pe, w_phi, qk_g[3])
    qn3, qr3 = qn.reshape(bp, seq_p, 1024), qr.reshape(bp, seq_p, 1024)
    oc, sel = nsa_cmp_select(qn3, kc, vc, 0, seq_p // NSA_BLOCK)
    os_ = nsa_selected_prompt(qr3, sel, ksel)
    ow = nsa_window_prompt(qr3, kwin)
    xp_new = nsa_out(oc.reshape(-1, 1024), os_.reshape(-1, 1024), ow.reshape(-1, 1024), gates, w_o_b, xp)
    kv_p = rows_t.reshape(bp, 4, NSA_GROUPS, HEAD_DIM, seq_p).transpose(0, 4, 1, 2, 3)
    wlen = min(NSA_WINDOW, seq_p)
    win_p = win_t[:, :, seq_p - wlen:].reshape(bp, 2, NSA_GROUPS, HEAD_DIM, wlen).transpose(0, 4, 1, 2, 3)

    pos_s = past_len + jnp.arange(t_s)
    qn, qr, gates, rows_t, win_t, _, _ = nsa_project(xs, norm_g, w_in, qk_g, jnp.tile(pos_s, bs), 1)
    rows_t, win_t = rows_t[0], win_t[0]
    layer_pages = cache_t.shape[0] // (state_win_t.shape[0] // bs)
    table = page_table + s * layer_pages
    paged = _PageIndex(table, lambda b, p, pt: (pt[b, p], 0, 0))
    kc, vc = nsa_compress(cache_t, paged, bs, npg, pe, w_phi, qk_g[3])
    qn_pad = _pad_axis(qn.reshape(bs, t_s, 1024), 1, 128)
    ns = -(-(past_len + t_s) // NSA_BLOCK)
    oc, sel = nsa_cmp_select(qn_pad, kc, vc, past_len, 256)
    oc = oc[:, :t_s].reshape(bs * t_s, 1024)
    nrow = NSA_GROUPS * t_s * NSA_REP
    sel_f = sel[:, :, :t_s, :ns].astype(F32)
    by_row = lambda a: jnp.broadcast_to(a[:, :, :, None, :], a.shape[:3] + (NSA_REP, a.shape[-1])).reshape(
        a.shape[0], nrow, a.shape[-1])
    past_bias = by_row(jnp.repeat(sel_f[..., :past_len // NSA_BLOCK], NSA_BLOCK, axis=-1))
    new_ok = jnp.arange(PAGE_SIZE)[None, :] <= jnp.arange(t_s)[:, None]
    new_bias = by_row(jnp.where(new_ok[None, None], sel_f[..., past_len // NSA_BLOCK][..., None], NEG_INF))
    q_bd = _group_diag(qr, t_s)
    npp = min(NSA_PAGES_PER_STEP, npg)
    kpage = lambda blk: (lambda u: pl.BlockSpec((None, NSA_GW, PAGE_SIZE),
                                                lambda b, p, pt: (pt[b, jnp.minimum(p * npp + u, npg - 1)], blk, 0)))
    k_new = _seq_cols(rows_t[2 * NSA_GW:3 * NSA_GW], bs, t_s).astype(BF16)
    v_new = _seq_cols(rows_t[3 * NSA_GW:4 * NSA_GW], bs, t_s).astype(BF16)
    os_ = _group_undiag(paged_attention(q_bd, cache_t, table, npg, npp, kpage(2), kpage(3), True, k_new, v_new,
                                        past_bias, new_bias, True), t_s)
    wbuf = state_win_t.shape[-1]
    wpg = wbuf // PAGE_SIZE
    wpage = lambda blk: (lambda u: pl.BlockSpec((None, NSA_GW, PAGE_SIZE),
                                                lambda b, p, pt: (s * bs + b, blk, jnp.minimum(p * wpg + u, wpg - 1))))
    kidx = jnp.arange(wbuf + PAGE_SIZE)
    wpos = jnp.where(kidx < wbuf, past_len - wbuf + kidx, past_len + kidx - wbuf)
    w_ok = ((wpos[None, :] <= pos_s[:, None]) & (wpos[None, :] > pos_s[:, None] - NSA_WINDOW) & (wpos[None, :] >= 0)
            & (kidx[None, :] < wbuf + t_s))
    wbias = by_row(jnp.broadcast_to(jnp.where(w_ok, 0.0, NEG_INF).astype(F32)[None, None],
                                    (1, NSA_GROUPS, t_s, wbuf + PAGE_SIZE)))
    kw_new = _seq_cols(win_t[0:NSA_GW], bs, t_s).astype(BF16)
    vw_new = _seq_cols(win_t[NSA_GW:2 * NSA_GW], bs, t_s).astype(BF16)
    ow = _group_undiag(paged_attention(q_bd, state_win_t, jnp.zeros((1, 1), jnp.int32), wpg, wpg, wpage(0), wpage(1), True,
                                       kw_new, vw_new, wbias[:, :, :wbuf], wbias[:, :, wbuf:], True), t_s)
    xs_new = nsa_out(oc, os_, ow, gates, w_o_b, xs)
    kv_s = rows_t.reshape(4, NSA_GROUPS, HEAD_DIM, bs, t_s).transpose(3, 4, 0, 1, 2)
    win_new = win_t.reshape(2 * NSA_GW, bs, t_s).transpose(1, 0, 2)
    win_all = jnp.concatenate([state_win_t[s * bs:(s + 1) * bs], win_new], axis=2)
    wlen = min(NSA_WINDOW, win_all.shape[2])
    win_s = win_all[:, :, win_all.shape[2] - wlen:].reshape(bs, 2, NSA_GROUPS, HEAD_DIM, wlen).transpose(0, 4, 1, 2, 3)
    return xp_new, xs_new, kv_p, kv_s, win_p, win_s


def _diff_layer(layer, xp, xs, seq_p, t_s, past_len, cache, page_table, norm_g, w_in, qk_g, lam, sub_g, w_o):
    bp = xp.shape[0] // seq_p
    bs = xs.shape[0] // t_s
    npg = past_len // PAGE_SIZE
    lam_init = 0.8 - 0.6 * math.exp(-0.3 * layer)
    w_in_b = w_in.astype(BF16)
    w_o_b = w_o.astype(BF16)
    cos_p, sin_p = _rope_tables(jnp.arange(seq_p))
    proj = norm_matmul(xp, norm_g, w_in_b, 768)
    q, kvf, kvb = diff_post(proj, jnp.tile(cos_p, (bp, 1)), jnp.tile(sin_p, (bp, 1)), qk_g)
    o = diff_flash_prompt(q.reshape(bp, seq_p, DIFF_WIDTH), kvb.reshape(bp, seq_p, 2 * DIFF_WIDTH), lam, lam_init)
    xp_new = diff_out(o.reshape(-1, DIFF_WIDTH), sub_g, w_o_b, xp, lam_init)
    kv_p = kvf.reshape(bp, seq_p, 2, DIFF_HEADS, 2 * HEAD_DIM)
    pos_s = past_len + jnp.arange(t_s)
    cos_s, sin_s = _rope_tables(pos_s)
    proj = norm_matmul(xs, norm_g, w_in_b, 768)
    q, kvf, kvb = diff_post(proj, jnp.tile(cos_s, (bs, 1)), jnp.tile(sin_s, (bs, 1)), qk_g)
    nrow = DIFF_HEADS * 2 * t_s
    q5 = q.reshape(bs, t_s, DIFF_HEADS, 2, HEAD_DIM).transpose(0, 2, 3, 1, 4)
    q_rows = (q5[:, :, :, :, None, :] * jnp.eye(2, dtype=q.dtype)[None, None, :, None, :, None]).reshape(
        bs, nrow, 2 * HEAD_DIM)
    kv5 = kvb.reshape(bs, t_s, 2, DIFF_HEADS, 2 * HEAD_DIM)
    k_new = _pad_axis(kv5[:, :, 0], 1, PAGE_SIZE).reshape(bs, PAGE_SIZE * DIFF_HEADS, 2 * HEAD_DIM)
    v_new = _pad_axis(kv5[:, :, 1], 1, PAGE_SIZE).reshape(bs, PAGE_SIZE * DIFF_HEADS, 2 * HEAD_DIM)
    npp = min(DIFF_PAGES_PER_STEP, npg)
    row_h = jnp.arange(nrow) // (2 * t_s)
    row_t = jnp.arange(nrow) % t_s
    slot_h = jnp.arange(PAGE_SIZE * DIFF_HEADS) % DIFF_HEADS
    slot_tok = jnp.arange(PAGE_SIZE * DIFF_HEADS) // DIFF_HEADS
    same_head = row_h[:, None] == slot_h[None, :]
    bias_page = jnp.where(same_head, 0.0, NEG_INF).astype(F32)
    bias_past = jnp.tile(bias_page, (1, npp))[None]
    bias_new = jnp.where(same_head & (slot_tok[None, :] <= row_t[:, None]), 0.0, NEG_INF).astype(F32)[None]
    page = lambda slot: (lambda u: pl.BlockSpec(
        (None, PAGE_SIZE, None, DIFF_HEADS, 2 * HEAD_DIM),
        lambda b, p, pt: (pt[b, jnp.minimum(p * npp + u, npg - 1)], 0, slot, 0, 0)))
    o = paged_attention(q_rows, cache, page_table, npg, npp, page(0), page(1), False, k_new, v_new,
                        bias_past, bias_new, False)
    od = o.reshape(bs, DIFF_HEADS, 2, t_s, 2 * HEAD_DIM).transpose(2, 0, 3, 1, 4).reshape(2, bs * t_s, DIFF_WIDTH)
    xs_new = diff_out(od[0], sub_g, w_o_b, xs, lam_init, o1=od[1], lam=lam)
    kv_s = kvf.reshape(bs, t_s, 2, DIFF_HEADS, 2 * HEAD_DIM)
    return xp_new, xs_new, kv_p, kv_s


def _gla_layer(xp, xs, seq_p, t_s, state, norm_g, w_in, w_a2, b_a, out_g, w_o):
    bp = xp.shape[0] // seq_p
    bs = xs.shape[0] // t_s
    w_in_b = _pad_cols(w_in, GLA_IN_PAD)
    w_o_b = w_o.astype(BF16)
    proj = norm_matmul(xp, norm_g, w_in_b, 640).reshape(bp, seq_p, GLA_IN_PAD)
    chunk = min(GLA_CHUNK, seq_p)
    s0 = jnp.zeros((bp, GLA_HEADS, GLA_DV, GLA_DK), F32)
    y, st = gla_core(proj, s0, w_a2, b_a, out_g, chunk, 4 if seq_p % (4 * chunk) == 0 else 1, chunk)
    xp_new = matmul_residual(y.reshape(-1, GLA_HEADS * GLA_DV), w_o_b, xp)
    st_p = st.transpose(0, 1, 3, 2)
    chunk_s = GLA_CHUNK
    proj = norm_matmul(xs, norm_g, w_in_b, 640).reshape(bs, t_s, GLA_IN_PAD)
    proj = _pad_axis(proj, 1, chunk_s)
    y, st = gla_core(proj, state.transpose(0, 1, 3, 2), w_a2, b_a, out_g, chunk_s, 1, t_s)
    xs_new = matmul_residual(y[:, :t_s].reshape(-1, GLA_HEADS * GLA_DV), w_o_b, xs)
    st_s = st.transpose(0, 1, 3, 2)
    return xp_new, xs_new, st_p, st_s


def kernel(x_prompt, x_sample, cache_nsa_kv, state_nsa_win, cache_diff_kv, state_gla, state_ffn, page_table, norm_g, ffn_w_up, ffn_conv_w, ffn_conv_b, ffn_w_down, nsa_w_in, nsa_qk_g, nsa_pe, nsa_w_phi, nsa_w_o, diff_w_in, diff_qk_g, diff_lam, diff_sub_g, diff_w_o, gla_w_in, gla_w_a2, gla_b_a, gla_out_g, gla_w_o):
    bp, seq_p, d = x_prompt.shape
    bs, t_s, _ = x_sample.shape
    past_len = page_table.shape[1] * PAGE_SIZE
    xp = x_prompt.reshape(bp * seq_p, d)
    xs = x_sample.reshape(bs * t_s, d)
    cache_t = cache_nsa_kv.transpose(0, 1, 3, 4, 5, 2).reshape(-1, 4 * NSA_GW, PAGE_SIZE)
    win_t = state_nsa_win.transpose(0, 1, 3, 4, 5, 2).reshape(-1, 2 * NSA_GW, state_nsa_win.shape[2])
    nsa_kv_p, nsa_kv_s, nsa_win_p, nsa_win_s = [], [], [], []
    diff_kv_p, diff_kv_s, gla_p, gla_s, ffn_p, ffn_s = [], [], [], [], [], []
    for i in range(DEPTH):
        kind, s = i % N_MIXERS, i // N_MIXERS
        if kind == 0:
            xp, xs, kvp, kvs, wp, ws = _nsa_layer(
                s, xp, xs, seq_p, t_s, past_len, cache_t, win_t, page_table,
                norm_g[i, 0], nsa_w_in[s], nsa_qk_g[s], nsa_pe[s], nsa_w_phi[s], nsa_w_o[s])
            nsa_kv_p.append(kvp); nsa_kv_s.append(kvs); nsa_win_p.append(wp); nsa_win_s.append(ws)
        elif kind == 1:
            xp, xs, kvp, kvs = _diff_layer(
                i, xp, xs, seq_p, t_s, past_len, cache_diff_kv[s], page_table, norm_g[i, 0], diff_w_in[s],
                diff_qk_g[s], diff_lam[s], diff_sub_g[s], diff_w_o[s])
            diff_kv_p.append(kvp); diff_kv_s.append(kvs)
        else:
            xp, xs, stp, sts = _gla_layer(xp, xs, seq_p, t_s, state_gla[s], norm_g[i, 0], gla_w_in[s], gla_w_a2[s],
                                          gla_b_a[s], gla_out_g[s], gla_w_o[s])
            gla_p.append(stp); gla_s.append(sts)
        w_up_b = ffn_w_up[i].astype(BF16)
        w_dn_b = ffn_w_down[i].astype(BF16)
        xp, tail_p = ffn_prompt(xp, norm_g[i, 1], w_up_b, ffn_conv_w[i], ffn_conv_b[i], w_dn_b, seq_p)
        xs, tail_s = ffn_sample(xs, norm_g[i, 1], w_up_b, ffn_conv_w[i], ffn_conv_b[i], w_dn_b, state_ffn[i], t_s)
        ffn_p.append(tail_p); ffn_s.append(tail_s)
    return (xp.reshape(bp, seq_p, d), xs.reshape(bs, t_s, d),
            jnp.stack(nsa_kv_p), jnp.stack(nsa_kv_s), jnp.stack(nsa_win_p), jnp.stack(nsa_win_s),
            jnp.stack(diff_kv_p), jnp.stack(diff_kv_s), jnp.stack(gla_p), jnp.stack(gla_s),
            jnp.stack(ffn_p), jnp.stack(ffn_s))
```

```python
import functools
import math

import jax
import jax.numpy as jnp
from jax import lax
from jax.experimental import pallas as pl
from jax.experimental.pallas import tpu as pltpu

F32 = jnp.float32
BF16 = jnp.bfloat16

D_MODEL = 1024
HEAD_DIM = 64
ROPE_THETA = 10000.0
NORM_EPS = 1e-6
NEG_INF = -1e30
DEPTH = 4
N_MIXERS = 3
PAGE_SIZE = 128
NSA_HEADS = 16
NSA_GROUPS = 4
NSA_REP = 4
NSA_BLOCK = 64
NSA_TOPN = 16
NSA_WINDOW = 512
NSA_FORCED = 1e9
NSA_NQ = NSA_HEADS * HEAD_DIM
NSA_NKV = 6 * NSA_GROUPS * HEAD_DIM
NSA_NGATE = 3 * NSA_HEADS
NSA_GW = NSA_GROUPS * HEAD_DIM
DIFF_HEADS = 8
DIFF_WIDTH = 1024
GLA_HEADS = 4
GLA_DK = 128
GLA_DV = 256
GLA_RANK = 16
GLA_TAU = 16.0
GLA_CHUNK = 64
GLA_MAIN = 2 * GLA_HEADS * GLA_DK + 2 * GLA_HEADS * GLA_DV
GLA_IN_PAD = GLA_MAIN + 128
D_FF = 2816
CONV_W = 3
LANES = 128
VMEM_LIMIT = 56 * 1024 * 1024
SEL_TK = 512
NSA_PAGES_PER_STEP = 16
DIFF_PAGES_PER_STEP = 8
CMP_PITCH = HEAD_DIM + 8


def _cparams(*sem):
    return pltpu.CompilerParams(dimension_semantics=sem, vmem_limit_bytes=VMEM_LIMIT)


def _dot(a, b):
    return jnp.dot(a, b, preferred_element_type=F32)


def _dot_nt(a, b):
    return lax.dot_general(a, b, (((1,), (1,)), ((), ())), preferred_element_type=F32)


def _split_dot(x, m):
    hi = x.astype(BF16)
    lo = (x - hi.astype(F32)).astype(BF16)
    return _dot(hi, m) + _dot(lo, m)


def _split_dot_left(m, x):
    hi = x.astype(BF16)
    lo = (x - hi.astype(F32)).astype(BF16)
    return _dot(m, hi) + _dot(m, lo)


def _rms_rows(x, g):
    ms = jnp.mean(x * x, axis=-1, keepdims=True)
    return x * lax.rsqrt(ms + NORM_EPS) * g


def _seg64_norm(x, g, seg_ones):
    ms = _split_dot(x * x, seg_ones) * (1.0 / HEAD_DIM)
    return x * lax.rsqrt(ms + NORM_EPS) * g


def _rope_slab(x, cos, sin_signed):
    lane = lax.broadcasted_iota(jnp.int32, x.shape, 1)
    first = (lane & 63) < 32
    partner = jnp.where(first, pltpu.roll(x, 96, 1), pltpu.roll(x, 32, 1))
    return x * cos + partner * sin_signed


def _rope_angles(pos):
    half = HEAD_DIM // 2
    inv = ROPE_THETA ** (-jnp.arange(half, dtype=F32) / half)
    ang = pos.astype(F32)[:, None] * inv
    return jnp.cos(ang), jnp.sin(ang)


def _rope_tables(pos):
    cos, sin = _rope_angles(pos)
    return jnp.tile(cos, (1, 4)), jnp.tile(jnp.concatenate([-sin, sin], axis=1), (1, 2))


def _seg_ones():
    i = jnp.arange(LANES)
    return (i[:, None] // HEAD_DIM == i[None, :] // HEAD_DIM).astype(BF16)


def _lanes(x, width):
    return x if width == LANES else jnp.tile(x, (1, width // LANES))


def _softmax_update(s, v_dot, m_ref, l_ref, acc_ref):
    m = m_ref[...]
    m_new = jnp.maximum(m, jnp.max(s, axis=-1, keepdims=True))
    alpha = jnp.exp2(m - m_new)
    p = jnp.exp2(s - _lanes(m_new, s.shape[1]))
    m_ref[...] = m_new
    l_ref[...] = alpha * l_ref[...] + jnp.sum(p, axis=-1, keepdims=True)
    acc_ref[...] = _lanes(alpha, acc_ref.shape[-1]) * acc_ref[...] + v_dot(p.astype(BF16))


def _softmax_result(l_ref, acc_ref):
    return acc_ref[...] / _lanes(l_ref[...], acc_ref.shape[-1])


def _softmax_update_vsum(s, v_dot, m_ref, acc_ref):
    m = m_ref[...]
    m_new = jnp.maximum(m, jnp.max(s, axis=-1, keepdims=True))
    alpha = jnp.exp2(m - m_new)
    p = jnp.exp2(s - _lanes(m_new, s.shape[1]))
    m_ref[...] = m_new
    acc_ref[...] = _lanes(alpha, acc_ref.shape[-1]) * acc_ref[...] + v_dot(p.astype(BF16))


def _softmax_init(m_ref, acc_ref, l_ref=None):
    m_ref[...] = jnp.full_like(m_ref, NEG_INF)
    acc_ref[...] = jnp.zeros_like(acc_ref)
    if l_ref is not None:
        l_ref[...] = jnp.zeros_like(l_ref)


QK_SCALE = HEAD_DIM ** -0.5
QK_SCALE_LOG2 = QK_SCALE * math.log2(math.e)


def _norm_matmul_kernel(x_ref, g_ref, w_ref, o_ref, h_ref):
    @pl.when(pl.program_id(1) == 0)
    def _():
        h_ref[...] = _rms_rows(x_ref[...], g_ref[...]).astype(BF16)

    o_ref[...] = _dot(h_ref[...], w_ref[...])


def norm_matmul(x, g, w, tn):
    n, d = x.shape
    nout = w.shape[1]
    tm = min(512, n)
    return pl.pallas_call(
        _norm_matmul_kernel,
        grid=(n // tm, nout // tn),
        in_specs=[pl.BlockSpec((tm, d), lambda i, j: (i, 0)),
                  pl.BlockSpec((1, d), lambda i, j: (0, 0)),
                  pl.BlockSpec((d, tn), lambda i, j: (0, j))],
        out_specs=pl.BlockSpec((tm, tn), lambda i, j: (i, j)),
        out_shape=jax.ShapeDtypeStruct((n, nout), F32),
        scratch_shapes=[pltpu.VMEM((tm, d), BF16)],
        compiler_params=_cparams("parallel", "arbitrary"),
        name="norm_matmul",
    )(x, g.reshape(1, d), w)


def _matmul_res_kernel(a_ref, w_ref, r_ref, o_ref):
    o_ref[...] = r_ref[...] + _dot(a_ref[...].astype(BF16), w_ref[...])


def matmul_residual(a, w, res):
    n, k = a.shape
    d = w.shape[1]
    tm = min(512, n)
    return pl.pallas_call(
        _matmul_res_kernel,
        grid=(n // tm,),
        in_specs=[pl.BlockSpec((tm, k), lambda i: (i, 0)),
                  pl.BlockSpec((k, d), lambda i: (0, 0)),
                  pl.BlockSpec((tm, d), lambda i: (i, 0))],
        out_specs=pl.BlockSpec((tm, d), lambda i: (i, 0)),
        out_shape=jax.ShapeDtypeStruct((n, d), F32),
        compiler_params=_cparams("parallel"),
        name="matmul_residual",
    )(a, w, res)


FFN_CHUNK = 256


def _silu(x):
    return x / (1.0 + jnp.exp(-x))


def _causal_conv(u, prev1, prev2, use1, use2, cw_ref, cb_ref):
    u1 = jnp.where(use1, prev1, pltpu.roll(u, 1, 0))
    u2 = jnp.where(use2, prev2, pltpu.roll(u, 2, 0))
    return cb_ref[...] + cw_ref[0:1, :] * u2 + cw_ref[1:2, :] * u1 + cw_ref[2:3, :] * u


def _ffn_prompt_kernel(tiles_per_seq, x_ref, g_ref, wg_ref, wu_ref, cwg_ref, cwu_ref, cbg_ref, cbu_ref,
                       wd_ref, o_ref, tg_ref, tu_ref, cg_ref, cu_ref):
    i = pl.program_id(0)
    tm = x_ref.shape[0]
    nj, _, c = wg_ref.shape
    x = x_ref[...]
    h = _rms_rows(x, g_ref[...]).astype(BF16)
    row = lax.broadcasted_iota(jnp.int32, (tm, c), 0)

    @pl.when((i % tiles_per_seq) == 0)
    def _():
        cg_ref[...] = jnp.zeros_like(cg_ref)
        cu_ref[...] = jnp.zeros_like(cu_ref)

    def conv(u, prev, cw_ref, cb_ref):
        prev2 = jnp.where(row == 0, prev[0:1, :], prev[1:2, :])
        return _causal_conv(u, prev[1:2, :], prev2, row == 0, row < 2, cw_ref, cb_ref)

    acc = x
    for j in range(nj):
        ug = _dot(h, wg_ref[j])
        uu = _dot(h, wu_ref[j])
        act = _silu(conv(ug, cg_ref[j], cwg_ref.at[j], cbg_ref.at[j])) * conv(uu, cu_ref[j], cwu_ref.at[j], cbu_ref.at[j])
        acc = acc + _dot(act.astype(BF16), wd_ref[j])
        cg_ref[j] = ug[tm - 2:tm, :]
        cu_ref[j] = uu[tm - 2:tm, :]
        tg_ref[0, :, j * c:(j + 1) * c] = ug[tm - 2:tm, :]
        tu_ref[0, :, j * c:(j + 1) * c] = uu[tm - 2:tm, :]
    o_ref[...] = acc


def _resident(a):
    return pl.BlockSpec(a.shape, lambda i: (0,) * a.ndim, pipeline_mode=pl.Buffered(1))


def ffn_prompt(x, g, w_up, conv_w, conv_b, w_down, seq_len):
    n, d = x.shape
    c = FFN_CHUNK
    nj = D_FF // c
    tm = min(512, seq_len)
    tps = seq_len // tm
    chunks = lambda a: a.reshape(a.shape[0], 2, nj, c).transpose(1, 2, 0, 3)
    wg, wu = chunks(w_up)
    cwg, cwu = chunks(conv_w)
    cbg, cbu = chunks(conv_b.reshape(1, 2 * D_FF))
    wd = w_down.reshape(nj, c, d)
    g2 = g.reshape(1, d)
    consts = [g2, wg, wu, cwg, cwu, cbg, cbu, wd]
    out, tg, tu = pl.pallas_call(
        functools.partial(_ffn_prompt_kernel, tps),
        grid=(n // tm,),
        in_specs=[pl.BlockSpec((tm, d), lambda i: (i, 0))] + [_resident(a) for a in consts],
        out_specs=[pl.BlockSpec((tm, d), lambda i: (i, 0)),
                   pl.BlockSpec((1, 2, D_FF), lambda i: (i, 0, 0)),
                   pl.BlockSpec((1, 2, D_FF), lambda i: (i, 0, 0))],
        out_shape=[jax.ShapeDtypeStruct((n, d), F32),
                   jax.ShapeDtypeStruct((n // tm, 2, D_FF), F32),
                   jax.ShapeDtypeStruct((n // tm, 2, D_FF), F32)],
        scratch_shapes=[pltpu.VMEM((nj, 2, c), F32),
                        pltpu.VMEM((nj, 2, c), F32)],
        compiler_params=_cparams("arbitrary"),
        name="ffn_prompt",
    )(x, *consts)
    return out, jnp.concatenate([tg, tu], axis=-1)[tps - 1::tps]


def _ffn_sample_kernel(t_seq, x_ref, g_ref, wg_ref, wu_ref, cwg_ref, cwu_ref, cbg_ref, cbu_ref, wd_ref,
                       p1g_ref, p2g_ref, p1u_ref, p2u_ref, o_ref, ug_ref, uu_ref, h_ref):
    j = pl.program_id(0)

    @pl.when(j == 0)
    def _():
        h_ref[...] = _rms_rows(x_ref[...], g_ref[...]).astype(BF16)

    h = h_ref[...]
    ug = _dot(h, wg_ref[...])
    uu = _dot(h, wu_ref[...])
    t = lax.broadcasted_iota(jnp.int32, ug.shape, 0) & (t_seq - 1)
    cg = _causal_conv(ug, p1g_ref[...], p2g_ref[...], t == 0, t < 2, cwg_ref, cbg_ref)
    cu = _causal_conv(uu, p1u_ref[...], p2u_ref[...], t == 0, t < 2, cwu_ref, cbu_ref)
    part = _dot((_silu(cg) * cu).astype(BF16), wd_ref[...])
    ug_ref[...] = ug
    uu_ref[...] = uu

    @pl.when(j == 0)
    def _():
        o_ref[...] = x_ref[...] + part

    @pl.when(j > 0)
    def _():
        o_ref[...] += part


def ffn_sample(x, g, w_up, conv_w, conv_b, w_down, buf, t_seq):
    n, d = x.shape
    nseq = n // t_seq
    c = FFN_CHUNK
    nj = D_FF // c
    cb = conv_b.reshape(1, 2 * D_FF)
    reps = t_seq // 2
    prev1 = jnp.concatenate([buf[:, ::-1]] * reps, axis=1).reshape(n, 2 * D_FF)
    prev2 = jnp.concatenate([buf] * reps, axis=1).reshape(n, 2 * D_FF)
    full = lambda blk, off=0: pl.BlockSpec(blk, lambda j: (0, j + off))
    out, ug, uu = pl.pallas_call(
        functools.partial(_ffn_sample_kernel, t_seq),
        grid=(nj,),
        in_specs=[pl.BlockSpec((n, d), lambda j: (0, 0)),
                  pl.BlockSpec((1, d), lambda j: (0, 0)),
                  full((d, c)), full((d, c), nj),
                  full((CONV_W, c)), full((CONV_W, c), nj),
                  full((1, c)), full((1, c), nj),
                  pl.BlockSpec((c, d), lambda j: (j, 0)),
                  full((n, c)), full((n, c)), full((n, c), nj), full((n, c), nj)],
        out_specs=[pl.BlockSpec((n, d), lambda j: (0, 0)),
                   full((n, c)), full((n, c))],
        out_shape=[jax.ShapeDtypeStruct((n, d), F32),
                   jax.ShapeDtypeStruct((n, D_FF), F32),
                   jax.ShapeDtypeStruct((n, D_FF), F32)],
        scratch_shapes=[pltpu.VMEM((n, d), BF16)],
        compiler_params=_cparams("arbitrary"),
        name="ffn_sample",
    )(x, g.reshape(1, d), w_up, w_up, conv_w, conv_w, cb, cb, w_down, prev1, prev2, prev1, prev2)
    u = jnp.concatenate([ug, uu], axis=-1).reshape(nseq, t_seq, 2 * D_FF)
    return out, u[:, t_seq - (CONV_W - 1):]


def _head_norm_t(x, g):
    ms = jnp.mean(x * x, axis=0, keepdims=True)
    return x * lax.rsqrt(ms + NORM_EPS) * g


def _rope_t(x, cos, sin):
    half = HEAD_DIM // 2
    x1, x2 = x[0:half, :], x[half:HEAD_DIM, :]
    return jnp.concatenate([x1 * cos - x2 * sin, x1 * sin + x2 * cos], axis=0)


def _nsa_project_kernel(x_ref, g_ref, wq_ref, wkv_ref, wg_ref, cos_ref, sin_ref, cost_ref, sint_ref, gq_ref, gk_ref,
                        so_ref, qn_ref, qr_ref, gate_ref, rows_ref, win_ref, ksel_ref, kwin_ref):
    hn = _rms_rows(x_ref[...], g_ref[...]).astype(BF16)
    cos, sin = cos_ref[...], sin_ref[...]
    so = so_ref[...]
    q_all = _dot(hn, wq_ref[...])
    for s in range(NSA_NQ // LANES):
        sl = slice(s * LANES, (s + 1) * LANES)
        q = _seg64_norm(q_all[:, sl], gq_ref[...], so)
        qn_ref[:, sl] = (q * QK_SCALE).astype(BF16)
        qr_ref[:, sl] = (_rope_slab(q, cos, sin) * QK_SCALE_LOG2).astype(BF16)
    gate_ref[...] = 1.0 / (1.0 + jnp.exp(-_dot(hn, wg_ref[...])))
    kvt = _dot_nt(wkv_ref[...], hn)
    gw = NSA_GW
    cost, sint = cost_ref[...], sint_ref[...]
    rows_ref[0, 0:2 * gw, :] = kvt[0:2 * gw, :]
    rows_ref[0, 3 * gw:4 * gw, :] = kvt[3 * gw:4 * gw, :]
    win_ref[0, gw:2 * gw, :] = kvt[5 * gw:6 * gw, :]
    for g in range(NSA_GROUPS):
        hs = slice(g * HEAD_DIM, (g + 1) * HEAD_DIM)
        rows_ref[0, 2 * gw + g * HEAD_DIM:2 * gw + (g + 1) * HEAD_DIM, :] = _rope_t(
            _head_norm_t(kvt[2 * gw + g * HEAD_DIM:2 * gw + (g + 1) * HEAD_DIM, :], gk_ref[0]), cost, sint)
        win_ref[0, hs, :] = _rope_t(
            _head_norm_t(kvt[4 * gw + g * HEAD_DIM:4 * gw + (g + 1) * HEAD_DIM, :], gk_ref[1]), cost, sint)
    tm = kvt.shape[1]
    for g in range(NSA_GROUPS):
        hs = slice(g * HEAD_DIM, (g + 1) * HEAD_DIM)
        ksel_ref[0, 0, g, 0:HEAD_DIM, :] = rows_ref[0, 2 * gw + g * HEAD_DIM:2 * gw + (g + 1) * HEAD_DIM, :].astype(BF16)
        ksel_ref[0, 0, g, HEAD_DIM:2 * HEAD_DIM, :] = jnp.zeros((HEAD_DIM, tm), BF16)
        ksel_ref[0, 0, g, 2 * HEAD_DIM:3 * HEAD_DIM, :] = kvt[3 * gw + g * HEAD_DIM:3 * gw + (g + 1) * HEAD_DIM, :].astype(BF16)
        ksel_ref[0, 0, g, 3 * HEAD_DIM:4 * HEAD_DIM, :] = jnp.ones((HEAD_DIM, tm), BF16)
    wb = win_ref[0].astype(BF16)
    for u in range(kwin_ref.shape[1]):
        ts = slice(u * LANES, (u + 1) * LANES)
        for g in range(NSA_GROUPS):
            kwin_ref[0, u, g, 0:HEAD_DIM, :] = wb[g * HEAD_DIM:(g + 1) * HEAD_DIM, ts]
            kwin_ref[0, u, g, HEAD_DIM:2 * HEAD_DIM, :] = jnp.zeros((HEAD_DIM, LANES), BF16)
            kwin_ref[0, u, g, 2 * HEAD_DIM:3 * HEAD_DIM, :] = wb[gw + g * HEAD_DIM:gw + (g + 1) * HEAD_DIM, ts]
            kwin_ref[0, u, g, 3 * HEAD_DIM:4 * HEAD_DIM, :] = jnp.ones((HEAD_DIM, LANES), BF16)


def nsa_project(x, norm_g, w_in, qk_g, pos, nseq):
    n, d = x.shape
    t = n // nseq
    tm = min(SEL_TK, t)
    nt = t // tm
    wq = w_in[:, :NSA_NQ].astype(BF16)
    wkv_t = w_in[:, NSA_NQ:NSA_NQ + NSA_NKV].T.astype(BF16)
    wg = jnp.pad(w_in[:, NSA_NQ + NSA_NKV:], ((0, 0), (0, LANES - NSA_NGATE))).astype(BF16)
    cos, sin = _rope_angles(pos)
    cos_q, sin_q = jnp.tile(cos, (1, 4)), jnp.tile(jnp.concatenate([-sin, sin], axis=1), (1, 2))
    gq = jnp.tile(qk_g[0:1], (1, 2))
    gk = qk_g[1:3].reshape(2, HEAD_DIM, 1)
    so = _seg_ones()
    const = lambda a: pl.BlockSpec(a.shape, lambda b, i: (0,) * a.ndim)
    row = lambda w: pl.BlockSpec((tm, w), lambda b, i: (b * nt + i, 0))
    return pl.pallas_call(
        _nsa_project_kernel,
        grid=(nseq, nt),
        in_specs=[row(d), const(norm_g.reshape(1, d)), const(wq), const(wkv_t), const(wg),
                  pl.BlockSpec((tm, LANES), lambda b, i: (i, 0)), pl.BlockSpec((tm, LANES), lambda b, i: (i, 0)),
                  pl.BlockSpec((HEAD_DIM // 2, tm), lambda b, i: (0, i)),
                  pl.BlockSpec((HEAD_DIM // 2, tm), lambda b, i: (0, i)),
                  const(gq), const(gk), const(so)],
        out_specs=[row(NSA_NQ), row(NSA_NQ), row(LANES),
                   pl.BlockSpec((1, 4 * NSA_GW, tm), lambda b, i: (b, 0, i)),
                   pl.BlockSpec((1, 2 * NSA_GW, tm), lambda b, i: (b, 0, i)),
                   pl.BlockSpec((1, 1, NSA_GROUPS, 4 * HEAD_DIM, tm), lambda b, i: (b, i, 0, 0, 0)),
                   pl.BlockSpec((1, tm // LANES, NSA_GROUPS, 4 * HEAD_DIM, LANES), lambda b, i: (b, i, 0, 0, 0))],
        out_shape=[jax.ShapeDtypeStruct((n, NSA_NQ), BF16), jax.ShapeDtypeStruct((n, NSA_NQ), BF16),
                   jax.ShapeDtypeStruct((n, LANES), F32),
                   jax.ShapeDtypeStruct((nseq, 4 * NSA_GW, t), F32),
                   jax.ShapeDtypeStruct((nseq, 2 * NSA_GW, t), F32),
                   jax.ShapeDtypeStruct((nseq, nt, NSA_GROUPS, 4 * HEAD_DIM, tm), BF16),
                   jax.ShapeDtypeStruct((nseq, t // LANES, NSA_GROUPS, 4 * HEAD_DIM, LANES), BF16)],
        compiler_params=_cparams("parallel", "parallel"),
        name="nsa_project",
    )(x, norm_g.reshape(1, d), wq, wkv_t, wg, cos_q, sin_q, cos.T, sin.T, gq, gk, so)


def _nsa_compress_kernel(npp, pt_ref, *refs):
    page_refs = refs[:npp]
    pe_ref, w_ref, g_ref, so_ref, kc_ref, vc_ref, seqk_ref, seqv_ref, acc_ref = refs[npp:]
    p = pl.program_id(1)
    gw = NSA_GW
    pitch = CMP_PITCH
    for u in range(npp):
        for g in range(NSA_GROUPS):
            lo = pl.multiple_of(((p * npp + u) * NSA_GROUPS + g) * pitch, 8)
            seqk_ref[pl.ds(lo, HEAD_DIM), :] = page_refs[u][g * HEAD_DIM:(g + 1) * HEAD_DIM, :]
            seqv_ref[pl.ds(lo, HEAD_DIM), :] = page_refs[u][gw + g * HEAD_DIM:gw + (g + 1) * HEAD_DIM, :]

    @pl.when(p == pl.num_programs(1) - 1)
    def _():
        nrow = seqk_ref.shape[0] // pitch
        acc_ref[...] = jnp.zeros_like(acc_ref)
        for dd in range(HEAD_DIM):
            a = jnp.concatenate([seqk_ref[pl.ds(dd, nrow, stride=pitch), :],
                                 seqv_ref[pl.ds(dd, nrow, stride=pitch), :]], axis=1) + pe_ref[dd:dd + 1, :]
            acc_ref[...] += _dot(a.astype(BF16), w_ref[dd])
        kc_ref[0] = _seg64_norm(acc_ref[:, 0:LANES], g_ref[...], so_ref[...])
        vc_ref[0] = acc_ref[:, LANES:2 * LANES]


def nsa_compress(pages_t, page_index, nseq, npg, pe, w_phi, g_c):
    npp = min(NSA_PAGES_PER_STEP, npg)
    nrow = npg * NSA_GROUPS
    pe_t = jnp.concatenate([jnp.tile(pe[0].T, (1, 2)), jnp.tile(pe[1].T, (1, 2))], axis=1)
    eye4 = jnp.eye(4, dtype=F32)
    w4 = jnp.stack([w_phi[0], w_phi[0], w_phi[1], w_phi[1]])
    w = jnp.einsum('ab,alde->dalbe', eye4, w4).reshape(HEAD_DIM, 2 * LANES, 2 * LANES).astype(BF16)
    g2 = jnp.tile(g_c.reshape(1, HEAD_DIM), (1, 2))
    so = _seg_ones()
    const = lambda a: pl.BlockSpec(a.shape, lambda b, p, pt: (0,) * a.ndim, pipeline_mode=pl.Buffered(1))
    page = lambda u: pl.BlockSpec((None, 2 * NSA_GW, PAGE_SIZE), lambda b, p, pt: page_index(b, p * npp + u, pt))
    out = pl.BlockSpec((1, nrow, LANES), lambda b, p, pt: (b, 0, 0))
    table = page_index.table
    kc, vc = pl.pallas_call(
        functools.partial(_nsa_compress_kernel, npp),
        grid_spec=pltpu.PrefetchScalarGridSpec(
            num_scalar_prefetch=1,
            grid=(nseq, npg // npp),
            in_specs=[page(u) for u in range(npp)] + [const(pe_t), const(w), const(g2), const(so)],
            out_specs=[out, out],
            scratch_shapes=[pltpu.VMEM((nrow * CMP_PITCH, PAGE_SIZE), F32), pltpu.VMEM((nrow * CMP_PITCH, PAGE_SIZE), F32),
                            pltpu.VMEM((nrow, 2 * LANES), F32)]),
        out_shape=[jax.ShapeDtypeStruct((nseq, nrow, LANES), F32)] * 2,
        compiler_params=_cparams("arbitrary", "arbitrary"),
        name="nsa_compress",
    )(table, *([pages_t] * npp), pe_t, w, g2, so)
    fix = lambda a: a.reshape(nseq, npg, NSA_GROUPS, 2, HEAD_DIM).transpose(0, 2, 1, 3, 4).reshape(
        nseq, NSA_GROUPS, 2 * npg, HEAD_DIM)
    return fix(kc), fix(vc)


class _PageIndex:
    def __init__(self, table, fn):
        self.table = table
        self._fn = fn

    def __call__(self, b, p, pt):
        return self._fn(b, p, pt)


def _stack_heads(q):
    head = lax.broadcasted_iota(jnp.int32, q.shape, 1) >> 6
    qf = q.astype(F32)
    return jnp.concatenate([jnp.where(head == r, qf, 0.0) for r in range(NSA_REP)], axis=0).astype(BF16)


def _unstack_heads(o4, tq):
    head = lax.broadcasted_iota(jnp.int32, (tq, o4.shape[1]), 1) >> 6
    out = jnp.zeros((tq, o4.shape[1]), F32)
    for r in range(NSA_REP):
        out = jnp.where(head == r, o4[r * tq:(r + 1) * tq, :], out)
    return out


def _tile_rows4(x):
    return jnp.concatenate([x] * NSA_REP, axis=0)


def _nsa_cmp_kernel(q0, ns_rows, q_ref, kc_ref, vc_ref, oc_ref, sel_ref):
    i = pl.program_id(2)
    tq = q_ref.shape[1]
    nc = kc_ref.shape[2]
    qst = _stack_heads(q_ref[0])
    kc, vc = kc_ref[0, 0], vc_ref[0, 0]
    base = q0 + i * tq
    qpos = base + (lax.broadcasted_iota(jnp.int32, (NSA_REP * tq, nc), 0) & (tq - 1))
    blk_end = (lax.broadcasted_iota(jnp.int32, (NSA_REP * tq, nc), 1) + 1) * NSA_BLOCK - 1
    ok = blk_end <= qpos
    s = jnp.where(ok, _dot_nt(qst, kc), NEG_INF)
    e = jnp.exp(s - jnp.max(s, axis=-1, keepdims=True))
    p = jnp.where(ok, e / jnp.sum(e, axis=-1, keepdims=True), 0.0)
    oc_ref[0] = _unstack_heads(_dot(p.astype(BF16), vc), tq)
    qpos_t = base + (lax.broadcasted_iota(jnp.int32, (nc, NSA_REP * tq), 1) & (tq - 1))
    blk_end_t = (lax.broadcasted_iota(jnp.int32, (nc, NSA_REP * tq), 0) + 1) * NSA_BLOCK - 1
    ok_t = blk_end_t <= qpos_t
    st = jnp.where(ok_t, _dot_nt(kc, qst), NEG_INF)
    et = jnp.exp(st - jnp.max(st, axis=0, keepdims=True))
    pt = jnp.where(ok_t, et / jnp.sum(et, axis=0, keepdims=True), 0.0)
    imp = pt[:, 0:tq]
    for r in range(1, NSA_REP):
        imp = imp + pt[:, r * tq:(r + 1) * tq]
    qpos_col = base + lax.broadcasted_iota(jnp.int32, (ns_rows, tq), 1)
    sel_ref[0, 0] = _select_blocks(imp, qpos_col, ns_rows).T.astype(BF16)


def _select_blocks(imp, qpos, ns_rows):
    nc, cols = imp.shape
    if ns_rows > nc:
        imp = jnp.concatenate([imp, jnp.zeros((ns_rows - nc, cols), F32)], axis=0)
    blk = lax.broadcasted_iota(jnp.int32, (ns_rows, cols), 0)
    cur = qpos >> 6
    forced = (blk == 0) | (blk == cur) | (blk == cur - 1)
    imp = jnp.where(forced, NSA_FORCED, imp)
    imp = jnp.where(blk <= cur, imp, NEG_INF)
    taken = jnp.float32(-3e38)
    blk_f = blk.astype(F32)

    def pick(_, carry):
        imp, sel = carry
        m = jnp.max(imp, axis=0, keepdims=True)
        first = jnp.min(jnp.where(imp == m, blk_f, float(ns_rows)), axis=0, keepdims=True)
        hit = blk_f == first
        return jnp.where(hit, taken, imp), jnp.where(hit, 0.0, sel)

    _, sel = lax.fori_loop(0, NSA_TOPN, pick, (imp, jnp.full((ns_rows, cols), NEG_INF, F32)))
    return sel


def _nsa_cmp_sample_kernel(q0, t_seq, ns_rows, q_ref, kc_ref, vc_ref, oc_ref, sel_ref):
    ntok = q_ref.shape[0]
    nkey = kc_ref.shape[1]
    nc = nkey // (ntok // t_seq)
    rows = NSA_REP * ntok
    qst = _stack_heads(q_ref[...])
    row = lax.broadcasted_iota(jnp.int32, (rows, nkey), 0) & (ntok - 1)
    col = lax.broadcasted_iota(jnp.int32, (rows, nkey), 1)
    qpos = q0 + (row & (t_seq - 1))
    same_seq = (row >> (t_seq.bit_length() - 1)) == (col >> (nc.bit_length() - 1))
    ok = same_seq & (((col & (nc - 1)) + 1) * NSA_BLOCK - 1 <= qpos)
    s = jnp.where(ok, _dot_nt(qst, kc_ref[0]), NEG_INF)
    e = jnp.exp(s - jnp.max(s, axis=-1, keepdims=True))
    p = jnp.where(ok, e / jnp.sum(e, axis=-1, keepdims=True), 0.0)
    oc_ref[...] = _unstack_heads(_dot(p.astype(BF16), vc_ref[0]), ntok)
    own = p[:, 0:nc]
    for u in range(1, nkey // nc):
        own = own + p[:, u * nc:(u + 1) * nc]
    imp = own[0:ntok, :]
    for r in range(1, NSA_REP):
        imp = imp + own[r * ntok:(r + 1) * ntok, :]
    qpos_col = q0 + (lax.broadcasted_iota(jnp.int32, (ns_rows, ntok), 1) & (t_seq - 1))
    sel_ref[0] = _select_blocks(imp.T, qpos_col, ns_rows).T.astype(BF16)


def nsa_cmp_select_sample(qn, kc, vc, q0, t_seq, ns_rows):
    ntok = qn.shape[0]
    nseq, _, nc, _ = kc.shape
    flat = lambda a: _tile_lanes4(a.transpose(1, 0, 2, 3).reshape(1, NSA_GROUPS, nseq * nc, HEAD_DIM))[0]
    kv = pl.BlockSpec((1, nseq * nc, 256), lambda g: (g, 0, 0))
    return pl.pallas_call(
        functools.partial(_nsa_cmp_sample_kernel, q0, t_seq, ns_rows),
        grid=(NSA_GROUPS,),
        in_specs=[pl.BlockSpec((ntok, 256), lambda g: (0, g)), kv, kv],
        out_specs=[pl.BlockSpec((ntok, 256), lambda g: (0, g)),
                   pl.BlockSpec((1, ntok, ns_rows), lambda g: (g, 0, 0))],
        out_shape=[jax.ShapeDtypeStruct((ntok, 1024), F32),
                   jax.ShapeDtypeStruct((NSA_GROUPS, ntok, ns_rows), BF16)],
        compiler_params=_cparams("parallel"),
        name="nsa_cmp_select_sample",
    )(qn, flat(kc), flat(vc))


def _tile_lanes4(x):
    return jnp.tile(x, (1, 1, 1, NSA_REP)).astype(BF16)


def nsa_cmp_select(qn, kc, vc, q0, ns_rows):
    b, t, _ = qn.shape
    tq = min(512, t)
    nc = kc.shape[2]
    kv = pl.BlockSpec((1, 1, nc, 256), lambda bi, g, i: (bi, g, 0, 0))
    return pl.pallas_call(
        functools.partial(_nsa_cmp_kernel, q0, ns_rows),
        grid=(b, NSA_GROUPS, t // tq),
        in_specs=[pl.BlockSpec((1, tq, 256), lambda bi, g, i: (bi, i, g)), kv, kv],
        out_specs=[pl.BlockSpec((1, tq, 256), lambda bi, g, i: (bi, i, g)),
                   pl.BlockSpec((1, 1, tq, ns_rows), lambda bi, g, i: (bi, g, i, 0))],
        out_shape=[jax.ShapeDtypeStruct((b, t, 1024), F32),
                   jax.ShapeDtypeStruct((b, NSA_GROUPS, t, ns_rows), BF16)],
        compiler_params=_cparams("parallel", "parallel", "parallel"),
        name="nsa_cmp_select",
    )(qn, _tile_lanes4(kc), _tile_lanes4(vc))


def _head_rows(q):
    lane = lax.broadcasted_iota(jnp.int32, (q.shape[0], LANES), 1)
    out = []
    for r in range(NSA_REP):
        slab = q[:, (r // 2) * LANES:(r // 2 + 1) * LANES]
        slab = pltpu.roll(slab, HEAD_DIM, 1) if r % 2 else slab
        out.append(jnp.where(lane < HEAD_DIM, slab, 0.0))
    return out


def _store_heads(o_ref, row0, acc, tq):
    lane = lax.broadcasted_iota(jnp.int32, (tq, LANES), 1)
    o4 = acc / pltpu.roll(acc, HEAD_DIM, 1)
    for half in range(NSA_REP // 2):
        even = o4[(2 * half) * tq:(2 * half + 1) * tq, :]
        odd = pltpu.roll(o4[(2 * half + 1) * tq:(2 * half + 2) * tq, :], HEAD_DIM, 1)
        o_ref[0, row0:row0 + tq, half * LANES:(half + 1) * LANES] = jnp.where(lane < HEAD_DIM, even, odd)


WIN_SUB = 4


def _nsa_window_kernel(nsub, q_ref, kv_ref, o_ref):
    i = pl.program_id(2)
    tq = q_ref.shape[1] // nsub
    ntile = (NSA_WINDOW + 2 * tq) // LANES
    span = ntile * LANES
    rows = NSA_REP * tq
    for pair in range(nsub // 2):
        j0 = jnp.maximum((i * nsub + 2 * pair) * tq - NSA_WINDOW, 0) // LANES
        k_aug = jnp.concatenate([kv_ref[0, j0 + u, 0, 0:2 * HEAD_DIM, :] for u in range(ntile)], axis=1)
        v_aug = jnp.concatenate([kv_ref[0, j0 + u, 0, 2 * HEAD_DIM:4 * HEAD_DIM, :] for u in range(ntile)], axis=1)
        kpos = j0 * LANES + lax.broadcasted_iota(jnp.int32, (rows, span), 1)
        for a in range(2 * pair, 2 * pair + 2):
            lhs = jnp.concatenate(_head_rows(q_ref[0, a * tq:(a + 1) * tq, :].astype(F32)), axis=0).astype(BF16)
            qpos = (i * nsub + a) * tq + (lax.broadcasted_iota(jnp.int32, (rows, span), 0) & (tq - 1))
            ok = (kpos <= qpos) & (kpos > qpos - NSA_WINDOW)
            s = jnp.where(ok, _dot(lhs, k_aug), NEG_INF)
            e = jnp.exp2(s - jnp.max(s, axis=-1, keepdims=True))
            _store_heads(o_ref, a * tq, _dot_nt(e.astype(BF16), v_aug), tq)


def nsa_window_prompt(qr, kwin):
    b, t, _ = qr.shape
    tq = WIN_SUB * 128
    nt = kwin.shape[1]
    return pl.pallas_call(
        functools.partial(_nsa_window_kernel, WIN_SUB),
        grid=(b, NSA_GROUPS, t // tq),
        in_specs=[pl.BlockSpec((1, tq, 256), lambda bi, g, i: (bi, i, g)),
                  pl.BlockSpec((1, nt, 1, 4 * HEAD_DIM, LANES), lambda bi, g, i: (bi, 0, g, 0, 0))],
        out_specs=pl.BlockSpec((1, tq, 256), lambda bi, g, i: (bi, i, g)),
        out_shape=jax.ShapeDtypeStruct((b, t, 1024), F32),
        compiler_params=_cparams("parallel", "parallel", "arbitrary"),
        name="nsa_window",
    )(qr, kwin)


SEL_SUB = 4


def _nsa_selected_kernel(q_ref, sel_ref, kv_ref, e_ref, o_ref, m_ref, acc_ref):
    i = pl.program_id(2)
    nsub = m_ref.shape[0]
    tq = q_ref.shape[1] // nsub
    tk = kv_ref.shape[4]
    rows = NSA_REP * tq

    def lhs(a):
        sel = sel_ref[0, 0, a * tq:(a + 1) * tq, :].astype(F32)
        heads = _head_rows(q_ref[0, a * tq:(a + 1) * tq, :].astype(F32))
        return jnp.concatenate([jnp.concatenate([h, sel], axis=1) for h in heads], axis=0).astype(BF16)

    qs = [lhs(a) for a in range(nsub)]
    _softmax_init(m_ref, acc_ref)

    def tile(j, causal):
        k_aug = jnp.concatenate([kv_ref[0, j, 0, 0:2 * HEAD_DIM, :], e_ref[j]], axis=0)
        v_aug = kv_ref[0, j, 0, 2 * HEAD_DIM:4 * HEAD_DIM, :]
        for a in range(nsub):
            s = _dot(qs[a], k_aug)
            if causal:
                qpos = (i * nsub + a) * tq + (lax.broadcasted_iota(jnp.int32, (rows, tk), 0) & (tq - 1))
                kpos = j * tk + lax.broadcasted_iota(jnp.int32, (rows, tk), 1)
                s = jnp.where(kpos <= qpos, s, NEG_INF)
            _softmax_update_vsum(s, lambda p: _dot_nt(p, v_aug), m_ref.at[a], acc_ref.at[a])

    nfull = (i * nsub * tq) // tk

    def body(j, c):
        tile(j, False)
        return c

    lax.fori_loop(0, nfull, body, 0)
    tile(nfull, True)
    for a in range(nsub):
        _store_heads(o_ref, a * tq, acc_ref[a], tq)


def nsa_selected_prompt(qr, sel, ksel):
    b, t, _ = qr.shape
    sub = 128
    tq = SEL_SUB * sub
    nt, _, _, tk = ksel.shape[1:]
    assert tk % tq == 0
    nblk = sel.shape[-1]
    e3 = (jnp.arange(nblk)[None, :, None] == (jnp.arange(nt)[:, None, None] * tk + jnp.arange(tk)[None, None, :]) // NSA_BLOCK
          ).astype(BF16)
    return pl.pallas_call(
        _nsa_selected_kernel,
        grid=(b, NSA_GROUPS, t // tq),
        in_specs=[pl.BlockSpec((1, tq, 256), lambda bi, g, i: (bi, i, g)),
                  pl.BlockSpec((1, 1, tq, nblk), lambda bi, g, i: (bi, g, i, 0)),
                  pl.BlockSpec((1, nt, 1, 4 * HEAD_DIM, tk), lambda bi, g, i: (bi, 0, g, 0, 0)),
                  pl.BlockSpec(e3.shape, lambda bi, g, i: (0, 0, 0))],
        out_specs=pl.BlockSpec((1, tq, 256), lambda bi, g, i: (bi, i, g)),
        out_shape=jax.ShapeDtypeStruct((b, t, 1024), F32),
        scratch_shapes=[pltpu.VMEM((SEL_SUB, NSA_REP * sub, LANES), F32),
                        pltpu.VMEM((SEL_SUB, NSA_REP * sub, LANES), F32)],
        compiler_params=_cparams("parallel", "parallel", "arbitrary"),
        name="nsa_selected",
    )(qr, sel, ksel, e3)


def _nsa_out_kernel(oc_ref, os_ref, ow_ref, gate_ref, ex_ref, w_ref, r_ref, o_ref):
    gate = gate_ref[...]
    comb = (_split_dot(gate, ex_ref[0]) * oc_ref[...] + _split_dot(gate, ex_ref[1]) * os_ref[...]
            + _split_dot(gate, ex_ref[2]) * ow_ref[...])
    o_ref[...] = r_ref[...] + _dot(comb.astype(BF16), w_ref[...])


def nsa_out(oc, os_, ow, gates, w_o, res):
    n, d = res.shape
    tm = min(256, n)
    lane = jnp.arange(1024) // HEAD_DIM
    ex = jnp.stack([(jnp.arange(LANES)[:, None] == lane[None, :] * 3 + k) for k in range(3)]).astype(BF16)
    row = lambda w: pl.BlockSpec((tm, w), lambda i: (i, 0))
    return pl.pallas_call(
        _nsa_out_kernel,
        grid=(n // tm,),
        in_specs=[row(1024), row(1024), row(1024), row(LANES),
                  pl.BlockSpec(ex.shape, lambda i: (0, 0, 0)),
                  pl.BlockSpec(w_o.shape, lambda i: (0, 0)), row(d)],
        out_specs=row(d),
        out_shape=jax.ShapeDtypeStruct((n, d), F32),
        compiler_params=_cparams("parallel"),
        name="nsa_out",
    )(oc, os_, ow, gates, ex, w_o, res)


def _paged_attn_kernel(npp, kv_t, pt_ref, q_ref, *refs):
    k_refs, v_refs = refs[:npp], refs[npp:2 * npp]
    kn_ref, vn_ref, bp_ref, bn_ref, o_ref, m_ref, l_ref, acc_ref = refs[2 * npp:]
    p = pl.program_id(1)
    last = pl.num_programs(1) - 1

    @pl.when(p == 0)
    def _():
        _softmax_init(m_ref, acc_ref, l_ref)

    def step(k, v, bias):
        if kv_t:
            s = _dot(q_ref[0], k) + bias
            _softmax_update(s, lambda e: _dot_nt(e, v), m_ref, l_ref, acc_ref)
        else:
            s = _dot_nt(q_ref[0], k) + bias
            _softmax_update(s, lambda e: _dot(e, v), m_ref, l_ref, acc_ref)

    def load(r):
        x = r[...]
        return x if kv_t else x.reshape(-1, x.shape[-1])

    @pl.when(p < last)
    def _():
        axis = 1 if kv_t else 0
        step(jnp.concatenate([load(r) for r in k_refs], axis=axis).astype(BF16),
             jnp.concatenate([load(r) for r in v_refs], axis=axis).astype(BF16), bp_ref[0])

    @pl.when(p == last)
    def _():
        step(kn_ref[0], vn_ref[0], bn_ref[0])
        o_ref[0] = _softmax_result(l_ref, acc_ref)


def paged_attention(q, pages, table, npg, npp, k_spec, v_spec, kv_t, k_new, v_new, bias_past, bias_new, past_per_step):
    nseq, rows, _ = q.shape
    lv = acc_w = v_new.shape[1] if kv_t else v_new.shape[2]
    seq = lambda a: pl.BlockSpec((1,) + a.shape[1:], lambda b, p, pt: (b, 0, 0))
    wpast = bias_past.shape[-1] if not past_per_step else bias_past.shape[-1] // (npg // npp)
    bp_spec = pl.BlockSpec((1, rows, wpast), lambda b, p, pt: (
        b if bias_past.shape[0] > 1 else 0, 0, jnp.minimum(p, npg // npp - 1) if past_per_step else 0))
    bn_spec = pl.BlockSpec((1,) + bias_new.shape[1:], lambda b, p, pt: (b if bias_new.shape[0] > 1 else 0, 0, 0))
    return pl.pallas_call(
        functools.partial(_paged_attn_kernel, npp, kv_t),
        grid_spec=pltpu.PrefetchScalarGridSpec(
            num_scalar_prefetch=1,
            grid=(nseq, npg // npp + 1),
            in_specs=[seq(q)] + [k_spec(u) for u in range(npp)] + [v_spec(u) for u in range(npp)]
                     + [seq(k_new), seq(v_new), bp_spec, bn_spec],
            out_specs=pl.BlockSpec((1, rows, acc_w), lambda b, p, pt: (b, 0, 0)),
            scratch_shapes=[pltpu.VMEM((rows, LANES), F32), pltpu.VMEM((rows, LANES), F32),
                            pltpu.VMEM((rows, acc_w), F32)]),
        out_shape=jax.ShapeDtypeStruct((nseq, rows, lv), F32),
        compiler_params=_cparams("arbitrary", "arbitrary"),
        name="paged_attention",
    )(table, q, *([pages] * (2 * npp)), k_new, v_new, bias_past, bias_new)


def _diff_post_kernel(p_ref, cos_ref, sin_ref, g_ref, so_ref, q_ref, kvf_ref, kvb_ref):
    cos, sin = cos_ref[...], sin_ref[...]
    so = so_ref[...]
    for s in range(DIFF_WIDTH // LANES):
        sl = slice(s * LANES, (s + 1) * LANES)
        q = _rope_slab(_seg64_norm(p_ref[:, sl], g_ref[0:1, :], so), cos, sin)
        q_ref[:, sl] = (q * QK_SCALE_LOG2).astype(BF16)
        k = _rope_slab(_seg64_norm(p_ref[:, DIFF_WIDTH + s * LANES:DIFF_WIDTH + (s + 1) * LANES], g_ref[1:2, :], so),
                       cos, sin)
        kvf_ref[:, sl] = k
        kvb_ref[:, sl] = k.astype(BF16)
    v = p_ref[:, 2 * DIFF_WIDTH:3 * DIFF_WIDTH]
    kvf_ref[:, DIFF_WIDTH:] = v
    kvb_ref[:, DIFF_WIDTH:] = v.astype(BF16)


def diff_post(proj, cos, sin, qk_g):
    n = proj.shape[0]
    tm = min(256, n)
    g2 = jnp.tile(qk_g, (1, 2))
    so = _seg_ones()
    row = lambda w: pl.BlockSpec((tm, w), lambda i: (i, 0))
    const = lambda a: pl.BlockSpec(a.shape, lambda i: (0, 0))
    return pl.pallas_call(
        _diff_post_kernel,
        grid=(n // tm,),
        in_specs=[row(3 * DIFF_WIDTH), row(LANES), row(LANES), const(g2), const(so)],
        out_specs=[row(DIFF_WIDTH), row(2 * DIFF_WIDTH), row(2 * DIFF_WIDTH)],
        out_shape=[jax.ShapeDtypeStruct((n, DIFF_WIDTH), BF16), jax.ShapeDtypeStruct((n, 2 * DIFF_WIDTH), F32),
                   jax.ShapeDtypeStruct((n, 2 * DIFF_WIDTH), BF16)],
        compiler_params=_cparams("parallel"),
        name="diff_post",
    )(proj, cos, sin, g2, so)


def _diff_lambda(lam_ref, lam_init):
    lf = lam_ref[...]
    a = jnp.sum(lf[0:1, :] * lf[1:2, :], axis=-1, keepdims=True)
    b = jnp.sum(lf[2:3, :] * lf[3:4, :], axis=-1, keepdims=True)
    return jnp.exp(a) - jnp.exp(b) + lam_init


DIFF_SUB = 4
DIFF_SUB_TQ = 256
DIFF_TK = 1024


def _diff_flash_kernel(lam_init, tk, q_ref, k_ref, v_ref, lam_ref, o_ref, m_ref, acc_ref):
    i = pl.program_id(2)
    nsub = m_ref.shape[0]
    tq = q_ref.shape[1] // nsub
    rows = 2 * tq

    def stack_components(q):
        comp = lax.broadcasted_iota(jnp.int32, q.shape, 1) >> 6
        return jnp.concatenate([jnp.where(comp == c, q, 0.0) for c in range(2)], axis=0).astype(BF16)

    qst = [stack_components(q_ref[0, a * tq:(a + 1) * tq, :].astype(F32)) for a in range(nsub)]
    _softmax_init(m_ref, acc_ref)
    ones = jnp.ones((tk, LANES), BF16)

    def tile(j, causal):
        lo = pl.multiple_of(j * tk, tk)
        k = k_ref[0, pl.ds(lo, tk), :]
        v_aug = jnp.concatenate([v_ref[0, pl.ds(lo, tk), :], ones], axis=1)
        for a in range(nsub):
            s = _dot_nt(qst[a], k)
            if causal:
                qpos = (i * nsub + a) * tq + (lax.broadcasted_iota(jnp.int32, (rows, tk), 0) & (tq - 1))
                kpos = j * tk + lax.broadcasted_iota(jnp.int32, (rows, tk), 1)
                s = jnp.where(kpos <= qpos, s, NEG_INF)
            _softmax_update_vsum(s, lambda p: _dot(p, v_aug), m_ref.at[a], acc_ref.at[a])

    nfull = (i * nsub * tq) // tk

    def body(j, c):
        tile(j, False)
        return c

    lax.fori_loop(0, nfull, body, 0)
    tile(nfull, True)
    lam = _diff_lambda(lam_ref, lam_init)
    for a in range(nsub):
        acc = acc_ref[a]
        o = acc[:, 0:LANES] / acc[:, LANES:2 * LANES]
        o_ref[0, a * tq:(a + 1) * tq, :] = o[0:tq, :] - lam * o[tq:rows, :]


def diff_flash_prompt(q, kvb, lam, lam_init):
    b, t, _ = q.shape
    sub = DIFF_SUB_TQ
    tq = DIFF_SUB * sub
    tk = DIFF_TK
    assert tk % tq == 0 and t % tk == 0
    return pl.pallas_call(
        functools.partial(_diff_flash_kernel, lam_init, tk),
        grid=(b, DIFF_HEADS, t // tq),
        in_specs=[pl.BlockSpec((1, tq, LANES), lambda bi, h, i: (bi, i, h)),
                  pl.BlockSpec((1, t, LANES), lambda bi, h, i: (bi, 0, h)),
                  pl.BlockSpec((1, t, LANES), lambda bi, h, i: (bi, 0, DIFF_HEADS + h)),
                  pl.BlockSpec(lam.shape, lambda bi, h, i: (0, 0))],
        out_specs=pl.BlockSpec((1, tq, LANES), lambda bi, h, i: (bi, i, h)),
        out_shape=jax.ShapeDtypeStruct((b, t, DIFF_WIDTH), F32),
        scratch_shapes=[pltpu.VMEM((DIFF_SUB, 2 * sub, LANES), F32), pltpu.VMEM((DIFF_SUB, 2 * sub, 2 * LANES), F32)],
        compiler_params=_cparams("parallel", "parallel", "arbitrary"),
        name="diff_flash",
    )(q, kvb, kvb, lam)


def _diff_out_kernel(lam_init, two, *refs):
    if two:
        o0_ref, o1_ref, lam_ref, g_ref, w_ref, r_ref, out_ref, h_ref = refs
        o = o0_ref[...] - _diff_lambda(lam_ref, lam_init) * o1_ref[...]
    else:
        o0_ref, g_ref, w_ref, r_ref, out_ref, h_ref = refs
        o = o0_ref[...]
    for s in range(DIFF_HEADS):
        sl = slice(s * LANES, (s + 1) * LANES)
        h_ref[:, sl] = (_rms_rows(o[:, sl], g_ref[...]) * (1.0 - lam_init)).astype(BF16)
    out_ref[...] = r_ref[...] + _dot(h_ref[...], w_ref[...])


def diff_out(o, sub_g, w_o, res, lam_init, o1=None, lam=None):
    n, d = res.shape
    tm = min(512, n)
    two = o1 is not None
    row = lambda w: pl.BlockSpec((tm, w), lambda i: (i, 0))
    const = lambda a: pl.BlockSpec(a.shape, lambda i: (0, 0))
    g = sub_g.reshape(1, LANES)
    ins = [o, o1, lam, g, w_o, res] if two else [o, g, w_o, res]
    specs = ([row(DIFF_WIDTH), row(DIFF_WIDTH), const(lam)] if two else [row(DIFF_WIDTH)]) + [const(g), const(w_o), row(d)]
    return pl.pallas_call(
        functools.partial(_diff_out_kernel, lam_init, two),
        grid=(n // tm,),
        in_specs=specs,
        out_specs=row(d),
        out_shape=jax.ShapeDtypeStruct((n, d), F32),
        scratch_shapes=[pltpu.VMEM((tm, DIFF_WIDTH), BF16)],
        compiler_params=_cparams("parallel"),
        name="diff_out",
    )(*ins)


def _gla_kernel(chunk, nchunk, t_valid, p_ref, s0_ref, wa_ref, ba_ref, g_ref, tri_ref, y_ref, sfin_ref, st_ref):
    step = pl.program_id(1)

    @pl.when(step == 0)
    def _():
        st_ref[...] = s0_ref[0]

    nk = GLA_HEADS * GLA_DK
    nv = GLA_HEADS * GLA_DV
    tri = tri_ref[...]
    causal = lax.broadcasted_iota(jnp.int32, (chunk, chunk), 0) >= lax.broadcasted_iota(jnp.int32, (chunk, chunk), 1)
    for ci in range(nchunk):
        r0 = ci * chunk
        rows = slice(r0, r0 + chunk)
        a1 = p_ref[0, rows, GLA_MAIN:GLA_MAIN + LANES].astype(BF16)
        z = _dot(a1, wa_ref[...]) + ba_ref[...]
        log_a = (jnp.minimum(z, 0.0) - jnp.log(1.0 + jnp.exp(-jnp.abs(z)))) * (1.0 / GLA_TAU)
        if t_valid < chunk:
            log_a = jnp.where(lax.broadcasted_iota(jnp.int32, log_a.shape, 0) < t_valid, log_a, 0.0)
        cum = _split_dot_left(tri, log_a)
        for h in range(GLA_HEADS):
            ksl = slice(h * GLA_DK, (h + 1) * GLA_DK)
            vsl = slice(h * GLA_DV, (h + 1) * GLA_DV)
            q = p_ref[0, rows, h * GLA_DK:(h + 1) * GLA_DK] * (GLA_DK ** -0.5)
            k = p_ref[0, rows, nk + h * GLA_DK:nk + (h + 1) * GLA_DK]
            v = p_ref[0, rows, 2 * nk + h * GLA_DV:2 * nk + (h + 1) * GLA_DV]
            r = p_ref[0, rows, 2 * nk + nv + h * GLA_DV:2 * nk + nv + (h + 1) * GLA_DV]
            cm = cum[:, ksl]
            last = cm[chunk - 1:chunk, :]
            qe = (q * jnp.exp(cm)).astype(BF16)
            ke = (k * jnp.exp(-cm)).astype(BF16)
            vb = v.astype(BF16)
            att = jnp.where(causal, _dot_nt(qe, ke), 0.0)
            st = st_ref[h]
            o = _dot_nt(qe, st.astype(BF16)) + _dot(att.astype(BF16), vb)
            kd = (k * jnp.exp(last - cm)).astype(BF16)
            st_ref[h] = st * jnp.exp(last) + _dot(v.T.astype(BF16), kd)
            y_ref[0, rows, vsl] = _rms_rows(o, g_ref[...]) * _silu(r)

    @pl.when(step == pl.num_programs(1) - 1)
    def _():
        sfin_ref[0] = st_ref[...]


def gla_core(proj, s0_t, w_a2, b_a, out_g, chunk, nchunk, t_valid):
    b, t, _ = proj.shape
    rows = chunk * nchunk
    wa = jnp.zeros((LANES, GLA_HEADS * GLA_DK), F32).at[:GLA_RANK].set(w_a2).astype(BF16)
    ba = b_a.reshape(1, -1)
    g = out_g.reshape(1, GLA_DV)
    tri = (jnp.arange(chunk)[:, None] >= jnp.arange(chunk)[None, :]).astype(BF16)
    const = lambda a: pl.BlockSpec(a.shape, lambda bi, s: (0,) * a.ndim)
    st_spec = pl.BlockSpec((1, GLA_HEADS, GLA_DV, GLA_DK), lambda bi, s: (bi, 0, 0, 0))
    return pl.pallas_call(
        functools.partial(_gla_kernel, chunk, nchunk, t_valid),
        grid=(b, t // rows),
        in_specs=[pl.BlockSpec((1, rows, GLA_IN_PAD), lambda bi, s: (bi, s, 0)), st_spec,
                  const(wa), const(ba), const(g), const(tri)],
        out_specs=[pl.BlockSpec((1, rows, GLA_HEADS * GLA_DV), lambda bi, s: (bi, s, 0)), st_spec],
        out_shape=[jax.ShapeDtypeStruct((b, t, GLA_HEADS * GLA_DV), F32),
                   jax.ShapeDtypeStruct((b, GLA_HEADS, GLA_DV, GLA_DK), F32)],
        scratch_shapes=[pltpu.VMEM((GLA_HEADS, GLA_DV, GLA_DK), F32)],
        compiler_params=_cparams("parallel", "arbitrary"),
        name="gla_core",
    )(proj, s0_t, wa, ba, g, tri)


def _pad_cols(w, width):
    return jnp.pad(w, ((0, 0), (0, width - w.shape[1]))).astype(BF16)


def _pad_axis(x, axis, size):
    pad = [(0, 0)] * x.ndim
    pad[axis] = (0, size - x.shape[axis])
    return jnp.pad(x, pad)


def _group_diag(q, t_seq):
    nseq = q.shape[0] // t_seq
    qg = q.reshape(nseq, t_seq, NSA_GROUPS, NSA_REP, HEAD_DIM).transpose(0, 2, 1, 3, 4)
    eye = jnp.eye(NSA_GROUPS, dtype=q.dtype)
    out = qg[:, :, :, :, None, :] * eye[None, :, None, None, :, None]
    return out.reshape(nseq, NSA_GROUPS * t_seq * NSA_REP, NSA_GROUPS * HEAD_DIM)


def _group_undiag(o, t_seq):
    nseq = o.shape[0]
    o6 = o.reshape(nseq, NSA_GROUPS, t_seq, NSA_REP, NSA_GROUPS, HEAD_DIM)
    od = jnp.stack([o6[:, g, :, :, g, :] for g in range(NSA_GROUPS)], axis=1)
    return od.transpose(0, 2, 1, 3, 4).reshape(nseq * t_seq, NSA_HEADS * HEAD_DIM)


def _seq_cols(x_t, nseq, t_seq):
    r = x_t.shape[0]
    return _pad_axis(x_t.reshape(r, nseq, t_seq).transpose(1, 0, 2), 2, PAGE_SIZE)


def _nsa_layer(s, xp, xs, seq_p, t_s, past_len, cache_t, state_win_t, page_table, norm_g, w_in, qk_g, pe, w_phi, w_o):
    bp = xp.shape[0] // seq_p
    bs = xs.shape[0] // t_s
    w_o_b = w_o.astype(BF16)
    npg = past_len // PAGE_SIZE

    qn, qr, gates, rows_t, win_t, ksel, kwin = nsa_project(xp, norm_g, w_in, qk_g, jnp.arange(seq_p), bp)
    own = _PageIndex(jnp.zeros((1, 1), jnp.int32), lambda b, p, pt: (b, 0, p))
    kc, vc = nsa_compress(rows_t, own, bp, seq_p // PAGE_SIZE, pe, w_phi, qk_g[3])
    qn3, qr3 = qn.reshape(bp, seq_p, 1024), qr.reshape(bp, seq_p, 1024)
    oc, sel = nsa_cmp_select(qn3, kc, vc, 0, seq_p // NSA_BLOCK)
    os_ = nsa_selected_prompt(qr3, sel, ksel)
    ow = nsa_window_prompt(qr3, kwin)
    xp_new = nsa_out(oc.reshape(-1, 1024), os_.reshape(-1, 1024), ow.reshape(-1, 1024), gates, w_o_b, xp)
    kv_p = rows_t.reshape(bp, 4, NSA_GROUPS, HEAD_DIM, seq_p).transpose(0, 4, 1, 2, 3)
    wlen = min(NSA_WINDOW, seq_p)
    win_p = win_t[:, :, seq_p - wlen:].reshape(bp, 2, NSA_GROUPS, HEAD_DIM, wlen).transpose(0, 4, 1, 2, 3)

    pos_s = past_len + jnp.arange(t_s)
    qn, qr, gates, rows_t, win_t, _, _ = nsa_project(xs, norm_g, w_in, qk_g, jnp.tile(pos_s, bs), 1)
    rows_t, win_t = rows_t[0], win_t[0]
    layer_pages = cache_t.shape[0] // (state_win_t.shape[0] // bs)
    table = page_table + s * layer_pages
    paged = _PageIndex(table, lambda b, p, pt: (pt[b, p], 0, 0))
    kc, vc = nsa_compress(cache_t, paged, bs, npg, pe, w_phi, qk_g[3])
    ns = -(-(past_len + t_s) // NSA_BLOCK)
    oc, sel = nsa_cmp_select_sample(qn, kc, vc, past_len, t_s, 256)
    nrow = NSA_GROUPS * t_s * NSA_REP
    sel_f = sel[:, :, :ns].astype(F32).reshape(NSA_GROUPS, bs, t_s, ns).transpose(1, 0, 2, 3)
    by_row = lambda a: jnp.broadcast_to(a[:, :, :, None, :], a.shape[:3] + (NSA_REP, a.shape[-1])).reshape(
        a.shape[0], nrow, a.shape[-1])
    past_bias = by_row(jnp.repeat(sel_f[..., :past_len // NSA_BLOCK], NSA_BLOCK, axis=-1))
    new_ok = jnp.arange(PAGE_SIZE)[None, :] <= jnp.arange(t_s)[:, None]
    new_bias = by_row(jnp.where(new_ok[None, None], sel_f[..., past_len // NSA_BLOCK][..., None], NEG_INF))
    q_bd = _group_diag(qr, t_s)
    npp = min(NSA_PAGES_PER_STEP, npg)
    kpage = lambda blk: (lambda u: pl.BlockSpec((None, NSA_GW, PAGE_SIZE),
                                                lambda b, p, pt: (pt[b, jnp.minimum(p * npp + u, npg - 1)], blk, 0)))
    k_new = _seq_cols(rows_t[2 * NSA_GW:3 * NSA_GW], bs, t_s).astype(BF16)
    v_new = _seq_cols(rows_t[3 * NSA_GW:4 * NSA_GW], bs, t_s).astype(BF16)
    os_ = _group_undiag(paged_attention(q_bd, cache_t, table, npg, npp, kpage(2), kpage(3), True, k_new, v_new,
                                        past_bias, new_bias, True), t_s)
    wbuf = state_win_t.shape[-1]
    wpg = wbuf // PAGE_SIZE
    wpage = lambda blk: (lambda u: pl.BlockSpec((None, NSA_GW, PAGE_SIZE),
                                                lambda b, p, pt: (s * bs + b, blk, jnp.minimum(p * wpg + u, wpg - 1))))
    kidx = jnp.arange(wbuf + PAGE_SIZE)
    wpos = jnp.where(kidx < wbuf, past_len - wbuf + kidx, past_len + kidx - wbuf)
    w_ok = ((wpos[None, :] <= pos_s[:, None]) & (wpos[None, :] > pos_s[:, None] - NSA_WINDOW) & (wpos[None, :] >= 0)
            & (kidx[None, :] < wbuf + t_s))
    wbias = by_row(jnp.broadcast_to(jnp.where(w_ok, 0.0, NEG_INF).astype(F32)[None, None],
                                    (1, NSA_GROUPS, t_s, wbuf + PAGE_SIZE)))
    kw_new = _seq_cols(win_t[0:NSA_GW], bs, t_s).astype(BF16)
    vw_new = _seq_cols(win_t[NSA_GW:2 * NSA_GW], bs, t_s).astype(BF16)
    ow = _group_undiag(paged_attention(q_bd, state_win_t, jnp.zeros((1, 1), jnp.int32), wpg, wpg, wpage(0), wpage(1), True,
                                       kw_new, vw_new, wbias[:, :, :wbuf], wbias[:, :, wbuf:], True), t_s)
    xs_new = nsa_out(oc, os_, ow, gates, w_o_b, xs)
    kv_s = rows_t.reshape(4, NSA_GROUPS, HEAD_DIM, bs, t_s).transpose(3, 4, 0, 1, 2)
    win_new = win_t.reshape(2 * NSA_GW, bs, t_s).transpose(1, 0, 2)
    win_all = jnp.concatenate([state_win_t[s * bs:(s + 1) * bs], win_new], axis=2)
    wlen = min(NSA_WINDOW, win_all.shape[2])
    win_s = win_all[:, :, win_all.shape[2] - wlen:].reshape(bs, 2, NSA_GROUPS, HEAD_DIM, wlen).transpose(0, 4, 1, 2, 3)
    return xp_new, xs_new, kv_p, kv_s, win_p, win_s


def _diff_layer(layer, xp, xs, seq_p, t_s, past_len, cache, page_table, norm_g, w_in, qk_g, lam, sub_g, w_o):
    bp = xp.shape[0] // seq_p
    bs = xs.shape[0] // t_s
    npg = past_len // PAGE_SIZE
    lam_init = 0.8 - 0.6 * math.exp(-0.3 * layer)
    w_in_b = w_in.astype(BF16)
    w_o_b = w_o.astype(BF16)
    cos_p, sin_p = _rope_tables(jnp.arange(seq_p))
    proj = norm_matmul(xp, norm_g, w_in_b, 768)
    q, kvf, kvb = diff_post(proj, jnp.tile(cos_p, (bp, 1)), jnp.tile(sin_p, (bp, 1)), qk_g)
    o = diff_flash_prompt(q.reshape(bp, seq_p, DIFF_WIDTH), kvb.reshape(bp, seq_p, 2 * DIFF_WIDTH), lam, lam_init)
    xp_new = diff_out(o.reshape(-1, DIFF_WIDTH), sub_g, w_o_b, xp, lam_init)
    kv_p = kvf.reshape(bp, seq_p, 2, DIFF_HEADS, 2 * HEAD_DIM)
    pos_s = past_len + jnp.arange(t_s)
    cos_s, sin_s = _rope_tables(pos_s)
    proj = norm_matmul(xs, norm_g, w_in_b, 768)
    q, kvf, kvb = diff_post(proj, jnp.tile(cos_s, (bs, 1)), jnp.tile(sin_s, (bs, 1)), qk_g)
    nrow = DIFF_HEADS * 2 * t_s
    q5 = q.reshape(bs, t_s, DIFF_HEADS, 2, HEAD_DIM).transpose(0, 2, 3, 1, 4)
    q_rows = (q5[:, :, :, :, None, :] * jnp.eye(2, dtype=q.dtype)[None, None, :, None, :, None]).reshape(
        bs, nrow, 2 * HEAD_DIM)
    kv5 = kvb.reshape(bs, t_s, 2, DIFF_HEADS, 2 * HEAD_DIM)
    k_new = _pad_axis(kv5[:, :, 0], 1, PAGE_SIZE).reshape(bs, PAGE_SIZE * DIFF_HEADS, 2 * HEAD_DIM)
    v_new = _pad_axis(kv5[:, :, 1], 1, PAGE_SIZE).reshape(bs, PAGE_SIZE * DIFF_HEADS, 2 * HEAD_DIM)
    npp = min(DIFF_PAGES_PER_STEP, npg)
    row_h = jnp.arange(nrow) // (2 * t_s)
    row_t = jnp.arange(nrow) % t_s
    slot_h = jnp.arange(PAGE_SIZE * DIFF_HEADS) % DIFF_HEADS
    slot_tok = jnp.arange(PAGE_SIZE * DIFF_HEADS) // DIFF_HEADS
    same_head = row_h[:, None] == slot_h[None, :]
    bias_page = jnp.where(same_head, 0.0, NEG_INF).astype(F32)
    bias_past = jnp.tile(bias_page, (1, npp))[None]
    bias_new = jnp.where(same_head & (slot_tok[None, :] <= row_t[:, None]), 0.0, NEG_INF).astype(F32)[None]
    page = lambda slot: (lambda u: pl.BlockSpec(
        (None, PAGE_SIZE, None, DIFF_HEADS, 2 * HEAD_DIM),
        lambda b, p, pt: (pt[b, jnp.minimum(p * npp + u, npg - 1)], 0, slot, 0, 0)))
    o = paged_attention(q_rows, cache, page_table, npg, npp, page(0), page(1), False, k_new, v_new,
                        bias_past, bias_new, False)
    od = o.reshape(bs, DIFF_HEADS, 2, t_s, 2 * HEAD_DIM).transpose(2, 0, 3, 1, 4).reshape(2, bs * t_s, DIFF_WIDTH)
    xs_new = diff_out(od[0], sub_g, w_o_b, xs, lam_init, o1=od[1], lam=lam)
    kv_s = kvf.reshape(bs, t_s, 2, DIFF_HEADS, 2 * HEAD_DIM)
    return xp_new, xs_new, kv_p, kv_s


def _gla_layer(xp, xs, seq_p, t_s, state, norm_g, w_in, w_a2, b_a, out_g, w_o):
    bp = xp.shape[0] // seq_p
    bs = xs.shape[0] // t_s
    w_in_b = _pad_cols(w_in, GLA_IN_PAD)
    w_o_b = w_o.astype(BF16)
    proj = norm_matmul(xp, norm_g, w_in_b, 640).reshape(bp, seq_p, GLA_IN_PAD)
    chunk = min(GLA_CHUNK, seq_p)
    s0 = jnp.zeros((bp, GLA_HEADS, GLA_DV, GLA_DK), F32)
    y, st = gla_core(proj, s0, w_a2, b_a, out_g, chunk, 4 if seq_p % (4 * chunk) == 0 else 1, chunk)
    xp_new = matmul_residual(y.reshape(-1, GLA_HEADS * GLA_DV), w_o_b, xp)
    st_p = st.transpose(0, 1, 3, 2)
    chunk_s = GLA_CHUNK
    proj = norm_matmul(xs, norm_g, w_in_b, 640).reshape(bs, t_s, GLA_IN_PAD)
    proj = _pad_axis(proj, 1, chunk_s)
    y, st = gla_core(proj, state.transpose(0, 1, 3, 2), w_a2, b_a, out_g, chunk_s, 1, t_s)
    xs_new = matmul_residual(y[:, :t_s].reshape(-1, GLA_HEADS * GLA_DV), w_o_b, xs)
    st_s = st.transpose(0, 1, 3, 2)
    return xp_new, xs_new, st_p, st_s


def kernel(x_prompt, x_sample, cache_nsa_kv, state_nsa_win, cache_diff_kv, state_gla, state_ffn, page_table, norm_g, ffn_w_up, ffn_conv_w, ffn_conv_b, ffn_w_down, nsa_w_in, nsa_qk_g, nsa_pe, nsa_w_phi, nsa_w_o, diff_w_in, diff_qk_g, diff_lam, diff_sub_g, diff_w_o, gla_w_in, gla_w_a2, gla_b_a, gla_out_g, gla_w_o):
    bp, seq_p, d = x_prompt.shape
    bs, t_s, _ = x_sample.shape
    past_len = page_table.shape[1] * PAGE_SIZE
    xp = x_prompt.reshape(bp * seq_p, d)
    xs = x_sample.reshape(bs * t_s, d)
    cache_t = cache_nsa_kv.transpose(0, 1, 3, 4, 5, 2).reshape(-1, 4 * NSA_GW, PAGE_SIZE)
    win_t = state_nsa_win.transpose(0, 1, 3, 4, 5, 2).reshape(-1, 2 * NSA_GW, state_nsa_win.shape[2])
    nsa_kv_p, nsa_kv_s, nsa_win_p, nsa_win_s = [], [], [], []
    diff_kv_p, diff_kv_s, gla_p, gla_s, ffn_p, ffn_s = [], [], [], [], [], []
    for i in range(DEPTH):
        kind, s = i % N_MIXERS, i // N_MIXERS
        if kind == 0:
            xp, xs, kvp, kvs, wp, ws = _nsa_layer(
                s, xp, xs, seq_p, t_s, past_len, cache_t, win_t, page_table,
                norm_g[i, 0], nsa_w_in[s], nsa_qk_g[s], nsa_pe[s], nsa_w_phi[s], nsa_w_o[s])
            nsa_kv_p.append(kvp); nsa_kv_s.append(kvs); nsa_win_p.append(wp); nsa_win_s.append(ws)
        elif kind == 1:
            xp, xs, kvp, kvs = _diff_layer(
                i, xp, xs, seq_p, t_s, past_len, cache_diff_kv[s], page_table, norm_g[i, 0], diff_w_in[s],
                diff_qk_g[s], diff_lam[s], diff_sub_g[s], diff_w_o[s])
            diff_kv_p.append(kvp); diff_kv_s.append(kvs)
        else:
            xp, xs, stp, sts = _gla_layer(xp, xs, seq_p, t_s, state_gla[s], norm_g[i, 0], gla_w_in[s], gla_w_a2[s],
                                          gla_b_a[s], gla_out_g[s], gla_w_o[s])
            gla_p.append(stp); gla_s.append(sts)
        w_up_b = ffn_w_up[i].astype(BF16)
        w_dn_b = ffn_w_down[i].astype(BF16)
        xp, tail_p = ffn_prompt(xp, norm_g[i, 1], w_up_b, ffn_conv_w[i], ffn_conv_b[i], w_dn_b, seq_p)
        xs, tail_s = ffn_sample(xs, norm_g[i, 1], w_up_b, ffn_conv_w[i], ffn_conv_b[i], w_dn_b, state_ffn[i], t_s)
        ffn_p.append(tail_p); ffn_s.append(tail_s)
    return (xp.reshape(bp, seq_p, d), xs.reshape(bs, t_s, d),
            jnp.stack(nsa_kv_p), jnp.stack(nsa_kv_s), jnp.stack(nsa_win_p), jnp.stack(nsa_win_s),
            jnp.stack(diff_kv_p), jnp.stack(diff_kv_s), jnp.stack(gla_p), jnp.stack(gla_s),
            jnp.stack(ffn_p), jnp.stack(ffn_s))
```

```python
import functools
import math

import jax
import jax.numpy as jnp
from jax import lax
from jax.experimental import pallas as pl
from jax.experimental.pallas import tpu as pltpu

F32 = jnp.float32
BF16 = jnp.bfloat16

D_MODEL = 1024
HEAD_DIM = 64
ROPE_THETA = 10000.0
NORM_EPS = 1e-6
NEG_INF = -1e30
DEPTH = 4
N_MIXERS = 3
PAGE_SIZE = 128
NSA_HEADS = 16
NSA_GROUPS = 4
NSA_REP = 4
NSA_BLOCK = 64
NSA_TOPN = 16
NSA_WINDOW = 512
NSA_FORCED = 1e9
NSA_NQ = NSA_HEADS * HEAD_DIM
NSA_NKV = 6 * NSA_GROUPS * HEAD_DIM
NSA_NGATE = 3 * NSA_HEADS
NSA_GW = NSA_GROUPS * HEAD_DIM
DIFF_HEADS = 8
DIFF_WIDTH = 1024
GLA_HEADS = 4
GLA_DK = 128
GLA_DV = 256
GLA_RANK = 16
GLA_TAU = 16.0
GLA_CHUNK = 64
GLA_MAIN = 2 * GLA_HEADS * GLA_DK + 2 * GLA_HEADS * GLA_DV
GLA_IN_PAD = GLA_MAIN + 128
D_FF = 2816
CONV_W = 3
LANES = 128
VMEM_LIMIT = 56 * 1024 * 1024
SEL_TK = 512
NSA_PAGES_PER_STEP = 16
DIFF_PAGES_PER_STEP = 8
CMP_PITCH = HEAD_DIM + 8


def _cparams(*sem):
    return pltpu.CompilerParams(dimension_semantics=sem, vmem_limit_bytes=VMEM_LIMIT)


def _dot(a, b):
    return jnp.dot(a, b, preferred_element_type=F32)


def _dot_nt(a, b):
    return lax.dot_general(a, b, (((1,), (1,)), ((), ())), preferred_element_type=F32)


def _split_dot(x, m):
    hi = x.astype(BF16)
    lo = (x - hi.astype(F32)).astype(BF16)
    return _dot(hi, m) + _dot(lo, m)


def _split_dot_left(m, x):
    hi = x.astype(BF16)
    lo = (x - hi.astype(F32)).astype(BF16)
    return _dot(m, hi) + _dot(m, lo)


def _rms_rows(x, g):
    ms = jnp.mean(x * x, axis=-1, keepdims=True)
    return x * lax.rsqrt(ms + NORM_EPS) * g


def _seg64_norm(x, g, seg_ones):
    ms = _split_dot(x * x, seg_ones) * (1.0 / HEAD_DIM)
    return x * lax.rsqrt(ms + NORM_EPS) * g


def _rope_slab(x, cos, sin_signed):
    lane = lax.broadcasted_iota(jnp.int32, x.shape, 1)
    first = (lane & 63) < 32
    partner = jnp.where(first, pltpu.roll(x, 96, 1), pltpu.roll(x, 32, 1))
    return x * cos + partner * sin_signed


def _rope_angles(pos):
    half = HEAD_DIM // 2
    inv = ROPE_THETA ** (-jnp.arange(half, dtype=F32) / half)
    ang = pos.astype(F32)[:, None] * inv
    return jnp.cos(ang), jnp.sin(ang)


def _rope_tables(pos):
    cos, sin = _rope_angles(pos)
    return jnp.tile(cos, (1, 4)), jnp.tile(jnp.concatenate([-sin, sin], axis=1), (1, 2))


def _seg_ones():
    i = jnp.arange(LANES)
    return (i[:, None] // HEAD_DIM == i[None, :] // HEAD_DIM).astype(BF16)


def _lanes(x, width):
    return x if width == LANES else jnp.tile(x, (1, width // LANES))


def _softmax_update(s, v_dot, m_ref, l_ref, acc_ref):
    m = m_ref[...]
    m_new = jnp.maximum(m, jnp.max(s, axis=-1, keepdims=True))
    alpha = jnp.exp2(m - m_new)
    p = jnp.exp2(s - _lanes(m_new, s.shape[1]))
    m_ref[...] = m_new
    l_ref[...] = alpha * l_ref[...] + jnp.sum(p, axis=-1, keepdims=True)
    acc_ref[...] = _lanes(alpha, acc_ref.shape[-1]) * acc_ref[...] + v_dot(p.astype(BF16))


def _softmax_result(l_ref, acc_ref):
    return acc_ref[...] / _lanes(l_ref[...], acc_ref.shape[-1])


def _softmax_update_vsum(s, v_dot, m_ref, acc_ref):
    m = m_ref[...]
    m_new = jnp.maximum(m, jnp.max(s, axis=-1, keepdims=True))
    alpha = jnp.exp2(m - m_new)
    p = jnp.exp2(s - _lanes(m_new, s.shape[1]))
    m_ref[...] = m_new
    acc_ref[...] = _lanes(alpha, acc_ref.shape[-1]) * acc_ref[...] + v_dot(p.astype(BF16))


def _softmax_init(m_ref, acc_ref, l_ref=None):
    m_ref[...] = jnp.full_like(m_ref, NEG_INF)
    acc_ref[...] = jnp.zeros_like(acc_ref)
    if l_ref is not None:
        l_ref[...] = jnp.zeros_like(l_ref)


QK_SCALE = HEAD_DIM ** -0.5
QK_SCALE_LOG2 = QK_SCALE * math.log2(math.e)


def _norm_matmul_kernel(x_ref, g_ref, w_ref, o_ref, h_ref):
    @pl.when(pl.program_id(1) == 0)
    def _():
        h_ref[...] = _rms_rows(x_ref[...], g_ref[...]).astype(BF16)

    o_ref[...] = _dot(h_ref[...], w_ref[...])


def norm_matmul(x, g, w, tn):
    n, d = x.shape
    nout = w.shape[1]
    tm = min(512, n)
    return pl.pallas_call(
        _norm_matmul_kernel,
        grid=(n // tm, nout // tn),
        in_specs=[pl.BlockSpec((tm, d), lambda i, j: (i, 0)),
                  pl.BlockSpec((1, d), lambda i, j: (0, 0)),
                  pl.BlockSpec((d, tn), lambda i, j: (0, j))],
        out_specs=pl.BlockSpec((tm, tn), lambda i, j: (i, j)),
        out_shape=jax.ShapeDtypeStruct((n, nout), F32),
        scratch_shapes=[pltpu.VMEM((tm, d), BF16)],
        compiler_params=_cparams("parallel", "arbitrary"),
        name="norm_matmul",
    )(x, g.reshape(1, d), w)


def _matmul_res_kernel(a_ref, w_ref, r_ref, o_ref):
    o_ref[...] = r_ref[...] + _dot(a_ref[...].astype(BF16), w_ref[...])


def matmul_residual(a, w, res):
    n, k = a.shape
    d = w.shape[1]
    tm = min(512, n)
    return pl.pallas_call(
        _matmul_res_kernel,
        grid=(n // tm,),
        in_specs=[pl.BlockSpec((tm, k), lambda i: (i, 0)),
                  pl.BlockSpec((k, d), lambda i: (0, 0)),
                  pl.BlockSpec((tm, d), lambda i: (i, 0))],
        out_specs=pl.BlockSpec((tm, d), lambda i: (i, 0)),
        out_shape=jax.ShapeDtypeStruct((n, d), F32),
        compiler_params=_cparams("parallel"),
        name="matmul_residual",
    )(a, w, res)


FFN_CHUNK = 256


def _silu(x):
    return x / (1.0 + jnp.exp(-x))


def _causal_conv(u, prev1, prev2, use1, use2, cw_ref, cb_ref):
    u1 = jnp.where(use1, prev1, pltpu.roll(u, 1, 0))
    u2 = jnp.where(use2, prev2, pltpu.roll(u, 2, 0))
    return cb_ref[...] + cw_ref[0:1, :] * u2 + cw_ref[1:2, :] * u1 + cw_ref[2:3, :] * u


def _ffn_prompt_kernel(tiles_per_seq, x_ref, g_ref, wg_ref, wu_ref, cwg_ref, cwu_ref, cbg_ref, cbu_ref,
                       wd_ref, o_ref, tg_ref, tu_ref, cg_ref, cu_ref):
    i = pl.program_id(0)
    tm = x_ref.shape[0]
    nj, _, c = wg_ref.shape
    x = x_ref[...]
    h = _rms_rows(x, g_ref[...]).astype(BF16)
    row = lax.broadcasted_iota(jnp.int32, (tm, c), 0)

    @pl.when((i % tiles_per_seq) == 0)
    def _():
        cg_ref[...] = jnp.zeros_like(cg_ref)
        cu_ref[...] = jnp.zeros_like(cu_ref)

    def conv(u, prev, cw_ref, cb_ref):
        prev2 = jnp.where(row == 0, prev[0:1, :], prev[1:2, :])
        return _causal_conv(u, prev[1:2, :], prev2, row == 0, row < 2, cw_ref, cb_ref)

    acc = x
    for j in range(nj):
        ug = _dot(h, wg_ref[j])
        uu = _dot(h, wu_ref[j])
        act = _silu(conv(ug, cg_ref[j], cwg_ref.at[j], cbg_ref.at[j])) * conv(uu, cu_ref[j], cwu_ref.at[j], cbu_ref.at[j])
        acc = acc + _dot(act.astype(BF16), wd_ref[j])
        cg_ref[j] = ug[tm - 2:tm, :]
        cu_ref[j] = uu[tm - 2:tm, :]
        tg_ref[0, :, j * c:(j + 1) * c] = ug[tm - 2:tm, :]
        tu_ref[0, :, j * c:(j + 1) * c] = uu[tm - 2:tm, :]
    o_ref[...] = acc


def _resident(a):
    return pl.BlockSpec(a.shape, lambda i: (0,) * a.ndim, pipeline_mode=pl.Buffered(1))


def ffn_prompt(x, g, w_up, conv_w, conv_b, w_down, seq_len):
    n, d = x.shape
    c = FFN_CHUNK
    nj = D_FF // c
    tm = min(512, seq_len)
    tps = seq_len // tm
    chunks = lambda a: a.reshape(a.shape[0], 2, nj, c).transpose(1, 2, 0, 3)
    wg, wu = chunks(w_up)
    cwg, cwu = chunks(conv_w)
    cbg, cbu = chunks(conv_b.reshape(1, 2 * D_FF))
    wd = w_down.reshape(nj, c, d)
    g2 = g.reshape(1, d)
    consts = [g2, wg, wu, cwg, cwu, cbg, cbu, wd]
    out, tg, tu = pl.pallas_call(
        functools.partial(_ffn_prompt_kernel, tps),
        grid=(n // tm,),
        in_specs=[pl.BlockSpec((tm, d), lambda i: (i, 0))] + [_resident(a) for a in consts],
        out_specs=[pl.BlockSpec((tm, d), lambda i: (i, 0)),
                   pl.BlockSpec((1, 2, D_FF), lambda i: (i, 0, 0)),
                   pl.BlockSpec((1, 2, D_FF), lambda i: (i, 0, 0))],
        out_shape=[jax.ShapeDtypeStruct((n, d), F32),
                   jax.ShapeDtypeStruct((n // tm, 2, D_FF), F32),
                   jax.ShapeDtypeStruct((n // tm, 2, D_FF), F32)],
        scratch_shapes=[pltpu.VMEM((nj, 2, c), F32),
                        pltpu.VMEM((nj, 2, c), F32)],
        compiler_params=_cparams("arbitrary"),
        name="ffn_prompt",
    )(x, *consts)
    return out, jnp.concatenate([tg, tu], axis=-1)[tps - 1::tps]


def _ffn_sample_kernel(t_seq, x_ref, g_ref, wg_ref, wu_ref, cwg_ref, cwu_ref, cbg_ref, cbu_ref, wd_ref,
                       p1g_ref, p2g_ref, p1u_ref, p2u_ref, o_ref, ug_ref, uu_ref, h_ref):
    j = pl.program_id(0)

    @pl.when(j == 0)
    def _():
        h_ref[...] = _rms_rows(x_ref[...], g_ref[...]).astype(BF16)

    h = h_ref[...]
    ug = _dot(h, wg_ref[...])
    uu = _dot(h, wu_ref[...])
    t = lax.broadcasted_iota(jnp.int32, ug.shape, 0) & (t_seq - 1)
    cg = _causal_conv(ug, p1g_ref[...], p2g_ref[...], t == 0, t < 2, cwg_ref, cbg_ref)
    cu = _causal_conv(uu, p1u_ref[...], p2u_ref[...], t == 0, t < 2, cwu_ref, cbu_ref)
    part = _dot((_silu(cg) * cu).astype(BF16), wd_ref[...])
    ug_ref[...] = ug
    uu_ref[...] = uu

    @pl.when(j == 0)
    def _():
        o_ref[...] = x_ref[...] + part

    @pl.when(j > 0)
    def _():
        o_ref[...] += part


def ffn_sample(x, g, w_up, conv_w, conv_b, w_down, buf, t_seq):
    n, d = x.shape
    nseq = n // t_seq
    c = FFN_CHUNK
    nj = D_FF // c
    cb = conv_b.reshape(1, 2 * D_FF)
    reps = t_seq // 2
    prev1 = jnp.concatenate([buf[:, ::-1]] * reps, axis=1).reshape(n, 2 * D_FF)
    prev2 = jnp.concatenate([buf] * reps, axis=1).reshape(n, 2 * D_FF)
    full = lambda blk, off=0: pl.BlockSpec(blk, lambda j: (0, j + off))
    out, ug, uu = pl.pallas_call(
        functools.partial(_ffn_sample_kernel, t_seq),
        grid=(nj,),
        in_specs=[pl.BlockSpec((n, d), lambda j: (0, 0)),
                  pl.BlockSpec((1, d), lambda j: (0, 0)),
                  full((d, c)), full((d, c), nj),
                  full((CONV_W, c)), full((CONV_W, c), nj),
                  full((1, c)), full((1, c), nj),
                  pl.BlockSpec((c, d), lambda j: (j, 0)),
                  full((n, c)), full((n, c)), full((n, c), nj), full((n, c), nj)],
        out_specs=[pl.BlockSpec((n, d), lambda j: (0, 0)),
                   full((n, c)), full((n, c))],
        out_shape=[jax.ShapeDtypeStruct((n, d), F32),
                   jax.ShapeDtypeStruct((n, D_FF), F32),
                   jax.ShapeDtypeStruct((n, D_FF), F32)],
        scratch_shapes=[pltpu.VMEM((n, d), BF16)],
        compiler_params=_cparams("arbitrary"),
        name="ffn_sample",
    )(x, g.reshape(1, d), w_up, w_up, conv_w, conv_w, cb, cb, w_down, prev1, prev2, prev1, prev2)
    u = jnp.concatenate([ug, uu], axis=-1).reshape(nseq, t_seq, 2 * D_FF)
    return out, u[:, t_seq - (CONV_W - 1):]


def _head_norm_t(x, g):
    ms = jnp.mean(x * x, axis=0, keepdims=True)
    return x * lax.rsqrt(ms + NORM_EPS) * g


def _rope_t(x, cos, sin):
    half = HEAD_DIM // 2
    x1, x2 = x[0:half, :], x[half:HEAD_DIM, :]
    return jnp.concatenate([x1 * cos - x2 * sin, x1 * sin + x2 * cos], axis=0)


def _nsa_project_kernel(x_ref, g_ref, wq_ref, wkv_ref, wg_ref, cos_ref, sin_ref, cost_ref, sint_ref, gq_ref, gk_ref,
                        so_ref, qn_ref, qr_ref, gate_ref, rows_ref, win_ref, ksel_ref, kwin_ref):
    hn = _rms_rows(x_ref[...], g_ref[...]).astype(BF16)
    cos, sin = cos_ref[...], sin_ref[...]
    so = so_ref[...]
    q_all = _dot(hn, wq_ref[...])
    for s in range(NSA_NQ // LANES):
        sl = slice(s * LANES, (s + 1) * LANES)
        q = _seg64_norm(q_all[:, sl], gq_ref[...], so)
        qn_ref[:, sl] = (q * QK_SCALE).astype(BF16)
        qr_ref[:, sl] = (_rope_slab(q, cos, sin) * QK_SCALE_LOG2).astype(BF16)
    gate_ref[...] = 1.0 / (1.0 + jnp.exp(-_dot(hn, wg_ref[...])))
    kvt = _dot_nt(wkv_ref[...], hn)
    gw = NSA_GW
    cost, sint = cost_ref[...], sint_ref[...]
    rows_ref[0, 0:2 * gw, :] = kvt[0:2 * gw, :]
    rows_ref[0, 3 * gw:4 * gw, :] = kvt[3 * gw:4 * gw, :]
    win_ref[0, gw:2 * gw, :] = kvt[5 * gw:6 * gw, :]
    for g in range(NSA_GROUPS):
        hs = slice(g * HEAD_DIM, (g + 1) * HEAD_DIM)
        rows_ref[0, 2 * gw + g * HEAD_DIM:2 * gw + (g + 1) * HEAD_DIM, :] = _rope_t(
            _head_norm_t(kvt[2 * gw + g * HEAD_DIM:2 * gw + (g + 1) * HEAD_DIM, :], gk_ref[0]), cost, sint)
        win_ref[0, hs, :] = _rope_t(
            _head_norm_t(kvt[4 * gw + g * HEAD_DIM:4 * gw + (g + 1) * HEAD_DIM, :], gk_ref[1]), cost, sint)
    tm = kvt.shape[1]
    for g in range(NSA_GROUPS):
        hs = slice(g * HEAD_DIM, (g + 1) * HEAD_DIM)
        ksel_ref[0, 0, g, 0:HEAD_DIM, :] = rows_ref[0, 2 * gw + g * HEAD_DIM:2 * gw + (g + 1) * HEAD_DIM, :].astype(BF16)
        ksel_ref[0, 0, g, HEAD_DIM:2 * HEAD_DIM, :] = jnp.zeros((HEAD_DIM, tm), BF16)
        ksel_ref[0, 0, g, 2 * HEAD_DIM:3 * HEAD_DIM, :] = kvt[3 * gw + g * HEAD_DIM:3 * gw + (g + 1) * HEAD_DIM, :].astype(BF16)
        ksel_ref[0, 0, g, 3 * HEAD_DIM:4 * HEAD_DIM, :] = jnp.ones((HEAD_DIM, tm), BF16)
    wb = win_ref[0].astype(BF16)
    for u in range(kwin_ref.shape[1]):
        ts = slice(u * LANES, (u + 1) * LANES)
        for g in range(NSA_GROUPS):
            kwin_ref[0, u, g, 0:HEAD_DIM, :] = wb[g * HEAD_DIM:(g + 1) * HEAD_DIM, ts]
            kwin_ref[0, u, g, HEAD_DIM:2 * HEAD_DIM, :] = jnp.zeros((HEAD_DIM, LANES), BF16)
            kwin_ref[0, u, g, 2 * HEAD_DIM:3 * HEAD_DIM, :] = wb[gw + g * HEAD_DIM:gw + (g + 1) * HEAD_DIM, ts]
            kwin_ref[0, u, g, 3 * HEAD_DIM:4 * HEAD_DIM, :] = jnp.ones((HEAD_DIM, LANES), BF16)


def nsa_project(x, norm_g, w_in, qk_g, pos, nseq):
    n, d = x.shape
    t = n // nseq
    tm = min(SEL_TK, t)
    nt = t // tm
    wq = w_in[:, :NSA_NQ].astype(BF16)
    wkv_t = w_in[:, NSA_NQ:NSA_NQ + NSA_NKV].T.astype(BF16)
    wg = jnp.pad(w_in[:, NSA_NQ + NSA_NKV:], ((0, 0), (0, LANES - NSA_NGATE))).astype(BF16)
    cos, sin = _rope_angles(pos)
    cos_q, sin_q = jnp.tile(cos, (1, 4)), jnp.tile(jnp.concatenate([-sin, sin], axis=1), (1, 2))
    gq = jnp.tile(qk_g[0:1], (1, 2))
    gk = qk_g[1:3].reshape(2, HEAD_DIM, 1)
    so = _seg_ones()
    const = lambda a: pl.BlockSpec(a.shape, lambda b, i: (0,) * a.ndim)
    row = lambda w: pl.BlockSpec((tm, w), lambda b, i: (b * nt + i, 0))
    return pl.pallas_call(
        _nsa_project_kernel,
        grid=(nseq, nt),
        in_specs=[row(d), const(norm_g.reshape(1, d)), const(wq), const(wkv_t), const(wg),
                  pl.BlockSpec((tm, LANES), lambda b, i: (i, 0)), pl.BlockSpec((tm, LANES), lambda b, i: (i, 0)),
                  pl.BlockSpec((HEAD_DIM // 2, tm), lambda b, i: (0, i)),
                  pl.BlockSpec((HEAD_DIM // 2, tm), lambda b, i: (0, i)),
                  const(gq), const(gk), const(so)],
        out_specs=[row(NSA_NQ), row(NSA_NQ), row(LANES),
                   pl.BlockSpec((1, 4 * NSA_GW, tm), lambda b, i: (b, 0, i)),
                   pl.BlockSpec((1, 2 * NSA_GW, tm), lambda b, i: (b, 0, i)),
                   pl.BlockSpec((1, 1, NSA_GROUPS, 4 * HEAD_DIM, tm), lambda b, i: (b, i, 0, 0, 0)),
                   pl.BlockSpec((1, tm // LANES, NSA_GROUPS, 4 * HEAD_DIM, LANES), lambda b, i: (b, i, 0, 0, 0))],
        out_shape=[jax.ShapeDtypeStruct((n, NSA_NQ), BF16), jax.ShapeDtypeStruct((n, NSA_NQ), BF16),
                   jax.ShapeDtypeStruct((n, LANES), F32),
                   jax.ShapeDtypeStruct((nseq, 4 * NSA_GW, t), F32),
                   jax.ShapeDtypeStruct((nseq, 2 * NSA_GW, t), F32),
                   jax.ShapeDtypeStruct((nseq, nt, NSA_GROUPS, 4 * HEAD_DIM, tm), BF16),
                   jax.ShapeDtypeStruct((nseq, t // LANES, NSA_GROUPS, 4 * HEAD_DIM, LANES), BF16)],
        compiler_params=_cparams("parallel", "parallel"),
        name="nsa_project",
    )(x, norm_g.reshape(1, d), wq, wkv_t, wg, cos_q, sin_q, cos.T, sin.T, gq, gk, so)


def _nsa_compress_kernel(npp, pt_ref, *refs):
    page_refs = refs[:npp]
    pe_ref, w_ref, g_ref, so_ref, kc_ref, vc_ref, seqk_ref, seqv_ref, acc_ref = refs[npp:]
    p = pl.program_id(1)
    gw = NSA_GW
    pitch = CMP_PITCH
    for u in range(npp):
        for g in range(NSA_GROUPS):
            lo = pl.multiple_of(((p * npp + u) * NSA_GROUPS + g) * pitch, 8)
            seqk_ref[pl.ds(lo, HEAD_DIM), :] = page_refs[u][g * HEAD_DIM:(g + 1) * HEAD_DIM, :]
            seqv_ref[pl.ds(lo, HEAD_DIM), :] = page_refs[u][gw + g * HEAD_DIM:gw + (g + 1) * HEAD_DIM, :]

    @pl.when(p == pl.num_programs(1) - 1)
    def _():
        nrow = seqk_ref.shape[0] // pitch
        acc_ref[...] = jnp.zeros_like(acc_ref)
        for dd in range(HEAD_DIM):
            a = jnp.concatenate([seqk_ref[pl.ds(dd, nrow, stride=pitch), :],
                                 seqv_ref[pl.ds(dd, nrow, stride=pitch), :]], axis=1) + pe_ref[dd:dd + 1, :]
            acc_ref[...] += _dot(a.astype(BF16), w_ref[dd])
        kc_ref[0] = _seg64_norm(acc_ref[:, 0:LANES], g_ref[...], so_ref[...])
        vc_ref[0] = acc_ref[:, LANES:2 * LANES]


def nsa_compress(pages_t, page_index, nseq, npg, pe, w_phi, g_c):
    npp = min(NSA_PAGES_PER_STEP, npg)
    nrow = npg * NSA_GROUPS
    pe_t = jnp.concatenate([jnp.tile(pe[0].T, (1, 2)), jnp.tile(pe[1].T, (1, 2))], axis=1)
    eye4 = jnp.eye(4, dtype=F32)
    w4 = jnp.stack([w_phi[0], w_phi[0], w_phi[1], w_phi[1]])
    w = jnp.einsum('ab,alde->dalbe', eye4, w4).reshape(HEAD_DIM, 2 * LANES, 2 * LANES).astype(BF16)
    g2 = jnp.tile(g_c.reshape(1, HEAD_DIM), (1, 2))
    so = _seg_ones()
    const = lambda a: pl.BlockSpec(a.shape, lambda b, p, pt: (0,) * a.ndim, pipeline_mode=pl.Buffered(1))
    page = lambda u: pl.BlockSpec((None, 2 * NSA_GW, PAGE_SIZE), lambda b, p, pt: page_index(b, p * npp + u, pt))
    out = pl.BlockSpec((1, nrow, LANES), lambda b, p, pt: (b, 0, 0))
    table = page_index.table
    kc, vc = pl.pallas_call(
        functools.partial(_nsa_compress_kernel, npp),
        grid_spec=pltpu.PrefetchScalarGridSpec(
            num_scalar_prefetch=1,
            grid=(nseq, npg // npp),
            in_specs=[page(u) for u in range(npp)] + [const(pe_t), const(w), const(g2), const(so)],
            out_specs=[out, out],
            scratch_shapes=[pltpu.VMEM((nrow * CMP_PITCH, PAGE_SIZE), F32), pltpu.VMEM((nrow * CMP_PITCH, PAGE_SIZE), F32),
                            pltpu.VMEM((nrow, 2 * LANES), F32)]),
        out_shape=[jax.ShapeDtypeStruct((nseq, nrow, LANES), F32)] * 2,
        compiler_params=_cparams("arbitrary", "arbitrary"),
        name="nsa_compress",
    )(table, *([pages_t] * npp), pe_t, w, g2, so)
    fix = lambda a: a.reshape(nseq, npg, NSA_GROUPS, 2, HEAD_DIM).transpose(0, 2, 1, 3, 4).reshape(
        nseq, NSA_GROUPS, 2 * npg, HEAD_DIM)
    return fix(kc), fix(vc)


class _PageIndex:
    def __init__(self, table, fn):
        self.table = table
        self._fn = fn

    def __call__(self, b, p, pt):
        return self._fn(b, p, pt)


def _stack_heads(q):
    head = lax.broadcasted_iota(jnp.int32, q.shape, 1) >> 6
    qf = q.astype(F32)
    return jnp.concatenate([jnp.where(head == r, qf, 0.0) for r in range(NSA_REP)], axis=0).astype(BF16)


def _unstack_heads(o4, tq):
    head = lax.broadcasted_iota(jnp.int32, (tq, o4.shape[1]), 1) >> 6
    out = jnp.zeros((tq, o4.shape[1]), F32)
    for r in range(NSA_REP):
        out = jnp.where(head == r, o4[r * tq:(r + 1) * tq, :], out)
    return out


def _tile_rows4(x):
    return jnp.concatenate([x] * NSA_REP, axis=0)


def _nsa_cmp_kernel(q0, ns_rows, q_ref, kc_ref, vc_ref, oc_ref, sel_ref):
    i = pl.program_id(2)
    tq = q_ref.shape[1]
    nc = kc_ref.shape[2]
    qst = _stack_heads(q_ref[0])
    kc, vc = kc_ref[0, 0], vc_ref[0, 0]
    base = q0 + i * tq
    qpos = base + (lax.broadcasted_iota(jnp.int32, (NSA_REP * tq, nc), 0) & (tq - 1))
    blk_end = (lax.broadcasted_iota(jnp.int32, (NSA_REP * tq, nc), 1) + 1) * NSA_BLOCK - 1
    ok = blk_end <= qpos
    s = jnp.where(ok, _dot_nt(qst, kc), NEG_INF)
    e = jnp.exp(s - jnp.max(s, axis=-1, keepdims=True))
    p = jnp.where(ok, e / jnp.sum(e, axis=-1, keepdims=True), 0.0)
    oc_ref[0] = _unstack_heads(_dot(p.astype(BF16), vc), tq)
    imp = p[0:tq, :]
    for r in range(1, NSA_REP):
        imp = imp + p[r * tq:(r + 1) * tq, :]
    qpos_col = base + lax.broadcasted_iota(jnp.int32, (ns_rows, tq), 1)
    sel_ref[0, 0] = _select_blocks(imp.T, qpos_col, ns_rows).T.astype(BF16)


def _select_blocks(imp, qpos, ns_rows):
    nc, cols = imp.shape
    if ns_rows > nc:
        imp = jnp.concatenate([imp, jnp.zeros((ns_rows - nc, cols), F32)], axis=0)
    blk = lax.broadcasted_iota(jnp.int32, (ns_rows, cols), 0)
    cur = qpos >> 6
    forced = (blk == 0) | (blk == cur) | (blk == cur - 1)
    imp = jnp.where(forced, NSA_FORCED, imp)
    imp = jnp.where(blk <= cur, imp, NEG_INF)
    taken = jnp.float32(-3e38)
    blk_f = blk.astype(F32)

    def pick(_, imp):
        m = jnp.max(imp, axis=0, keepdims=True)
        first = jnp.min(jnp.where(imp == m, blk_f, float(ns_rows)), axis=0, keepdims=True)
        return jnp.where(blk_f == first, taken, imp)

    imp = lax.fori_loop(0, NSA_TOPN, pick, imp)
    return jnp.where(imp == taken, 0.0, NEG_INF)


def _nsa_cmp_sample_kernel(q0, t_seq, ns_rows, q_ref, kc_ref, vc_ref, oc_ref, sel_ref):
    ntok = q_ref.shape[0]
    nkey = kc_ref.shape[1]
    nc = nkey // (ntok // t_seq)
    rows = NSA_REP * ntok
    qst = _stack_heads(q_ref[...])
    row = lax.broadcasted_iota(jnp.int32, (rows, nkey), 0) & (ntok - 1)
    col = lax.broadcasted_iota(jnp.int32, (rows, nkey), 1)
    qpos = q0 + (row & (t_seq - 1))
    same_seq = (row >> (t_seq.bit_length() - 1)) == (col >> (nc.bit_length() - 1))
    ok = same_seq & (((col & (nc - 1)) + 1) * NSA_BLOCK - 1 <= qpos)
    s = jnp.where(ok, _dot_nt(qst, kc_ref[0]), NEG_INF)
    e = jnp.exp(s - jnp.max(s, axis=-1, keepdims=True))
    p = jnp.where(ok, e / jnp.sum(e, axis=-1, keepdims=True), 0.0)
    oc_ref[...] = _unstack_heads(_dot(p.astype(BF16), vc_ref[0]), ntok)
    own = p[:, 0:nc]
    for u in range(1, nkey // nc):
        own = own + p[:, u * nc:(u + 1) * nc]
    imp = own[0:ntok, :]
    for r in range(1, NSA_REP):
        imp = imp + own[r * ntok:(r + 1) * ntok, :]
    qpos_col = q0 + (lax.broadcasted_iota(jnp.int32, (ns_rows, ntok), 1) & (t_seq - 1))
    sel_ref[0] = _select_blocks(imp.T, qpos_col, ns_rows).T.astype(BF16)


def nsa_cmp_select_sample(qn, kc, vc, q0, t_seq, ns_rows):
    ntok = qn.shape[0]
    nseq, _, nc, _ = kc.shape
    flat = lambda a: _tile_lanes4(a.transpose(1, 0, 2, 3).reshape(1, NSA_GROUPS, nseq * nc, HEAD_DIM))[0]
    kv = pl.BlockSpec((1, nseq * nc, 256), lambda g: (g, 0, 0))
    return pl.pallas_call(
        functools.partial(_nsa_cmp_sample_kernel, q0, t_seq, ns_rows),
        grid=(NSA_GROUPS,),
        in_specs=[pl.BlockSpec((ntok, 256), lambda g: (0, g)), kv, kv],
        out_specs=[pl.BlockSpec((ntok, 256), lambda g: (0, g)),
                   pl.BlockSpec((1, ntok, ns_rows), lambda g: (g, 0, 0))],
        out_shape=[jax.ShapeDtypeStruct((ntok, 1024), F32),
                   jax.ShapeDtypeStruct((NSA_GROUPS, ntok, ns_rows), BF16)],
        compiler_params=_cparams("parallel"),
        name="nsa_cmp_select_sample",
    )(qn, flat(kc), flat(vc))


def _tile_lanes4(x):
    return jnp.tile(x, (1, 1, 1, NSA_REP)).astype(BF16)


def nsa_cmp_select(qn, kc, vc, q0, ns_rows):
    b, t, _ = qn.shape
    tq = min(512, t)
    nc = kc.shape[2]
    kv = pl.BlockSpec((1, 1, nc, 256), lambda bi, g, i: (bi, g, 0, 0))
    return pl.pallas_call(
        functools.partial(_nsa_cmp_kernel, q0, ns_rows),
        grid=(b, NSA_GROUPS, t // tq),
        in_specs=[pl.BlockSpec((1, tq, 256), lambda bi, g, i: (bi, i, g)), kv, kv],
        out_specs=[pl.BlockSpec((1, tq, 256), lambda bi, g, i: (bi, i, g)),
                   pl.BlockSpec((1, 1, tq, ns_rows), lambda bi, g, i: (bi, g, i, 0))],
        out_shape=[jax.ShapeDtypeStruct((b, t, 1024), F32),
                   jax.ShapeDtypeStruct((b, NSA_GROUPS, t, ns_rows), BF16)],
        compiler_params=_cparams("parallel", "parallel", "parallel"),
        name="nsa_cmp_select",
    )(qn, _tile_lanes4(kc), _tile_lanes4(vc))


def _head_rows(q):
    lane = lax.broadcasted_iota(jnp.int32, (q.shape[0], LANES), 1)
    out = []
    for r in range(NSA_REP):
        slab = q[:, (r // 2) * LANES:(r // 2 + 1) * LANES]
        slab = pltpu.roll(slab, HEAD_DIM, 1) if r % 2 else slab
        out.append(jnp.where(lane < HEAD_DIM, slab, 0.0))
    return out


def _store_heads(o_ref, row0, acc, tq):
    lane = lax.broadcasted_iota(jnp.int32, (tq, LANES), 1)
    o4 = acc / pltpu.roll(acc, HEAD_DIM, 1)
    for half in range(NSA_REP // 2):
        even = o4[(2 * half) * tq:(2 * half + 1) * tq, :]
        odd = pltpu.roll(o4[(2 * half + 1) * tq:(2 * half + 2) * tq, :], HEAD_DIM, 1)
        o_ref[0, row0:row0 + tq, half * LANES:(half + 1) * LANES] = jnp.where(lane < HEAD_DIM, even, odd)


WIN_SUB = 4


def _nsa_window_kernel(nsub, q_ref, kv_ref, o_ref):
    i = pl.program_id(2)
    tq = q_ref.shape[1] // nsub
    ntile = (NSA_WINDOW + 2 * tq) // LANES
    span = ntile * LANES
    rows = NSA_REP * tq
    for pair in range(nsub // 2):
        j0 = jnp.maximum((i * nsub + 2 * pair) * tq - NSA_WINDOW, 0) // LANES
        k_aug = jnp.concatenate([kv_ref[0, j0 + u, 0, 0:2 * HEAD_DIM, :] for u in range(ntile)], axis=1)
        v_aug = jnp.concatenate([kv_ref[0, j0 + u, 0, 2 * HEAD_DIM:4 * HEAD_DIM, :] for u in range(ntile)], axis=1)
        kpos = j0 * LANES + lax.broadcasted_iota(jnp.int32, (rows, span), 1)
        for a in range(2 * pair, 2 * pair + 2):
            lhs = jnp.concatenate(_head_rows(q_ref[0, a * tq:(a + 1) * tq, :].astype(F32)), axis=0).astype(BF16)
            qpos = (i * nsub + a) * tq + (lax.broadcasted_iota(jnp.int32, (rows, span), 0) & (tq - 1))
            ok = (kpos <= qpos) & (kpos > qpos - NSA_WINDOW)
            s = jnp.where(ok, _dot(lhs, k_aug), NEG_INF)
            e = jnp.exp2(s - jnp.max(s, axis=-1, keepdims=True))
            _store_heads(o_ref, a * tq, _dot_nt(e.astype(BF16), v_aug), tq)


def nsa_window_prompt(qr, kwin):
    b, t, _ = qr.shape
    tq = WIN_SUB * 128
    nt = kwin.shape[1]
    return pl.pallas_call(
        functools.partial(_nsa_window_kernel, WIN_SUB),
        grid=(b, NSA_GROUPS, t // tq),
        in_specs=[pl.BlockSpec((1, tq, 256), lambda bi, g, i: (bi, i, g)),
                  pl.BlockSpec((1, nt, 1, 4 * HEAD_DIM, LANES), lambda bi, g, i: (bi, 0, g, 0, 0))],
        out_specs=pl.BlockSpec((1, tq, 256), lambda bi, g, i: (bi, i, g)),
        out_shape=jax.ShapeDtypeStruct((b, t, 1024), F32),
        compiler_params=_cparams("parallel", "parallel", "arbitrary"),
        name="nsa_window",
    )(qr, kwin)


SEL_SUB = 8
SEL_KT = 2


def _nsa_selected_kernel(kt, q_ref, sel_ref, kv_ref, e_ref, o_ref, m_ref, acc_ref):
    i = pl.program_id(2)
    nsub = m_ref.shape[0]
    tq = q_ref.shape[1] // nsub
    tk = kt * kv_ref.shape[4]
    rows = NSA_REP * tq

    def lhs(a):
        sel = sel_ref[0, 0, a * tq:(a + 1) * tq, :].astype(F32)
        heads = _head_rows(q_ref[0, a * tq:(a + 1) * tq, :].astype(F32))
        return jnp.concatenate([jnp.concatenate([h, sel], axis=1) for h in heads], axis=0).astype(BF16)

    qs = [lhs(a) for a in range(nsub)]
    _softmax_init(m_ref, acc_ref)

    def tile(j, causal):
        span = lambda piece: jnp.concatenate([piece(j * kt + u) for u in range(kt)], axis=1)
        k_aug = jnp.concatenate([span(lambda jj: kv_ref[0, jj, 0, 0:2 * HEAD_DIM, :]), span(lambda jj: e_ref[jj])],
                                axis=0)
        v_aug = span(lambda jj: kv_ref[0, jj, 0, 2 * HEAD_DIM:4 * HEAD_DIM, :])
        for a in range(nsub):
            s = _dot(qs[a], k_aug)
            if causal:
                qpos = (i * nsub + a) * tq + (lax.broadcasted_iota(jnp.int32, (rows, tk), 0) & (tq - 1))
                kpos = j * tk + lax.broadcasted_iota(jnp.int32, (rows, tk), 1)
                s = jnp.where(kpos <= qpos, s, NEG_INF)
            _softmax_update_vsum(s, lambda p: _dot_nt(p, v_aug), m_ref.at[a], acc_ref.at[a])

    nfull = (i * nsub * tq) // tk

    def body(j, c):
        tile(j, False)
        return c

    lax.fori_loop(0, nfull, body, 0)
    tile(nfull, True)
    for a in range(nsub):
        _store_heads(o_ref, a * tq, acc_ref[a], tq)


def nsa_selected_prompt(qr, sel, ksel):
    b, t, _ = qr.shape
    sub = 128
    tq = SEL_SUB * sub
    nt, _, _, tk = ksel.shape[1:]
    assert (SEL_KT * tk) % tq == 0 and nt % SEL_KT == 0
    nblk = sel.shape[-1]
    e3 = (jnp.arange(nblk)[None, :, None] == (jnp.arange(nt)[:, None, None] * tk + jnp.arange(tk)[None, None, :]) // NSA_BLOCK
          ).astype(BF16)
    return pl.pallas_call(
        functools.partial(_nsa_selected_kernel, SEL_KT),
        grid=(b, NSA_GROUPS, t // tq),
        in_specs=[pl.BlockSpec((1, tq, 256), lambda bi, g, i: (bi, i, g)),
                  pl.BlockSpec((1, 1, tq, nblk), lambda bi, g, i: (bi, g, i, 0)),
                  pl.BlockSpec((1, nt, 1, 4 * HEAD_DIM, tk), lambda bi, g, i: (bi, 0, g, 0, 0)),
                  pl.BlockSpec(e3.shape, lambda bi, g, i: (0, 0, 0))],
        out_specs=pl.BlockSpec((1, tq, 256), lambda bi, g, i: (bi, i, g)),
        out_shape=jax.ShapeDtypeStruct((b, t, 1024), F32),
        scratch_shapes=[pltpu.VMEM((SEL_SUB, NSA_REP * sub, LANES), F32),
                        pltpu.VMEM((SEL_SUB, NSA_REP * sub, LANES), F32)],
        compiler_params=_cparams("parallel", "parallel", "arbitrary"),
        name="nsa_selected",
    )(qr, sel, ksel, e3)


def _nsa_out_kernel(oc_ref, os_ref, ow_ref, gate_ref, ex_ref, w_ref, r_ref, o_ref):
    gate = gate_ref[...]
    comb = (_split_dot(gate, ex_ref[0]) * oc_ref[...] + _split_dot(gate, ex_ref[1]) * os_ref[...]
            + _split_dot(gate, ex_ref[2]) * ow_ref[...])
    o_ref[...] = r_ref[...] + _dot(comb.astype(BF16), w_ref[...])


def nsa_out(oc, os_, ow, gates, w_o, res):
    n, d = res.shape
    tm = min(256, n)
    lane = jnp.arange(1024) // HEAD_DIM
    ex = jnp.stack([(jnp.arange(LANES)[:, None] == lane[None, :] * 3 + k) for k in range(3)]).astype(BF16)
    row = lambda w: pl.BlockSpec((tm, w), lambda i: (i, 0))
    return pl.pallas_call(
        _nsa_out_kernel,
        grid=(n // tm,),
        in_specs=[row(1024), row(1024), row(1024), row(LANES),
                  pl.BlockSpec(ex.shape, lambda i: (0, 0, 0)),
                  pl.BlockSpec(w_o.shape, lambda i: (0, 0)), row(d)],
        out_specs=row(d),
        out_shape=jax.ShapeDtypeStruct((n, d), F32),
        compiler_params=_cparams("parallel"),
        name="nsa_out",
    )(oc, os_, ow, gates, ex, w_o, res)


def _paged_attn_kernel(npp, kv_t, pt_ref, q_ref, *refs):
    k_refs, v_refs = refs[:npp], refs[npp:2 * npp]
    kn_ref, vn_ref, bp_ref, bn_ref, o_ref, m_ref, l_ref, acc_ref = refs[2 * npp:]
    p = pl.program_id(1)
    last = pl.num_programs(1) - 1

    @pl.when(p == 0)
    def _():
        _softmax_init(m_ref, acc_ref, l_ref)

    def step(k, v, bias):
        if kv_t:
            s = _dot(q_ref[0], k) + bias
            _softmax_update(s, lambda e: _dot_nt(e, v), m_ref, l_ref, acc_ref)
        else:
            s = _dot_nt(q_ref[0], k) + bias
            _softmax_update(s, lambda e: _dot(e, v), m_ref, l_ref, acc_ref)

    def load(r):
        x = r[...]
        return x if kv_t else x.reshape(-1, x.shape[-1])

    @pl.when(p < last)
    def _():
        axis = 1 if kv_t else 0
        step(jnp.concatenate([load(r) for r in k_refs], axis=axis).astype(BF16),
             jnp.concatenate([load(r) for r in v_refs], axis=axis).astype(BF16), bp_ref[0])

    @pl.when(p == last)
    def _():
        step(kn_ref[0], vn_ref[0], bn_ref[0])
        o_ref[0] = _softmax_result(l_ref, acc_ref)


def paged_attention(q, pages, table, npg, npp, k_spec, v_spec, kv_t, k_new, v_new, bias_past, bias_new, past_per_step):
    nseq, rows, _ = q.shape
    lv = acc_w = v_new.shape[1] if kv_t else v_new.shape[2]
    seq = lambda a: pl.BlockSpec((1,) + a.shape[1:], lambda b, p, pt: (b, 0, 0))
    wpast = bias_past.shape[-1] if not past_per_step else bias_past.shape[-1] // (npg // npp)
    bp_spec = pl.BlockSpec((1, rows, wpast), lambda b, p, pt: (
        b if bias_past.shape[0] > 1 else 0, 0, jnp.minimum(p, npg // npp - 1) if past_per_step else 0))
    bn_spec = pl.BlockSpec((1,) + bias_new.shape[1:], lambda b, p, pt: (b if bias_new.shape[0] > 1 else 0, 0, 0))
    return pl.pallas_call(
        functools.partial(_paged_attn_kernel, npp, kv_t),
        grid_spec=pltpu.PrefetchScalarGridSpec(
            num_scalar_prefetch=1,
            grid=(nseq, npg // npp + 1),
            in_specs=[seq(q)] + [k_spec(u) for u in range(npp)] + [v_spec(u) for u in range(npp)]
                     + [seq(k_new), seq(v_new), bp_spec, bn_spec],
            out_specs=pl.BlockSpec((1, rows, acc_w), lambda b, p, pt: (b, 0, 0)),
            scratch_shapes=[pltpu.VMEM((rows, LANES), F32), pltpu.VMEM((rows, LANES), F32),
                            pltpu.VMEM((rows, acc_w), F32)]),
        out_shape=jax.ShapeDtypeStruct((nseq, rows, lv), F32),
        compiler_params=_cparams("arbitrary", "arbitrary"),
        name="paged_attention",
    )(table, q, *([pages] * (2 * npp)), k_new, v_new, bias_past, bias_new)


def _diff_post_kernel(p_ref, cos_ref, sin_ref, g_ref, so_ref, q_ref, kvf_ref, kvb_ref):
    cos, sin = cos_ref[...], sin_ref[...]
    so = so_ref[...]
    for s in range(DIFF_WIDTH // LANES):
        sl = slice(s * LANES, (s + 1) * LANES)
        q = _rope_slab(_seg64_norm(p_ref[:, sl], g_ref[0:1, :], so), cos, sin)
        q_ref[:, sl] = (q * QK_SCALE_LOG2).astype(BF16)
        k = _rope_slab(_seg64_norm(p_ref[:, DIFF_WIDTH + s * LANES:DIFF_WIDTH + (s + 1) * LANES], g_ref[1:2, :], so),
                       cos, sin)
        kvf_ref[:, sl] = k
        kvb_ref[:, sl] = k.astype(BF16)
    v = p_ref[:, 2 * DIFF_WIDTH:3 * DIFF_WIDTH]
    kvf_ref[:, DIFF_WIDTH:] = v
    kvb_ref[:, DIFF_WIDTH:] = v.astype(BF16)


def diff_post(proj, cos, sin, qk_g):
    n = proj.shape[0]
    tm = min(256, n)
    g2 = jnp.tile(qk_g, (1, 2))
    so = _seg_ones()
    row = lambda w: pl.BlockSpec((tm, w), lambda i: (i, 0))
    const = lambda a: pl.BlockSpec(a.shape, lambda i: (0, 0))
    return pl.pallas_call(
        _diff_post_kernel,
        grid=(n // tm,),
        in_specs=[row(3 * DIFF_WIDTH), row(LANES), row(LANES), const(g2), const(so)],
        out_specs=[row(DIFF_WIDTH), row(2 * DIFF_WIDTH), row(2 * DIFF_WIDTH)],
        out_shape=[jax.ShapeDtypeStruct((n, DIFF_WIDTH), BF16), jax.ShapeDtypeStruct((n, 2 * DIFF_WIDTH), F32),
                   jax.ShapeDtypeStruct((n, 2 * DIFF_WIDTH), BF16)],
        compiler_params=_cparams("parallel"),
        name="diff_post",
    )(proj, cos, sin, g2, so)


def _diff_lambda(lam_ref, lam_init):
    lf = lam_ref[...]
    a = jnp.sum(lf[0:1, :] * lf[1:2, :], axis=-1, keepdims=True)
    b = jnp.sum(lf[2:3, :] * lf[3:4, :], axis=-1, keepdims=True)
    return jnp.exp(a) - jnp.exp(b) + lam_init


DIFF_SUB = 4
DIFF_SUB_TQ = 256
DIFF_TK = 1024


def _diff_flash_kernel(lam_init, tk, q_ref, k_ref, v_ref, lam_ref, o_ref, m_ref, acc_ref):
    i = pl.program_id(2)
    nsub = m_ref.shape[0]
    tq = q_ref.shape[1] // nsub
    rows = 2 * tq

    def stack_components(q):
        comp = lax.broadcasted_iota(jnp.int32, q.shape, 1) >> 6
        return jnp.concatenate([jnp.where(comp == c, q, 0.0) for c in range(2)], axis=0).astype(BF16)

    qst = [stack_components(q_ref[0, a * tq:(a + 1) * tq, :].astype(F32)) for a in range(nsub)]
    _softmax_init(m_ref, acc_ref)
    ones = jnp.ones((tk, LANES), BF16)

    def tile(j, causal):
        lo = pl.multiple_of(j * tk, tk)
        k = k_ref[0, pl.ds(lo, tk), :]
        v_aug = jnp.concatenate([v_ref[0, pl.ds(lo, tk), :], ones], axis=1)
        for a in range(nsub):
            s = _dot_nt(qst[a], k)
            if causal:
                qpos = (i * nsub + a) * tq + (lax.broadcasted_iota(jnp.int32, (rows, tk), 0) & (tq - 1))
                kpos = j * tk + lax.broadcasted_iota(jnp.int32, (rows, tk), 1)
                s = jnp.where(kpos <= qpos, s, NEG_INF)
            _softmax_update_vsum(s, lambda p: _dot(p, v_aug), m_ref.at[a], acc_ref.at[a])

    nfull = (i * nsub * tq) // tk

    def body(j, c):
        tile(j, False)
        return c

    lax.fori_loop(0, nfull, body, 0)
    tile(nfull, True)
    lam = _diff_lambda(lam_ref, lam_init)
    for a in range(nsub):
        acc = acc_ref[a]
        o = acc[:, 0:LANES] / acc[:, LANES:2 * LANES]
        o_ref[0, a * tq:(a + 1) * tq, :] = o[0:tq, :] - lam * o[tq:rows, :]


def diff_flash_prompt(q, kvb, lam, lam_init):
    b, t, _ = q.shape
    sub = DIFF_SUB_TQ
    tq = DIFF_SUB * sub
    tk = DIFF_TK
    assert tk % tq == 0 and t % tk == 0
    return pl.pallas_call(
        functools.partial(_diff_flash_kernel, lam_init, tk),
        grid=(b, DIFF_HEADS, t // tq),
        in_specs=[pl.BlockSpec((1, tq, LANES), lambda bi, h, i: (bi, i, h)),
                  pl.BlockSpec((1, t, LANES), lambda bi, h, i: (bi, 0, h)),
                  pl.BlockSpec((1, t, LANES), lambda bi, h, i: (bi, 0, DIFF_HEADS + h)),
                  pl.BlockSpec(lam.shape, lambda bi, h, i: (0, 0))],
        out_specs=pl.BlockSpec((1, tq, LANES), lambda bi, h, i: (bi, i, h)),
        out_shape=jax.ShapeDtypeStruct((b, t, DIFF_WIDTH), F32),
        scratch_shapes=[pltpu.VMEM((DIFF_SUB, 2 * sub, LANES), F32), pltpu.VMEM((DIFF_SUB, 2 * sub, 2 * LANES), F32)],
        compiler_params=_cparams("parallel", "parallel", "arbitrary"),
        name="diff_flash",
    )(q, kvb, kvb, lam)


def _diff_out_kernel(lam_init, two, *refs):
    if two:
        o0_ref, o1_ref, lam_ref, g_ref, w_ref, r_ref, out_ref, h_ref = refs
        o = o0_ref[...] - _diff_lambda(lam_ref, lam_init) * o1_ref[...]
    else:
        o0_ref, g_ref, w_ref, r_ref, out_ref, h_ref = refs
        o = o0_ref[...]
    for s in range(DIFF_HEADS):
        sl = slice(s * LANES, (s + 1) * LANES)
        h_ref[:, sl] = (_rms_rows(o[:, sl], g_ref[...]) * (1.0 - lam_init)).astype(BF16)
    out_ref[...] = r_ref[...] + _dot(h_ref[...], w_ref[...])


def diff_out(o, sub_g, w_o, res, lam_init, o1=None, lam=None):
    n, d = res.shape
    tm = min(512, n)
    two = o1 is not None
    row = lambda w: pl.BlockSpec((tm, w), lambda i: (i, 0))
    const = lambda a: pl.BlockSpec(a.shape, lambda i: (0, 0))
    g = sub_g.reshape(1, LANES)
    ins = [o, o1, lam, g, w_o, res] if two else [o, g, w_o, res]
    specs = ([row(DIFF_WIDTH), row(DIFF_WIDTH), const(lam)] if two else [row(DIFF_WIDTH)]) + [const(g), const(w_o), row(d)]
    return pl.pallas_call(
        functools.partial(_diff_out_kernel, lam_init, two),
        grid=(n // tm,),
        in_specs=specs,
        out_specs=row(d),
        out_shape=jax.ShapeDtypeStruct((n, d), F32),
        scratch_shapes=[pltpu.VMEM((tm, DIFF_WIDTH), BF16)],
        compiler_params=_cparams("parallel"),
        name="diff_out",
    )(*ins)


def _gla_kernel(chunk, nchunk, t_valid, p_ref, s0_ref, wa_ref, ba_ref, g_ref, tri_ref, y_ref, sfin_ref, st_ref):
    step = pl.program_id(1)

    @pl.when(step == 0)
    def _():
        st_ref[...] = s0_ref[0]

    nk = GLA_HEADS * GLA_DK
    nv = GLA_HEADS * GLA_DV
    tri = tri_ref[...]
    causal = lax.broadcasted_iota(jnp.int32, (chunk, chunk), 0) >= lax.broadcasted_iota(jnp.int32, (chunk, chunk), 1)
    for ci in range(nchunk):
        r0 = ci * chunk
        rows = slice(r0, r0 + chunk)
        a1 = p_ref[0, rows, GLA_MAIN:GLA_MAIN + LANES].astype(BF16)
        z = _dot(a1, wa_ref[...]) + ba_ref[...]
        log_a = (jnp.minimum(z, 0.0) - jnp.log(1.0 + jnp.exp(-jnp.abs(z)))) * (1.0 / GLA_TAU)
        if t_valid < chunk:
            log_a = jnp.where(lax.broadcasted_iota(jnp.int32, log_a.shape, 0) < t_valid, log_a, 0.0)
        cum = _split_dot_left(tri, log_a)
        for h in range(GLA_HEADS):
            ksl = slice(h * GLA_DK, (h + 1) * GLA_DK)
            vsl = slice(h * GLA_DV, (h + 1) * GLA_DV)
            q = p_ref[0, rows, h * GLA_DK:(h + 1) * GLA_DK] * (GLA_DK ** -0.5)
            k = p_ref[0, rows, nk + h * GLA_DK:nk + (h + 1) * GLA_DK]
            v = p_ref[0, rows, 2 * nk + h * GLA_DV:2 * nk + (h + 1) * GLA_DV]
            r = p_ref[0, rows, 2 * nk + nv + h * GLA_DV:2 * nk + nv + (h + 1) * GLA_DV]
            cm = cum[:, ksl]
            last = cm[chunk - 1:chunk, :]
            qe = (q * jnp.exp(cm)).astype(BF16)
            ke = (k * jnp.exp(-cm)).astype(BF16)
            vb = v.astype(BF16)
            att = jnp.where(causal, _dot_nt(qe, ke), 0.0)
            st = st_ref[h]
            o = _dot_nt(qe, st.astype(BF16)) + _dot(att.astype(BF16), vb)
            kd = (k * jnp.exp(last - cm)).astype(BF16)
            st_ref[h] = st * jnp.exp(last) + _dot(v.T.astype(BF16), kd)
            y_ref[0, rows, vsl] = _rms_rows(o, g_ref[...]) * _silu(r)

    @pl.when(step == pl.num_programs(1) - 1)
    def _():
        sfin_ref[0] = st_ref[...]


def gla_core(proj, s0_t, w_a2, b_a, out_g, chunk, nchunk, t_valid):
    b, t, _ = proj.shape
    rows = chunk * nchunk
    wa = jnp.zeros((LANES, GLA_HEADS * GLA_DK), F32).at[:GLA_RANK].set(w_a2).astype(BF16)
    ba = b_a.reshape(1, -1)
    g = out_g.reshape(1, GLA_DV)
    tri = (jnp.arange(chunk)[:, None] >= jnp.arange(chunk)[None, :]).astype(BF16)
    const = lambda a: pl.BlockSpec(a.shape, lambda bi, s: (0,) * a.ndim)
    st_spec = pl.BlockSpec((1, GLA_HEADS, GLA_DV, GLA_DK), lambda bi, s: (bi, 0, 0, 0))
    return pl.pallas_call(
        functools.partial(_gla_kernel, chunk, nchunk, t_valid),
        grid=(b, t // rows),
        in_specs=[pl.BlockSpec((1, rows, GLA_IN_PAD), lambda bi, s: (bi, s, 0)), st_spec,
                  const(wa), const(ba), const(g), const(tri)],
        out_specs=[pl.BlockSpec((1, rows, GLA_HEADS * GLA_DV), lambda bi, s: (bi, s, 0)), st_spec],
        out_shape=[jax.ShapeDtypeStruct((b, t, GLA_HEADS * GLA_DV), F32),
                   jax.ShapeDtypeStruct((b, GLA_HEADS, GLA_DV, GLA_DK), F32)],
        scratch_shapes=[pltpu.VMEM((GLA_HEADS, GLA_DV, GLA_DK), F32)],
        compiler_params=_cparams("parallel", "arbitrary"),
        name="gla_core",
    )(proj, s0_t, wa, ba, g, tri)


def _pad_cols(w, width):
    return jnp.pad(w, ((0, 0), (0, width - w.shape[1]))).astype(BF16)


def _pad_axis(x, axis, size):
    pad = [(0, 0)] * x.ndim
    pad[axis] = (0, size - x.shape[axis])
    return jnp.pad(x, pad)


def _group_diag(q, t_seq):
    nseq = q.shape[0] // t_seq
    qg = q.reshape(nseq, t_seq, NSA_GROUPS, NSA_REP, HEAD_DIM).transpose(0, 2, 1, 3, 4)
    eye = jnp.eye(NSA_GROUPS, dtype=q.dtype)
    out = qg[:, :, :, :, None, :] * eye[None, :, None, None, :, None]
    return out.reshape(nseq, NSA_GROUPS * t_seq * NSA_REP, NSA_GROUPS * HEAD_DIM)


def _group_undiag(o, t_seq):
    nseq = o.shape[0]
    o6 = o.reshape(nseq, NSA_GROUPS, t_seq, NSA_REP, NSA_GROUPS, HEAD_DIM)
    od = jnp.stack([o6[:, g, :, :, g, :] for g in range(NSA_GROUPS)], axis=1)
    return od.transpose(0, 2, 1, 3, 4).reshape(nseq * t_seq, NSA_HEADS * HEAD_DIM)


def _seq_cols(x_t, nseq, t_seq):
    r = x_t.shape[0]
    return _pad_axis(x_t.reshape(r, nseq, t_seq).transpose(1, 0, 2), 2, PAGE_SIZE)


def _nsa_layer(s, xp, xs, seq_p, t_s, past_len, cache_t, state_win_t, page_table, norm_g, w_in, qk_g, pe, w_phi, w_o):
    bp = xp.shape[0] // seq_p
    bs = xs.shape[0] // t_s
    w_o_b = w_o.astype(BF16)
    npg = past_len // PAGE_SIZE

    qn, qr, gates, rows_t, win_t, ksel, kwin = nsa_project(xp, norm_g, w_in, qk_g, jnp.arange(seq_p), bp)
    own = _PageIndex(jnp.zeros((1, 1), jnp.int32), lambda b, p, pt: (b, 0, p))
    kc, vc = nsa_compress(rows_t, own, bp, seq_p // PAGE_SIZE, pe, w_phi, qk_g[3])
    qn3, qr3 = qn.reshape(bp, seq_p, 1024), qr.reshape(bp, seq_p, 1024)
    oc, sel = nsa_cmp_select(qn3, kc, vc, 0, seq_p // NSA_BLOCK)
    os_ = nsa_selected_prompt(qr3, sel, ksel)
    ow = nsa_window_prompt(qr3, kwin)
    xp_new = nsa_out(oc.reshape(-1, 1024), os_.reshape(-1, 1024), ow.reshape(-1, 1024), gates, w_o_b, xp)
    kv_p = rows_t.reshape(bp, 4, NSA_GROUPS, HEAD_DIM, seq_p).transpose(0, 4, 1, 2, 3)
    wlen = min(NSA_WINDOW, seq_p)
    win_p = win_t[:, :, seq_p - wlen:].reshape(bp, 2, NSA_GROUPS, HEAD_DIM, wlen).transpose(0, 4, 1, 2, 3)

    pos_s = past_len + jnp.arange(t_s)
    qn, qr, gates, rows_t, win_t, _, _ = nsa_project(xs, norm_g, w_in, qk_g, jnp.tile(pos_s, bs), 1)
    rows_t, win_t = rows_t[0], win_t[0]
    layer_pages = cache_t.shape[0] // (state_win_t.shape[0] // bs)
    table = page_table + s * layer_pages
    paged = _PageIndex(table, lambda b, p, pt: (pt[b, p], 0, 0))
    kc, vc = nsa_compress(cache_t, paged, bs, npg, pe, w_phi, qk_g[3])
    ns = -(-(past_len + t_s) // NSA_BLOCK)
    oc, sel = nsa_cmp_select_sample(qn, kc, vc, past_len, t_s, 256)
    nrow = NSA_GROUPS * t_s * NSA_REP
    sel_f = sel[:, :, :ns].astype(F32).reshape(NSA_GROUPS, bs, t_s, ns).transpose(1, 0, 2, 3)
    by_row = lambda a: jnp.broadcast_to(a[:, :, :, None, :], a.shape[:3] + (NSA_REP, a.shape[-1])).reshape(
        a.shape[0], nrow, a.shape[-1])
    past_bias = by_row(jnp.repeat(sel_f[..., :past_len // NSA_BLOCK], NSA_BLOCK, axis=-1))
    new_ok = jnp.arange(PAGE_SIZE)[None, :] <= jnp.arange(t_s)[:, None]
    new_bias = by_row(jnp.where(new_ok[None, None], sel_f[..., past_len // NSA_BLOCK][..., None], NEG_INF))
    q_bd = _group_diag(qr, t_s)
    npp = min(NSA_PAGES_PER_STEP, npg)
    kpage = lambda blk: (lambda u: pl.BlockSpec((None, NSA_GW, PAGE_SIZE),
                                                lambda b, p, pt: (pt[b, jnp.minimum(p * npp + u, npg - 1)], blk, 0)))
    k_new = _seq_cols(rows_t[2 * NSA_GW:3 * NSA_GW], bs, t_s).astype(BF16)
    v_new = _seq_cols(rows_t[3 * NSA_GW:4 * NSA_GW], bs, t_s).astype(BF16)
    os_ = _group_undiag(paged_attention(q_bd, cache_t, table, npg, npp, kpage(2), kpage(3), True, k_new, v_new,
                                        past_bias, new_bias, True), t_s)
    wbuf = state_win_t.shape[-1]
    wpg = wbuf // PAGE_SIZE
    wpage = lambda blk: (lambda u: pl.BlockSpec((None, NSA_GW, PAGE_SIZE),
                                                lambda b, p, pt: (s * bs + b, blk, jnp.minimum(p * wpg + u, wpg - 1))))
    kidx = jnp.arange(wbuf + PAGE_SIZE)
    wpos = jnp.where(kidx < wbuf, past_len - wbuf + kidx, past_len + kidx - wbuf)
    w_ok = ((wpos[None, :] <= pos_s[:, None]) & (wpos[None, :] > pos_s[:, None] - NSA_WINDOW) & (wpos[None, :] >= 0)
            & (kidx[None, :] < wbuf + t_s))
    wbias = by_row(jnp.broadcast_to(jnp.where(w_ok, 0.0, NEG_INF).astype(F32)[None, None],
                                    (1, NSA_GROUPS, t_s, wbuf + PAGE_SIZE)))
    kw_new = _seq_cols(win_t[0:NSA_GW], bs, t_s).astype(BF16)
    vw_new = _seq_cols(win_t[NSA_GW:2 * NSA_GW], bs, t_s).astype(BF16)
    ow = _group_undiag(paged_attention(q_bd, state_win_t, jnp.zeros((1, 1), jnp.int32), wpg, wpg, wpage(0), wpage(1), True,
                                       kw_new, vw_new, wbias[:, :, :wbuf], wbias[:, :, wbuf:], True), t_s)
    xs_new = nsa_out(oc, os_, ow, gates, w_o_b, xs)
    kv_s = rows_t.reshape(4, NSA_GROUPS, HEAD_DIM, bs, t_s).transpose(3, 4, 0, 1, 2)
    win_new = win_t.reshape(2 * NSA_GW, bs, t_s).transpose(1, 0, 2)
    win_all = jnp.concatenate([state_win_t[s * bs:(s + 1) * bs], win_new], axis=2)
    wlen = min(NSA_WINDOW, win_all.shape[2])
    win_s = win_all[:, :, win_all.shape[2] - wlen:].reshape(bs, 2, NSA_GROUPS, HEAD_DIM, wlen).transpose(0, 4, 1, 2, 3)
    return xp_new, xs_new, kv_p, kv_s, win_p, win_s


def _diff_layer(layer, xp, xs, seq_p, t_s, past_len, cache, page_table, norm_g, w_in, qk_g, lam, sub_g, w_o):
    bp = xp.shape[0] // seq_p
    bs = xs.shape[0] // t_s
    npg = past_len // PAGE_SIZE
    lam_init = 0.8 - 0.6 * math.exp(-0.3 * layer)
    w_in_b = w_in.astype(BF16)
    w_o_b = w_o.astype(BF16)
    cos_p, sin_p = _rope_tables(jnp.arange(seq_p))
    proj = norm_matmul(xp, norm_g, w_in_b, 768)
    q, kvf, kvb = diff_post(proj, jnp.tile(cos_p, (bp, 1)), jnp.tile(sin_p, (bp, 1)), qk_g)
    o = diff_flash_prompt(q.reshape(bp, seq_p, DIFF_WIDTH), kvb.reshape(bp, seq_p, 2 * DIFF_WIDTH), lam, lam_init)
    xp_new = diff_out(o.reshape(-1, DIFF_WIDTH), sub_g, w_o_b, xp, lam_init)
    kv_p = kvf.reshape(bp, seq_p, 2, DIFF_HEADS, 2 * HEAD_DIM)
    pos_s = past_len + jnp.arange(t_s)
    cos_s, sin_s = _rope_tables(pos_s)
    proj = norm_matmul(xs, norm_g, w_in_b, 768)
    q, kvf, kvb = diff_post(proj, jnp.tile(cos_s, (bs, 1)), jnp.tile(sin_s, (bs, 1)), qk_g)
    nrow = DIFF_HEADS * 2 * t_s
    q5 = q.reshape(bs, t_s, DIFF_HEADS, 2, HEAD_DIM).transpose(0, 2, 3, 1, 4)
    q_rows = (q5[:, :, :, :, None, :] * jnp.eye(2, dtype=q.dtype)[None, None, :, None, :, None]).reshape(
        bs, nrow, 2 * HEAD_DIM)
    kv5 = kvb.reshape(bs, t_s, 2, DIFF_HEADS, 2 * HEAD_DIM)
    k_new = _pad_axis(kv5[:, :, 0], 1, PAGE_SIZE).reshape(bs, PAGE_SIZE * DIFF_HEADS, 2 * HEAD_DIM)
    v_new = _pad_axis(kv5[:, :, 1], 1, PAGE_SIZE).reshape(bs, PAGE_SIZE * DIFF_HEADS, 2 * HEAD_DIM)
    npp = min(DIFF_PAGES_PER_STEP, npg)
    row_h = jnp.arange(nrow) // (2 * t_s)
    row_t = jnp.arange(nrow) % t_s
    slot_h = jnp.arange(PAGE_SIZE * DIFF_HEADS) % DIFF_HEADS
    slot_tok = jnp.arange(PAGE_SIZE * DIFF_HEADS) // DIFF_HEADS
    same_head = row_h[:, None] == slot_h[None, :]
    bias_page = jnp.where(same_head, 0.0, NEG_INF).astype(F32)
    bias_past = jnp.tile(bias_page, (1, npp))[None]
    bias_new = jnp.where(same_head & (slot_tok[None, :] <= row_t[:, None]), 0.0, NEG_INF).astype(F32)[None]
    page = lambda slot: (lambda u: pl.BlockSpec(
        (None, PAGE_SIZE, None, DIFF_HEADS, 2 * HEAD_DIM),
        lambda b, p, pt: (pt[b, jnp.minimum(p * npp + u, npg - 1)], 0, slot, 0, 0)))
    o = paged_attention(q_rows, cache, page_table, npg, npp, page(0), page(1), False, k_new, v_new,
                        bias_past, bias_new, False)
    od = o.reshape(bs, DIFF_HEADS, 2, t_s, 2 * HEAD_DIM).transpose(2, 0, 3, 1, 4).reshape(2, bs * t_s, DIFF_WIDTH)
    xs_new = diff_out(od[0], sub_g, w_o_b, xs, lam_init, o1=od[1], lam=lam)
    kv_s = kvf.reshape(bs, t_s, 2, DIFF_HEADS, 2 * HEAD_DIM)
    return xp_new, xs_new, kv_p, kv_s


def _gla_layer(xp, xs, seq_p, t_s, state, norm_g, w_in, w_a2, b_a, out_g, w_o):
    bp = xp.shape[0] // seq_p
    bs = xs.shape[0] // t_s
    w_in_b = _pad_cols(w_in, GLA_IN_PAD)
    w_o_b = w_o.astype(BF16)
    proj = norm_matmul(xp, norm_g, w_in_b, 640).reshape(bp, seq_p, GLA_IN_PAD)
    chunk = min(GLA_CHUNK, seq_p)
    s0 = jnp.zeros((bp, GLA_HEADS, GLA_DV, GLA_DK), F32)
    y, st = gla_core(proj, s0, w_a2, b_a, out_g, chunk, 4 if seq_p % (4 * chunk) == 0 else 1, chunk)
    xp_new = matmul_residual(y.reshape(-1, GLA_HEADS * GLA_DV), w_o_b, xp)
    st_p = st.transpose(0, 1, 3, 2)
    chunk_s = GLA_CHUNK
    proj = norm_matmul(xs, norm_g, w_in_b, 640).reshape(bs, t_s, GLA_IN_PAD)
    proj = _pad_axis(proj, 1, chunk_s)
    y, st = gla_core(proj, state.transpose(0, 1, 3, 2), w_a2, b_a, out_g, chunk_s, 1, t_s)
    xs_new = matmul_residual(y[:, :t_s].reshape(-1, GLA_HEADS * GLA_DV), w_o_b, xs)
    st_s = st.transpose(0, 1, 3, 2)
    return xp_new, xs_new, st_p, st_s


def kernel(x_prompt, x_sample, cache_nsa_kv, state_nsa_win, cache_diff_kv, state_gla, state_ffn, page_table, norm_g, ffn_w_up, ffn_conv_w, ffn_conv_b, ffn_w_down, nsa_w_in, nsa_qk_g, nsa_pe, nsa_w_phi, nsa_w_o, diff_w_in, diff_qk_g, diff_lam, diff_sub_g, diff_w_o, gla_w_in, gla_w_a2, gla_b_a, gla_out_g, gla_w_o):
    bp, seq_p, d = x_prompt.shape
    bs, t_s, _ = x_sample.shape
    past_len = page_table.shape[1] * PAGE_SIZE
    xp = x_prompt.reshape(bp * seq_p, d)
    xs = x_sample.reshape(bs * t_s, d)
    cache_t = cache_nsa_kv.transpose(0, 1, 3, 4, 5, 2).reshape(-1, 4 * NSA_GW, PAGE_SIZE)
    win_t = state_nsa_win.transpose(0, 1, 3, 4, 5, 2).reshape(-1, 2 * NSA_GW, state_nsa_win.shape[2])
    nsa_kv_p, nsa_kv_s, nsa_win_p, nsa_win_s = [], [], [], []
    diff_kv_p, diff_kv_s, gla_p, gla_s, ffn_p, ffn_s = [], [], [], [], [], []
    for i in range(DEPTH):
        kind, s = i % N_MIXERS, i // N_MIXERS
        if kind == 0:
            xp, xs, kvp, kvs, wp, ws = _nsa_layer(
                s, xp, xs, seq_p, t_s, past_len, cache_t, win_t, page_table,
                norm_g[i, 0], nsa_w_in[s], nsa_qk_g[s], nsa_pe[s], nsa_w_phi[s], nsa_w_o[s])
            nsa_kv_p.append(kvp); nsa_kv_s.append(kvs); nsa_win_p.append(wp); nsa_win_s.append(ws)
        elif kind == 1:
            xp, xs, kvp, kvs = _diff_layer(
                i, xp, xs, seq_p, t_s, past_len, cache_diff_kv[s], page_table, norm_g[i, 0], diff_w_in[s],
                diff_qk_g[s], diff_lam[s], diff_sub_g[s], diff_w_o[s])
            diff_kv_p.append(kvp); diff_kv_s.append(kvs)
        else:
            xp, xs, stp, sts = _gla_layer(xp, xs, seq_p, t_s, state_gla[s], norm_g[i, 0], gla_w_in[s], gla_w_a2[s],
                                          gla_b_a[s], gla_out_g[s], gla_w_o[s])
            gla_p.append(stp); gla_s.append(sts)
        w_up_b = ffn_w_up[i].astype(BF16)
        w_dn_b = ffn_w_down[i].astype(BF16)
        xp, tail_p = ffn_prompt(xp, norm_g[i, 1], w_up_b, ffn_conv_w[i], ffn_conv_b[i], w_dn_b, seq_p)
        xs, tail_s = ffn_sample(xs, norm_g[i, 1], w_up_b, ffn_conv_w[i], ffn_conv_b[i], w_dn_b, state_ffn[i], t_s)
        ffn_p.append(tail_p); ffn_s.append(tail_s)
    return (xp.reshape(bp, seq_p, d), xs.reshape(bs, t_s, d),
            jnp.stack(nsa_kv_p), jnp.stack(nsa_kv_s), jnp.stack(nsa_win_p), jnp.stack(nsa_win_s),
            jnp.stack(diff_kv_p), jnp.stack(diff_kv_s), jnp.stack(gla_p), jnp.stack(gla_s),
            jnp.stack(ffn_p), jnp.stack(ffn_s))
```

```python
import functools
import math

import jax
import jax.numpy as jnp
from jax import lax
from jax.experimental import pallas as pl
from jax.experimental.pallas import tpu as pltpu

F32 = jnp.float32
BF16 = jnp.bfloat16

D_MODEL = 1024
HEAD_DIM = 64
ROPE_THETA = 10000.0
NORM_EPS = 1e-6
NEG_INF = -1e30
DEPTH = 4
N_MIXERS = 3
PAGE_SIZE = 128
NSA_HEADS = 16
NSA_GROUPS = 4
NSA_REP = 4
NSA_BLOCK = 64
NSA_TOPN = 16
NSA_WINDOW = 512
NSA_FORCED = 1e9
NSA_NQ = NSA_HEADS * HEAD_DIM
NSA_NKV = 6 * NSA_GROUPS * HEAD_DIM
NSA_NGATE = 3 * NSA_HEADS
NSA_GW = NSA_GROUPS * HEAD_DIM
DIFF_HEADS = 8
DIFF_WIDTH = 1024
GLA_HEADS = 4
GLA_DK = 128
GLA_DV = 256
GLA_RANK = 16
GLA_TAU = 16.0
GLA_CHUNK = 64
GLA_MAIN = 2 * GLA_HEADS * GLA_DK + 2 * GLA_HEADS * GLA_DV
GLA_IN_PAD = GLA_MAIN + 128
D_FF = 2816
CONV_W = 3
LANES = 128
VMEM_LIMIT = 56 * 1024 * 1024
SEL_TK = 512
NSA_PAGES_PER_STEP = 16
DIFF_PAGES_PER_STEP = 8
CMP_PITCH = HEAD_DIM + 8


def _cparams(*sem):
    return pltpu.CompilerParams(dimension_semantics=sem, vmem_limit_bytes=VMEM_LIMIT)


def _dot(a, b):
    return jnp.dot(a, b, preferred_element_type=F32)


def _dot_nt(a, b):
    return lax.dot_general(a, b, (((1,), (1,)), ((), ())), preferred_element_type=F32)


def _split_dot(x, m):
    hi = x.astype(BF16)
    lo = (x - hi.astype(F32)).astype(BF16)
    return _dot(hi, m) + _dot(lo, m)


def _split_dot_left(m, x):
    hi = x.astype(BF16)
    lo = (x - hi.astype(F32)).astype(BF16)
    return _dot(m, hi) + _dot(m, lo)


def _rms_rows(x, g):
    ms = jnp.mean(x * x, axis=-1, keepdims=True)
    return x * lax.rsqrt(ms + NORM_EPS) * g


def _seg64_norm(x, g, seg_ones):
    ms = _split_dot(x * x, seg_ones) * (1.0 / HEAD_DIM)
    return x * lax.rsqrt(ms + NORM_EPS) * g


def _rope_slab(x, cos, sin_signed):
    lane = lax.broadcasted_iota(jnp.int32, x.shape, 1)
    first = (lane & 63) < 32
    partner = jnp.where(first, pltpu.roll(x, 96, 1), pltpu.roll(x, 32, 1))
    return x * cos + partner * sin_signed


def _rope_angles(pos):
    half = HEAD_DIM // 2
    inv = ROPE_THETA ** (-jnp.arange(half, dtype=F32) / half)
    ang = pos.astype(F32)[:, None] * inv
    return jnp.cos(ang), jnp.sin(ang)


def _rope_tables(pos):
    cos, sin = _rope_angles(pos)
    return jnp.tile(cos, (1, 4)), jnp.tile(jnp.concatenate([-sin, sin], axis=1), (1, 2))


def _seg_ones():
    i = jnp.arange(LANES)
    return (i[:, None] // HEAD_DIM == i[None, :] // HEAD_DIM).astype(BF16)


def _lanes(x, width):
    return x if width == LANES else jnp.tile(x, (1, width // LANES))


def _softmax_update(s, v_dot, m_ref, l_ref, acc_ref):
    m = m_ref[...]
    m_new = jnp.maximum(m, jnp.max(s, axis=-1, keepdims=True))
    alpha = jnp.exp2(m - m_new)
    p = jnp.exp2(s - _lanes(m_new, s.shape[1]))
    m_ref[...] = m_new
    l_ref[...] = alpha * l_ref[...] + jnp.sum(p, axis=-1, keepdims=True)
    acc_ref[...] = _lanes(alpha, acc_ref.shape[-1]) * acc_ref[...] + v_dot(p.astype(BF16))


def _softmax_result(l_ref, acc_ref):
    return acc_ref[...] / _lanes(l_ref[...], acc_ref.shape[-1])


def _softmax_update_vsum(s, v_dot, m_ref, acc_ref):
    m = m_ref[...]
    m_new = jnp.maximum(m, jnp.max(s, axis=-1, keepdims=True))
    alpha = jnp.exp2(m - m_new)
    p = jnp.exp2(s - _lanes(m_new, s.shape[1]))
    m_ref[...] = m_new
    acc_ref[...] = _lanes(alpha, acc_ref.shape[-1]) * acc_ref[...] + v_dot(p.astype(BF16))


def _softmax_init(m_ref, acc_ref, l_ref=None):
    m_ref[...] = jnp.full_like(m_ref, NEG_INF)
    acc_ref[...] = jnp.zeros_like(acc_ref)
    if l_ref is not None:
        l_ref[...] = jnp.zeros_like(l_ref)


QK_SCALE = HEAD_DIM ** -0.5
QK_SCALE_LOG2 = QK_SCALE * math.log2(math.e)


def _norm_matmul_kernel(x_ref, g_ref, w_ref, o_ref, h_ref):
    @pl.when(pl.program_id(1) == 0)
    def _():
        h_ref[...] = _rms_rows(x_ref[...], g_ref[...]).astype(BF16)

    o_ref[...] = _dot(h_ref[...], w_ref[...])


def norm_matmul(x, g, w, tn):
    n, d = x.shape
    nout = w.shape[1]
    tm = min(512, n)
    return pl.pallas_call(
        _norm_matmul_kernel,
        grid=(n // tm, nout // tn),
        in_specs=[pl.BlockSpec((tm, d), lambda i, j: (i, 0)),
                  pl.BlockSpec((1, d), lambda i, j: (0, 0)),
                  pl.BlockSpec((d, tn), lambda i, j: (0, j))],
        out_specs=pl.BlockSpec((tm, tn), lambda i, j: (i, j)),
        out_shape=jax.ShapeDtypeStruct((n, nout), F32),
        scratch_shapes=[pltpu.VMEM((tm, d), BF16)],
        compiler_params=_cparams("parallel", "arbitrary"),
        name="norm_matmul",
    )(x, g.reshape(1, d), w)


def _matmul_res_kernel(a_ref, w_ref, r_ref, o_ref):
    o_ref[...] = r_ref[...] + _dot(a_ref[...].astype(BF16), w_ref[...])


def matmul_residual(a, w, res):
    n, k = a.shape
    d = w.shape[1]
    tm = min(512, n)
    return pl.pallas_call(
        _matmul_res_kernel,
        grid=(n // tm,),
        in_specs=[pl.BlockSpec((tm, k), lambda i: (i, 0)),
                  pl.BlockSpec((k, d), lambda i: (0, 0)),
                  pl.BlockSpec((tm, d), lambda i: (i, 0))],
        out_specs=pl.BlockSpec((tm, d), lambda i: (i, 0)),
        out_shape=jax.ShapeDtypeStruct((n, d), F32),
        compiler_params=_cparams("parallel"),
        name="matmul_residual",
    )(a, w, res)


FFN_CHUNK = 256


def _silu(x):
    return x / (1.0 + jnp.exp(-x))


def _causal_conv(u, prev1, prev2, use1, use2, cw_ref, cb_ref):
    u1 = jnp.where(use1, prev1, pltpu.roll(u, 1, 0))
    u2 = jnp.where(use2, prev2, pltpu.roll(u, 2, 0))
    return cb_ref[...] + cw_ref[0:1, :] * u2 + cw_ref[1:2, :] * u1 + cw_ref[2:3, :] * u


def _ffn_prompt_kernel(tiles_per_seq, x_ref, g_ref, wg_ref, wu_ref, cwg_ref, cwu_ref, cbg_ref, cbu_ref,
                       wd_ref, o_ref, tg_ref, tu_ref, cg_ref, cu_ref):
    i = pl.program_id(0)
    tm = x_ref.shape[0]
    nj, _, c = wg_ref.shape
    x = x_ref[...]
    h = _rms_rows(x, g_ref[...]).astype(BF16)
    row = lax.broadcasted_iota(jnp.int32, (tm, c), 0)

    @pl.when((i % tiles_per_seq) == 0)
    def _():
        cg_ref[...] = jnp.zeros_like(cg_ref)
        cu_ref[...] = jnp.zeros_like(cu_ref)

    def conv(u, prev, cw_ref, cb_ref):
        prev2 = jnp.where(row == 0, prev[0:1, :], prev[1:2, :])
        return _causal_conv(u, prev[1:2, :], prev2, row == 0, row < 2, cw_ref, cb_ref)

    acc = x
    for j in range(nj):
        ug = _dot(h, wg_ref[j])
        uu = _dot(h, wu_ref[j])
        act = _silu(conv(ug, cg_ref[j], cwg_ref.at[j], cbg_ref.at[j])) * conv(uu, cu_ref[j], cwu_ref.at[j], cbu_ref.at[j])
        acc = acc + _dot(act.astype(BF16), wd_ref[j])
        cg_ref[j] = ug[tm - 2:tm, :]
        cu_ref[j] = uu[tm - 2:tm, :]
        tg_ref[0, :, j * c:(j + 1) * c] = ug[tm - 2:tm, :]
        tu_ref[0, :, j * c:(j + 1) * c] = uu[tm - 2:tm, :]
    o_ref[...] = acc


def _resident(a):
    return pl.BlockSpec(a.shape, lambda i: (0,) * a.ndim, pipeline_mode=pl.Buffered(1))


def ffn_prompt(x, g, w_up, conv_w, conv_b, w_down, seq_len):
    n, d = x.shape
    c = FFN_CHUNK
    nj = D_FF // c
    tm = min(512, seq_len)
    tps = seq_len // tm
    chunks = lambda a: a.reshape(a.shape[0], 2, nj, c).transpose(1, 2, 0, 3)
    wg, wu = chunks(w_up)
    cwg, cwu = chunks(conv_w)
    cbg, cbu = chunks(conv_b.reshape(1, 2 * D_FF))
    wd = w_down.reshape(nj, c, d)
    g2 = g.reshape(1, d)
    consts = [g2, wg, wu, cwg, cwu, cbg, cbu, wd]
    out, tg, tu = pl.pallas_call(
        functools.partial(_ffn_prompt_kernel, tps),
        grid=(n // tm,),
        in_specs=[pl.BlockSpec((tm, d), lambda i: (i, 0))] + [_resident(a) for a in consts],
        out_specs=[pl.BlockSpec((tm, d), lambda i: (i, 0)),
                   pl.BlockSpec((1, 2, D_FF), lambda i: (i, 0, 0)),
                   pl.BlockSpec((1, 2, D_FF), lambda i: (i, 0, 0))],
        out_shape=[jax.ShapeDtypeStruct((n, d), F32),
                   jax.ShapeDtypeStruct((n // tm, 2, D_FF), F32),
                   jax.ShapeDtypeStruct((n // tm, 2, D_FF), F32)],
        scratch_shapes=[pltpu.VMEM((nj, 2, c), F32),
                        pltpu.VMEM((nj, 2, c), F32)],
        compiler_params=_cparams("arbitrary"),
        name="ffn_prompt",
    )(x, *consts)
    return out, jnp.concatenate([tg, tu], axis=-1)[tps - 1::tps]


def _ffn_sample_kernel(t_seq, x_ref, g_ref, wg_ref, wu_ref, cwg_ref, cwu_ref, cbg_ref, cbu_ref, wd_ref,
                       p1g_ref, p2g_ref, p1u_ref, p2u_ref, o_ref, ug_ref, uu_ref, h_ref):
    j = pl.program_id(0)

    @pl.when(j == 0)
    def _():
        h_ref[...] = _rms_rows(x_ref[...], g_ref[...]).astype(BF16)

    h = h_ref[...]
    ug = _dot(h, wg_ref[...])
    uu = _dot(h, wu_ref[...])
    t = lax.broadcasted_iota(jnp.int32, ug.shape, 0) & (t_seq - 1)
    cg = _causal_conv(ug, p1g_ref[...], p2g_ref[...], t == 0, t < 2, cwg_ref, cbg_ref)
    cu = _causal_conv(uu, p1u_ref[...], p2u_ref[...], t == 0, t < 2, cwu_ref, cbu_ref)
    part = _dot((_silu(cg) * cu).astype(BF16), wd_ref[...])
    ug_ref[...] = ug
    uu_ref[...] = uu

    @pl.when(j == 0)
    def _():
        o_ref[...] = x_ref[...] + part

    @pl.when(j > 0)
    def _():
        o_ref[...] += part


def ffn_sample(x, g, w_up, conv_w, conv_b, w_down, buf, t_seq):
    n, d = x.shape
    nseq = n // t_seq
    c = FFN_CHUNK
    nj = D_FF // c
    cb = conv_b.reshape(1, 2 * D_FF)
    reps = t_seq // 2
    prev1 = jnp.concatenate([buf[:, ::-1]] * reps, axis=1).reshape(n, 2 * D_FF)
    prev2 = jnp.concatenate([buf] * reps, axis=1).reshape(n, 2 * D_FF)
    full = lambda blk, off=0: pl.BlockSpec(blk, lambda j: (0, j + off))
    out, ug, uu = pl.pallas_call(
        functools.partial(_ffn_sample_kernel, t_seq),
        grid=(nj,),
        in_specs=[pl.BlockSpec((n, d), lambda j: (0, 0)),
                  pl.BlockSpec((1, d), lambda j: (0, 0)),
                  full((d, c)), full((d, c), nj),
                  full((CONV_W, c)), full((CONV_W, c), nj),
                  full((1, c)), full((1, c), nj),
                  pl.BlockSpec((c, d), lambda j: (j, 0)),
                  full((n, c)), full((n, c)), full((n, c), nj), full((n, c), nj)],
        out_specs=[pl.BlockSpec((n, d), lambda j: (0, 0)),
                   full((n, c)), full((n, c))],
        out_shape=[jax.ShapeDtypeStruct((n, d), F32),
                   jax.ShapeDtypeStruct((n, D_FF), F32),
                   jax.ShapeDtypeStruct((n, D_FF), F32)],
        scratch_shapes=[pltpu.VMEM((n, d), BF16)],
        compiler_params=_cparams("arbitrary"),
        name="ffn_sample",
    )(x, g.reshape(1, d), w_up, w_up, conv_w, conv_w, cb, cb, w_down, prev1, prev2, prev1, prev2)
    u = jnp.concatenate([ug, uu], axis=-1).reshape(nseq, t_seq, 2 * D_FF)
    return out, u[:, t_seq - (CONV_W - 1):]


def _head_norm_t(x, g):
    ms = jnp.mean(x * x, axis=0, keepdims=True)
    return x * lax.rsqrt(ms + NORM_EPS) * g


def _rope_t(x, cos, sin):
    half = HEAD_DIM // 2
    x1, x2 = x[0:half, :], x[half:HEAD_DIM, :]
    return jnp.concatenate([x1 * cos - x2 * sin, x1 * sin + x2 * cos], axis=0)


def _nsa_project_kernel(x_ref, g_ref, wq_ref, wkv_ref, wg_ref, cos_ref, sin_ref, cost_ref, sint_ref, gq_ref, gk_ref,
                        so_ref, qn_ref, qr_ref, gate_ref, rows_ref, win_ref, ksel_ref, kwin_ref):
    hn = _rms_rows(x_ref[...], g_ref[...]).astype(BF16)
    cos, sin = cos_ref[...], sin_ref[...]
    so = so_ref[...]
    q_all = _dot(hn, wq_ref[...])
    for s in range(NSA_NQ // LANES):
        sl = slice(s * LANES, (s + 1) * LANES)
        q = _seg64_norm(q_all[:, sl], gq_ref[...], so)
        qn_ref[:, sl] = (q * QK_SCALE).astype(BF16)
        qr_ref[:, sl] = (_rope_slab(q, cos, sin) * QK_SCALE_LOG2).astype(BF16)
    gate_ref[...] = 1.0 / (1.0 + jnp.exp(-_dot(hn, wg_ref[...])))
    kvt = _dot_nt(wkv_ref[...], hn)
    gw = NSA_GW
    cost, sint = cost_ref[...], sint_ref[...]
    rows_ref[0, 0:2 * gw, :] = kvt[0:2 * gw, :]
    rows_ref[0, 3 * gw:4 * gw, :] = kvt[3 * gw:4 * gw, :]
    win_ref[0, gw:2 * gw, :] = kvt[5 * gw:6 * gw, :]
    for g in range(NSA_GROUPS):
        hs = slice(g * HEAD_DIM, (g + 1) * HEAD_DIM)
        rows_ref[0, 2 * gw + g * HEAD_DIM:2 * gw + (g + 1) * HEAD_DIM, :] = _rope_t(
            _head_norm_t(kvt[2 * gw + g * HEAD_DIM:2 * gw + (g + 1) * HEAD_DIM, :], gk_ref[0]), cost, sint)
        win_ref[0, hs, :] = _rope_t(
            _head_norm_t(kvt[4 * gw + g * HEAD_DIM:4 * gw + (g + 1) * HEAD_DIM, :], gk_ref[1]), cost, sint)
    tm = kvt.shape[1]
    for g in range(NSA_GROUPS):
        hs = slice(g * HEAD_DIM, (g + 1) * HEAD_DIM)
        ksel_ref[0, 0, g, 0:HEAD_DIM, :] = rows_ref[0, 2 * gw + g * HEAD_DIM:2 * gw + (g + 1) * HEAD_DIM, :].astype(BF16)
        ksel_ref[0, 0, g, HEAD_DIM:2 * HEAD_DIM, :] = jnp.zeros((HEAD_DIM, tm), BF16)
        ksel_ref[0, 0, g, 2 * HEAD_DIM:3 * HEAD_DIM, :] = kvt[3 * gw + g * HEAD_DIM:3 * gw + (g + 1) * HEAD_DIM, :].astype(BF16)
        ksel_ref[0, 0, g, 3 * HEAD_DIM:4 * HEAD_DIM, :] = jnp.ones((HEAD_DIM, tm), BF16)
    wb = win_ref[0].astype(BF16)
    for u in range(kwin_ref.shape[1]):
        ts = slice(u * LANES, (u + 1) * LANES)
        for g in range(NSA_GROUPS):
            kwin_ref[0, u, g, 0:HEAD_DIM, :] = wb[g * HEAD_DIM:(g + 1) * HEAD_DIM, ts]
            kwin_ref[0, u, g, HEAD_DIM:2 * HEAD_DIM, :] = jnp.zeros((HEAD_DIM, LANES), BF16)
            kwin_ref[0, u, g, 2 * HEAD_DIM:3 * HEAD_DIM, :] = wb[gw + g * HEAD_DIM:gw + (g + 1) * HEAD_DIM, ts]
            kwin_ref[0, u, g, 3 * HEAD_DIM:4 * HEAD_DIM, :] = jnp.ones((HEAD_DIM, LANES), BF16)


def nsa_project(x, norm_g, w_in, qk_g, pos, nseq):
    n, d = x.shape
    t = n // nseq
    tm = min(SEL_TK, t)
    nt = t // tm
    wq = w_in[:, :NSA_NQ].astype(BF16)
    wkv_t = w_in[:, NSA_NQ:NSA_NQ + NSA_NKV].T.astype(BF16)
    wg = jnp.pad(w_in[:, NSA_NQ + NSA_NKV:], ((0, 0), (0, LANES - NSA_NGATE))).astype(BF16)
    cos, sin = _rope_angles(pos)
    cos_q, sin_q = jnp.tile(cos, (1, 4)), jnp.tile(jnp.concatenate([-sin, sin], axis=1), (1, 2))
    gq = jnp.tile(qk_g[0:1], (1, 2))
    gk = qk_g[1:3].reshape(2, HEAD_DIM, 1)
    so = _seg_ones()
    const = lambda a: pl.BlockSpec(a.shape, lambda b, i: (0,) * a.ndim)
    row = lambda w: pl.BlockSpec((tm, w), lambda b, i: (b * nt + i, 0))
    return pl.pallas_call(
        _nsa_project_kernel,
        grid=(nseq, nt),
        in_specs=[row(d), const(norm_g.reshape(1, d)), const(wq), const(wkv_t), const(wg),
                  pl.BlockSpec((tm, LANES), lambda b, i: (i, 0)), pl.BlockSpec((tm, LANES), lambda b, i: (i, 0)),
                  pl.BlockSpec((HEAD_DIM // 2, tm), lambda b, i: (0, i)),
                  pl.BlockSpec((HEAD_DIM // 2, tm), lambda b, i: (0, i)),
                  const(gq), const(gk), const(so)],
        out_specs=[row(NSA_NQ), row(NSA_NQ), row(LANES),
                   pl.BlockSpec((1, 4 * NSA_GW, tm), lambda b, i: (b, 0, i)),
                   pl.BlockSpec((1, 2 * NSA_GW, tm), lambda b, i: (b, 0, i)),
                   pl.BlockSpec((1, 1, NSA_GROUPS, 4 * HEAD_DIM, tm), lambda b, i: (b, i, 0, 0, 0)),
                   pl.BlockSpec((1, tm // LANES, NSA_GROUPS, 4 * HEAD_DIM, LANES), lambda b, i: (b, i, 0, 0, 0))],
        out_shape=[jax.ShapeDtypeStruct((n, NSA_NQ), BF16), jax.ShapeDtypeStruct((n, NSA_NQ), BF16),
                   jax.ShapeDtypeStruct((n, LANES), F32),
                   jax.ShapeDtypeStruct((nseq, 4 * NSA_GW, t), F32),
                   jax.ShapeDtypeStruct((nseq, 2 * NSA_GW, t), F32),
                   jax.ShapeDtypeStruct((nseq, nt, NSA_GROUPS, 4 * HEAD_DIM, tm), BF16),
                   jax.ShapeDtypeStruct((nseq, t // LANES, NSA_GROUPS, 4 * HEAD_DIM, LANES), BF16)],
        compiler_params=_cparams("parallel", "parallel"),
        name="nsa_project",
    )(x, norm_g.reshape(1, d), wq, wkv_t, wg, cos_q, sin_q, cos.T, sin.T, gq, gk, so)


def _nsa_compress_kernel(npp, pt_ref, *refs):
    page_refs = refs[:npp]
    pe_ref, w_ref, g_ref, so_ref, kc_ref, vc_ref, seqk_ref, seqv_ref, acc_ref = refs[npp:]
    p = pl.program_id(1)
    gw = NSA_GW
    pitch = CMP_PITCH
    for u in range(npp):
        for g in range(NSA_GROUPS):
            lo = pl.multiple_of(((p * npp + u) * NSA_GROUPS + g) * pitch, 8)
            seqk_ref[pl.ds(lo, HEAD_DIM), :] = page_refs[u][g * HEAD_DIM:(g + 1) * HEAD_DIM, :]
            seqv_ref[pl.ds(lo, HEAD_DIM), :] = page_refs[u][gw + g * HEAD_DIM:gw + (g + 1) * HEAD_DIM, :]

    @pl.when(p == pl.num_programs(1) - 1)
    def _():
        nrow = seqk_ref.shape[0] // pitch
        acc_ref[...] = jnp.zeros_like(acc_ref)
        for dd in range(HEAD_DIM):
            a = jnp.concatenate([seqk_ref[pl.ds(dd, nrow, stride=pitch), :],
                                 seqv_ref[pl.ds(dd, nrow, stride=pitch), :]], axis=1) + pe_ref[dd:dd + 1, :]
            acc_ref[...] += _dot(a.astype(BF16), w_ref[dd])
        kc_ref[0] = _seg64_norm(acc_ref[:, 0:LANES], g_ref[...], so_ref[...])
        vc_ref[0] = acc_ref[:, LANES:2 * LANES]


def nsa_compress(pages_t, page_index, nseq, npg, pe, w_phi, g_c):
    npp = min(NSA_PAGES_PER_STEP, npg)
    nrow = npg * NSA_GROUPS
    pe_t = jnp.concatenate([jnp.tile(pe[0].T, (1, 2)), jnp.tile(pe[1].T, (1, 2))], axis=1)
    eye4 = jnp.eye(4, dtype=F32)
    w4 = jnp.stack([w_phi[0], w_phi[0], w_phi[1], w_phi[1]])
    w = jnp.einsum('ab,alde->dalbe', eye4, w4).reshape(HEAD_DIM, 2 * LANES, 2 * LANES).astype(BF16)
    g2 = jnp.tile(g_c.reshape(1, HEAD_DIM), (1, 2))
    so = _seg_ones()
    const = lambda a: pl.BlockSpec(a.shape, lambda b, p, pt: (0,) * a.ndim, pipeline_mode=pl.Buffered(1))
    page = lambda u: pl.BlockSpec((None, 2 * NSA_GW, PAGE_SIZE), lambda b, p, pt: page_index(b, p * npp + u, pt))
    out = pl.BlockSpec((1, nrow, LANES), lambda b, p, pt: (b, 0, 0))
    table = page_index.table
    kc, vc = pl.pallas_call(
        functools.partial(_nsa_compress_kernel, npp),
        grid_spec=pltpu.PrefetchScalarGridSpec(
            num_scalar_prefetch=1,
            grid=(nseq, npg // npp),
            in_specs=[page(u) for u in range(npp)] + [const(pe_t), const(w), const(g2), const(so)],
            out_specs=[out, out],
            scratch_shapes=[pltpu.VMEM((nrow * CMP_PITCH, PAGE_SIZE), F32), pltpu.VMEM((nrow * CMP_PITCH, PAGE_SIZE), F32),
                            pltpu.VMEM((nrow, 2 * LANES), F32)]),
        out_shape=[jax.ShapeDtypeStruct((nseq, nrow, LANES), F32)] * 2,
        compiler_params=_cparams("arbitrary", "arbitrary"),
        name="nsa_compress",
    )(table, *([pages_t] * npp), pe_t, w, g2, so)
    fix = lambda a: a.reshape(nseq, npg, NSA_GROUPS, 2, HEAD_DIM).transpose(0, 2, 1, 3, 4).reshape(
        nseq, NSA_GROUPS, 2 * npg, HEAD_DIM)
    return fix(kc), fix(vc)


class _PageIndex:
    def __init__(self, table, fn):
        self.table = table
        self._fn = fn

    def __call__(self, b, p, pt):
        return self._fn(b, p, pt)


def _stack_heads(q):
    head = lax.broadcasted_iota(jnp.int32, q.shape, 1) >> 6
    qf = q.astype(F32)
    return jnp.concatenate([jnp.where(head == r, qf, 0.0) for r in range(NSA_REP)], axis=0).astype(BF16)


def _unstack_heads(o4, tq):
    head = lax.broadcasted_iota(jnp.int32, (tq, o4.shape[1]), 1) >> 6
    out = jnp.zeros((tq, o4.shape[1]), F32)
    for r in range(NSA_REP):
        out = jnp.where(head == r, o4[r * tq:(r + 1) * tq, :], out)
    return out


def _tile_rows4(x):
    return jnp.concatenate([x] * NSA_REP, axis=0)


def _nsa_cmp_kernel(q0, ns_rows, q_ref, kc_ref, vc_ref, oc_ref, sel_ref):
    i = pl.program_id(2)
    tq = q_ref.shape[1]
    nc = kc_ref.shape[2]
    qst = _stack_heads(q_ref[0])
    kc, vc = kc_ref[0, 0], vc_ref[0, 0]
    base = q0 + i * tq
    qpos = base + (lax.broadcasted_iota(jnp.int32, (NSA_REP * tq, nc), 0) & (tq - 1))
    blk_end = (lax.broadcasted_iota(jnp.int32, (NSA_REP * tq, nc), 1) + 1) * NSA_BLOCK - 1
    ok = blk_end <= qpos
    s = jnp.where(ok, _dot_nt(qst, kc), NEG_INF)
    e = jnp.exp(s - jnp.max(s, axis=-1, keepdims=True))
    p = jnp.where(ok, e / jnp.sum(e, axis=-1, keepdims=True), 0.0)
    oc_ref[0] = _unstack_heads(_dot(p.astype(BF16), vc), tq)
    imp = p[0:tq, :]
    for r in range(1, NSA_REP):
        imp = imp + p[r * tq:(r + 1) * tq, :]
    qpos_col = base + lax.broadcasted_iota(jnp.int32, (ns_rows, tq), 1)
    sel_ref[0, 0] = _select_blocks(imp.T, qpos_col, ns_rows).T.astype(BF16)


def _select_blocks(imp, qpos, ns_rows):
    nc, cols = imp.shape
    if ns_rows > nc:
        imp = jnp.concatenate([imp, jnp.zeros((ns_rows - nc, cols), F32)], axis=0)
    blk = lax.broadcasted_iota(jnp.int32, (ns_rows, cols), 0)
    cur = qpos >> 6
    forced = (blk == 0) | (blk == cur) | (blk == cur - 1)
    imp = jnp.where(forced, NSA_FORCED, imp)
    imp = jnp.where(blk <= cur, imp, NEG_INF)
    taken = jnp.float32(-3e38)
    blk_f = blk.astype(F32)

    def pick(_, imp):
        m = jnp.max(imp, axis=0, keepdims=True)
        first = jnp.min(jnp.where(imp == m, blk_f, float(ns_rows)), axis=0, keepdims=True)
        return jnp.where(blk_f == first, taken, imp)

    imp = lax.fori_loop(0, NSA_TOPN, pick, imp)
    return jnp.where(imp == taken, 0.0, NEG_INF)


def _nsa_cmp_sample_kernel(q0, t_seq, ns_rows, q_ref, kc_ref, vc_ref, oc_ref, sel_ref):
    ntok = q_ref.shape[0]
    nkey = kc_ref.shape[1]
    nc = nkey // (ntok // t_seq)
    rows = NSA_REP * ntok
    qst = _stack_heads(q_ref[...])
    row = lax.broadcasted_iota(jnp.int32, (rows, nkey), 0) & (ntok - 1)
    col = lax.broadcasted_iota(jnp.int32, (rows, nkey), 1)
    qpos = q0 + (row & (t_seq - 1))
    same_seq = (row >> (t_seq.bit_length() - 1)) == (col >> (nc.bit_length() - 1))
    ok = same_seq & (((col & (nc - 1)) + 1) * NSA_BLOCK - 1 <= qpos)
    s = jnp.where(ok, _dot_nt(qst, kc_ref[0]), NEG_INF)
    e = jnp.exp(s - jnp.max(s, axis=-1, keepdims=True))
    p = jnp.where(ok, e / jnp.sum(e, axis=-1, keepdims=True), 0.0)
    oc_ref[...] = _unstack_heads(_dot(p.astype(BF16), vc_ref[0]), ntok)
    own = p[:, 0:nc]
    for u in range(1, nkey // nc):
        own = own + p[:, u * nc:(u + 1) * nc]
    imp = own[0:ntok, :]
    for r in range(1, NSA_REP):
        imp = imp + own[r * ntok:(r + 1) * ntok, :]
    qpos_col = q0 + (lax.broadcasted_iota(jnp.int32, (ns_rows, ntok), 1) & (t_seq - 1))
    sel_ref[0] = _select_blocks(imp.T, qpos_col, ns_rows).T.astype(BF16)


def nsa_cmp_select_sample(qn, kc, vc, q0, t_seq, ns_rows):
    ntok = qn.shape[0]
    nseq, _, nc, _ = kc.shape
    flat = lambda a: _tile_lanes4(a.transpose(1, 0, 2, 3).reshape(1, NSA_GROUPS, nseq * nc, HEAD_DIM))[0]
    kv = pl.BlockSpec((1, nseq * nc, 256), lambda g: (g, 0, 0))
    return pl.pallas_call(
        functools.partial(_nsa_cmp_sample_kernel, q0, t_seq, ns_rows),
        grid=(NSA_GROUPS,),
        in_specs=[pl.BlockSpec((ntok, 256), lambda g: (0, g)), kv, kv],
        out_specs=[pl.BlockSpec((ntok, 256), lambda g: (0, g)),
                   pl.BlockSpec((1, ntok, ns_rows), lambda g: (g, 0, 0))],
        out_shape=[jax.ShapeDtypeStruct((ntok, 1024), F32),
                   jax.ShapeDtypeStruct((NSA_GROUPS, ntok, ns_rows), BF16)],
        compiler_params=_cparams("parallel"),
        name="nsa_cmp_select_sample",
    )(qn, flat(kc), flat(vc))


def _tile_lanes4(x):
    return jnp.tile(x, (1, 1, 1, NSA_REP)).astype(BF16)


def nsa_cmp_select(qn, kc, vc, q0, ns_rows):
    b, t, _ = qn.shape
    tq = min(512, t)
    nc = kc.shape[2]
    kv = pl.BlockSpec((1, 1, nc, 256), lambda bi, g, i: (bi, g, 0, 0))
    return pl.pallas_call(
        functools.partial(_nsa_cmp_kernel, q0, ns_rows),
        grid=(b, NSA_GROUPS, t // tq),
        in_specs=[pl.BlockSpec((1, tq, 256), lambda bi, g, i: (bi, i, g)), kv, kv],
        out_specs=[pl.BlockSpec((1, tq, 256), lambda bi, g, i: (bi, i, g)),
                   pl.BlockSpec((1, 1, tq, ns_rows), lambda bi, g, i: (bi, g, i, 0))],
        out_shape=[jax.ShapeDtypeStruct((b, t, 1024), F32),
                   jax.ShapeDtypeStruct((b, NSA_GROUPS, t, ns_rows), BF16)],
        compiler_params=_cparams("parallel", "parallel", "parallel"),
        name="nsa_cmp_select",
    )(qn, _tile_lanes4(kc), _tile_lanes4(vc))


def _head_rows(q):
    lane = lax.broadcasted_iota(jnp.int32, (q.shape[0], LANES), 1)
    out = []
    for r in range(NSA_REP):
        slab = q[:, (r // 2) * LANES:(r // 2 + 1) * LANES]
        slab = pltpu.roll(slab, HEAD_DIM, 1) if r % 2 else slab
        out.append(jnp.where(lane < HEAD_DIM, slab, 0.0))
    return out


def _store_heads(o_ref, row0, acc, tq):
    lane = lax.broadcasted_iota(jnp.int32, (tq, LANES), 1)
    o4 = acc / pltpu.roll(acc, HEAD_DIM, 1)
    for half in range(NSA_REP // 2):
        even = o4[(2 * half) * tq:(2 * half + 1) * tq, :]
        odd = pltpu.roll(o4[(2 * half + 1) * tq:(2 * half + 2) * tq, :], HEAD_DIM, 1)
        o_ref[0, row0:row0 + tq, half * LANES:(half + 1) * LANES] = jnp.where(lane < HEAD_DIM, even, odd)


WIN_SUB = 4


def _nsa_window_kernel(nsub, q_ref, kv_ref, o_ref):
    i = pl.program_id(2)
    tq = q_ref.shape[1] // nsub
    ntile = (NSA_WINDOW + 2 * tq) // LANES
    span = ntile * LANES
    rows = NSA_REP * tq
    for pair in range(nsub // 2):
        j0 = jnp.maximum((i * nsub + 2 * pair) * tq - NSA_WINDOW, 0) // LANES
        k_aug = jnp.concatenate([kv_ref[0, j0 + u, 0, 0:2 * HEAD_DIM, :] for u in range(ntile)], axis=1)
        v_aug = jnp.concatenate([kv_ref[0, j0 + u, 0, 2 * HEAD_DIM:4 * HEAD_DIM, :] for u in range(ntile)], axis=1)
        kpos = j0 * LANES + lax.broadcasted_iota(jnp.int32, (rows, span), 1)
        for a in range(2 * pair, 2 * pair + 2):
            lhs = jnp.concatenate(_head_rows(q_ref[0, a * tq:(a + 1) * tq, :].astype(F32)), axis=0).astype(BF16)
            qpos = (i * nsub + a) * tq + (lax.broadcasted_iota(jnp.int32, (rows, span), 0) & (tq - 1))
            ok = (kpos <= qpos) & (kpos > qpos - NSA_WINDOW)
            s = jnp.where(ok, _dot(lhs, k_aug), NEG_INF)
            e = jnp.exp2(s - jnp.max(s, axis=-1, keepdims=True))
            _store_heads(o_ref, a * tq, _dot_nt(e.astype(BF16), v_aug), tq)


def nsa_window_prompt(qr, kwin):
    b, t, _ = qr.shape
    tq = WIN_SUB * 128
    nt = kwin.shape[1]
    return pl.pallas_call(
        functools.partial(_nsa_window_kernel, WIN_SUB),
        grid=(b, NSA_GROUPS, t // tq),
        in_specs=[pl.BlockSpec((1, tq, 256), lambda bi, g, i: (bi, i, g)),
                  pl.BlockSpec((1, nt, 1, 4 * HEAD_DIM, LANES), lambda bi, g, i: (bi, 0, g, 0, 0))],
        out_specs=pl.BlockSpec((1, tq, 256), lambda bi, g, i: (bi, i, g)),
        out_shape=jax.ShapeDtypeStruct((b, t, 1024), F32),
        compiler_params=_cparams("parallel", "parallel", "arbitrary"),
        name="nsa_window",
    )(qr, kwin)


SEL_SUB = 8
SEL_KT = 2


def _nsa_selected_kernel(kt, q_ref, sel_ref, kv_ref, e_ref, o_ref, m_ref, acc_ref):
    i = pl.program_id(2)
    nsub = m_ref.shape[0]
    tq = q_ref.shape[1] // nsub
    tk = kt * kv_ref.shape[4]
    rows = NSA_REP * tq

    def lhs(a):
        sel = sel_ref[0, 0, a * tq:(a + 1) * tq, :].astype(F32)
        heads = _head_rows(q_ref[0, a * tq:(a + 1) * tq, :].astype(F32))
        return jnp.concatenate([jnp.concatenate([h, sel], axis=1) for h in heads], axis=0).astype(BF16)

    qs = [lhs(a) for a in range(nsub)]
    _softmax_init(m_ref, acc_ref)

    def tile(j, causal):
        span = lambda piece: jnp.concatenate([piece(j * kt + u) for u in range(kt)], axis=1)
        k_aug = jnp.concatenate([span(lambda jj: kv_ref[0, jj, 0, 0:2 * HEAD_DIM, :]), span(lambda jj: e_ref[jj])],
                                axis=0)
        v_aug = span(lambda jj: kv_ref[0, jj, 0, 2 * HEAD_DIM:4 * HEAD_DIM, :])
        for a in range(nsub):
            s = _dot(qs[a], k_aug)
            if causal:
                qpos = (i * nsub + a) * tq + (lax.broadcasted_iota(jnp.int32, (rows, tk), 0) & (tq - 1))
                kpos = j * tk + lax.broadcasted_iota(jnp.int32, (rows, tk), 1)
                s = jnp.where(kpos <= qpos, s, NEG_INF)
            _softmax_update_vsum(s, lambda p: _dot_nt(p, v_aug), m_ref.at[a], acc_ref.at[a])

    nfull = (i * nsub * tq) // tk

    def body(j, c):
        tile(j, False)
        return c

    lax.fori_loop(0, nfull, body, 0)
    tile(nfull, True)
    for a in range(nsub):
        _store_heads(o_ref, a * tq, acc_ref[a], tq)


def nsa_selected_prompt(qr, sel, ksel):
    b, t, _ = qr.shape
    sub = 128
    tq = SEL_SUB * sub
    nt, _, _, tk = ksel.shape[1:]
    assert (SEL_KT * tk) % tq == 0 and nt % SEL_KT == 0
    nblk = sel.shape[-1]
    e3 = (jnp.arange(nblk)[None, :, None] == (jnp.arange(nt)[:, None, None] * tk + jnp.arange(tk)[None, None, :]) // NSA_BLOCK
          ).astype(BF16)
    return pl.pallas_call(
        functools.partial(_nsa_selected_kernel, SEL_KT),
        grid=(b, NSA_GROUPS, t // tq),
        in_specs=[pl.BlockSpec((1, tq, 256), lambda bi, g, i: (bi, i, g)),
                  pl.BlockSpec((1, 1, tq, nblk), lambda bi, g, i: (bi, g, i, 0)),
                  pl.BlockSpec((1, nt, 1, 4 * HEAD_DIM, tk), lambda bi, g, i: (bi, 0, g, 0, 0)),
                  pl.BlockSpec(e3.shape, lambda bi, g, i: (0, 0, 0))],
        out_specs=pl.BlockSpec((1, tq, 256), lambda bi, g, i: (bi, i, g)),
        out_shape=jax.ShapeDtypeStruct((b, t, 1024), F32),
        scratch_shapes=[pltpu.VMEM((SEL_SUB, NSA_REP * sub, LANES), F32),
                        pltpu.VMEM((SEL_SUB, NSA_REP * sub, LANES), F32)],
        compiler_params=_cparams("parallel", "parallel", "arbitrary"),
        name="nsa_selected",
    )(qr, sel, ksel, e3)


def _nsa_out_kernel(oc_ref, os_ref, ow_ref, gate_ref, ex_ref, w_ref, r_ref, o_ref):
    gate = gate_ref[...]
    comb = (_split_dot(gate, ex_ref[0]) * oc_ref[...] + _split_dot(gate, ex_ref[1]) * os_ref[...]
            + _split_dot(gate, ex_ref[2]) * ow_ref[...])
    o_ref[...] = r_ref[...] + _dot(comb.astype(BF16), w_ref[...])


def nsa_out(oc, os_, ow, gates, w_o, res):
    n, d = res.shape
    tm = min(256, n)
    lane = jnp.arange(1024) // HEAD_DIM
    ex = jnp.stack([(jnp.arange(LANES)[:, None] == lane[None, :] * 3 + k) for k in range(3)]).astype(BF16)
    row = lambda w: pl.BlockSpec((tm, w), lambda i: (i, 0))
    return pl.pallas_call(
        _nsa_out_kernel,
        grid=(n // tm,),
        in_specs=[row(1024), row(1024), row(1024), row(LANES),
                  pl.BlockSpec(ex.shape, lambda i: (0, 0, 0)),
                  pl.BlockSpec(w_o.shape, lambda i: (0, 0)), row(d)],
        out_specs=row(d),
        out_shape=jax.ShapeDtypeStruct((n, d), F32),
        compiler_params=_cparams("parallel"),
        name="nsa_out",
    )(oc, os_, ow, gates, ex, w_o, res)


def _paged_attn_kernel(npp, kv_t, extra_rows, pt_ref, q_ref, *refs):
    k_refs, v_refs = refs[:npp], refs[npp:2 * npp]
    kn_ref, vn_ref, bp_ref, bn_ref, o_ref, m_ref, l_ref, acc_ref = refs[2 * npp:]
    p = pl.program_id(1)
    last = pl.num_programs(1) - 1

    @pl.when(p == 0)
    def _():
        _softmax_init(m_ref, acc_ref, l_ref)

    def step(k, v, bias):
        if kv_t:
            s = _dot(q_ref[0], k)
            s = s if bias is None else s + bias
            _softmax_update(s, lambda e: _dot_nt(e, v), m_ref, l_ref, acc_ref)
        else:
            s = _dot_nt(q_ref[0], k) + bias
            _softmax_update(s, lambda e: _dot(e, v), m_ref, l_ref, acc_ref)

    def load(r):
        x = r[...]
        return x if kv_t else x.reshape(-1, x.shape[-1])

    @pl.when(p < last)
    def _():
        axis = 1 if kv_t else 0
        k = jnp.concatenate([load(r) for r in k_refs], axis=axis).astype(BF16)
        v = jnp.concatenate([load(r) for r in v_refs], axis=axis).astype(BF16)
        if extra_rows:
            step(jnp.concatenate([k, bp_ref[0]], axis=0), v, None)
        else:
            step(k, v, bp_ref[0])

    @pl.when(p == last)
    def _():
        step(kn_ref[0], vn_ref[0], bn_ref[0])
        o_ref[0] = _softmax_result(l_ref, acc_ref)


def paged_attention(q, pages, table, npg, npp, k_spec, v_spec, kv_t, k_new, v_new, bias_past, bias_new, past_per_step,
                    extra_rows=False):
    nseq, rows, _ = q.shape
    lv = acc_w = v_new.shape[1] if kv_t else v_new.shape[2]
    seq = lambda a: pl.BlockSpec((1,) + a.shape[1:], lambda b, p, pt: (b, 0, 0))
    nstep = npg // npp
    if extra_rows:
        bp_spec = pl.BlockSpec((1,) + bias_past.shape[1:], lambda b, p, pt: (jnp.minimum(p, nstep - 1), 0, 0))
    else:
        wpast = bias_past.shape[-1] if not past_per_step else bias_past.shape[-1] // nstep
        bp_spec = pl.BlockSpec((1, rows, wpast), lambda b, p, pt: (
            b if bias_past.shape[0] > 1 else 0, 0, jnp.minimum(p, nstep - 1) if past_per_step else 0))
    bn_spec = pl.BlockSpec((1,) + bias_new.shape[1:], lambda b, p, pt: (b if bias_new.shape[0] > 1 else 0, 0, 0))
    return pl.pallas_call(
        functools.partial(_paged_attn_kernel, npp, kv_t, extra_rows),
        grid_spec=pltpu.PrefetchScalarGridSpec(
            num_scalar_prefetch=1,
            grid=(nseq, npg // npp + 1),
            in_specs=[seq(q)] + [k_spec(u) for u in range(npp)] + [v_spec(u) for u in range(npp)]
                     + [seq(k_new), seq(v_new), bp_spec, bn_spec],
            out_specs=pl.BlockSpec((1, rows, acc_w), lambda b, p, pt: (b, 0, 0)),
            scratch_shapes=[pltpu.VMEM((rows, LANES), F32), pltpu.VMEM((rows, LANES), F32),
                            pltpu.VMEM((rows, acc_w), F32)]),
        out_shape=jax.ShapeDtypeStruct((nseq, rows, lv), F32),
        compiler_params=_cparams("arbitrary", "arbitrary"),
        name="paged_attention",
    )(table, q, *([pages] * (2 * npp)), k_new, v_new, bias_past, bias_new)


def _diff_project_kernel(x_ref, ng_ref, w_ref, cos_ref, sin_ref, g_ref, so_ref, q_ref, kvf_ref, kvb_ref):
    hn = _rms_rows(x_ref[...], ng_ref[...]).astype(BF16)
    cos, sin = cos_ref[...], sin_ref[...]
    so = so_ref[...]
    qp = _dot(hn, w_ref[:, 0:DIFF_WIDTH])
    kp = _dot(hn, w_ref[:, DIFF_WIDTH:2 * DIFF_WIDTH])
    v = _dot(hn, w_ref[:, 2 * DIFF_WIDTH:3 * DIFF_WIDTH])
    for s in range(DIFF_WIDTH // LANES):
        sl = slice(s * LANES, (s + 1) * LANES)
        q = _rope_slab(_seg64_norm(qp[:, sl], g_ref[0:1, :], so), cos, sin)
        q_ref[:, sl] = (q * QK_SCALE_LOG2).astype(BF16)
        k = _rope_slab(_seg64_norm(kp[:, sl], g_ref[1:2, :], so), cos, sin)
        kvf_ref[:, sl] = k
        kvb_ref[:, sl] = k.astype(BF16)
    kvf_ref[:, DIFF_WIDTH:] = v
    kvb_ref[:, DIFF_WIDTH:] = v.astype(BF16)


def diff_project(x, norm_g, w_in, cos, sin, qk_g):
    n, d = x.shape
    tm = min(256, n)
    g2 = jnp.tile(qk_g, (1, 2))
    so = _seg_ones()
    ng = norm_g.reshape(1, d)
    row = lambda w: pl.BlockSpec((tm, w), lambda i: (i, 0))
    return pl.pallas_call(
        _diff_project_kernel,
        grid=(n // tm,),
        in_specs=[row(d), _resident(ng), _resident(w_in), row(LANES), row(LANES), _resident(g2), _resident(so)],
        out_specs=[row(DIFF_WIDTH), row(2 * DIFF_WIDTH), row(2 * DIFF_WIDTH)],
        out_shape=[jax.ShapeDtypeStruct((n, DIFF_WIDTH), BF16), jax.ShapeDtypeStruct((n, 2 * DIFF_WIDTH), F32),
                   jax.ShapeDtypeStruct((n, 2 * DIFF_WIDTH), BF16)],
        compiler_params=_cparams("parallel"),
        name="diff_project",
    )(x, ng, w_in, cos, sin, g2, so)


def _diff_lambda(lam_ref, lam_init):
    lf = lam_ref[...]
    a = jnp.sum(lf[0:1, :] * lf[1:2, :], axis=-1, keepdims=True)
    b = jnp.sum(lf[2:3, :] * lf[3:4, :], axis=-1, keepdims=True)
    return jnp.exp(a) - jnp.exp(b) + lam_init


DIFF_SUB = 8
DIFF_SUB_TQ = 128
DIFF_TK = 1024


def _diff_flash_kernel(lam_init, tk, q_ref, k_ref, v_ref, lam_ref, o_ref, m_ref, acc_ref):
    i = pl.program_id(2)
    nsub = m_ref.shape[0]
    tq = q_ref.shape[1] // nsub
    rows = 2 * tq

    def stack_components(q):
        comp = lax.broadcasted_iota(jnp.int32, q.shape, 1) >> 6
        return jnp.concatenate([jnp.where(comp == c, q, 0.0) for c in range(2)], axis=0).astype(BF16)

    qst = [stack_components(q_ref[0, a * tq:(a + 1) * tq, :].astype(F32)) for a in range(nsub)]
    _softmax_init(m_ref, acc_ref)
    ones = jnp.ones((tk, LANES), BF16)

    def tile(j, causal):
        lo = pl.multiple_of(j * tk, tk)
        k = k_ref[0, pl.ds(lo, tk), :]
        v_aug = jnp.concatenate([v_ref[0, pl.ds(lo, tk), :], ones], axis=1)
        for a in range(nsub):
            s = _dot_nt(qst[a], k)
            if causal:
                qpos = (i * nsub + a) * tq + (lax.broadcasted_iota(jnp.int32, (rows, tk), 0) & (tq - 1))
                kpos = j * tk + lax.broadcasted_iota(jnp.int32, (rows, tk), 1)
                s = jnp.where(kpos <= qpos, s, NEG_INF)
            _softmax_update_vsum(s, lambda p: _dot(p, v_aug), m_ref.at[a], acc_ref.at[a])

    nfull = (i * nsub * tq) // tk

    def body(j, c):
        tile(j, False)
        return c

    lax.fori_loop(0, nfull, body, 0)
    tile(nfull, True)
    lam = _diff_lambda(lam_ref, lam_init)
    for a in range(nsub):
        acc = acc_ref[a]
        o = acc[:, 0:LANES] / acc[:, LANES:2 * LANES]
        o_ref[0, a * tq:(a + 1) * tq, :] = o[0:tq, :] - lam * o[tq:rows, :]


def diff_flash_prompt(q, kvb, lam, lam_init):
    b, t, _ = q.shape
    sub = DIFF_SUB_TQ
    tq = DIFF_SUB * sub
    tk = DIFF_TK
    assert tk % tq == 0 and t % tk == 0
    return pl.pallas_call(
        functools.partial(_diff_flash_kernel, lam_init, tk),
        grid=(b, DIFF_HEADS, t // tq),
        in_specs=[pl.BlockSpec((1, tq, LANES), lambda bi, h, i: (bi, i, h)),
                  pl.BlockSpec((1, t, LANES), lambda bi, h, i: (bi, 0, h)),
                  pl.BlockSpec((1, t, LANES), lambda bi, h, i: (bi, 0, DIFF_HEADS + h)),
                  pl.BlockSpec(lam.shape, lambda bi, h, i: (0, 0))],
        out_specs=pl.BlockSpec((1, tq, LANES), lambda bi, h, i: (bi, i, h)),
        out_shape=jax.ShapeDtypeStruct((b, t, DIFF_WIDTH), F32),
        scratch_shapes=[pltpu.VMEM((DIFF_SUB, 2 * sub, LANES), F32), pltpu.VMEM((DIFF_SUB, 2 * sub, 2 * LANES), F32)],
        compiler_params=_cparams("parallel", "parallel", "arbitrary"),
        name="diff_flash",
    )(q, kvb, kvb, lam)


def _diff_out_kernel(lam_init, two, *refs):
    if two:
        o0_ref, o1_ref, lam_ref, g_ref, w_ref, r_ref, out_ref, h_ref = refs
        o = o0_ref[...] - _diff_lambda(lam_ref, lam_init) * o1_ref[...]
    else:
        o0_ref, g_ref, w_ref, r_ref, out_ref, h_ref = refs
        o = o0_ref[...]
    for s in range(DIFF_HEADS):
        sl = slice(s * LANES, (s + 1) * LANES)
        h_ref[:, sl] = (_rms_rows(o[:, sl], g_ref[...]) * (1.0 - lam_init)).astype(BF16)
    out_ref[...] = r_ref[...] + _dot(h_ref[...], w_ref[...])


def diff_out(o, sub_g, w_o, res, lam_init, o1=None, lam=None):
    n, d = res.shape
    tm = min(512, n)
    two = o1 is not None
    row = lambda w: pl.BlockSpec((tm, w), lambda i: (i, 0))
    const = lambda a: pl.BlockSpec(a.shape, lambda i: (0, 0))
    g = sub_g.reshape(1, LANES)
    ins = [o, o1, lam, g, w_o, res] if two else [o, g, w_o, res]
    specs = ([row(DIFF_WIDTH), row(DIFF_WIDTH), const(lam)] if two else [row(DIFF_WIDTH)]) + [const(g), const(w_o), row(d)]
    return pl.pallas_call(
        functools.partial(_diff_out_kernel, lam_init, two),
        grid=(n // tm,),
        in_specs=specs,
        out_specs=row(d),
        out_shape=jax.ShapeDtypeStruct((n, d), F32),
        scratch_shapes=[pltpu.VMEM((tm, DIFF_WIDTH), BF16)],
        compiler_params=_cparams("parallel"),
        name="diff_out",
    )(*ins)


def _gla_kernel(chunk, nchunk, t_valid, x_ref, ng_ref, w_ref, s0_ref, wa_ref, ba_ref, g_ref, tri_ref,
                y_ref, sfin_ref, st_ref, p_ref):
    step = pl.program_id(1)

    @pl.when(step == 0)
    def _():
        st_ref[...] = s0_ref[0]

    p_ref[0] = _dot(_rms_rows(x_ref[0], ng_ref[...]).astype(BF16), w_ref[...])

    nk = GLA_HEADS * GLA_DK
    nv = GLA_HEADS * GLA_DV
    tri = tri_ref[...]
    causal = lax.broadcasted_iota(jnp.int32, (chunk, chunk), 0) >= lax.broadcasted_iota(jnp.int32, (chunk, chunk), 1)
    for ci in range(nchunk):
        r0 = ci * chunk
        rows = slice(r0, r0 + chunk)
        a1 = p_ref[0, rows, GLA_MAIN:GLA_MAIN + LANES].astype(BF16)
        z = _dot(a1, wa_ref[...]) + ba_ref[...]
        log_a = (jnp.minimum(z, 0.0) - jnp.log(1.0 + jnp.exp(-jnp.abs(z)))) * (1.0 / GLA_TAU)
        if t_valid < chunk:
            log_a = jnp.where(lax.broadcasted_iota(jnp.int32, log_a.shape, 0) < t_valid, log_a, 0.0)
        cum = _split_dot_left(tri, log_a)
        for h in range(GLA_HEADS):
            ksl = slice(h * GLA_DK, (h + 1) * GLA_DK)
            vsl = slice(h * GLA_DV, (h + 1) * GLA_DV)
            q = p_ref[0, rows, h * GLA_DK:(h + 1) * GLA_DK] * (GLA_DK ** -0.5)
            k = p_ref[0, rows, nk + h * GLA_DK:nk + (h + 1) * GLA_DK]
            v = p_ref[0, rows, 2 * nk + h * GLA_DV:2 * nk + (h + 1) * GLA_DV]
            r = p_ref[0, rows, 2 * nk + nv + h * GLA_DV:2 * nk + nv + (h + 1) * GLA_DV]
            cm = cum[:, ksl]
            last = cm[chunk - 1:chunk, :]
            qe = (q * jnp.exp(cm)).astype(BF16)
            ke = (k * jnp.exp(-cm)).astype(BF16)
            vb = v.astype(BF16)
            att = jnp.where(causal, _dot_nt(qe, ke), 0.0)
            st = st_ref[h]
            o = _dot_nt(qe, st.astype(BF16)) + _dot(att.astype(BF16), vb)
            kd = (k * jnp.exp(last - cm)).astype(BF16)
            st_ref[h] = st * jnp.exp(last) + _dot(v.T.astype(BF16), kd)
            y_ref[0, rows, vsl] = _rms_rows(o, g_ref[...]) * _silu(r)

    @pl.when(step == pl.num_programs(1) - 1)
    def _():
        sfin_ref[0] = st_ref[...]


def gla_core(x, norm_g, w_in, s0_t, w_a2, b_a, out_g, chunk, nchunk, t_valid):
    b, t, d = x.shape
    rows = chunk * nchunk
    ng = norm_g.reshape(1, d)
    wa = jnp.zeros((LANES, GLA_HEADS * GLA_DK), F32).at[:GLA_RANK].set(w_a2).astype(BF16)
    ba = b_a.reshape(1, -1)
    g = out_g.reshape(1, GLA_DV)
    tri = (jnp.arange(chunk)[:, None] >= jnp.arange(chunk)[None, :]).astype(BF16)
    const = lambda a: pl.BlockSpec(a.shape, lambda bi, s: (0,) * a.ndim)
    st_spec = pl.BlockSpec((1, GLA_HEADS, GLA_DV, GLA_DK), lambda bi, s: (bi, 0, 0, 0))
    return pl.pallas_call(
        functools.partial(_gla_kernel, chunk, nchunk, t_valid),
        grid=(b, t // rows),
        in_specs=[pl.BlockSpec((1, rows, d), lambda bi, s: (bi, s, 0)), const(ng), const(w_in), st_spec,
                  const(wa), const(ba), const(g), const(tri)],
        out_specs=[pl.BlockSpec((1, rows, GLA_HEADS * GLA_DV), lambda bi, s: (bi, s, 0)), st_spec],
        out_shape=[jax.ShapeDtypeStruct((b, t, GLA_HEADS * GLA_DV), F32),
                   jax.ShapeDtypeStruct((b, GLA_HEADS, GLA_DV, GLA_DK), F32)],
        scratch_shapes=[pltpu.VMEM((GLA_HEADS, GLA_DV, GLA_DK), F32), pltpu.VMEM((1, rows, GLA_IN_PAD), F32)],
        compiler_params=_cparams("parallel", "arbitrary"),
        name="gla_core",
    )(x, ng, w_in, s0_t, wa, ba, g, tri)


def _pad_cols(w, width):
    return jnp.pad(w, ((0, 0), (0, width - w.shape[1]))).astype(BF16)


def _pad_axis(x, axis, size):
    pad = [(0, 0)] * x.ndim
    pad[axis] = (0, size - x.shape[axis])
    return jnp.pad(x, pad)


def _group_diag(q, t_seq):
    nseq = q.shape[0] // t_seq
    qg = q.reshape(nseq, t_seq, NSA_GROUPS, NSA_REP, HEAD_DIM).transpose(0, 2, 1, 3, 4)
    eye = jnp.eye(NSA_GROUPS, dtype=q.dtype)
    out = qg[:, :, :, :, None, :] * eye[None, :, None, None, :, None]
    return out.reshape(nseq, NSA_GROUPS * t_seq * NSA_REP, NSA_GROUPS * HEAD_DIM)


def _group_undiag(o, t_seq):
    nseq = o.shape[0]
    o6 = o.reshape(nseq, NSA_GROUPS, t_seq, NSA_REP, NSA_GROUPS, HEAD_DIM)
    od = jnp.stack([o6[:, g, :, :, g, :] for g in range(NSA_GROUPS)], axis=1)
    return od.transpose(0, 2, 1, 3, 4).reshape(nseq * t_seq, NSA_HEADS * HEAD_DIM)


def _seq_cols(x_t, nseq, t_seq):
    r = x_t.shape[0]
    return _pad_axis(x_t.reshape(r, nseq, t_seq).transpose(1, 0, 2), 2, PAGE_SIZE)


def _nsa_layer(s, xp, xs, seq_p, t_s, past_len, cache_t, state_win_t, page_table, norm_g, w_in, qk_g, pe, w_phi, w_o):
    bp = xp.shape[0] // seq_p
    bs = xs.shape[0] // t_s
    w_o_b = w_o.astype(BF16)
    npg = past_len // PAGE_SIZE

    qn, qr, gates, rows_t, win_t, ksel, kwin = nsa_project(xp, norm_g, w_in, qk_g, jnp.arange(seq_p), bp)
    own = _PageIndex(jnp.zeros((1, 1), jnp.int32), lambda b, p, pt: (b, 0, p))
    kc, vc = nsa_compress(rows_t, own, bp, seq_p // PAGE_SIZE, pe, w_phi, qk_g[3])
    qn3, qr3 = qn.reshape(bp, seq_p, 1024), qr.reshape(bp, seq_p, 1024)
    oc, sel = nsa_cmp_select(qn3, kc, vc, 0, seq_p // NSA_BLOCK)
    os_ = nsa_selected_prompt(qr3, sel, ksel)
    ow = nsa_window_prompt(qr3, kwin)
    xp_new = nsa_out(oc.reshape(-1, 1024), os_.reshape(-1, 1024), ow.reshape(-1, 1024), gates, w_o_b, xp)
    kv_p = rows_t.reshape(bp, 4, NSA_GROUPS, HEAD_DIM, seq_p).transpose(0, 4, 1, 2, 3)
    wlen = min(NSA_WINDOW, seq_p)
    win_p = win_t[:, :, seq_p - wlen:].reshape(bp, 2, NSA_GROUPS, HEAD_DIM, wlen).transpose(0, 4, 1, 2, 3)

    pos_s = past_len + jnp.arange(t_s)
    qn, qr, gates, rows_t, win_t, _, _ = nsa_project(xs, norm_g, w_in, qk_g, jnp.tile(pos_s, bs), 1)
    rows_t, win_t = rows_t[0], win_t[0]
    layer_pages = cache_t.shape[0] // (state_win_t.shape[0] // bs)
    table = page_table + s * layer_pages
    paged = _PageIndex(table, lambda b, p, pt: (pt[b, p], 0, 0))
    kc, vc = nsa_compress(cache_t, paged, bs, npg, pe, w_phi, qk_g[3])
    ns = -(-(past_len + t_s) // NSA_BLOCK)
    oc, sel = nsa_cmp_select_sample(qn, kc, vc, past_len, t_s, 256)
    nrow = NSA_GROUPS * t_s * NSA_REP
    sel_f = sel[:, :, :ns].astype(F32).reshape(NSA_GROUPS, bs, t_s, ns).transpose(1, 0, 2, 3)
    by_row = lambda a: jnp.broadcast_to(a[:, :, :, None, :], a.shape[:3] + (NSA_REP, a.shape[-1])).reshape(
        a.shape[0], nrow, a.shape[-1])
    new_ok = jnp.arange(PAGE_SIZE)[None, :] <= jnp.arange(t_s)[:, None]
    new_bias = by_row(jnp.where(new_ok[None, None], sel_f[..., past_len // NSA_BLOCK][..., None], NEG_INF))
    q_bd = _group_diag(qr, t_s)
    npp = min(NSA_PAGES_PER_STEP, npg)
    kpage = lambda blk: (lambda u: pl.BlockSpec((None, NSA_GW, PAGE_SIZE),
                                                lambda b, p, pt: (pt[b, jnp.minimum(p * npp + u, npg - 1)], blk, 0)))
    k_new = _seq_cols(rows_t[2 * NSA_GW:3 * NSA_GW], bs, t_s).astype(BF16)
    v_new = _seq_cols(rows_t[3 * NSA_GW:4 * NSA_GW], bs, t_s).astype(BF16)
    nblk = sel.shape[-1]
    sel_rows = by_row(sel.reshape(NSA_GROUPS, bs, t_s, nblk).transpose(1, 0, 2, 3))
    q_aug = jnp.concatenate([q_bd, sel_rows], axis=2)
    k_new_aug = jnp.concatenate([k_new, jnp.zeros((bs, nblk, PAGE_SIZE), BF16)], axis=1)
    wstep = npp * PAGE_SIZE
    key_blk = (jnp.arange(npg // npp)[:, None, None] * wstep + jnp.arange(wstep)[None, None, :]) // NSA_BLOCK
    e_steps = (jnp.arange(nblk)[None, :, None] == key_blk).astype(BF16)
    os_ = _group_undiag(paged_attention(q_aug, cache_t, table, npg, npp, kpage(2), kpage(3), True, k_new_aug, v_new,
                                        e_steps, new_bias, True, extra_rows=True), t_s)
    wbuf = state_win_t.shape[-1]
    wpg = wbuf // PAGE_SIZE
    wpage = lambda blk: (lambda u: pl.BlockSpec((None, NSA_GW, PAGE_SIZE),
                                                lambda b, p, pt: (s * bs + b, blk, jnp.minimum(p * wpg + u, wpg - 1))))
    kidx = jnp.arange(wbuf + PAGE_SIZE)
    wpos = jnp.where(kidx < wbuf, past_len - wbuf + kidx, past_len + kidx - wbuf)
    w_ok = ((wpos[None, :] <= pos_s[:, None]) & (wpos[None, :] > pos_s[:, None] - NSA_WINDOW) & (wpos[None, :] >= 0)
            & (kidx[None, :] < wbuf + t_s))
    wbias = by_row(jnp.broadcast_to(jnp.where(w_ok, 0.0, NEG_INF).astype(F32)[None, None],
                                    (1, NSA_GROUPS, t_s, wbuf + PAGE_SIZE)))
    kw_new = _seq_cols(win_t[0:NSA_GW], bs, t_s).astype(BF16)
    vw_new = _seq_cols(win_t[NSA_GW:2 * NSA_GW], bs, t_s).astype(BF16)
    ow = _group_undiag(paged_attention(q_bd, state_win_t, jnp.zeros((1, 1), jnp.int32), wpg, wpg, wpage(0), wpage(1), True,
                                       kw_new, vw_new, wbias[:, :, :wbuf], wbias[:, :, wbuf:], True), t_s)
    xs_new = nsa_out(oc, os_, ow, gates, w_o_b, xs)
    kv_s = rows_t.reshape(4, NSA_GROUPS, HEAD_DIM, bs, t_s).transpose(3, 4, 0, 1, 2)
    win_new = win_t.reshape(2 * NSA_GW, bs, t_s).transpose(1, 0, 2)
    win_all = jnp.concatenate([state_win_t[s * bs:(s + 1) * bs], win_new], axis=2)
    wlen = min(NSA_WINDOW, win_all.shape[2])
    win_s = win_all[:, :, win_all.shape[2] - wlen:].reshape(bs, 2, NSA_GROUPS, HEAD_DIM, wlen).transpose(0, 4, 1, 2, 3)
    return xp_new, xs_new, kv_p, kv_s, win_p, win_s


def _diff_layer(layer, xp, xs, seq_p, t_s, past_len, cache, page_table, norm_g, w_in, qk_g, lam, sub_g, w_o):
    bp = xp.shape[0] // seq_p
    bs = xs.shape[0] // t_s
    npg = past_len // PAGE_SIZE
    lam_init = 0.8 - 0.6 * math.exp(-0.3 * layer)
    w_in_b = w_in.astype(BF16)
    w_o_b = w_o.astype(BF16)
    cos_p, sin_p = _rope_tables(jnp.arange(seq_p))
    q, kvf, kvb = diff_project(xp, norm_g, w_in_b, jnp.tile(cos_p, (bp, 1)), jnp.tile(sin_p, (bp, 1)), qk_g)
    o = diff_flash_prompt(q.reshape(bp, seq_p, DIFF_WIDTH), kvb.reshape(bp, seq_p, 2 * DIFF_WIDTH), lam, lam_init)
    xp_new = diff_out(o.reshape(-1, DIFF_WIDTH), sub_g, w_o_b, xp, lam_init)
    kv_p = kvf.reshape(bp, seq_p, 2, DIFF_HEADS, 2 * HEAD_DIM)
    pos_s = past_len + jnp.arange(t_s)
    cos_s, sin_s = _rope_tables(pos_s)
    q, kvf, kvb = diff_project(xs, norm_g, w_in_b, jnp.tile(cos_s, (bs, 1)), jnp.tile(sin_s, (bs, 1)), qk_g)
    nrow = DIFF_HEADS * 2 * t_s
    q5 = q.reshape(bs, t_s, DIFF_HEADS, 2, HEAD_DIM).transpose(0, 2, 3, 1, 4)
    q_rows = (q5[:, :, :, :, None, :] * jnp.eye(2, dtype=q.dtype)[None, None, :, None, :, None]).reshape(
        bs, nrow, 2 * HEAD_DIM)
    kv5 = kvb.reshape(bs, t_s, 2, DIFF_HEADS, 2 * HEAD_DIM)
    k_new = _pad_axis(kv5[:, :, 0], 1, PAGE_SIZE).reshape(bs, PAGE_SIZE * DIFF_HEADS, 2 * HEAD_DIM)
    v_new = _pad_axis(kv5[:, :, 1], 1, PAGE_SIZE).reshape(bs, PAGE_SIZE * DIFF_HEADS, 2 * HEAD_DIM)
    npp = min(DIFF_PAGES_PER_STEP, npg)
    row_h = jnp.arange(nrow) // (2 * t_s)
    row_t = jnp.arange(nrow) % t_s
    slot_h = jnp.arange(PAGE_SIZE * DIFF_HEADS) % DIFF_HEADS
    slot_tok = jnp.arange(PAGE_SIZE * DIFF_HEADS) // DIFF_HEADS
    same_head = row_h[:, None] == slot_h[None, :]
    bias_page = jnp.where(same_head, 0.0, NEG_INF).astype(F32)
    bias_past = jnp.tile(bias_page, (1, npp))[None]
    bias_new = jnp.where(same_head & (slot_tok[None, :] <= row_t[:, None]), 0.0, NEG_INF).astype(F32)[None]
    page = lambda slot: (lambda u: pl.BlockSpec(
        (None, PAGE_SIZE, None, DIFF_HEADS, 2 * HEAD_DIM),
        lambda b, p, pt: (pt[b, jnp.minimum(p * npp + u, npg - 1)], 0, slot, 0, 0)))
    o = paged_attention(q_rows, cache, page_table, npg, npp, page(0), page(1), False, k_new, v_new,
                        bias_past, bias_new, False)
    od = o.reshape(bs, DIFF_HEADS, 2, t_s, 2 * HEAD_DIM).transpose(2, 0, 3, 1, 4).reshape(2, bs * t_s, DIFF_WIDTH)
    xs_new = diff_out(od[0], sub_g, w_o_b, xs, lam_init, o1=od[1], lam=lam)
    kv_s = kvf.reshape(bs, t_s, 2, DIFF_HEADS, 2 * HEAD_DIM)
    return xp_new, xs_new, kv_p, kv_s


def _gla_layer(xp, xs, seq_p, t_s, state, norm_g, w_in, w_a2, b_a, out_g, w_o):
    bp = xp.shape[0] // seq_p
    bs = xs.shape[0] // t_s
    w_in_b = _pad_cols(w_in, GLA_IN_PAD)
    w_o_b = w_o.astype(BF16)
    d = xp.shape[1]
    chunk = min(GLA_CHUNK, seq_p)
    s0 = jnp.zeros((bp, GLA_HEADS, GLA_DV, GLA_DK), F32)
    y, st = gla_core(xp.reshape(bp, seq_p, d), norm_g, w_in_b, s0, w_a2, b_a, out_g, chunk,
                     4 if seq_p % (4 * chunk) == 0 else 1, chunk)
    xp_new = matmul_residual(y.reshape(-1, GLA_HEADS * GLA_DV), w_o_b, xp)
    st_p = st.transpose(0, 1, 3, 2)
    chunk_s = GLA_CHUNK
    xs_pad = _pad_axis(xs.reshape(bs, t_s, d), 1, chunk_s)
    y, st = gla_core(xs_pad, norm_g, w_in_b, state.transpose(0, 1, 3, 2), w_a2, b_a, out_g, chunk_s, 1, t_s)
    xs_new = matmul_residual(y[:, :t_s].reshape(-1, GLA_HEADS * GLA_DV), w_o_b, xs)
    st_s = st.transpose(0, 1, 3, 2)
    return xp_new, xs_new, st_p, st_s


def kernel(x_prompt, x_sample, cache_nsa_kv, state_nsa_win, cache_diff_kv, state_gla, state_ffn, page_table, norm_g, ffn_w_up, ffn_conv_w, ffn_conv_b, ffn_w_down, nsa_w_in, nsa_qk_g, nsa_pe, nsa_w_phi, nsa_w_o, diff_w_in, diff_qk_g, diff_lam, diff_sub_g, diff_w_o, gla_w_in, gla_w_a2, gla_b_a, gla_out_g, gla_w_o):
    bp, seq_p, d = x_prompt.shape
    bs, t_s, _ = x_sample.shape
    past_len = page_table.shape[1] * PAGE_SIZE
    xp = x_prompt.reshape(bp * seq_p, d)
    xs = x_sample.reshape(bs * t_s, d)
    cache_t = cache_nsa_kv.transpose(0, 1, 3, 4, 5, 2).reshape(-1, 4 * NSA_GW, PAGE_SIZE)
    win_t = state_nsa_win.transpose(0, 1, 3, 4, 5, 2).reshape(-1, 2 * NSA_GW, state_nsa_win.shape[2])
    nsa_kv_p, nsa_kv_s, nsa_win_p, nsa_win_s = [], [], [], []
    diff_kv_p, diff_kv_s, gla_p, gla_s, ffn_p, ffn_s = [], [], [], [], [], []
    for i in range(DEPTH):
        kind, s = i % N_MIXERS, i // N_MIXERS
        if kind == 0:
            xp, xs, kvp, kvs, wp, ws = _nsa_layer(
                s, xp, xs, seq_p, t_s, past_len, cache_t, win_t, page_table,
                norm_g[i, 0], nsa_w_in[s], nsa_qk_g[s], nsa_pe[s], nsa_w_phi[s], nsa_w_o[s])
            nsa_kv_p.append(kvp); nsa_kv_s.append(kvs); nsa_win_p.append(wp); nsa_win_s.append(ws)
        elif kind == 1:
            xp, xs, kvp, kvs = _diff_layer(
                i, xp, xs, seq_p, t_s, past_len, cache_diff_kv[s], page_table, norm_g[i, 0], diff_w_in[s],
                diff_qk_g[s], diff_lam[s], diff_sub_g[s], diff_w_o[s])
            diff_kv_p.append(kvp); diff_kv_s.append(kvs)
        else:
            xp, xs, stp, sts = _gla_layer(xp, xs, seq_p, t_s, state_gla[s], norm_g[i, 0], gla_w_in[s], gla_w_a2[s],
                                          gla_b_a[s], gla_out_g[s], gla_w_o[s])
            gla_p.append(stp); gla_s.append(sts)
        w_up_b = ffn_w_up[i].astype(BF16)
        w_dn_b = ffn_w_down[i].astype(BF16)
        xp, tail_p = ffn_prompt(xp, norm_g[i, 1], w_up_b, ffn_conv_w[i], ffn_conv_b[i], w_dn_b, seq_p)
        xs, tail_s = ffn_sample(xs, norm_g[i, 1], w_up_b, ffn_conv_w[i], ffn_conv_b[i], w_dn_b, state_ffn[i], t_s)
        ffn_p.append(tail_p); ffn_s.append(tail_s)
    return (xp.reshape(bp, seq_p, d), xs.reshape(bs, t_s, d),
            jnp.stack(nsa_kv_p), jnp.stack(nsa_kv_s), jnp.stack(nsa_win_p), jnp.stack(nsa_win_s),
            jnp.stack(diff_kv_p), jnp.stack(diff_kv_s), jnp.stack(gla_p), jnp.stack(gla_s),
            jnp.stack(ffn_p), jnp.stack(ffn_s))
```

```python
import functools
import math

import jax
import jax.numpy as jnp
from jax import lax
from jax.experimental import pallas as pl
from jax.experimental.pallas import tpu as pltpu

F32 = jnp.float32
BF16 = jnp.bfloat16

D_MODEL = 1024
HEAD_DIM = 64
ROPE_THETA = 10000.0
NORM_EPS = 1e-6
NEG_INF = -1e30
DEPTH = 4
N_MIXERS = 3
PAGE_SIZE = 128
NSA_HEADS = 16
NSA_GROUPS = 4
NSA_REP = 4
NSA_BLOCK = 64
NSA_TOPN = 16
NSA_WINDOW = 512
NSA_FORCED = 1e9
NSA_NQ = NSA_HEADS * HEAD_DIM
NSA_NKV = 6 * NSA_GROUPS * HEAD_DIM
NSA_NGATE = 3 * NSA_HEADS
NSA_GW = NSA_GROUPS * HEAD_DIM
DIFF_HEADS = 8
DIFF_WIDTH = 1024
GLA_HEADS = 4
GLA_DK = 128
GLA_DV = 256
GLA_RANK = 16
GLA_TAU = 16.0
GLA_CHUNK = 64
GLA_MAIN = 2 * GLA_HEADS * GLA_DK + 2 * GLA_HEADS * GLA_DV
GLA_IN_PAD = GLA_MAIN + 128
D_FF = 2816
CONV_W = 3
LANES = 128
VMEM_LIMIT = 56 * 1024 * 1024
SEL_TK = 512
NSA_PAGES_PER_STEP = 16
DIFF_PAGES_PER_STEP = 8
CMP_PITCH = HEAD_DIM + 8


def _cparams(*sem):
    return pltpu.CompilerParams(dimension_semantics=sem, vmem_limit_bytes=VMEM_LIMIT)


def _dot(a, b):
    return jnp.dot(a, b, preferred_element_type=F32)


def _dot_nt(a, b):
    return lax.dot_general(a, b, (((1,), (1,)), ((), ())), preferred_element_type=F32)


def _split_dot(x, m):
    hi = x.astype(BF16)
    lo = (x - hi.astype(F32)).astype(BF16)
    return _dot(hi, m) + _dot(lo, m)


def _split_dot_left(m, x):
    hi = x.astype(BF16)
    lo = (x - hi.astype(F32)).astype(BF16)
    return _dot(m, hi) + _dot(m, lo)


def _rms_rows(x, g):
    ms = jnp.mean(x * x, axis=-1, keepdims=True)
    return x * lax.rsqrt(ms + NORM_EPS) * g


def _seg64_norm(x, g, seg_ones):
    ms = _split_dot(x * x, seg_ones) * (1.0 / HEAD_DIM)
    return x * lax.rsqrt(ms + NORM_EPS) * g


def _rope_slab(x, cos, sin_signed):
    lane = lax.broadcasted_iota(jnp.int32, x.shape, 1)
    first = (lane & 63) < 32
    partner = jnp.where(first, pltpu.roll(x, 96, 1), pltpu.roll(x, 32, 1))
    return x * cos + partner * sin_signed


def _rope_angles(pos):
    half = HEAD_DIM // 2
    inv = ROPE_THETA ** (-jnp.arange(half, dtype=F32) / half)
    ang = pos.astype(F32)[:, None] * inv
    return jnp.cos(ang), jnp.sin(ang)


def _rope_tables(pos):
    cos, sin = _rope_angles(pos)
    return jnp.tile(cos, (1, 4)), jnp.tile(jnp.concatenate([-sin, sin], axis=1), (1, 2))


def _seg_ones():
    i = jnp.arange(LANES)
    return (i[:, None] // HEAD_DIM == i[None, :] // HEAD_DIM).astype(BF16)


def _lanes(x, width):
    return x if width == LANES else jnp.tile(x, (1, width // LANES))


def _softmax_update(s, v_dot, m_ref, l_ref, acc_ref):
    m = m_ref[...]
    m_new = jnp.maximum(m, jnp.max(s, axis=-1, keepdims=True))
    alpha = jnp.exp2(m - m_new)
    p = jnp.exp2(s - _lanes(m_new, s.shape[1]))
    m_ref[...] = m_new
    l_ref[...] = alpha * l_ref[...] + jnp.sum(p, axis=-1, keepdims=True)
    acc_ref[...] = _lanes(alpha, acc_ref.shape[-1]) * acc_ref[...] + v_dot(p.astype(BF16))


def _softmax_result(l_ref, acc_ref):
    return acc_ref[...] / _lanes(l_ref[...], acc_ref.shape[-1])


def _softmax_update_vsum(s, v_dot, m_ref, acc_ref):
    m = m_ref[...]
    m_new = jnp.maximum(m, jnp.max(s, axis=-1, keepdims=True))
    alpha = jnp.exp2(m - m_new)
    p = jnp.exp2(s - _lanes(m_new, s.shape[1]))
    m_ref[...] = m_new
    acc_ref[...] = _lanes(alpha, acc_ref.shape[-1]) * acc_ref[...] + v_dot(p.astype(BF16))


def _softmax_init(m_ref, acc_ref, l_ref=None):
    m_ref[...] = jnp.full_like(m_ref, NEG_INF)
    acc_ref[...] = jnp.zeros_like(acc_ref)
    if l_ref is not None:
        l_ref[...] = jnp.zeros_like(l_ref)


QK_SCALE = HEAD_DIM ** -0.5
QK_SCALE_LOG2 = QK_SCALE * math.log2(math.e)


def _norm_matmul_kernel(x_ref, g_ref, w_ref, o_ref, h_ref):
    @pl.when(pl.program_id(1) == 0)
    def _():
        h_ref[...] = _rms_rows(x_ref[...], g_ref[...]).astype(BF16)

    o_ref[...] = _dot(h_ref[...], w_ref[...])


def norm_matmul(x, g, w, tn):
    n, d = x.shape
    nout = w.shape[1]
    tm = min(512, n)
    return pl.pallas_call(
        _norm_matmul_kernel,
        grid=(n // tm, nout // tn),
        in_specs=[pl.BlockSpec((tm, d), lambda i, j: (i, 0)),
                  pl.BlockSpec((1, d), lambda i, j: (0, 0)),
                  pl.BlockSpec((d, tn), lambda i, j: (0, j))],
        out_specs=pl.BlockSpec((tm, tn), lambda i, j: (i, j)),
        out_shape=jax.ShapeDtypeStruct((n, nout), F32),
        scratch_shapes=[pltpu.VMEM((tm, d), BF16)],
        compiler_params=_cparams("parallel", "arbitrary"),
        name="norm_matmul",
    )(x, g.reshape(1, d), w)


def _matmul_res_kernel(a_ref, w_ref, r_ref, o_ref):
    o_ref[...] = r_ref[...] + _dot(a_ref[...].astype(BF16), w_ref[...])


def matmul_residual(a, w, res):
    n, k = a.shape
    d = w.shape[1]
    tm = min(512, n)
    return pl.pallas_call(
        _matmul_res_kernel,
        grid=(n // tm,),
        in_specs=[pl.BlockSpec((tm, k), lambda i: (i, 0)),
                  pl.BlockSpec((k, d), lambda i: (0, 0)),
                  pl.BlockSpec((tm, d), lambda i: (i, 0))],
        out_specs=pl.BlockSpec((tm, d), lambda i: (i, 0)),
        out_shape=jax.ShapeDtypeStruct((n, d), F32),
        compiler_params=_cparams("parallel"),
        name="matmul_residual",
    )(a, w, res)


FFN_CHUNK = 256


def _silu(x):
    return x / (1.0 + jnp.exp(-x))


def _causal_conv(u, prev1, prev2, use1, use2, cw_ref, cb_ref):
    u1 = jnp.where(use1, prev1, pltpu.roll(u, 1, 0))
    u2 = jnp.where(use2, prev2, pltpu.roll(u, 2, 0))
    return cb_ref[...] + cw_ref[0:1, :] * u2 + cw_ref[1:2, :] * u1 + cw_ref[2:3, :] * u


def _ffn_prompt_kernel(tiles_per_seq, x_ref, g_ref, wg_ref, wu_ref, cwg_ref, cwu_ref, cbg_ref, cbu_ref,
                       wd_ref, o_ref, tg_ref, tu_ref, cg_ref, cu_ref, act_ref):
    i = pl.program_id(0)
    tm = x_ref.shape[0]
    nj, _, c = wg_ref.shape
    x = x_ref[...]
    h = _rms_rows(x, g_ref[...]).astype(BF16)
    row = lax.broadcasted_iota(jnp.int32, (tm, c), 0)

    @pl.when((i % tiles_per_seq) == 0)
    def _():
        cg_ref[...] = jnp.zeros_like(cg_ref)
        cu_ref[...] = jnp.zeros_like(cu_ref)

    def conv(u, prev, cw_ref, cb_ref):
        prev2 = jnp.where(row == 0, prev[0:1, :], prev[1:2, :])
        return _causal_conv(u, prev[1:2, :], prev2, row == 0, row < 2, cw_ref, cb_ref)

    for j in range(nj):
        ug = _dot(h, wg_ref[j])
        uu = _dot(h, wu_ref[j])
        act = _silu(conv(ug, cg_ref[j], cwg_ref.at[j], cbg_ref.at[j])) * conv(uu, cu_ref[j], cwu_ref.at[j], cbu_ref.at[j])
        act_ref[:, j * c:(j + 1) * c] = act.astype(BF16)
        cg_ref[j] = ug[tm - 2:tm, :]
        cu_ref[j] = uu[tm - 2:tm, :]
        tg_ref[0, :, j * c:(j + 1) * c] = ug[tm - 2:tm, :]
        tu_ref[0, :, j * c:(j + 1) * c] = uu[tm - 2:tm, :]
    o_ref[...] = x + _dot(act_ref[...], wd_ref[...])


def _resident(a):
    return pl.BlockSpec(a.shape, lambda i: (0,) * a.ndim, pipeline_mode=pl.Buffered(1))


def ffn_prompt(x, g, w_up, conv_w, conv_b, w_down, seq_len):
    n, d = x.shape
    c = FFN_CHUNK
    nj = D_FF // c
    tm = min(512, seq_len)
    tps = seq_len // tm
    chunks = lambda a: a.reshape(a.shape[0], 2, nj, c).transpose(1, 2, 0, 3)
    wg, wu = chunks(w_up)
    cwg, cwu = chunks(conv_w)
    cbg, cbu = chunks(conv_b.reshape(1, 2 * D_FF))
    g2 = g.reshape(1, d)
    consts = [g2, wg, wu, cwg, cwu, cbg, cbu, w_down]
    out, tg, tu = pl.pallas_call(
        functools.partial(_ffn_prompt_kernel, tps),
        grid=(n // tm,),
        in_specs=[pl.BlockSpec((tm, d), lambda i: (i, 0))] + [_resident(a) for a in consts],
        out_specs=[pl.BlockSpec((tm, d), lambda i: (i, 0)),
                   pl.BlockSpec((1, 2, D_FF), lambda i: (i, 0, 0)),
                   pl.BlockSpec((1, 2, D_FF), lambda i: (i, 0, 0))],
        out_shape=[jax.ShapeDtypeStruct((n, d), F32),
                   jax.ShapeDtypeStruct((n // tm, 2, D_FF), F32),
                   jax.ShapeDtypeStruct((n // tm, 2, D_FF), F32)],
        scratch_shapes=[pltpu.VMEM((nj, 2, c), F32),
                        pltpu.VMEM((nj, 2, c), F32),
                        pltpu.VMEM((tm, D_FF), BF16)],
        compiler_params=_cparams("arbitrary"),
        name="ffn_prompt",
    )(x, *consts)
    return out, jnp.concatenate([tg, tu], axis=-1)[tps - 1::tps]


def _ffn_sample_kernel(t_seq, x_ref, g_ref, wg_ref, wu_ref, cwg_ref, cwu_ref, cbg_ref, cbu_ref, wd_ref,
                       p1g_ref, p2g_ref, p1u_ref, p2u_ref, o_ref, ug_ref, uu_ref, h_ref):
    j = pl.program_id(0)

    @pl.when(j == 0)
    def _():
        h_ref[...] = _rms_rows(x_ref[...], g_ref[...]).astype(BF16)

    h = h_ref[...]
    ug = _dot(h, wg_ref[...])
    uu = _dot(h, wu_ref[...])
    t = lax.broadcasted_iota(jnp.int32, ug.shape, 0) & (t_seq - 1)
    cg = _causal_conv(ug, p1g_ref[...], p2g_ref[...], t == 0, t < 2, cwg_ref, cbg_ref)
    cu = _causal_conv(uu, p1u_ref[...], p2u_ref[...], t == 0, t < 2, cwu_ref, cbu_ref)
    part = _dot((_silu(cg) * cu).astype(BF16), wd_ref[...])
    ug_ref[...] = ug
    uu_ref[...] = uu

    @pl.when(j == 0)
    def _():
        o_ref[...] = x_ref[...] + part

    @pl.when(j > 0)
    def _():
        o_ref[...] += part


def ffn_sample(x, g, w_up, conv_w, conv_b, w_down, buf, t_seq):
    n, d = x.shape
    nseq = n // t_seq
    c = FFN_CHUNK
    nj = D_FF // c
    cb = conv_b.reshape(1, 2 * D_FF)
    reps = t_seq // 2
    prev1 = jnp.broadcast_to(buf[:, 1:2], (nseq, t_seq, 2 * D_FF)).reshape(n, 2 * D_FF)
    prev2 = jnp.concatenate([buf] * reps, axis=1).reshape(n, 2 * D_FF)
    full = lambda blk, off=0: pl.BlockSpec(blk, lambda j: (0, j + off))
    out, ug, uu = pl.pallas_call(
        functools.partial(_ffn_sample_kernel, t_seq),
        grid=(nj,),
        in_specs=[pl.BlockSpec((n, d), lambda j: (0, 0)),
                  pl.BlockSpec((1, d), lambda j: (0, 0)),
                  full((d, c)), full((d, c), nj),
                  full((CONV_W, c)), full((CONV_W, c), nj),
                  full((1, c)), full((1, c), nj),
                  pl.BlockSpec((c, d), lambda j: (j, 0)),
                  full((n, c)), full((n, c)), full((n, c), nj), full((n, c), nj)],
        out_specs=[pl.BlockSpec((n, d), lambda j: (0, 0)),
                   full((n, c)), full((n, c))],
        out_shape=[jax.ShapeDtypeStruct((n, d), F32),
                   jax.ShapeDtypeStruct((n, D_FF), F32),
                   jax.ShapeDtypeStruct((n, D_FF), F32)],
        scratch_shapes=[pltpu.VMEM((n, d), BF16)],
        compiler_params=_cparams("arbitrary"),
        name="ffn_sample",
    )(x, g.reshape(1, d), w_up, w_up, conv_w, conv_w, cb, cb, w_down, prev1, prev2, prev1, prev2)
    u = jnp.concatenate([ug, uu], axis=-1).reshape(nseq, t_seq, 2 * D_FF)
    return out, u[:, t_seq - (CONV_W - 1):]


def _head_norm_t(x, g):
    ms = jnp.mean(x * x, axis=0, keepdims=True)
    return x * lax.rsqrt(ms + NORM_EPS) * g


def _rope_t(x, cos, sin):
    half = HEAD_DIM // 2
    x1, x2 = x[0:half, :], x[half:HEAD_DIM, :]
    return jnp.concatenate([x1 * cos - x2 * sin, x1 * sin + x2 * cos], axis=0)


def _nsa_project_kernel(x_ref, g_ref, wq_ref, wkv_ref, wg_ref, cos_ref, sin_ref, cost_ref, sint_ref, gq_ref, gk_ref,
                        so_ref, qn_ref, qr_ref, gate_ref, rows_ref, win_ref, ksel_ref, kwin_ref):
    hn = _rms_rows(x_ref[...], g_ref[...]).astype(BF16)
    cos, sin = cos_ref[...], sin_ref[...]
    so = so_ref[...]
    q_all = _dot(hn, wq_ref[...])
    for s in range(NSA_NQ // LANES):
        sl = slice(s * LANES, (s + 1) * LANES)
        q = _seg64_norm(q_all[:, sl], gq_ref[...], so)
        qn_ref[:, sl] = (q * QK_SCALE).astype(BF16)
        qr_ref[:, sl] = (_rope_slab(q, cos, sin) * QK_SCALE_LOG2).astype(BF16)
    gate_ref[...] = 1.0 / (1.0 + jnp.exp(-_dot(hn, wg_ref[...])))
    kvt = _dot_nt(wkv_ref[...], hn)
    gw = NSA_GW
    cost, sint = cost_ref[...], sint_ref[...]
    rows_ref[0, 0:2 * gw, :] = kvt[0:2 * gw, :]
    rows_ref[0, 3 * gw:4 * gw, :] = kvt[3 * gw:4 * gw, :]
    win_ref[0, gw:2 * gw, :] = kvt[5 * gw:6 * gw, :]
    for g in range(NSA_GROUPS):
        hs = slice(g * HEAD_DIM, (g + 1) * HEAD_DIM)
        rows_ref[0, 2 * gw + g * HEAD_DIM:2 * gw + (g + 1) * HEAD_DIM, :] = _rope_t(
            _head_norm_t(kvt[2 * gw + g * HEAD_DIM:2 * gw + (g + 1) * HEAD_DIM, :], gk_ref[0]), cost, sint)
        win_ref[0, hs, :] = _rope_t(
            _head_norm_t(kvt[4 * gw + g * HEAD_DIM:4 * gw + (g + 1) * HEAD_DIM, :], gk_ref[1]), cost, sint)
    tm = kvt.shape[1]
    for g in range(NSA_GROUPS):
        hs = slice(g * HEAD_DIM, (g + 1) * HEAD_DIM)
        ksel_ref[0, 0, g, 0:HEAD_DIM, :] = rows_ref[0, 2 * gw + g * HEAD_DIM:2 * gw + (g + 1) * HEAD_DIM, :].astype(BF16)
        ksel_ref[0, 0, g, HEAD_DIM:2 * HEAD_DIM, :] = jnp.zeros((HEAD_DIM, tm), BF16)
        ksel_ref[0, 0, g, 2 * HEAD_DIM:3 * HEAD_DIM, :] = kvt[3 * gw + g * HEAD_DIM:3 * gw + (g + 1) * HEAD_DIM, :].astype(BF16)
        ksel_ref[0, 0, g, 3 * HEAD_DIM:4 * HEAD_DIM, :] = jnp.ones((HEAD_DIM, tm), BF16)
    wb = win_ref[0].astype(BF16)
    for u in range(kwin_ref.shape[1]):
        ts = slice(u * LANES, (u + 1) * LANES)
        for g in range(NSA_GROUPS):
            kwin_ref[0, u, g, 0:HEAD_DIM, :] = wb[g * HEAD_DIM:(g + 1) * HEAD_DIM, ts]
            kwin_ref[0, u, g, HEAD_DIM:2 * HEAD_DIM, :] = jnp.zeros((HEAD_DIM, LANES), BF16)
            kwin_ref[0, u, g, 2 * HEAD_DIM:3 * HEAD_DIM, :] = wb[gw + g * HEAD_DIM:gw + (g + 1) * HEAD_DIM, ts]
            kwin_ref[0, u, g, 3 * HEAD_DIM:4 * HEAD_DIM, :] = jnp.ones((HEAD_DIM, LANES), BF16)


def nsa_project(x, norm_g, w_in, qk_g, pos, nseq):
    n, d = x.shape
    t = n // nseq
    tm = min(SEL_TK, t)
    nt = t // tm
    wq = w_in[:, :NSA_NQ].astype(BF16)
    wkv_t = w_in[:, NSA_NQ:NSA_NQ + NSA_NKV].T.astype(BF16)
    wg = jnp.pad(w_in[:, NSA_NQ + NSA_NKV:], ((0, 0), (0, LANES - NSA_NGATE))).astype(BF16)
    cos, sin = _rope_angles(pos)
    cos_q, sin_q = jnp.tile(cos, (1, 4)), jnp.tile(jnp.concatenate([-sin, sin], axis=1), (1, 2))
    gq = jnp.tile(qk_g[0:1], (1, 2))
    gk = qk_g[1:3].reshape(2, HEAD_DIM, 1)
    so = _seg_ones()
    const = lambda a: pl.BlockSpec(a.shape, lambda b, i: (0,) * a.ndim)
    row = lambda w: pl.BlockSpec((tm, w), lambda b, i: (b * nt + i, 0))
    return pl.pallas_call(
        _nsa_project_kernel,
        grid=(nseq, nt),
        in_specs=[row(d), const(norm_g.reshape(1, d)), const(wq), const(wkv_t), const(wg),
                  pl.BlockSpec((tm, LANES), lambda b, i: (i, 0)), pl.BlockSpec((tm, LANES), lambda b, i: (i, 0)),
                  pl.BlockSpec((HEAD_DIM // 2, tm), lambda b, i: (0, i)),
                  pl.BlockSpec((HEAD_DIM // 2, tm), lambda b, i: (0, i)),
                  const(gq), const(gk), const(so)],
        out_specs=[row(NSA_NQ), row(NSA_NQ), row(LANES),
                   pl.BlockSpec((1, 4 * NSA_GW, tm), lambda b, i: (b, 0, i)),
                   pl.BlockSpec((1, 2 * NSA_GW, tm), lambda b, i: (b, 0, i)),
                   pl.BlockSpec((1, 1, NSA_GROUPS, 4 * HEAD_DIM, tm), lambda b, i: (b, i, 0, 0, 0)),
                   pl.BlockSpec((1, tm // LANES, NSA_GROUPS, 4 * HEAD_DIM, LANES), lambda b, i: (b, i, 0, 0, 0))],
        out_shape=[jax.ShapeDtypeStruct((n, NSA_NQ), BF16), jax.ShapeDtypeStruct((n, NSA_NQ), BF16),
                   jax.ShapeDtypeStruct((n, LANES), F32),
                   jax.ShapeDtypeStruct((nseq, 4 * NSA_GW, t), F32),
                   jax.ShapeDtypeStruct((nseq, 2 * NSA_GW, t), F32),
                   jax.ShapeDtypeStruct((nseq, nt, NSA_GROUPS, 4 * HEAD_DIM, tm), BF16),
                   jax.ShapeDtypeStruct((nseq, t // LANES, NSA_GROUPS, 4 * HEAD_DIM, LANES), BF16)],
        compiler_params=_cparams("parallel", "parallel"),
        name="nsa_project",
    )(x, norm_g.reshape(1, d), wq, wkv_t, wg, cos_q, sin_q, cos.T, sin.T, gq, gk, so)


def _nsa_compress_kernel(npp, pt_ref, *refs):
    page_refs = refs[:npp]
    pe_ref, w_ref, g_ref, so_ref, kc_ref, vc_ref, seqk_ref, seqv_ref, acc_ref = refs[npp:]
    p = pl.program_id(1)
    gw = NSA_GW
    pitch = CMP_PITCH
    for u in range(npp):
        for g in range(NSA_GROUPS):
            lo = pl.multiple_of(((p * npp + u) * NSA_GROUPS + g) * pitch, 8)
            seqk_ref[pl.ds(lo, HEAD_DIM), :] = page_refs[u][g * HEAD_DIM:(g + 1) * HEAD_DIM, :]
            seqv_ref[pl.ds(lo, HEAD_DIM), :] = page_refs[u][gw + g * HEAD_DIM:gw + (g + 1) * HEAD_DIM, :]

    @pl.when(p == pl.num_programs(1) - 1)
    def _():
        nrow = seqk_ref.shape[0] // pitch
        acc_ref[...] = jnp.zeros_like(acc_ref)
        for dd in range(HEAD_DIM):
            a = jnp.concatenate([seqk_ref[pl.ds(dd, nrow, stride=pitch), :],
                                 seqv_ref[pl.ds(dd, nrow, stride=pitch), :]], axis=1) + pe_ref[dd:dd + 1, :]
            acc_ref[...] += _dot(a.astype(BF16), w_ref[dd])
        kc_ref[0] = _seg64_norm(acc_ref[:, 0:LANES], g_ref[...], so_ref[...])
        vc_ref[0] = acc_ref[:, LANES:2 * LANES]


def nsa_compress(pages_t, page_index, nseq, npg, pe, w_phi, g_c):
    npp = min(NSA_PAGES_PER_STEP, npg)
    nrow = npg * NSA_GROUPS
    pe_t = jnp.concatenate([jnp.tile(pe[0].T, (1, 2)), jnp.tile(pe[1].T, (1, 2))], axis=1)
    eye4 = jnp.eye(4, dtype=F32)
    w4 = jnp.stack([w_phi[0], w_phi[0], w_phi[1], w_phi[1]])
    w = jnp.einsum('ab,alde->dalbe', eye4, w4).reshape(HEAD_DIM, 2 * LANES, 2 * LANES).astype(BF16)
    g2 = jnp.tile(g_c.reshape(1, HEAD_DIM), (1, 2))
    so = _seg_ones()
    const = lambda a: pl.BlockSpec(a.shape, lambda b, p, pt: (0,) * a.ndim, pipeline_mode=pl.Buffered(1))
    page = lambda u: pl.BlockSpec((None, 2 * NSA_GW, PAGE_SIZE), lambda b, p, pt: page_index(b, p * npp + u, pt))
    out = pl.BlockSpec((1, nrow, LANES), lambda b, p, pt: (b, 0, 0))
    table = page_index.table
    kc, vc = pl.pallas_call(
        functools.partial(_nsa_compress_kernel, npp),
        grid_spec=pltpu.PrefetchScalarGridSpec(
            num_scalar_prefetch=1,
            grid=(nseq, npg // npp),
            in_specs=[page(u) for u in range(npp)] + [const(pe_t), const(w), const(g2), const(so)],
            out_specs=[out, out],
            scratch_shapes=[pltpu.VMEM((nrow * CMP_PITCH, PAGE_SIZE), F32), pltpu.VMEM((nrow * CMP_PITCH, PAGE_SIZE), F32),
                            pltpu.VMEM((nrow, 2 * LANES), F32)]),
        out_shape=[jax.ShapeDtypeStruct((nseq, nrow, LANES), F32)] * 2,
        compiler_params=_cparams("arbitrary", "arbitrary"),
        name="nsa_compress",
    )(table, *([pages_t] * npp), pe_t, w, g2, so)
    fix = lambda a: a.reshape(nseq, npg, NSA_GROUPS, 2, HEAD_DIM).transpose(0, 2, 1, 3, 4).reshape(
        nseq, NSA_GROUPS, 2 * npg, HEAD_DIM)
    return fix(kc), fix(vc)


class _PageIndex:
    def __init__(self, table, fn):
        self.table = table
        self._fn = fn

    def __call__(self, b, p, pt):
        return self._fn(b, p, pt)


def _stack_heads(q):
    head = lax.broadcasted_iota(jnp.int32, q.shape, 1) >> 6
    qf = q.astype(F32)
    return jnp.concatenate([jnp.where(head == r, qf, 0.0) for r in range(NSA_REP)], axis=0).astype(BF16)


def _unstack_heads(o4, tq):
    head = lax.broadcasted_iota(jnp.int32, (tq, o4.shape[1]), 1) >> 6
    out = jnp.zeros((tq, o4.shape[1]), F32)
    for r in range(NSA_REP):
        out = jnp.where(head == r, o4[r * tq:(r + 1) * tq, :], out)
    return out


def _tile_rows4(x):
    return jnp.concatenate([x] * NSA_REP, axis=0)


def _nsa_cmp_kernel(q0, ns_rows, q_ref, kc_ref, vc_ref, oc_ref, sel_ref):
    i = pl.program_id(2)
    tq = q_ref.shape[1]
    nc = kc_ref.shape[2]
    qst = _stack_heads(q_ref[0])
    kc, vc = kc_ref[0, 0], vc_ref[0, 0]
    base = q0 + i * tq
    qpos = base + (lax.broadcasted_iota(jnp.int32, (NSA_REP * tq, nc), 0) & (tq - 1))
    blk_end = (lax.broadcasted_iota(jnp.int32, (NSA_REP * tq, nc), 1) + 1) * NSA_BLOCK - 1
    ok = blk_end <= qpos
    s = jnp.where(ok, _dot_nt(qst, kc), NEG_INF)
    e = jnp.exp(s - jnp.max(s, axis=-1, keepdims=True))
    p = jnp.where(ok, e / jnp.sum(e, axis=-1, keepdims=True), 0.0)
    oc_ref[0] = _unstack_heads(_dot(p.astype(BF16), vc), tq)
    imp = p[0:tq, :]
    for r in range(1, NSA_REP):
        imp = imp + p[r * tq:(r + 1) * tq, :]
    qpos_col = base + lax.broadcasted_iota(jnp.int32, (ns_rows, tq), 1)
    sel_ref[0, 0] = _select_blocks(imp.T, qpos_col, ns_rows).T.astype(BF16)


def _select_blocks(imp, qpos, ns_rows):
    nc, cols = imp.shape
    if ns_rows > nc:
        imp = jnp.concatenate([imp, jnp.zeros((ns_rows - nc, cols), F32)], axis=0)
    blk = lax.broadcasted_iota(jnp.int32, (ns_rows, cols), 0)
    cur = qpos >> 6
    forced = (blk == 0) | (blk == cur) | (blk == cur - 1)
    imp = jnp.where(forced, NSA_FORCED, imp)
    imp = jnp.where(blk <= cur, imp, NEG_INF)
    taken = jnp.float32(-3e38)
    blk_f = blk.astype(F32)

    def pick(_, imp):
        m = jnp.max(imp, axis=0, keepdims=True)
        first = jnp.min(jnp.where(imp == m, blk_f, float(ns_rows)), axis=0, keepdims=True)
        return jnp.where(blk_f == first, taken, imp)

    imp = lax.fori_loop(0, NSA_TOPN, pick, imp)
    return jnp.where(imp == taken, 0.0, NEG_INF)


def _nsa_cmp_sample_kernel(q0, t_seq, ns_rows, q_ref, kc_ref, vc_ref, oc_ref, sel_ref):
    ntok = q_ref.shape[0]
    nkey = kc_ref.shape[1]
    nc = nkey // (ntok // t_seq)
    rows = NSA_REP * ntok
    qst = _stack_heads(q_ref[...])
    row = lax.broadcasted_iota(jnp.int32, (rows, nkey), 0) & (ntok - 1)
    col = lax.broadcasted_iota(jnp.int32, (rows, nkey), 1)
    qpos = q0 + (row & (t_seq - 1))
    same_seq = (row >> (t_seq.bit_length() - 1)) == (col >> (nc.bit_length() - 1))
    ok = same_seq & (((col & (nc - 1)) + 1) * NSA_BLOCK - 1 <= qpos)
    s = jnp.where(ok, _dot_nt(qst, kc_ref[0]), NEG_INF)
    e = jnp.exp(s - jnp.max(s, axis=-1, keepdims=True))
    p = jnp.where(ok, e / jnp.sum(e, axis=-1, keepdims=True), 0.0)
    oc_ref[...] = _unstack_heads(_dot(p.astype(BF16), vc_ref[0]), ntok)
    own = p[:, 0:nc]
    for u in range(1, nkey // nc):
        own = own + p[:, u * nc:(u + 1) * nc]
    imp = own[0:ntok, :]
    for r in range(1, NSA_REP):
        imp = imp + own[r * ntok:(r + 1) * ntok, :]
    qpos_col = q0 + (lax.broadcasted_iota(jnp.int32, (ns_rows, ntok), 1) & (t_seq - 1))
    sel_ref[0] = _select_blocks(imp.T, qpos_col, ns_rows).T.astype(BF16)


def nsa_cmp_select_sample(qn, kc, vc, q0, t_seq, ns_rows):
    ntok = qn.shape[0]
    nseq, _, nc, _ = kc.shape
    flat = lambda a: _tile_lanes4(a.transpose(1, 0, 2, 3).reshape(1, NSA_GROUPS, nseq * nc, HEAD_DIM))[0]
    kv = pl.BlockSpec((1, nseq * nc, 256), lambda g: (g, 0, 0))
    return pl.pallas_call(
        functools.partial(_nsa_cmp_sample_kernel, q0, t_seq, ns_rows),
        grid=(NSA_GROUPS,),
        in_specs=[pl.BlockSpec((ntok, 256), lambda g: (0, g)), kv, kv],
        out_specs=[pl.BlockSpec((ntok, 256), lambda g: (0, g)),
                   pl.BlockSpec((1, ntok, ns_rows), lambda g: (g, 0, 0))],
        out_shape=[jax.ShapeDtypeStruct((ntok, 1024), F32),
                   jax.ShapeDtypeStruct((NSA_GROUPS, ntok, ns_rows), BF16)],
        compiler_params=_cparams("parallel"),
        name="nsa_cmp_select_sample",
    )(qn, flat(kc), flat(vc))


def _tile_lanes4(x):
    return jnp.tile(x, (1, 1, 1, NSA_REP)).astype(BF16)


def nsa_cmp_select(qn, kc, vc, q0, ns_rows):
    b, t, _ = qn.shape
    tq = min(512, t)
    nc = kc.shape[2]
    kv = pl.BlockSpec((1, 1, nc, 256), lambda bi, g, i: (bi, g, 0, 0))
    return pl.pallas_call(
        functools.partial(_nsa_cmp_kernel, q0, ns_rows),
        grid=(b, NSA_GROUPS, t // tq),
        in_specs=[pl.BlockSpec((1, tq, 256), lambda bi, g, i: (bi, i, g)), kv, kv],
        out_specs=[pl.BlockSpec((1, tq, 256), lambda bi, g, i: (bi, i, g)),
                   pl.BlockSpec((1, 1, tq, ns_rows), lambda bi, g, i: (bi, g, i, 0))],
        out_shape=[jax.ShapeDtypeStruct((b, t, 1024), F32),
                   jax.ShapeDtypeStruct((b, NSA_GROUPS, t, ns_rows), BF16)],
        compiler_params=_cparams("parallel", "parallel", "parallel"),
        name="nsa_cmp_select",
    )(qn, _tile_lanes4(kc), _tile_lanes4(vc))


def _head_rows(q):
    lane = lax.broadcasted_iota(jnp.int32, (q.shape[0], LANES), 1)
    out = []
    for r in range(NSA_REP):
        slab = q[:, (r // 2) * LANES:(r // 2 + 1) * LANES]
        slab = pltpu.roll(slab, HEAD_DIM, 1) if r % 2 else slab
        out.append(jnp.where(lane < HEAD_DIM, slab, 0.0))
    return out


def _store_heads(o_ref, row0, acc, tq):
    lane = lax.broadcasted_iota(jnp.int32, (tq, LANES), 1)
    o4 = acc / pltpu.roll(acc, HEAD_DIM, 1)
    for half in range(NSA_REP // 2):
        even = o4[(2 * half) * tq:(2 * half + 1) * tq, :]
        odd = pltpu.roll(o4[(2 * half + 1) * tq:(2 * half + 2) * tq, :], HEAD_DIM, 1)
        o_ref[0, row0:row0 + tq, half * LANES:(half + 1) * LANES] = jnp.where(lane < HEAD_DIM, even, odd)


WIN_SUB = 4


def _nsa_window_kernel(nsub, q_ref, kv_ref, o_ref):
    i = pl.program_id(2)
    tq = q_ref.shape[1] // nsub
    ntile = (NSA_WINDOW + 2 * tq) // LANES
    span = ntile * LANES
    rows = NSA_REP * tq
    for pair in range(nsub // 2):
        j0 = jnp.maximum((i * nsub + 2 * pair) * tq - NSA_WINDOW, 0) // LANES
        k_aug = jnp.concatenate([kv_ref[0, j0 + u, 0, 0:2 * HEAD_DIM, :] for u in range(ntile)], axis=1)
        v_aug = jnp.concatenate([kv_ref[0, j0 + u, 0, 2 * HEAD_DIM:4 * HEAD_DIM, :] for u in range(ntile)], axis=1)
        kpos = j0 * LANES + lax.broadcasted_iota(jnp.int32, (rows, span), 1)
        for a in range(2 * pair, 2 * pair + 2):
            lhs = jnp.concatenate(_head_rows(q_ref[0, a * tq:(a + 1) * tq, :].astype(F32)), axis=0).astype(BF16)
            qpos = (i * nsub + a) * tq + (lax.broadcasted_iota(jnp.int32, (rows, span), 0) & (tq - 1))
            ok = (kpos <= qpos) & (kpos > qpos - NSA_WINDOW)
            s = jnp.where(ok, _dot(lhs, k_aug), NEG_INF)
            e = jnp.exp2(s - jnp.max(s, axis=-1, keepdims=True))
            _store_heads(o_ref, a * tq, _dot_nt(e.astype(BF16), v_aug), tq)


def nsa_window_prompt(qr, kwin):
    b, t, _ = qr.shape
    tq = WIN_SUB * 128
    nt = kwin.shape[1]
    return pl.pallas_call(
        functools.partial(_nsa_window_kernel, WIN_SUB),
        grid=(b, NSA_GROUPS, t // tq),
        in_specs=[pl.BlockSpec((1, tq, 256), lambda bi, g, i: (bi, i, g)),
                  pl.BlockSpec((1, nt, 1, 4 * HEAD_DIM, LANES), lambda bi, g, i: (bi, 0, g, 0, 0))],
        out_specs=pl.BlockSpec((1, tq, 256), lambda bi, g, i: (bi, i, g)),
        out_shape=jax.ShapeDtypeStruct((b, t, 1024), F32),
        compiler_params=_cparams("parallel", "parallel", "arbitrary"),
        name="nsa_window",
    )(qr, kwin)


SEL_SUB = 8
SEL_KT = 2


def _nsa_selected_kernel(kt, q_ref, sel_ref, kv_ref, e_ref, o_ref, m_ref, acc_ref):
    i = pl.program_id(2)
    nsub = m_ref.shape[0]
    tq = q_ref.shape[1] // nsub
    tk = kt * kv_ref.shape[4]
    rows = NSA_REP * tq

    def lhs(a):
        sel = sel_ref[0, 0, a * tq:(a + 1) * tq, :].astype(F32)
        heads = _head_rows(q_ref[0, a * tq:(a + 1) * tq, :].astype(F32))
        return jnp.concatenate([jnp.concatenate([h, sel], axis=1) for h in heads], axis=0).astype(BF16)

    qs = [lhs(a) for a in range(nsub)]
    _softmax_init(m_ref, acc_ref)

    def tile(j, causal):
        span = lambda piece: jnp.concatenate([piece(j * kt + u) for u in range(kt)], axis=1)
        k_aug = jnp.concatenate([span(lambda jj: kv_ref[0, jj, 0, 0:2 * HEAD_DIM, :]), span(lambda jj: e_ref[jj])],
                                axis=0)
        v_aug = span(lambda jj: kv_ref[0, jj, 0, 2 * HEAD_DIM:4 * HEAD_DIM, :])
        for a in range(nsub):
            s = _dot(qs[a], k_aug)
            if causal:
                qpos = (i * nsub + a) * tq + (lax.broadcasted_iota(jnp.int32, (rows, tk), 0) & (tq - 1))
                kpos = j * tk + lax.broadcasted_iota(jnp.int32, (rows, tk), 1)
                s = jnp.where(kpos <= qpos, s, NEG_INF)
            _softmax_update_vsum(s, lambda p: _dot_nt(p, v_aug), m_ref.at[a], acc_ref.at[a])

    nfull = (i * nsub * tq) // tk

    def body(j, c):
        tile(j, False)
        return c

    lax.fori_loop(0, nfull, body, 0)
    tile(nfull, True)
    for a in range(nsub):
        _store_heads(o_ref, a * tq, acc_ref[a], tq)


def nsa_selected_prompt(qr, sel, ksel):
    b, t, _ = qr.shape
    sub = 128
    tq = SEL_SUB * sub
    nt, _, _, tk = ksel.shape[1:]
    assert (SEL_KT * tk) % tq == 0 and nt % SEL_KT == 0
    nblk = sel.shape[-1]
    e3 = (jnp.arange(nblk)[None, :, None] == (jnp.arange(nt)[:, None, None] * tk + jnp.arange(tk)[None, None, :]) // NSA_BLOCK
          ).astype(BF16)
    return pl.pallas_call(
        functools.partial(_nsa_selected_kernel, SEL_KT),
        grid=(b, NSA_GROUPS, t // tq),
        in_specs=[pl.BlockSpec((1, tq, 256), lambda bi, g, i: (bi, i, g)),
                  pl.BlockSpec((1, 1, tq, nblk), lambda bi, g, i: (bi, g, i, 0)),
                  pl.BlockSpec((1, nt, 1, 4 * HEAD_DIM, tk), lambda bi, g, i: (bi, 0, g, 0, 0)),
                  pl.BlockSpec(e3.shape, lambda bi, g, i: (0, 0, 0))],
        out_specs=pl.BlockSpec((1, tq, 256), lambda bi, g, i: (bi, i, g)),
        out_shape=jax.ShapeDtypeStruct((b, t, 1024), F32),
        scratch_shapes=[pltpu.VMEM((SEL_SUB, NSA_REP * sub, LANES), F32),
                        pltpu.VMEM((SEL_SUB, NSA_REP * sub, LANES), F32)],
        compiler_params=_cparams("parallel", "parallel", "arbitrary"),
        name="nsa_selected",
    )(qr, sel, ksel, e3)


def _nsa_out_kernel(oc_ref, os_ref, ow_ref, gate_ref, ex_ref, w_ref, r_ref, o_ref):
    gate = gate_ref[...]
    comb = (_split_dot(gate, ex_ref[0]) * oc_ref[...] + _split_dot(gate, ex_ref[1]) * os_ref[...]
            + _split_dot(gate, ex_ref[2]) * ow_ref[...])
    o_ref[...] = r_ref[...] + _dot(comb.astype(BF16), w_ref[...])


def nsa_out(oc, os_, ow, gates, w_o, res):
    n, d = res.shape
    tm = min(256, n)
    lane = jnp.arange(1024) // HEAD_DIM
    ex = jnp.stack([(jnp.arange(LANES)[:, None] == lane[None, :] * 3 + k) for k in range(3)]).astype(BF16)
    row = lambda w: pl.BlockSpec((tm, w), lambda i: (i, 0))
    return pl.pallas_call(
        _nsa_out_kernel,
        grid=(n // tm,),
        in_specs=[row(1024), row(1024), row(1024), row(LANES),
                  pl.BlockSpec(ex.shape, lambda i: (0, 0, 0)),
                  pl.BlockSpec(w_o.shape, lambda i: (0, 0)), row(d)],
        out_specs=row(d),
        out_shape=jax.ShapeDtypeStruct((n, d), F32),
        compiler_params=_cparams("parallel"),
        name="nsa_out",
    )(oc, os_, ow, gates, ex, w_o, res)


def _paged_attn_kernel(npp, kv_t, extra_rows, pt_ref, q_ref, *refs):
    k_refs, v_refs = refs[:npp], refs[npp:2 * npp]
    kn_ref, vn_ref, bp_ref, bn_ref, o_ref, m_ref, l_ref, acc_ref = refs[2 * npp:]
    p = pl.program_id(1)
    last = pl.num_programs(1) - 1

    @pl.when(p == 0)
    def _():
        _softmax_init(m_ref, acc_ref, l_ref)

    def step(k, v, bias):
        if kv_t:
            s = _dot(q_ref[0], k)
            s = s if bias is None else s + bias
            _softmax_update(s, lambda e: _dot_nt(e, v), m_ref, l_ref, acc_ref)
        else:
            s = _dot_nt(q_ref[0], k) + bias
            _softmax_update(s, lambda e: _dot(e, v), m_ref, l_ref, acc_ref)

    def load(r):
        x = r[...]
        return x if kv_t else x.reshape(-1, x.shape[-1])

    @pl.when(p < last)
    def _():
        axis = 1 if kv_t else 0
        k = jnp.concatenate([load(r) for r in k_refs], axis=axis).astype(BF16)
        v = jnp.concatenate([load(r) for r in v_refs], axis=axis).astype(BF16)
        if extra_rows:
            step(jnp.concatenate([k, bp_ref[0]], axis=0), v, None)
        else:
            step(k, v, bp_ref[0])

    @pl.when(p == last)
    def _():
        step(kn_ref[0], vn_ref[0], bn_ref[0])
        o_ref[0] = _softmax_result(l_ref, acc_ref)


def paged_attention(q, pages, table, npg, npp, k_spec, v_spec, kv_t, k_new, v_new, bias_past, bias_new, past_per_step,
                    extra_rows=False):
    nseq, rows, _ = q.shape
    lv = acc_w = v_new.shape[1] if kv_t else v_new.shape[2]
    seq = lambda a: pl.BlockSpec((1,) + a.shape[1:], lambda b, p, pt: (b, 0, 0))
    nstep = npg // npp
    if extra_rows:
        bp_spec = pl.BlockSpec((1,) + bias_past.shape[1:], lambda b, p, pt: (jnp.minimum(p, nstep - 1), 0, 0))
    else:
        wpast = bias_past.shape[-1] if not past_per_step else bias_past.shape[-1] // nstep
        bp_spec = pl.BlockSpec((1, rows, wpast), lambda b, p, pt: (
            b if bias_past.shape[0] > 1 else 0, 0, jnp.minimum(p, nstep - 1) if past_per_step else 0))
    bn_spec = pl.BlockSpec((1,) + bias_new.shape[1:], lambda b, p, pt: (b if bias_new.shape[0] > 1 else 0, 0, 0))
    return pl.pallas_call(
        functools.partial(_paged_attn_kernel, npp, kv_t, extra_rows),
        grid_spec=pltpu.PrefetchScalarGridSpec(
            num_scalar_prefetch=1,
            grid=(nseq, npg // npp + 1),
            in_specs=[seq(q)] + [k_spec(u) for u in range(npp)] + [v_spec(u) for u in range(npp)]
                     + [seq(k_new), seq(v_new), bp_spec, bn_spec],
            out_specs=pl.BlockSpec((1, rows, acc_w), lambda b, p, pt: (b, 0, 0)),
            scratch_shapes=[pltpu.VMEM((rows, LANES), F32), pltpu.VMEM((rows, LANES), F32),
                            pltpu.VMEM((rows, acc_w), F32)]),
        out_shape=jax.ShapeDtypeStruct((nseq, rows, lv), F32),
        compiler_params=_cparams("arbitrary", "arbitrary"),
        name="paged_attention",
    )(table, q, *([pages] * (2 * npp)), k_new, v_new, bias_past, bias_new)


def _diff_project_kernel(x_ref, ng_ref, w_ref, cos_ref, sin_ref, g_ref, so_ref, q_ref, kvf_ref, kvb_ref):
    hn = _rms_rows(x_ref[...], ng_ref[...]).astype(BF16)
    cos, sin = cos_ref[...], sin_ref[...]
    so = so_ref[...]
    qp = _dot(hn, w_ref[:, 0:DIFF_WIDTH])
    kp = _dot(hn, w_ref[:, DIFF_WIDTH:2 * DIFF_WIDTH])
    v = _dot(hn, w_ref[:, 2 * DIFF_WIDTH:3 * DIFF_WIDTH])
    for s in range(DIFF_WIDTH // LANES):
        sl = slice(s * LANES, (s + 1) * LANES)
        q = _rope_slab(_seg64_norm(qp[:, sl], g_ref[0:1, :], so), cos, sin)
        q_ref[:, sl] = (q * QK_SCALE_LOG2).astype(BF16)
        k = _rope_slab(_seg64_norm(kp[:, sl], g_ref[1:2, :], so), cos, sin)
        kvf_ref[:, sl] = k
        kvb_ref[:, sl] = k.astype(BF16)
    kvf_ref[:, DIFF_WIDTH:] = v
    kvb_ref[:, DIFF_WIDTH:] = v.astype(BF16)


def diff_project(x, norm_g, w_in, cos, sin, qk_g):
    n, d = x.shape
    tm = min(256, n)
    g2 = jnp.tile(qk_g, (1, 2))
    so = _seg_ones()
    ng = norm_g.reshape(1, d)
    row = lambda w: pl.BlockSpec((tm, w), lambda i: (i, 0))
    return pl.pallas_call(
        _diff_project_kernel,
        grid=(n // tm,),
        in_specs=[row(d), _resident(ng), _resident(w_in), row(LANES), row(LANES), _resident(g2), _resident(so)],
        out_specs=[row(DIFF_WIDTH), row(2 * DIFF_WIDTH), row(2 * DIFF_WIDTH)],
        out_shape=[jax.ShapeDtypeStruct((n, DIFF_WIDTH), BF16), jax.ShapeDtypeStruct((n, 2 * DIFF_WIDTH), F32),
                   jax.ShapeDtypeStruct((n, 2 * DIFF_WIDTH), BF16)],
        compiler_params=_cparams("parallel"),
        name="diff_project",
    )(x, ng, w_in, cos, sin, g2, so)


def _diff_lambda(lam_ref, lam_init):
    lf = lam_ref[...]
    a = jnp.sum(lf[0:1, :] * lf[1:2, :], axis=-1, keepdims=True)
    b = jnp.sum(lf[2:3, :] * lf[3:4, :], axis=-1, keepdims=True)
    return jnp.exp(a) - jnp.exp(b) + lam_init


DIFF_SUB = 8
DIFF_SUB_TQ = 128
DIFF_TK = 1024


def _diff_flash_kernel(lam_init, tk, q_ref, k_ref, v_ref, lam_ref, o_ref, m_ref, acc_ref):
    i = pl.program_id(2)
    nsub = m_ref.shape[0]
    tq = q_ref.shape[1] // nsub
    rows = 2 * tq

    def stack_components(q):
        comp = lax.broadcasted_iota(jnp.int32, q.shape, 1) >> 6
        return jnp.concatenate([jnp.where(comp == c, q, 0.0) for c in range(2)], axis=0).astype(BF16)

    qst = [stack_components(q_ref[0, a * tq:(a + 1) * tq, :].astype(F32)) for a in range(nsub)]
    _softmax_init(m_ref, acc_ref)
    ones = jnp.ones((tk, LANES), BF16)

    def tile(j, causal):
        lo = pl.multiple_of(j * tk, tk)
        k = k_ref[0, pl.ds(lo, tk), :]
        v_aug = jnp.concatenate([v_ref[0, pl.ds(lo, tk), :], ones], axis=1)
        for a in range(nsub):
            s = _dot_nt(qst[a], k)
            if causal:
                qpos = (i * nsub + a) * tq + (lax.broadcasted_iota(jnp.int32, (rows, tk), 0) & (tq - 1))
                kpos = j * tk + lax.broadcasted_iota(jnp.int32, (rows, tk), 1)
                s = jnp.where(kpos <= qpos, s, NEG_INF)
            _softmax_update_vsum(s, lambda p: _dot(p, v_aug), m_ref.at[a], acc_ref.at[a])

    nfull = (i * nsub * tq) // tk

    def body(j, c):
        tile(j, False)
        return c

    lax.fori_loop(0, nfull, body, 0)
    tile(nfull, True)
    lam = _diff_lambda(lam_ref, lam_init)
    for a in range(nsub):
        acc = acc_ref[a]
        o = acc[:, 0:LANES] / acc[:, LANES:2 * LANES]
        o_ref[0, a * tq:(a + 1) * tq, :] = o[0:tq, :] - lam * o[tq:rows, :]


def diff_flash_prompt(q, kvb, lam, lam_init):
    b, t, _ = q.shape
    sub = DIFF_SUB_TQ
    tq = DIFF_SUB * sub
    tk = DIFF_TK
    assert tk % tq == 0 and t % tk == 0
    return pl.pallas_call(
        functools.partial(_diff_flash_kernel, lam_init, tk),
        grid=(b, DIFF_HEADS, t // tq),
        in_specs=[pl.BlockSpec((1, tq, LANES), lambda bi, h, i: (bi, i, h)),
                  pl.BlockSpec((1, t, LANES), lambda bi, h, i: (bi, 0, h)),
                  pl.BlockSpec((1, t, LANES), lambda bi, h, i: (bi, 0, DIFF_HEADS + h)),
                  pl.BlockSpec(lam.shape, lambda bi, h, i: (0, 0))],
        out_specs=pl.BlockSpec((1, tq, LANES), lambda bi, h, i: (bi, i, h)),
        out_shape=jax.ShapeDtypeStruct((b, t, DIFF_WIDTH), F32),
        scratch_shapes=[pltpu.VMEM((DIFF_SUB, 2 * sub, LANES), F32), pltpu.VMEM((DIFF_SUB, 2 * sub, 2 * LANES), F32)],
        compiler_params=_cparams("parallel", "parallel", "arbitrary"),
        name="diff_flash",
    )(q, kvb, kvb, lam)


def _diff_out_kernel(lam_init, two, *refs):
    if two:
        o0_ref, o1_ref, lam_ref, g_ref, w_ref, r_ref, out_ref, h_ref = refs
        o = o0_ref[...] - _diff_lambda(lam_ref, lam_init) * o1_ref[...]
    else:
        o0_ref, g_ref, w_ref, r_ref, out_ref, h_ref = refs
        o = o0_ref[...]
    for s in range(DIFF_HEADS):
        sl = slice(s * LANES, (s + 1) * LANES)
        h_ref[:, sl] = (_rms_rows(o[:, sl], g_ref[...]) * (1.0 - lam_init)).astype(BF16)
    out_ref[...] = r_ref[...] + _dot(h_ref[...], w_ref[...])


def diff_out(o, sub_g, w_o, res, lam_init, o1=None, lam=None):
    n, d = res.shape
    tm = min(512, n)
    two = o1 is not None
    row = lambda w: pl.BlockSpec((tm, w), lambda i: (i, 0))
    const = lambda a: pl.BlockSpec(a.shape, lambda i: (0, 0))
    g = sub_g.reshape(1, LANES)
    ins = [o, o1, lam, g, w_o, res] if two else [o, g, w_o, res]
    specs = ([row(DIFF_WIDTH), row(DIFF_WIDTH), const(lam)] if two else [row(DIFF_WIDTH)]) + [const(g), const(w_o), row(d)]
    return pl.pallas_call(
        functools.partial(_diff_out_kernel, lam_init, two),
        grid=(n // tm,),
        in_specs=specs,
        out_specs=row(d),
        out_shape=jax.ShapeDtypeStruct((n, d), F32),
        scratch_shapes=[pltpu.VMEM((tm, DIFF_WIDTH), BF16)],
        compiler_params=_cparams("parallel"),
        name="diff_out",
    )(*ins)


def _gla_kernel(chunk, nchunk, t_valid, x_ref, ng_ref, w_ref, s0_ref, wa_ref, ba_ref, g_ref, tri_ref,
                y_ref, sfin_ref, st_ref, p_ref):
    step = pl.program_id(1)

    @pl.when(step == 0)
    def _():
        st_ref[...] = s0_ref[0]

    p_ref[0] = _dot(_rms_rows(x_ref[0], ng_ref[...]).astype(BF16), w_ref[...])

    nk = GLA_HEADS * GLA_DK
    nv = GLA_HEADS * GLA_DV
    tri = tri_ref[...]
    causal = lax.broadcasted_iota(jnp.int32, (chunk, chunk), 0) >= lax.broadcasted_iota(jnp.int32, (chunk, chunk), 1)
    for ci in range(nchunk):
        r0 = ci * chunk
        rows = slice(r0, r0 + chunk)
        a1 = p_ref[0, rows, GLA_MAIN:GLA_MAIN + LANES].astype(BF16)
        z = _dot(a1, wa_ref[...]) + ba_ref[...]
        log_a = (jnp.minimum(z, 0.0) - jnp.log(1.0 + jnp.exp(-jnp.abs(z)))) * (1.0 / GLA_TAU)
        if t_valid < chunk:
            log_a = jnp.where(lax.broadcasted_iota(jnp.int32, log_a.shape, 0) < t_valid, log_a, 0.0)
        cum = _split_dot_left(tri, log_a)
        for h in range(GLA_HEADS):
            ksl = slice(h * GLA_DK, (h + 1) * GLA_DK)
            vsl = slice(h * GLA_DV, (h + 1) * GLA_DV)
            q = p_ref[0, rows, h * GLA_DK:(h + 1) * GLA_DK] * (GLA_DK ** -0.5)
            k = p_ref[0, rows, nk + h * GLA_DK:nk + (h + 1) * GLA_DK]
            v = p_ref[0, rows, 2 * nk + h * GLA_DV:2 * nk + (h + 1) * GLA_DV]
            r = p_ref[0, rows, 2 * nk + nv + h * GLA_DV:2 * nk + nv + (h + 1) * GLA_DV]
            cm = cum[:, ksl]
            last = cm[chunk - 1:chunk, :]
            qe = (q * jnp.exp(cm)).astype(BF16)
            ke = (k * jnp.exp(-cm)).astype(BF16)
            vb = v.astype(BF16)
            att = jnp.where(causal, _dot_nt(qe, ke), 0.0)
            st = st_ref[h]
            o = _dot_nt(qe, st.astype(BF16)) + _dot(att.astype(BF16), vb)
            kd = (k * jnp.exp(last - cm)).astype(BF16)
            st_ref[h] = st * jnp.exp(last) + _dot(v.T.astype(BF16), kd)
            y_ref[0, rows, vsl] = _rms_rows(o, g_ref[...]) * _silu(r)

    @pl.when(step == pl.num_programs(1) - 1)
    def _():
        sfin_ref[0] = st_ref[...]


def gla_core(x, norm_g, w_in, s0_t, w_a2, b_a, out_g, chunk, nchunk, t_valid):
    b, t, d = x.shape
    rows = chunk * nchunk
    ng = norm_g.reshape(1, d)
    wa = jnp.zeros((LANES, GLA_HEADS * GLA_DK), F32).at[:GLA_RANK].set(w_a2).astype(BF16)
    ba = b_a.reshape(1, -1)
    g = out_g.reshape(1, GLA_DV)
    tri = (jnp.arange(chunk)[:, None] >= jnp.arange(chunk)[None, :]).astype(BF16)
    const = lambda a: pl.BlockSpec(a.shape, lambda bi, s: (0,) * a.ndim)
    st_spec = pl.BlockSpec((1, GLA_HEADS, GLA_DV, GLA_DK), lambda bi, s: (bi, 0, 0, 0))
    return pl.pallas_call(
        functools.partial(_gla_kernel, chunk, nchunk, t_valid),
        grid=(b, t // rows),
        in_specs=[pl.BlockSpec((1, rows, d), lambda bi, s: (bi, s, 0)), const(ng), const(w_in), st_spec,
                  const(wa), const(ba), const(g), const(tri)],
        out_specs=[pl.BlockSpec((1, rows, GLA_HEADS * GLA_DV), lambda bi, s: (bi, s, 0)), st_spec],
        out_shape=[jax.ShapeDtypeStruct((b, t, GLA_HEADS * GLA_DV), F32),
                   jax.ShapeDtypeStruct((b, GLA_HEADS, GLA_DV, GLA_DK), F32)],
        scratch_shapes=[pltpu.VMEM((GLA_HEADS, GLA_DV, GLA_DK), F32), pltpu.VMEM((1, rows, GLA_IN_PAD), F32)],
        compiler_params=_cparams("parallel", "arbitrary"),
        name="gla_core",
    )(x, ng, w_in, s0_t, wa, ba, g, tri)


def _pad_cols(w, width):
    return jnp.pad(w, ((0, 0), (0, width - w.shape[1]))).astype(BF16)


def _pad_axis(x, axis, size):
    pad = [(0, 0)] * x.ndim
    pad[axis] = (0, size - x.shape[axis])
    return jnp.pad(x, pad)


def _group_diag(q, t_seq):
    nseq = q.shape[0] // t_seq
    qg = q.reshape(nseq, t_seq, NSA_GROUPS, NSA_REP, HEAD_DIM).transpose(0, 2, 1, 3, 4)
    eye = jnp.eye(NSA_GROUPS, dtype=q.dtype)
    out = qg[:, :, :, :, None, :] * eye[None, :, None, None, :, None]
    return out.reshape(nseq, NSA_GROUPS * t_seq * NSA_REP, NSA_GROUPS * HEAD_DIM)


def _group_undiag(o, t_seq):
    nseq = o.shape[0]
    o6 = o.reshape(nseq, NSA_GROUPS, t_seq, NSA_REP, NSA_GROUPS, HEAD_DIM)
    od = jnp.stack([o6[:, g, :, :, g, :] for g in range(NSA_GROUPS)], axis=1)
    return od.transpose(0, 2, 1, 3, 4).reshape(nseq * t_seq, NSA_HEADS * HEAD_DIM)


def _seq_cols(x_t, nseq, t_seq):
    r = x_t.shape[0]
    return _pad_axis(x_t.reshape(r, nseq, t_seq).transpose(1, 0, 2), 2, PAGE_SIZE)


def _nsa_layer(s, xp, xs, seq_p, t_s, past_len, cache_t, state_win_t, page_table, norm_g, w_in, qk_g, pe, w_phi, w_o):
    bp = xp.shape[0] // seq_p
    bs = xs.shape[0] // t_s
    w_o_b = w_o.astype(BF16)
    npg = past_len // PAGE_SIZE

    qn, qr, gates, rows_t, win_t, ksel, kwin = nsa_project(xp, norm_g, w_in, qk_g, jnp.arange(seq_p), bp)
    own = _PageIndex(jnp.zeros((1, 1), jnp.int32), lambda b, p, pt: (b, 0, p))
    kc, vc = nsa_compress(rows_t, own, bp, seq_p // PAGE_SIZE, pe, w_phi, qk_g[3])
    qn3, qr3 = qn.reshape(bp, seq_p, 1024), qr.reshape(bp, seq_p, 1024)
    oc, sel = nsa_cmp_select(qn3, kc, vc, 0, seq_p // NSA_BLOCK)
    os_ = nsa_selected_prompt(qr3, sel, ksel)
    ow = nsa_window_prompt(qr3, kwin)
    xp_new = nsa_out(oc.reshape(-1, 1024), os_.reshape(-1, 1024), ow.reshape(-1, 1024), gates, w_o_b, xp)
    kv_p = rows_t.reshape(bp, 4, NSA_GROUPS, HEAD_DIM, seq_p).transpose(0, 4, 1, 2, 3)
    wlen = min(NSA_WINDOW, seq_p)
    win_p = win_t[:, :, seq_p - wlen:].reshape(bp, 2, NSA_GROUPS, HEAD_DIM, wlen).transpose(0, 4, 1, 2, 3)

    pos_s = past_len + jnp.arange(t_s)
    qn, qr, gates, rows_t, win_t, _, _ = nsa_project(xs, norm_g, w_in, qk_g, jnp.tile(pos_s, bs), 1)
    rows_t, win_t = rows_t[0], win_t[0]
    layer_pages = cache_t.shape[0] // (state_win_t.shape[0] // bs)
    table = page_table + s * layer_pages
    paged = _PageIndex(table, lambda b, p, pt: (pt[b, p], 0, 0))
    kc, vc = nsa_compress(cache_t, paged, bs, npg, pe, w_phi, qk_g[3])
    ns = -(-(past_len + t_s) // NSA_BLOCK)
    oc, sel = nsa_cmp_select_sample(qn, kc, vc, past_len, t_s, 256)
    nrow = NSA_GROUPS * t_s * NSA_REP
    sel_f = sel[:, :, :ns].astype(F32).reshape(NSA_GROUPS, bs, t_s, ns).transpose(1, 0, 2, 3)
    by_row = lambda a: jnp.broadcast_to(a[:, :, :, None, :], a.shape[:3] + (NSA_REP, a.shape[-1])).reshape(
        a.shape[0], nrow, a.shape[-1])
    new_ok = jnp.arange(PAGE_SIZE)[None, :] <= jnp.arange(t_s)[:, None]
    new_bias = by_row(jnp.where(new_ok[None, None], sel_f[..., past_len // NSA_BLOCK][..., None], NEG_INF))
    q_bd = _group_diag(qr, t_s)
    npp = min(NSA_PAGES_PER_STEP, npg)
    kpage = lambda blk: (lambda u: pl.BlockSpec((None, NSA_GW, PAGE_SIZE),
                                                lambda b, p, pt: (pt[b, jnp.minimum(p * npp + u, npg - 1)], blk, 0)))
    k_new = _seq_cols(rows_t[2 * NSA_GW:3 * NSA_GW], bs, t_s).astype(BF16)
    v_new = _seq_cols(rows_t[3 * NSA_GW:4 * NSA_GW], bs, t_s).astype(BF16)
    nblk = sel.shape[-1]
    sel_rows = by_row(sel.reshape(NSA_GROUPS, bs, t_s, nblk).transpose(1, 0, 2, 3))
    q_aug = jnp.concatenate([q_bd, sel_rows], axis=2)
    k_new_aug = jnp.concatenate([k_new, jnp.zeros((bs, nblk, PAGE_SIZE), BF16)], axis=1)
    wstep = npp * PAGE_SIZE
    key_blk = (jnp.arange(npg // npp)[:, None, None] * wstep + jnp.arange(wstep)[None, None, :]) // NSA_BLOCK
    e_steps = (jnp.arange(nblk)[None, :, None] == key_blk).astype(BF16)
    os_ = _group_undiag(paged_attention(q_aug, cache_t, table, npg, npp, kpage(2), kpage(3), True, k_new_aug, v_new,
                                        e_steps, new_bias, True, extra_rows=True), t_s)
    wbuf = state_win_t.shape[-1]
    wpg = wbuf // PAGE_SIZE
    wpage = lambda blk: (lambda u: pl.BlockSpec((None, NSA_GW, PAGE_SIZE),
                                                lambda b, p, pt: (s * bs + b, blk, jnp.minimum(p * wpg + u, wpg - 1))))
    kidx = jnp.arange(wbuf + PAGE_SIZE)
    wpos = jnp.where(kidx < wbuf, past_len - wbuf + kidx, past_len + kidx - wbuf)
    w_ok = ((wpos[None, :] <= pos_s[:, None]) & (wpos[None, :] > pos_s[:, None] - NSA_WINDOW) & (wpos[None, :] >= 0)
            & (kidx[None, :] < wbuf + t_s))
    wbias = by_row(jnp.broadcast_to(jnp.where(w_ok, 0.0, NEG_INF).astype(F32)[None, None],
                                    (1, NSA_GROUPS, t_s, wbuf + PAGE_SIZE)))
    kw_new = _seq_cols(win_t[0:NSA_GW], bs, t_s).astype(BF16)
    vw_new = _seq_cols(win_t[NSA_GW:2 * NSA_GW], bs, t_s).astype(BF16)
    ow = _group_undiag(paged_attention(q_bd, state_win_t, jnp.zeros((1, 1), jnp.int32), wpg, wpg, wpage(0), wpage(1), True,
                                       kw_new, vw_new, wbias[:, :, :wbuf], wbias[:, :, wbuf:], True), t_s)
    xs_new = nsa_out(oc, os_, ow, gates, w_o_b, xs)
    kv_s = rows_t.reshape(4, NSA_GROUPS, HEAD_DIM, bs, t_s).transpose(3, 4, 0, 1, 2)
    win_new = win_t.reshape(2 * NSA_GW, bs, t_s).transpose(1, 0, 2)
    win_all = jnp.concatenate([state_win_t[s * bs:(s + 1) * bs], win_new], axis=2)
    wlen = min(NSA_WINDOW, win_all.shape[2])
    win_s = win_all[:, :, win_all.shape[2] - wlen:].reshape(bs, 2, NSA_GROUPS, HEAD_DIM, wlen).transpose(0, 4, 1, 2, 3)
    return xp_new, xs_new, kv_p, kv_s, win_p, win_s


def _diff_layer(layer, xp, xs, seq_p, t_s, past_len, cache, page_table, norm_g, w_in, qk_g, lam, sub_g, w_o):
    bp = xp.shape[0] // seq_p
    bs = xs.shape[0] // t_s
    npg = past_len // PAGE_SIZE
    lam_init = 0.8 - 0.6 * math.exp(-0.3 * layer)
    w_in_b = w_in.astype(BF16)
    w_o_b = w_o.astype(BF16)
    cos_p, sin_p = _rope_tables(jnp.arange(seq_p))
    q, kvf, kvb = diff_project(xp, norm_g, w_in_b, jnp.tile(cos_p, (bp, 1)), jnp.tile(sin_p, (bp, 1)), qk_g)
    o = diff_flash_prompt(q.reshape(bp, seq_p, DIFF_WIDTH), kvb.reshape(bp, seq_p, 2 * DIFF_WIDTH), lam, lam_init)
    xp_new = diff_out(o.reshape(-1, DIFF_WIDTH), sub_g, w_o_b, xp, lam_init)
    kv_p = kvf.reshape(bp, seq_p, 2, DIFF_HEADS, 2 * HEAD_DIM)
    pos_s = past_len + jnp.arange(t_s)
    cos_s, sin_s = _rope_tables(pos_s)
    q, kvf, kvb = diff_project(xs, norm_g, w_in_b, jnp.tile(cos_s, (bs, 1)), jnp.tile(sin_s, (bs, 1)), qk_g)
    nrow = DIFF_HEADS * 2 * t_s
    q5 = q.reshape(bs, t_s, DIFF_HEADS, 2, HEAD_DIM).transpose(0, 2, 3, 1, 4)
    q_rows = (q5[:, :, :, :, None, :] * jnp.eye(2, dtype=q.dtype)[None, None, :, None, :, None]).reshape(
        bs, nrow, 2 * HEAD_DIM)
    kv5 = kvb.reshape(bs, t_s, 2, DIFF_HEADS, 2 * HEAD_DIM)
    k_new = _pad_axis(kv5[:, :, 0], 1, PAGE_SIZE).reshape(bs, PAGE_SIZE * DIFF_HEADS, 2 * HEAD_DIM)
    v_new = _pad_axis(kv5[:, :, 1], 1, PAGE_SIZE).reshape(bs, PAGE_SIZE * DIFF_HEADS, 2 * HEAD_DIM)
    npp = min(DIFF_PAGES_PER_STEP, npg)
    row_h = jnp.arange(nrow) // (2 * t_s)
    row_t = jnp.arange(nrow) % t_s
    slot_h = jnp.arange(PAGE_SIZE * DIFF_HEADS) % DIFF_HEADS
    slot_tok = jnp.arange(PAGE_SIZE * DIFF_HEADS) // DIFF_HEADS
    same_head = row_h[:, None] == slot_h[None, :]
    bias_page = jnp.where(same_head, 0.0, NEG_INF).astype(F32)
    bias_past = jnp.tile(bias_page, (1, npp))[None]
    bias_new = jnp.where(same_head & (slot_tok[None, :] <= row_t[:, None]), 0.0, NEG_INF).astype(F32)[None]
    page = lambda slot: (lambda u: pl.BlockSpec(
        (None, PAGE_SIZE, None, DIFF_HEADS, 2 * HEAD_DIM),
        lambda b, p, pt: (pt[b, jnp.minimum(p * npp + u, npg - 1)], 0, slot, 0, 0)))
    o = paged_attention(q_rows, cache, page_table, npg, npp, page(0), page(1), False, k_new, v_new,
                        bias_past, bias_new, False)
    od = o.reshape(bs, DIFF_HEADS, 2, t_s, 2 * HEAD_DIM).transpose(2, 0, 3, 1, 4).reshape(2, bs * t_s, DIFF_WIDTH)
    xs_new = diff_out(od[0], sub_g, w_o_b, xs, lam_init, o1=od[1], lam=lam)
    kv_s = kvf.reshape(bs, t_s, 2, DIFF_HEADS, 2 * HEAD_DIM)
    return xp_new, xs_new, kv_p, kv_s


def _gla_layer(xp, xs, seq_p, t_s, state, norm_g, w_in, w_a2, b_a, out_g, w_o):
    bp = xp.shape[0] // seq_p
    bs = xs.shape[0] // t_s
    w_in_b = _pad_cols(w_in, GLA_IN_PAD)
    w_o_b = w_o.astype(BF16)
    d = xp.shape[1]
    chunk = min(GLA_CHUNK, seq_p)
    s0 = jnp.zeros((bp, GLA_HEADS, GLA_DV, GLA_DK), F32)
    y, st = gla_core(xp.reshape(bp, seq_p, d), norm_g, w_in_b, s0, w_a2, b_a, out_g, chunk,
                     4 if seq_p % (4 * chunk) == 0 else 1, chunk)
    xp_new = matmul_residual(y.reshape(-1, GLA_HEADS * GLA_DV), w_o_b, xp)
    st_p = st.transpose(0, 1, 3, 2)
    chunk_s = GLA_CHUNK
    xs_pad = _pad_axis(xs.reshape(bs, t_s, d), 1, chunk_s)
    y, st = gla_core(xs_pad, norm_g, w_in_b, state.transpose(0, 1, 3, 2), w_a2, b_a, out_g, chunk_s, 1, t_s)
    xs_new = matmul_residual(y[:, :t_s].reshape(-1, GLA_HEADS * GLA_DV), w_o_b, xs)
    st_s = st.transpose(0, 1, 3, 2)
    return xp_new, xs_new, st_p, st_s


def kernel(x_prompt, x_sample, cache_nsa_kv, state_nsa_win, cache_diff_kv, state_gla, state_ffn, page_table, norm_g, ffn_w_up, ffn_conv_w, ffn_conv_b, ffn_w_down, nsa_w_in, nsa_qk_g, nsa_pe, nsa_w_phi, nsa_w_o, diff_w_in, diff_qk_g, diff_lam, diff_sub_g, diff_w_o, gla_w_in, gla_w_a2, gla_b_a, gla_out_g, gla_w_o):
    bp, seq_p, d = x_prompt.shape
    bs, t_s, _ = x_sample.shape
    past_len = page_table.shape[1] * PAGE_SIZE
    xp = x_prompt.reshape(bp * seq_p, d)
    xs = x_sample.reshape(bs * t_s, d)
    cache_t = cache_nsa_kv.transpose(0, 1, 3, 4, 5, 2).reshape(-1, 4 * NSA_GW, PAGE_SIZE)
    win_t = state_nsa_win.transpose(0, 1, 3, 4, 5, 2).reshape(-1, 2 * NSA_GW, state_nsa_win.shape[2])
    nsa_kv_p, nsa_kv_s, nsa_win_p, nsa_win_s = [], [], [], []
    diff_kv_p, diff_kv_s, gla_p, gla_s, ffn_p, ffn_s = [], [], [], [], [], []
    for i in range(DEPTH):
        kind, s = i % N_MIXERS, i // N_MIXERS
        if kind == 0:
            xp, xs, kvp, kvs, wp, ws = _nsa_layer(
                s, xp, xs, seq_p, t_s, past_len, cache_t, win_t, page_table,
                norm_g[i, 0], nsa_w_in[s], nsa_qk_g[s], nsa_pe[s], nsa_w_phi[s], nsa_w_o[s])
            nsa_kv_p.append(kvp); nsa_kv_s.append(kvs); nsa_win_p.append(wp); nsa_win_s.append(ws)
        elif kind == 1:
            xp, xs, kvp, kvs = _diff_layer(
                i, xp, xs, seq_p, t_s, past_len, cache_diff_kv[s], page_table, norm_g[i, 0], diff_w_in[s],
                diff_qk_g[s], diff_lam[s], diff_sub_g[s], diff_w_o[s])
            diff_kv_p.append(kvp); diff_kv_s.append(kvs)
        else:
            xp, xs, stp, sts = _gla_layer(xp, xs, seq_p, t_s, state_gla[s], norm_g[i, 0], gla_w_in[s], gla_w_a2[s],
                                          gla_b_a[s], gla_out_g[s], gla_w_o[s])
            gla_p.append(stp); gla_s.append(sts)
        w_up_b = ffn_w_up[i].astype(BF16)
        w_dn_b = ffn_w_down[i].astype(BF16)
        xp, tail_p = ffn_prompt(xp, norm_g[i, 1], w_up_b, ffn_conv_w[i], ffn_conv_b[i], w_dn_b, seq_p)
        xs, tail_s = ffn_sample(xs, norm_g[i, 1], w_up_b, ffn_conv_w[i], ffn_conv_b[i], w_dn_b, state_ffn[i], t_s)
        ffn_p.append(tail_p); ffn_s.append(tail_s)
    return (xp.reshape(bp, seq_p, d), xs.reshape(bs, t_s, d),
            jnp.stack(nsa_kv_p), jnp.stack(nsa_kv_s), jnp.stack(nsa_win_p), jnp.stack(nsa_win_s),
            jnp.stack(diff_kv_p), jnp.stack(diff_kv_s), jnp.stack(gla_p), jnp.stack(gla_s),
            jnp.stack(ffn_p), jnp.stack(ffn_s))
```

```python
import functools
import math

import jax
import jax.numpy as jnp
from jax import lax
from jax.experimental import pallas as pl
from jax.experimental.pallas import tpu as pltpu

F32 = jnp.float32
BF16 = jnp.bfloat16

HEAD_DIM = 64
ROPE_THETA = 10000.0
NORM_EPS = 1e-6
NEG_INF = -1e30
DEPTH = 4
N_MIXERS = 3
PAGE_SIZE = 128
NSA_HEADS = 16
NSA_GROUPS = 4
NSA_REP = 4
NSA_BLOCK = 64
NSA_TOPN = 16
NSA_WINDOW = 512
NSA_FORCED = 1e9
NSA_NQ = NSA_HEADS * HEAD_DIM
NSA_NKV = 6 * NSA_GROUPS * HEAD_DIM
NSA_NGATE = 3 * NSA_HEADS
NSA_GW = NSA_GROUPS * HEAD_DIM
DIFF_HEADS = 8
DIFF_WIDTH = 1024
GLA_HEADS = 4
GLA_DK = 128
GLA_DV = 256
GLA_RANK = 16
GLA_TAU = 16.0
GLA_CHUNK = 64
GLA_MAIN = 2 * GLA_HEADS * GLA_DK + 2 * GLA_HEADS * GLA_DV
GLA_IN_PAD = GLA_MAIN + 128
D_FF = 2816
CONV_W = 3
LANES = 128
VMEM_LIMIT = 56 * 1024 * 1024
SEL_TK = 512
NSA_PAGES_PER_STEP = 16
DIFF_PAGES_PER_STEP = 8
CMP_PITCH = HEAD_DIM + 8


def _cparams(*sem):
    return pltpu.CompilerParams(dimension_semantics=sem, vmem_limit_bytes=VMEM_LIMIT)


def _dot(a, b):
    return jnp.dot(a, b, preferred_element_type=F32)


def _dot_nt(a, b):
    return lax.dot_general(a, b, (((1,), (1,)), ((), ())), preferred_element_type=F32)


def _split_dot(x, m):
    hi = x.astype(BF16)
    lo = (x - hi.astype(F32)).astype(BF16)
    return _dot(hi, m) + _dot(lo, m)


def _split_dot_left(m, x):
    hi = x.astype(BF16)
    lo = (x - hi.astype(F32)).astype(BF16)
    return _dot(m, hi) + _dot(m, lo)


def _rms_rows(x, g):
    ms = jnp.mean(x * x, axis=-1, keepdims=True)
    return x * lax.rsqrt(ms + NORM_EPS) * g


def _seg64_norm(x, g, seg_ones):
    ms = _split_dot(x * x, seg_ones) * (1.0 / HEAD_DIM)
    return x * lax.rsqrt(ms + NORM_EPS) * g


def _rope_slab(x, cos, sin_signed):
    lane = lax.broadcasted_iota(jnp.int32, x.shape, 1)
    first = (lane & 63) < 32
    partner = jnp.where(first, pltpu.roll(x, 96, 1), pltpu.roll(x, 32, 1))
    return x * cos + partner * sin_signed


def _rope_angles(pos):
    half = HEAD_DIM // 2
    inv = ROPE_THETA ** (-jnp.arange(half, dtype=F32) / half)
    ang = pos.astype(F32)[:, None] * inv
    return jnp.cos(ang), jnp.sin(ang)


def _rope_tables(pos):
    cos, sin = _rope_angles(pos)
    return jnp.tile(cos, (1, 4)), jnp.tile(jnp.concatenate([-sin, sin], axis=1), (1, 2))


def _seg_ones():
    i = jnp.arange(LANES)
    return (i[:, None] // HEAD_DIM == i[None, :] // HEAD_DIM).astype(BF16)


def _lanes(x, width):
    return x if width == LANES else jnp.tile(x, (1, width // LANES))


def _softmax_update(s, v_dot, m_ref, l_ref, acc_ref):
    m = m_ref[...]
    m_new = jnp.maximum(m, jnp.max(s, axis=-1, keepdims=True))
    alpha = jnp.exp2(m - m_new)
    p = jnp.exp2(s - _lanes(m_new, s.shape[1]))
    m_ref[...] = m_new
    l_ref[...] = alpha * l_ref[...] + jnp.sum(p, axis=-1, keepdims=True)
    acc_ref[...] = _lanes(alpha, acc_ref.shape[-1]) * acc_ref[...] + v_dot(p.astype(BF16))


def _softmax_result(l_ref, acc_ref):
    return acc_ref[...] / _lanes(l_ref[...], acc_ref.shape[-1])


def _softmax_update_vsum(s, v_dot, m_ref, acc_ref):
    m = m_ref[...]
    m_new = jnp.maximum(m, jnp.max(s, axis=-1, keepdims=True))
    alpha = jnp.exp2(m - m_new)
    p = jnp.exp2(s - _lanes(m_new, s.shape[1]))
    m_ref[...] = m_new
    acc_ref[...] = _lanes(alpha, acc_ref.shape[-1]) * acc_ref[...] + v_dot(p.astype(BF16))


def _softmax_init(m_ref, acc_ref, l_ref=None):
    m_ref[...] = jnp.full_like(m_ref, NEG_INF)
    acc_ref[...] = jnp.zeros_like(acc_ref)
    if l_ref is not None:
        l_ref[...] = jnp.zeros_like(l_ref)


DIAG_WIDTH_STEP = 256


def _round_up(x, m):
    return -(-x // m) * m


QK_SCALE = HEAD_DIM ** -0.5
QK_SCALE_LOG2 = QK_SCALE * math.log2(math.e)


def _matmul_res_kernel(a_ref, w_ref, r_ref, o_ref):
    o_ref[...] = r_ref[...] + _dot(a_ref[...].astype(BF16), w_ref[...])


def matmul_residual(a, w, res):
    n, k = a.shape
    d = w.shape[1]
    tm = min(512, n)
    return pl.pallas_call(
        _matmul_res_kernel,
        grid=(n // tm,),
        in_specs=[pl.BlockSpec((tm, k), lambda i: (i, 0)),
                  pl.BlockSpec((k, d), lambda i: (0, 0)),
                  pl.BlockSpec((tm, d), lambda i: (i, 0))],
        out_specs=pl.BlockSpec((tm, d), lambda i: (i, 0)),
        out_shape=jax.ShapeDtypeStruct((n, d), F32),
        compiler_params=_cparams("parallel"),
        name="matmul_residual",
    )(a, w, res)


FFN_CHUNK = 256


def _silu(x):
    return x / (1.0 + jnp.exp(-x))


def _causal_conv(u, prev1, prev2, use1, use2, cw_ref, cb_ref):
    u1 = jnp.where(use1, prev1, pltpu.roll(u, 1, 0))
    u2 = jnp.where(use2, prev2, pltpu.roll(u, 2, 0))
    return cb_ref[...] + cw_ref[0:1, :] * u2 + cw_ref[1:2, :] * u1 + cw_ref[2:3, :] * u


def _ffn_prompt_kernel(tiles_per_seq, x_ref, g_ref, wg_ref, wu_ref, cwg_ref, cwu_ref, cbg_ref, cbu_ref,
                       wd_ref, o_ref, tg_ref, tu_ref, cg_ref, cu_ref, act_ref):
    i = pl.program_id(0)
    tm = x_ref.shape[0]
    nj, _, c = wg_ref.shape
    x = x_ref[...]
    h = _rms_rows(x, g_ref[...]).astype(BF16)
    row = lax.broadcasted_iota(jnp.int32, (tm, c), 0)

    @pl.when((i % tiles_per_seq) == 0)
    def _():
        cg_ref[...] = jnp.zeros_like(cg_ref)
        cu_ref[...] = jnp.zeros_like(cu_ref)

    def conv(u, prev, cw_ref, cb_ref):
        prev2 = jnp.where(row == 0, prev[0:1, :], prev[1:2, :])
        return _causal_conv(u, prev[1:2, :], prev2, row == 0, row < 2, cw_ref, cb_ref)

    for j in range(nj):
        ug = _dot(h, wg_ref[j])
        uu = _dot(h, wu_ref[j])
        act = _silu(conv(ug, cg_ref[j], cwg_ref.at[j], cbg_ref.at[j])) * conv(uu, cu_ref[j], cwu_ref.at[j], cbu_ref.at[j])
        act_ref[:, j * c:(j + 1) * c] = act.astype(BF16)
        cg_ref[j] = ug[tm - 2:tm, :]
        cu_ref[j] = uu[tm - 2:tm, :]
        tg_ref[0, :, j * c:(j + 1) * c] = ug[tm - 2:tm, :]
        tu_ref[0, :, j * c:(j + 1) * c] = uu[tm - 2:tm, :]
    o_ref[...] = x + _dot(act_ref[...], wd_ref[...])


def _resident(a):
    return pl.BlockSpec(a.shape, lambda i: (0,) * a.ndim, pipeline_mode=pl.Buffered(1))


def ffn_prompt(x, g, w_up, conv_w, conv_b, w_down, seq_len):
    n, d = x.shape
    c = FFN_CHUNK
    nj = D_FF // c
    tm = min(512, seq_len)
    tps = seq_len // tm
    chunks = lambda a: a.reshape(a.shape[0], 2, nj, c).transpose(1, 2, 0, 3)
    wg, wu = chunks(w_up)
    cwg, cwu = chunks(conv_w)
    cbg, cbu = chunks(conv_b.reshape(1, 2 * D_FF))
    g2 = g.reshape(1, d)
    consts = [g2, wg, wu, cwg, cwu, cbg, cbu, w_down]
    out, tg, tu = pl.pallas_call(
        functools.partial(_ffn_prompt_kernel, tps),
        grid=(n // tm,),
        in_specs=[pl.BlockSpec((tm, d), lambda i: (i, 0))] + [_resident(a) for a in consts],
        out_specs=[pl.BlockSpec((tm, d), lambda i: (i, 0)),
                   pl.BlockSpec((1, 2, D_FF), lambda i: (i, 0, 0)),
                   pl.BlockSpec((1, 2, D_FF), lambda i: (i, 0, 0))],
        out_shape=[jax.ShapeDtypeStruct((n, d), F32),
                   jax.ShapeDtypeStruct((n // tm, 2, D_FF), F32),
                   jax.ShapeDtypeStruct((n // tm, 2, D_FF), F32)],
        scratch_shapes=[pltpu.VMEM((nj, 2, c), F32),
                        pltpu.VMEM((nj, 2, c), F32),
                        pltpu.VMEM((tm, D_FF), BF16)],
        compiler_params=_cparams("arbitrary"),
        name="ffn_prompt",
    )(x, *consts)
    return out, jnp.concatenate([tg, tu], axis=-1)[tps - 1::tps]


def _ffn_sample_kernel(t_seq, x_ref, g_ref, wg_ref, wu_ref, cwg_ref, cwu_ref, cbg_ref, cbu_ref, wd_ref,
                       p1g_ref, p2g_ref, p1u_ref, p2u_ref, o_ref, ug_ref, uu_ref, h_ref):
    j = pl.program_id(0)

    @pl.when(j == 0)
    def _():
        h_ref[...] = _rms_rows(x_ref[...], g_ref[...]).astype(BF16)

    h = h_ref[...]
    ug = _dot(h, wg_ref[...])
    uu = _dot(h, wu_ref[...])
    t = lax.broadcasted_iota(jnp.int32, ug.shape, 0) & (t_seq - 1)
    cg = _causal_conv(ug, p1g_ref[...], p2g_ref[...], t == 0, t < 2, cwg_ref, cbg_ref)
    cu = _causal_conv(uu, p1u_ref[...], p2u_ref[...], t == 0, t < 2, cwu_ref, cbu_ref)
    part = _dot((_silu(cg) * cu).astype(BF16), wd_ref[...])
    ug_ref[...] = ug
    uu_ref[...] = uu

    @pl.when(j == 0)
    def _():
        o_ref[...] = x_ref[...] + part

    @pl.when(j > 0)
    def _():
        o_ref[...] += part


def ffn_sample(x, g, w_up, conv_w, conv_b, w_down, buf, t_seq):
    n, d = x.shape
    nseq = n // t_seq
    c = FFN_CHUNK
    nj = D_FF // c
    cb = conv_b.reshape(1, 2 * D_FF)
    reps = t_seq // 2
    prev1 = jnp.broadcast_to(buf[:, 1:2], (nseq, t_seq, 2 * D_FF)).reshape(n, 2 * D_FF)
    prev2 = jnp.concatenate([buf] * reps, axis=1).reshape(n, 2 * D_FF)
    full = lambda blk, off=0: pl.BlockSpec(blk, lambda j: (0, j + off))
    out, ug, uu = pl.pallas_call(
        functools.partial(_ffn_sample_kernel, t_seq),
        grid=(nj,),
        in_specs=[pl.BlockSpec((n, d), lambda j: (0, 0)),
                  pl.BlockSpec((1, d), lambda j: (0, 0)),
                  full((d, c)), full((d, c), nj),
                  full((CONV_W, c)), full((CONV_W, c), nj),
                  full((1, c)), full((1, c), nj),
                  pl.BlockSpec((c, d), lambda j: (j, 0)),
                  full((n, c)), full((n, c)), full((n, c), nj), full((n, c), nj)],
        out_specs=[pl.BlockSpec((n, d), lambda j: (0, 0)),
                   full((n, c)), full((n, c))],
        out_shape=[jax.ShapeDtypeStruct((n, d), F32),
                   jax.ShapeDtypeStruct((n, D_FF), F32),
                   jax.ShapeDtypeStruct((n, D_FF), F32)],
        scratch_shapes=[pltpu.VMEM((n, d), BF16)],
        compiler_params=_cparams("arbitrary"),
        name="ffn_sample",
    )(x, g.reshape(1, d), w_up, w_up, conv_w, conv_w, cb, cb, w_down, prev1, prev2, prev1, prev2)
    u = jnp.concatenate([ug, uu], axis=-1).reshape(nseq, t_seq, 2 * D_FF)
    return out, u[:, t_seq - (CONV_W - 1):]


def _head_norm_t(x, g):
    ms = jnp.mean(x * x, axis=0, keepdims=True)
    return x * lax.rsqrt(ms + NORM_EPS) * g


def _rope_t(x, cos, sin):
    half = HEAD_DIM // 2
    x1, x2 = x[0:half, :], x[half:HEAD_DIM, :]
    return jnp.concatenate([x1 * cos - x2 * sin, x1 * sin + x2 * cos], axis=0)


def _nsa_project_kernel(x_ref, g_ref, wq_ref, wkv_ref, wg_ref, cos_ref, sin_ref, cost_ref, sint_ref, gq_ref, gk_ref,
                        so_ref, qn_ref, qr_ref, gate_ref, rows_ref, win_ref, ksel_ref, kwin_ref):
    hn = _rms_rows(x_ref[...], g_ref[...]).astype(BF16)
    cos, sin = cos_ref[...], sin_ref[...]
    so = so_ref[...]
    q_all = _dot(hn, wq_ref[...])
    for s in range(NSA_NQ // LANES):
        sl = slice(s * LANES, (s + 1) * LANES)
        q = _seg64_norm(q_all[:, sl], gq_ref[...], so)
        qn_ref[:, sl] = (q * QK_SCALE).astype(BF16)
        qr_ref[:, sl] = (_rope_slab(q, cos, sin) * QK_SCALE_LOG2).astype(BF16)
    gate_ref[...] = 1.0 / (1.0 + jnp.exp(-_dot(hn, wg_ref[...])))
    kvt = _dot_nt(wkv_ref[...], hn)
    gw = NSA_GW
    cost, sint = cost_ref[...], sint_ref[...]
    rows_ref[0, 0:2 * gw, :] = kvt[0:2 * gw, :]
    rows_ref[0, 3 * gw:4 * gw, :] = kvt[3 * gw:4 * gw, :]
    win_ref[0, gw:2 * gw, :] = kvt[5 * gw:6 * gw, :]
    for g in range(NSA_GROUPS):
        hs = slice(g * HEAD_DIM, (g + 1) * HEAD_DIM)
        rows_ref[0, 2 * gw + g * HEAD_DIM:2 * gw + (g + 1) * HEAD_DIM, :] = _rope_t(
            _head_norm_t(kvt[2 * gw + g * HEAD_DIM:2 * gw + (g + 1) * HEAD_DIM, :], gk_ref[0]), cost, sint)
        win_ref[0, hs, :] = _rope_t(
            _head_norm_t(kvt[4 * gw + g * HEAD_DIM:4 * gw + (g + 1) * HEAD_DIM, :], gk_ref[1]), cost, sint)
    tm = kvt.shape[1]
    for g in range(NSA_GROUPS):
        hs = slice(g * HEAD_DIM, (g + 1) * HEAD_DIM)
        ksel_ref[0, 0, g, 0:HEAD_DIM, :] = rows_ref[0, 2 * gw + g * HEAD_DIM:2 * gw + (g + 1) * HEAD_DIM, :].astype(BF16)
        ksel_ref[0, 0, g, HEAD_DIM:2 * HEAD_DIM, :] = jnp.zeros((HEAD_DIM, tm), BF16)
        ksel_ref[0, 0, g, 2 * HEAD_DIM:3 * HEAD_DIM, :] = kvt[3 * gw + g * HEAD_DIM:3 * gw + (g + 1) * HEAD_DIM, :].astype(BF16)
        ksel_ref[0, 0, g, 3 * HEAD_DIM:4 * HEAD_DIM, :] = jnp.ones((HEAD_DIM, tm), BF16)
    wb = win_ref[0].astype(BF16)
    for u in range(kwin_ref.shape[1]):
        ts = slice(u * LANES, (u + 1) * LANES)
        for g in range(NSA_GROUPS):
            kwin_ref[0, u, g, 0:HEAD_DIM, :] = wb[g * HEAD_DIM:(g + 1) * HEAD_DIM, ts]
            kwin_ref[0, u, g, HEAD_DIM:2 * HEAD_DIM, :] = jnp.zeros((HEAD_DIM, LANES), BF16)
            kwin_ref[0, u, g, 2 * HEAD_DIM:3 * HEAD_DIM, :] = wb[gw + g * HEAD_DIM:gw + (g + 1) * HEAD_DIM, ts]
            kwin_ref[0, u, g, 3 * HEAD_DIM:4 * HEAD_DIM, :] = jnp.ones((HEAD_DIM, LANES), BF16)


def nsa_project(x, norm_g, w_in, qk_g, pos, nseq):
    n, d = x.shape
    t = n // nseq
    tm = min(SEL_TK, t)
    nt = t // tm
    wq = w_in[:, :NSA_NQ].astype(BF16)
    wkv_t = w_in[:, NSA_NQ:NSA_NQ + NSA_NKV].T.astype(BF16)
    wg = jnp.pad(w_in[:, NSA_NQ + NSA_NKV:], ((0, 0), (0, LANES - NSA_NGATE))).astype(BF16)
    cos, sin = _rope_angles(pos)
    cos_q, sin_q = jnp.tile(cos, (1, 4)), jnp.tile(jnp.concatenate([-sin, sin], axis=1), (1, 2))
    gq = jnp.tile(qk_g[0:1], (1, 2))
    gk = qk_g[1:3].reshape(2, HEAD_DIM, 1)
    so = _seg_ones()
    const = lambda a: pl.BlockSpec(a.shape, lambda b, i: (0,) * a.ndim)
    row = lambda w: pl.BlockSpec((tm, w), lambda b, i: (b * nt + i, 0))
    return pl.pallas_call(
        _nsa_project_kernel,
        grid=(nseq, nt),
        in_specs=[row(d), const(norm_g.reshape(1, d)), const(wq), const(wkv_t), const(wg),
                  pl.BlockSpec((tm, LANES), lambda b, i: (i, 0)), pl.BlockSpec((tm, LANES), lambda b, i: (i, 0)),
                  pl.BlockSpec((HEAD_DIM // 2, tm), lambda b, i: (0, i)),
                  pl.BlockSpec((HEAD_DIM // 2, tm), lambda b, i: (0, i)),
                  const(gq), const(gk), const(so)],
        out_specs=[row(NSA_NQ), row(NSA_NQ), row(LANES),
                   pl.BlockSpec((1, 4 * NSA_GW, tm), lambda b, i: (b, 0, i)),
                   pl.BlockSpec((1, 2 * NSA_GW, tm), lambda b, i: (b, 0, i)),
                   pl.BlockSpec((1, 1, NSA_GROUPS, 4 * HEAD_DIM, tm), lambda b, i: (b, i, 0, 0, 0)),
                   pl.BlockSpec((1, tm // LANES, NSA_GROUPS, 4 * HEAD_DIM, LANES), lambda b, i: (b, i, 0, 0, 0))],
        out_shape=[jax.ShapeDtypeStruct((n, NSA_NQ), BF16), jax.ShapeDtypeStruct((n, NSA_NQ), BF16),
                   jax.ShapeDtypeStruct((n, LANES), F32),
                   jax.ShapeDtypeStruct((nseq, 4 * NSA_GW, t), F32),
                   jax.ShapeDtypeStruct((nseq, 2 * NSA_GW, t), F32),
                   jax.ShapeDtypeStruct((nseq, nt, NSA_GROUPS, 4 * HEAD_DIM, tm), BF16),
                   jax.ShapeDtypeStruct((nseq, t // LANES, NSA_GROUPS, 4 * HEAD_DIM, LANES), BF16)],
        compiler_params=_cparams("parallel", "parallel"),
        name="nsa_project",
    )(x, norm_g.reshape(1, d), wq, wkv_t, wg, cos_q, sin_q, cos.T, sin.T, gq, gk, so)


def _nsa_compress_kernel(npp, pt_ref, *refs):
    page_refs = refs[:npp]
    pe_ref, w_ref, g_ref, so_ref, kc_ref, vc_ref, seqk_ref, seqv_ref, acc_ref = refs[npp:]
    p = pl.program_id(1)
    gw = NSA_GW
    pitch = CMP_PITCH
    for u in range(npp):
        for g in range(NSA_GROUPS):
            lo = pl.multiple_of(((p * npp + u) * NSA_GROUPS + g) * pitch, 8)
            seqk_ref[pl.ds(lo, HEAD_DIM), :] = page_refs[u][g * HEAD_DIM:(g + 1) * HEAD_DIM, :]
            seqv_ref[pl.ds(lo, HEAD_DIM), :] = page_refs[u][gw + g * HEAD_DIM:gw + (g + 1) * HEAD_DIM, :]

    @pl.when(p == pl.num_programs(1) - 1)
    def _():
        nrow = seqk_ref.shape[0] // pitch
        acc_ref[...] = jnp.zeros_like(acc_ref)
        for dd in range(HEAD_DIM):
            a = jnp.concatenate([seqk_ref[pl.ds(dd, nrow, stride=pitch), :],
                                 seqv_ref[pl.ds(dd, nrow, stride=pitch), :]], axis=1) + pe_ref[dd:dd + 1, :]
            acc_ref[...] += _dot(a.astype(BF16), w_ref[dd])
        kc_ref[0] = _seg64_norm(acc_ref[:, 0:LANES], g_ref[...], so_ref[...])
        vc_ref[0] = acc_ref[:, LANES:2 * LANES]


def nsa_compress(pages_t, page_index, nseq, npg, pe, w_phi, g_c):
    npp = min(NSA_PAGES_PER_STEP, npg)
    nrow = npg * NSA_GROUPS
    pe_t = jnp.concatenate([jnp.tile(pe[0].T, (1, 2)), jnp.tile(pe[1].T, (1, 2))], axis=1)
    eye4 = jnp.eye(4, dtype=F32)
    w4 = jnp.stack([w_phi[0], w_phi[0], w_phi[1], w_phi[1]])
    w = jnp.einsum('ab,alde->dalbe', eye4, w4).reshape(HEAD_DIM, 2 * LANES, 2 * LANES).astype(BF16)
    g2 = jnp.tile(g_c.reshape(1, HEAD_DIM), (1, 2))
    so = _seg_ones()
    const = lambda a: pl.BlockSpec(a.shape, lambda b, p, pt: (0,) * a.ndim, pipeline_mode=pl.Buffered(1))
    page = lambda u: pl.BlockSpec((None, 2 * NSA_GW, PAGE_SIZE), lambda b, p, pt: page_index(b, p * npp + u, pt))
    out = pl.BlockSpec((1, nrow, LANES), lambda b, p, pt: (b, 0, 0))
    table = page_index.table
    kc, vc = pl.pallas_call(
        functools.partial(_nsa_compress_kernel, npp),
        grid_spec=pltpu.PrefetchScalarGridSpec(
            num_scalar_prefetch=1,
            grid=(nseq, npg // npp),
            in_specs=[page(u) for u in range(npp)] + [const(pe_t), const(w), const(g2), const(so)],
            out_specs=[out, out],
            scratch_shapes=[pltpu.VMEM((nrow * CMP_PITCH, PAGE_SIZE), F32), pltpu.VMEM((nrow * CMP_PITCH, PAGE_SIZE), F32),
                            pltpu.VMEM((nrow, 2 * LANES), F32)]),
        out_shape=[jax.ShapeDtypeStruct((nseq, nrow, LANES), F32)] * 2,
        compiler_params=_cparams("arbitrary", "arbitrary"),
        name="nsa_compress",
    )(table, *([pages_t] * npp), pe_t, w, g2, so)
    fix = lambda a: a.reshape(nseq, npg, NSA_GROUPS, 2, HEAD_DIM).transpose(0, 2, 1, 3, 4).reshape(
        nseq, NSA_GROUPS, 2 * npg, HEAD_DIM)
    return fix(kc), fix(vc)


class _PageIndex:
    def __init__(self, table, fn):
        self.table = table
        self._fn = fn

    def __call__(self, b, p, pt):
        return self._fn(b, p, pt)


def _stack_heads(q):
    head = lax.broadcasted_iota(jnp.int32, q.shape, 1) >> 6
    qf = q.astype(F32)
    return jnp.concatenate([jnp.where(head == r, qf, 0.0) for r in range(NSA_REP)], axis=0).astype(BF16)


def _unstack_heads(o4, tq):
    head = lax.broadcasted_iota(jnp.int32, (tq, o4.shape[1]), 1) >> 6
    out = jnp.zeros((tq, o4.shape[1]), F32)
    for r in range(NSA_REP):
        out = jnp.where(head == r, o4[r * tq:(r + 1) * tq, :], out)
    return out


def _nsa_cmp_kernel(q0, ns_rows, q_ref, kc_ref, vc_ref, oc_ref, sel_ref):
    i = pl.program_id(2)
    tq = q_ref.shape[1]
    nc = kc_ref.shape[2]
    qst = _stack_heads(q_ref[0])
    kc, vc = kc_ref[0, 0], vc_ref[0, 0]
    base = q0 + i * tq
    qpos = base + (lax.broadcasted_iota(jnp.int32, (NSA_REP * tq, nc), 0) & (tq - 1))
    blk_end = (lax.broadcasted_iota(jnp.int32, (NSA_REP * tq, nc), 1) + 1) * NSA_BLOCK - 1
    ok = blk_end <= qpos
    s = jnp.where(ok, _dot_nt(qst, kc), NEG_INF)
    e = jnp.exp(s - jnp.max(s, axis=-1, keepdims=True))
    p = jnp.where(ok, e / jnp.sum(e, axis=-1, keepdims=True), 0.0)
    oc_ref[0] = _unstack_heads(_dot(p.astype(BF16), vc), tq)
    imp = p[0:tq, :]
    for r in range(1, NSA_REP):
        imp = imp + p[r * tq:(r + 1) * tq, :]
    qpos_col = base + lax.broadcasted_iota(jnp.int32, (ns_rows, tq), 1)
    sel_ref[0, 0] = _select_blocks(imp.T, qpos_col, ns_rows).T.astype(BF16)


def _select_blocks(imp, qpos, ns_rows):
    nc, cols = imp.shape
    if ns_rows > nc:
        imp = jnp.concatenate([imp, jnp.zeros((ns_rows - nc, cols), F32)], axis=0)
    blk = lax.broadcasted_iota(jnp.int32, (ns_rows, cols), 0)
    cur = qpos >> 6
    forced = (blk == 0) | (blk == cur) | (blk == cur - 1)
    imp = jnp.where(forced, NSA_FORCED, imp)
    imp = jnp.where(blk <= cur, imp, NEG_INF)
    taken = jnp.float32(-3e38)
    blk_f = blk.astype(F32)

    def pick(_, imp):
        m = jnp.max(imp, axis=0, keepdims=True)
        first = jnp.min(jnp.where(imp == m, blk_f, float(ns_rows)), axis=0, keepdims=True)
        return jnp.where(blk_f == first, taken, imp)

    imp = lax.fori_loop(0, NSA_TOPN, pick, imp)
    return jnp.where(imp == taken, 0.0, NEG_INF)


def _nsa_cmp_sample_kernel(q0, t_seq, ns_rows, q_ref, kc_ref, vc_ref, oc_ref, sel_ref):
    ntok = q_ref.shape[0]
    nkey = kc_ref.shape[1]
    nc = nkey // (ntok // t_seq)
    rows = NSA_REP * ntok
    qst = _stack_heads(q_ref[...])
    row = lax.broadcasted_iota(jnp.int32, (rows, nkey), 0) & (ntok - 1)
    col = lax.broadcasted_iota(jnp.int32, (rows, nkey), 1)
    qpos = q0 + (row & (t_seq - 1))
    same_seq = (row >> (t_seq.bit_length() - 1)) == (col >> (nc.bit_length() - 1))
    ok = same_seq & (((col & (nc - 1)) + 1) * NSA_BLOCK - 1 <= qpos)
    s = jnp.where(ok, _dot_nt(qst, kc_ref[0]), NEG_INF)
    e = jnp.exp(s - jnp.max(s, axis=-1, keepdims=True))
    p = jnp.where(ok, e / jnp.sum(e, axis=-1, keepdims=True), 0.0)
    oc_ref[...] = _unstack_heads(_dot(p.astype(BF16), vc_ref[0]), ntok)
    own = p[:, 0:nc]
    for u in range(1, nkey // nc):
        own = own + p[:, u * nc:(u + 1) * nc]
    imp = own[0:ntok, :]
    for r in range(1, NSA_REP):
        imp = imp + own[r * ntok:(r + 1) * ntok, :]
    qpos_col = q0 + (lax.broadcasted_iota(jnp.int32, (ns_rows, ntok), 1) & (t_seq - 1))
    sel_ref[0] = _select_blocks(imp.T, qpos_col, ns_rows).T.astype(BF16)


def nsa_cmp_select_sample(qn, kc, vc, q0, t_seq, ns_rows):
    ntok = qn.shape[0]
    nseq, _, nc, _ = kc.shape
    flat = lambda a: _tile_lanes4(a.transpose(1, 0, 2, 3).reshape(1, NSA_GROUPS, nseq * nc, HEAD_DIM))[0]
    kv = pl.BlockSpec((1, nseq * nc, 256), lambda g: (g, 0, 0))
    return pl.pallas_call(
        functools.partial(_nsa_cmp_sample_kernel, q0, t_seq, ns_rows),
        grid=(NSA_GROUPS,),
        in_specs=[pl.BlockSpec((ntok, 256), lambda g: (0, g)), kv, kv],
        out_specs=[pl.BlockSpec((ntok, 256), lambda g: (0, g)),
                   pl.BlockSpec((1, ntok, ns_rows), lambda g: (g, 0, 0))],
        out_shape=[jax.ShapeDtypeStruct((ntok, 1024), F32),
                   jax.ShapeDtypeStruct((NSA_GROUPS, ntok, ns_rows), BF16)],
        compiler_params=_cparams("parallel"),
        name="nsa_cmp_select_sample",
    )(qn, flat(kc), flat(vc))


def _tile_lanes4(x):
    return jnp.tile(x, (1, 1, 1, NSA_REP)).astype(BF16)


def nsa_cmp_select(qn, kc, vc, q0, ns_rows):
    b, t, _ = qn.shape
    tq = min(512, t)
    nc = kc.shape[2]
    kv = pl.BlockSpec((1, 1, nc, 256), lambda bi, g, i: (bi, g, 0, 0))
    return pl.pallas_call(
        functools.partial(_nsa_cmp_kernel, q0, ns_rows),
        grid=(b, NSA_GROUPS, t // tq),
        in_specs=[pl.BlockSpec((1, tq, 256), lambda bi, g, i: (bi, i, g)), kv, kv],
        out_specs=[pl.BlockSpec((1, tq, 256), lambda bi, g, i: (bi, i, g)),
                   pl.BlockSpec((1, 1, tq, ns_rows), lambda bi, g, i: (bi, g, i, 0))],
        out_shape=[jax.ShapeDtypeStruct((b, t, 1024), F32),
                   jax.ShapeDtypeStruct((b, NSA_GROUPS, t, ns_rows), BF16)],
        compiler_params=_cparams("parallel", "parallel", "parallel"),
        name="nsa_cmp_select",
    )(qn, _tile_lanes4(kc), _tile_lanes4(vc))


def _head_rows(q):
    lane = lax.broadcasted_iota(jnp.int32, (q.shape[0], LANES), 1)
    out = []
    for r in range(NSA_REP):
        slab = q[:, (r // 2) * LANES:(r // 2 + 1) * LANES]
        slab = pltpu.roll(slab, HEAD_DIM, 1) if r % 2 else slab
        out.append(jnp.where(lane < HEAD_DIM, slab, 0.0))
    return out


def _store_heads(o_ref, row0, acc, tq):
    lane = lax.broadcasted_iota(jnp.int32, (tq, LANES), 1)
    o4 = acc / pltpu.roll(acc, HEAD_DIM, 1)
    for half in range(NSA_REP // 2):
        even = o4[(2 * half) * tq:(2 * half + 1) * tq, :]
        odd = pltpu.roll(o4[(2 * half + 1) * tq:(2 * half + 2) * tq, :], HEAD_DIM, 1)
        o_ref[0, row0:row0 + tq, half * LANES:(half + 1) * LANES] = jnp.where(lane < HEAD_DIM, even, odd)


WIN_SUB = 4


def _nsa_window_kernel(nsub, q_ref, kv_ref, o_ref):
    i = pl.program_id(2)
    tq = q_ref.shape[1] // nsub
    ntile = (NSA_WINDOW + 2 * tq) // LANES
    span = ntile * LANES
    rows = NSA_REP * tq
    for pair in range(nsub // 2):
        j0 = jnp.maximum((i * nsub + 2 * pair) * tq - NSA_WINDOW, 0) // LANES
        k_aug = jnp.concatenate([kv_ref[0, j0 + u, 0, 0:2 * HEAD_DIM, :] for u in range(ntile)], axis=1)
        v_aug = jnp.concatenate([kv_ref[0, j0 + u, 0, 2 * HEAD_DIM:4 * HEAD_DIM, :] for u in range(ntile)], axis=1)
        kpos = j0 * LANES + lax.broadcasted_iota(jnp.int32, (rows, span), 1)
        for a in range(2 * pair, 2 * pair + 2):
            lhs = jnp.concatenate(_head_rows(q_ref[0, a * tq:(a + 1) * tq, :].astype(F32)), axis=0).astype(BF16)
            qpos = (i * nsub + a) * tq + (lax.broadcasted_iota(jnp.int32, (rows, span), 0) & (tq - 1))
            ok = (kpos <= qpos) & (kpos > qpos - NSA_WINDOW)
            s = jnp.where(ok, _dot(lhs, k_aug), NEG_INF)
            e = jnp.exp2(s - jnp.max(s, axis=-1, keepdims=True))
            _store_heads(o_ref, a * tq, _dot_nt(e.astype(BF16), v_aug), tq)


def nsa_window_prompt(qr, kwin):
    b, t, _ = qr.shape
    tq = WIN_SUB * 128
    nt = kwin.shape[1]
    return pl.pallas_call(
        functools.partial(_nsa_window_kernel, WIN_SUB),
        grid=(b, NSA_GROUPS, t // tq),
        in_specs=[pl.BlockSpec((1, tq, 256), lambda bi, g, i: (bi, i, g)),
                  pl.BlockSpec((1, nt, 1, 4 * HEAD_DIM, LANES), lambda bi, g, i: (bi, 0, g, 0, 0))],
        out_specs=pl.BlockSpec((1, tq, 256), lambda bi, g, i: (bi, i, g)),
        out_shape=jax.ShapeDtypeStruct((b, t, 1024), F32),
        compiler_params=_cparams("parallel", "parallel", "arbitrary"),
        name="nsa_window",
    )(qr, kwin)


SEL_SUB = 8
SEL_KT = 2


def _nsa_selected_kernel(kt, q_ref, sel_ref, kv_ref, e_ref, o_ref, m_ref, acc_ref):
    i = pl.program_id(2)
    nsub = m_ref.shape[0]
    tq = q_ref.shape[1] // nsub
    tk = kt * kv_ref.shape[4]
    rows = NSA_REP * tq

    def lhs(a):
        sel = sel_ref[0, 0, a * tq:(a + 1) * tq, :].astype(F32)
        heads = _head_rows(q_ref[0, a * tq:(a + 1) * tq, :].astype(F32))
        return jnp.concatenate([jnp.concatenate([h, sel], axis=1) for h in heads], axis=0).astype(BF16)

    qs = [lhs(a) for a in range(nsub)]
    _softmax_init(m_ref, acc_ref)

    def tile(j, causal):
        span = lambda piece: jnp.concatenate([piece(j * kt + u) for u in range(kt)], axis=1)
        k_aug = jnp.concatenate([span(lambda jj: kv_ref[0, jj, 0, 0:2 * HEAD_DIM, :]), span(lambda jj: e_ref[jj])],
                                axis=0)
        v_aug = span(lambda jj: kv_ref[0, jj, 0, 2 * HEAD_DIM:4 * HEAD_DIM, :])
        for a in range(nsub):
            w = min(tk, _round_up((a + 1) * tq, DIAG_WIDTH_STEP)) if causal else tk
            ka, va = k_aug[:, 0:w], v_aug[:, 0:w]
            s = _dot(qs[a], ka)
            if causal:
                qpos = (i * nsub + a) * tq + (lax.broadcasted_iota(jnp.int32, (rows, w), 0) & (tq - 1))
                kpos = j * tk + lax.broadcasted_iota(jnp.int32, (rows, w), 1)
                s = jnp.where(kpos <= qpos, s, NEG_INF)
            _softmax_update_vsum(s, lambda p, va=va: _dot_nt(p, va), m_ref.at[a], acc_ref.at[a])

    nfull = (i * nsub * tq) // tk

    def body(j, c):
        tile(j, False)
        return c

    lax.fori_loop(0, nfull, body, 0)
    tile(nfull, True)
    for a in range(nsub):
        _store_heads(o_ref, a * tq, acc_ref[a], tq)


def nsa_selected_prompt(qr, sel, ksel):
    b, t, _ = qr.shape
    sub = 128
    tq = SEL_SUB * sub
    nt, _, _, tk = ksel.shape[1:]
    assert SEL_KT * tk == tq and nt % SEL_KT == 0
    nblk = sel.shape[-1]
    e3 = (jnp.arange(nblk)[None, :, None] == (jnp.arange(nt)[:, None, None] * tk + jnp.arange(tk)[None, None, :]) // NSA_BLOCK
          ).astype(BF16)
    return pl.pallas_call(
        functools.partial(_nsa_selected_kernel, SEL_KT),
        grid=(b, NSA_GROUPS, t // tq),
        in_specs=[pl.BlockSpec((1, tq, 256), lambda bi, g, i: (bi, i, g)),
                  pl.BlockSpec((1, 1, tq, nblk), lambda bi, g, i: (bi, g, i, 0)),
                  pl.BlockSpec((1, nt, 1, 4 * HEAD_DIM, tk), lambda bi, g, i: (bi, 0, g, 0, 0)),
                  pl.BlockSpec(e3.shape, lambda bi, g, i: (0, 0, 0))],
        out_specs=pl.BlockSpec((1, tq, 256), lambda bi, g, i: (bi, i, g)),
        out_shape=jax.ShapeDtypeStruct((b, t, 1024), F32),
        scratch_shapes=[pltpu.VMEM((SEL_SUB, NSA_REP * sub, LANES), F32),
                        pltpu.VMEM((SEL_SUB, NSA_REP * sub, LANES), F32)],
        compiler_params=_cparams("parallel", "parallel", "arbitrary"),
        name="nsa_selected",
    )(qr, sel, ksel, e3)


def _nsa_out_kernel(oc_ref, os_ref, ow_ref, gate_ref, ex_ref, w_ref, r_ref, o_ref):
    gate = gate_ref[...]
    comb = (_split_dot(gate, ex_ref[0]) * oc_ref[...] + _split_dot(gate, ex_ref[1]) * os_ref[...]
            + _split_dot(gate, ex_ref[2]) * ow_ref[...])
    o_ref[...] = r_ref[...] + _dot(comb.astype(BF16), w_ref[...])


def nsa_out(oc, os_, ow, gates, w_o, res):
    n, d = res.shape
    tm = min(256, n)
    lane = jnp.arange(1024) // HEAD_DIM
    ex = jnp.stack([(jnp.arange(LANES)[:, None] == lane[None, :] * 3 + k) for k in range(3)]).astype(BF16)
    row = lambda w: pl.BlockSpec((tm, w), lambda i: (i, 0))
    return pl.pallas_call(
        _nsa_out_kernel,
        grid=(n // tm,),
        in_specs=[row(1024), row(1024), row(1024), row(LANES),
                  pl.BlockSpec(ex.shape, lambda i: (0, 0, 0)),
                  pl.BlockSpec(w_o.shape, lambda i: (0, 0)), row(d)],
        out_specs=row(d),
        out_shape=jax.ShapeDtypeStruct((n, d), F32),
        compiler_params=_cparams("parallel"),
        name="nsa_out",
    )(oc, os_, ow, gates, ex, w_o, res)


def _paged_attn_kernel(npp, kv_t, extra_rows, pt_ref, q_ref, *refs):
    k_refs, v_refs = refs[:npp], refs[npp:2 * npp]
    kn_ref, vn_ref, bp_ref, bn_ref, o_ref, m_ref, l_ref, acc_ref = refs[2 * npp:]
    p = pl.program_id(1)
    last = pl.num_programs(1) - 1

    @pl.when(p == 0)
    def _():
        _softmax_init(m_ref, acc_ref, l_ref)

    def step(k, v, bias):
        if kv_t:
            s = _dot(q_ref[0], k)
            s = s if bias is None else s + bias
            _softmax_update(s, lambda e: _dot_nt(e, v), m_ref, l_ref, acc_ref)
        else:
            s = _dot_nt(q_ref[0], k) + bias
            _softmax_update(s, lambda e: _dot(e, v), m_ref, l_ref, acc_ref)

    def load(r):
        x = r[...]
        return x if kv_t else x.reshape(-1, x.shape[-1])

    @pl.when(p < last)
    def _():
        axis = 1 if kv_t else 0
        k = jnp.concatenate([load(r) for r in k_refs], axis=axis).astype(BF16)
        v = jnp.concatenate([load(r) for r in v_refs], axis=axis).astype(BF16)
        if extra_rows:
            step(jnp.concatenate([k, bp_ref[0]], axis=0), v, None)
        else:
            step(k, v, bp_ref[0])

    @pl.when(p == last)
    def _():
        step(kn_ref[0], vn_ref[0], bn_ref[0])
        o_ref[0] = _softmax_result(l_ref, acc_ref)


def paged_attention(q, pages, table, npg, npp, k_spec, v_spec, kv_t, k_new, v_new, bias_past, bias_new, past_per_step,
                    extra_rows=False):
    nseq, rows, _ = q.shape
    lv = acc_w = v_new.shape[1] if kv_t else v_new.shape[2]
    seq = lambda a: pl.BlockSpec((1,) + a.shape[1:], lambda b, p, pt: (b, 0, 0))
    nstep = npg // npp
    if extra_rows:
        bp_spec = pl.BlockSpec((1,) + bias_past.shape[1:], lambda b, p, pt: (jnp.minimum(p, nstep - 1), 0, 0))
    else:
        wpast = bias_past.shape[-1] if not past_per_step else bias_past.shape[-1] // nstep
        bp_spec = pl.BlockSpec((1, rows, wpast), lambda b, p, pt: (
            b if bias_past.shape[0] > 1 else 0, 0, jnp.minimum(p, nstep - 1) if past_per_step else 0))
    bn_spec = pl.BlockSpec((1,) + bias_new.shape[1:], lambda b, p, pt: (b if bias_new.shape[0] > 1 else 0, 0, 0))
    return pl.pallas_call(
        functools.partial(_paged_attn_kernel, npp, kv_t, extra_rows),
        grid_spec=pltpu.PrefetchScalarGridSpec(
            num_scalar_prefetch=1,
            grid=(nseq, npg // npp + 1),
            in_specs=[seq(q)] + [k_spec(u) for u in range(npp)] + [v_spec(u) for u in range(npp)]
                     + [seq(k_new), seq(v_new), bp_spec, bn_spec],
            out_specs=pl.BlockSpec((1, rows, acc_w), lambda b, p, pt: (b, 0, 0)),
            scratch_shapes=[pltpu.VMEM((rows, LANES), F32), pltpu.VMEM((rows, LANES), F32),
                            pltpu.VMEM((rows, acc_w), F32)]),
        out_shape=jax.ShapeDtypeStruct((nseq, rows, lv), F32),
        compiler_params=_cparams("arbitrary", "arbitrary"),
        name="paged_attention",
    )(table, q, *([pages] * (2 * npp)), k_new, v_new, bias_past, bias_new)


def _diff_project_kernel(x_ref, ng_ref, w_ref, cos_ref, sin_ref, g_ref, so_ref, q_ref, kvf_ref, kvb_ref):
    hn = _rms_rows(x_ref[...], ng_ref[...]).astype(BF16)
    cos, sin = cos_ref[...], sin_ref[...]
    so = so_ref[...]
    qp = _dot(hn, w_ref[:, 0:DIFF_WIDTH])
    kp = _dot(hn, w_ref[:, DIFF_WIDTH:2 * DIFF_WIDTH])
    v = _dot(hn, w_ref[:, 2 * DIFF_WIDTH:3 * DIFF_WIDTH])
    for s in range(DIFF_WIDTH // LANES):
        sl = slice(s * LANES, (s + 1) * LANES)
        q = _rope_slab(_seg64_norm(qp[:, sl], g_ref[0:1, :], so), cos, sin)
        q_ref[:, sl] = (q * QK_SCALE_LOG2).astype(BF16)
        k = _rope_slab(_seg64_norm(kp[:, sl], g_ref[1:2, :], so), cos, sin)
        kvf_ref[:, sl] = k
        kvb_ref[:, sl] = k.astype(BF16)
    kvf_ref[:, DIFF_WIDTH:] = v
    kvb_ref[:, DIFF_WIDTH:] = v.astype(BF16)


def diff_project(x, norm_g, w_in, cos, sin, qk_g):
    n, d = x.shape
    tm = min(256, n)
    g2 = jnp.tile(qk_g, (1, 2))
    so = _seg_ones()
    ng = norm_g.reshape(1, d)
    row = lambda w: pl.BlockSpec((tm, w), lambda i: (i, 0))
    return pl.pallas_call(
        _diff_project_kernel,
        grid=(n // tm,),
        in_specs=[row(d), _resident(ng), _resident(w_in), row(LANES), row(LANES), _resident(g2), _resident(so)],
        out_specs=[row(DIFF_WIDTH), row(2 * DIFF_WIDTH), row(2 * DIFF_WIDTH)],
        out_shape=[jax.ShapeDtypeStruct((n, DIFF_WIDTH), BF16), jax.ShapeDtypeStruct((n, 2 * DIFF_WIDTH), F32),
                   jax.ShapeDtypeStruct((n, 2 * DIFF_WIDTH), BF16)],
        compiler_params=_cparams("parallel"),
        name="diff_project",
    )(x, ng, w_in, cos, sin, g2, so)


def _diff_lambda(lam_ref, lam_init):
    lf = lam_ref[...]
    a = jnp.sum(lf[0:1, :] * lf[1:2, :], axis=-1, keepdims=True)
    b = jnp.sum(lf[2:3, :] * lf[3:4, :], axis=-1, keepdims=True)
    return jnp.exp(a) - jnp.exp(b) + lam_init


DIFF_SUB = 8
DIFF_SUB_TQ = 128
DIFF_TK = 1024


def _diff_flash_kernel(lam_init, tk, q_ref, k_ref, v_ref, lam_ref, o_ref, m_ref, acc_ref):
    i = pl.program_id(2)
    nsub = m_ref.shape[0]
    tq = q_ref.shape[1] // nsub
    rows = 2 * tq

    def stack_components(q):
        comp = lax.broadcasted_iota(jnp.int32, q.shape, 1) >> 6
        return jnp.concatenate([jnp.where(comp == c, q, 0.0) for c in range(2)], axis=0).astype(BF16)

    qst = [stack_components(q_ref[0, a * tq:(a + 1) * tq, :].astype(F32)) for a in range(nsub)]
    _softmax_init(m_ref, acc_ref)
    ones = jnp.ones((tk, LANES), BF16)

    def tile(j, causal):
        lo = pl.multiple_of(j * tk, tk)
        k = k_ref[0, pl.ds(lo, tk), :]
        v_aug = jnp.concatenate([v_ref[0, pl.ds(lo, tk), :], ones], axis=1)
        for a in range(nsub):
            w = min(tk, _round_up((a + 1) * tq, DIAG_WIDTH_STEP)) if causal else tk
            ka, va = k[0:w, :], v_aug[0:w, :]
            s = _dot_nt(qst[a], ka)
            if causal:
                qpos = (i * nsub + a) * tq + (lax.broadcasted_iota(jnp.int32, (rows, w), 0) & (tq - 1))
                kpos = j * tk + lax.broadcasted_iota(jnp.int32, (rows, w), 1)
                s = jnp.where(kpos <= qpos, s, NEG_INF)
            _softmax_update_vsum(s, lambda p, va=va: _dot(p, va), m_ref.at[a], acc_ref.at[a])

    nfull = (i * nsub * tq) // tk

    def body(j, c):
        tile(j, False)
        return c

    lax.fori_loop(0, nfull, body, 0)
    tile(nfull, True)
    lam = _diff_lambda(lam_ref, lam_init)
    for a in range(nsub):
        acc = acc_ref[a]
        o = acc[:, 0:LANES] / acc[:, LANES:2 * LANES]
        o_ref[0, a * tq:(a + 1) * tq, :] = o[0:tq, :] - lam * o[tq:rows, :]


def diff_flash_prompt(q, kvb, lam, lam_init):
    b, t, _ = q.shape
    sub = DIFF_SUB_TQ
    tq = DIFF_SUB * sub
    tk = DIFF_TK
    assert tk == tq and t % tk == 0
    return pl.pallas_call(
        functools.partial(_diff_flash_kernel, lam_init, tk),
        grid=(b, DIFF_HEADS, t // tq),
        in_specs=[pl.BlockSpec((1, tq, LANES), lambda bi, h, i: (bi, i, h)),
                  pl.BlockSpec((1, t, LANES), lambda bi, h, i: (bi, 0, h)),
                  pl.BlockSpec((1, t, LANES), lambda bi, h, i: (bi, 0, DIFF_HEADS + h)),
                  pl.BlockSpec(lam.shape, lambda bi, h, i: (0, 0))],
        out_specs=pl.BlockSpec((1, tq, LANES), lambda bi, h, i: (bi, i, h)),
        out_shape=jax.ShapeDtypeStruct((b, t, DIFF_WIDTH), F32),
        scratch_shapes=[pltpu.VMEM((DIFF_SUB, 2 * sub, LANES), F32), pltpu.VMEM((DIFF_SUB, 2 * sub, 2 * LANES), F32)],
        compiler_params=_cparams("parallel", "parallel", "arbitrary"),
        name="diff_flash",
    )(q, kvb, kvb, lam)


def _diff_out_kernel(lam_init, two, *refs):
    if two:
        o0_ref, o1_ref, lam_ref, g_ref, w_ref, r_ref, out_ref, h_ref = refs
        o = o0_ref[...] - _diff_lambda(lam_ref, lam_init) * o1_ref[...]
    else:
        o0_ref, g_ref, w_ref, r_ref, out_ref, h_ref = refs
        o = o0_ref[...]
    for s in range(DIFF_HEADS):
        sl = slice(s * LANES, (s + 1) * LANES)
        h_ref[:, sl] = (_rms_rows(o[:, sl], g_ref[...]) * (1.0 - lam_init)).astype(BF16)
    out_ref[...] = r_ref[...] + _dot(h_ref[...], w_ref[...])


def diff_out(o, sub_g, w_o, res, lam_init, o1=None, lam=None):
    n, d = res.shape
    tm = min(512, n)
    two = o1 is not None
    row = lambda w: pl.BlockSpec((tm, w), lambda i: (i, 0))
    const = lambda a: pl.BlockSpec(a.shape, lambda i: (0, 0))
    g = sub_g.reshape(1, LANES)
    ins = [o, o1, lam, g, w_o, res] if two else [o, g, w_o, res]
    specs = ([row(DIFF_WIDTH), row(DIFF_WIDTH), const(lam)] if two else [row(DIFF_WIDTH)]) + [const(g), const(w_o), row(d)]
    return pl.pallas_call(
        functools.partial(_diff_out_kernel, lam_init, two),
        grid=(n // tm,),
        in_specs=specs,
        out_specs=row(d),
        out_shape=jax.ShapeDtypeStruct((n, d), F32),
        scratch_shapes=[pltpu.VMEM((tm, DIFF_WIDTH), BF16)],
        compiler_params=_cparams("parallel"),
        name="diff_out",
    )(*ins)


def _gla_kernel(chunk, nchunk, t_valid, x_ref, ng_ref, w_ref, s0_ref, wa_ref, ba_ref, g_ref, tri_ref,
                y_ref, sfin_ref, st_ref, p_ref):
    step = pl.program_id(1)

    @pl.when(step == 0)
    def _():
        st_ref[...] = s0_ref[0]

    p_ref[0] = _dot(_rms_rows(x_ref[0], ng_ref[...]).astype(BF16), w_ref[...])

    nk = GLA_HEADS * GLA_DK
    nv = GLA_HEADS * GLA_DV
    tri = tri_ref[...]
    causal = lax.broadcasted_iota(jnp.int32, (chunk, chunk), 0) >= lax.broadcasted_iota(jnp.int32, (chunk, chunk), 1)
    for ci in range(nchunk):
        r0 = ci * chunk
        rows = slice(r0, r0 + chunk)
        a1 = p_ref[0, rows, GLA_MAIN:GLA_MAIN + LANES].astype(BF16)
        z = _dot(a1, wa_ref[...]) + ba_ref[...]
        log_a = (jnp.minimum(z, 0.0) - jnp.log(1.0 + jnp.exp(-jnp.abs(z)))) * (1.0 / GLA_TAU)
        if t_valid < chunk:
            log_a = jnp.where(lax.broadcasted_iota(jnp.int32, log_a.shape, 0) < t_valid, log_a, 0.0)
        cum = _split_dot_left(tri, log_a)
        for h in range(GLA_HEADS):
            ksl = slice(h * GLA_DK, (h + 1) * GLA_DK)
            vsl = slice(h * GLA_DV, (h + 1) * GLA_DV)
            q = p_ref[0, rows, h * GLA_DK:(h + 1) * GLA_DK] * (GLA_DK ** -0.5)
            k = p_ref[0, rows, nk + h * GLA_DK:nk + (h + 1) * GLA_DK]
            v = p_ref[0, rows, 2 * nk + h * GLA_DV:2 * nk + (h + 1) * GLA_DV]
            r = p_ref[0, rows, 2 * nk + nv + h * GLA_DV:2 * nk + nv + (h + 1) * GLA_DV]
            cm = cum[:, ksl]
            last = cm[chunk - 1:chunk, :]
            qe = (q * jnp.exp(cm)).astype(BF16)
            ke = (k * jnp.exp(-cm)).astype(BF16)
            vb = v.astype(BF16)
            att = jnp.where(causal, _dot_nt(qe, ke), 0.0)
            st = st_ref[h]
            o = _dot_nt(qe, st.astype(BF16)) + _dot(att.astype(BF16), vb)
            kd = (k * jnp.exp(last - cm)).astype(BF16)
            st_ref[h] = st * jnp.exp(last) + _dot(v.T.astype(BF16), kd)
            y_ref[0, rows, vsl] = _rms_rows(o, g_ref[...]) * _silu(r)

    @pl.when(step == pl.num_programs(1) - 1)
    def _():
        sfin_ref[0] = st_ref[...]


def gla_core(x, norm_g, w_in, s0_t, w_a2, b_a, out_g, chunk, nchunk, t_valid):
    b, t, d = x.shape
    rows = chunk * nchunk
    ng = norm_g.reshape(1, d)
    wa = jnp.zeros((LANES, GLA_HEADS * GLA_DK), F32).at[:GLA_RANK].set(w_a2).astype(BF16)
    ba = b_a.reshape(1, -1)
    g = out_g.reshape(1, GLA_DV)
    tri = (jnp.arange(chunk)[:, None] >= jnp.arange(chunk)[None, :]).astype(BF16)
    const = lambda a: pl.BlockSpec(a.shape, lambda bi, s: (0,) * a.ndim)
    st_spec = pl.BlockSpec((1, GLA_HEADS, GLA_DV, GLA_DK), lambda bi, s: (bi, 0, 0, 0))
    return pl.pallas_call(
        functools.partial(_gla_kernel, chunk, nchunk, t_valid),
        grid=(b, t // rows),
        in_specs=[pl.BlockSpec((1, rows, d), lambda bi, s: (bi, s, 0)), const(ng), const(w_in), st_spec,
                  const(wa), const(ba), const(g), const(tri)],
        out_specs=[pl.BlockSpec((1, rows, GLA_HEADS * GLA_DV), lambda bi, s: (bi, s, 0)), st_spec],
        out_shape=[jax.ShapeDtypeStruct((b, t, GLA_HEADS * GLA_DV), F32),
                   jax.ShapeDtypeStruct((b, GLA_HEADS, GLA_DV, GLA_DK), F32)],
        scratch_shapes=[pltpu.VMEM((GLA_HEADS, GLA_DV, GLA_DK), F32), pltpu.VMEM((1, rows, GLA_IN_PAD), F32)],
        compiler_params=_cparams("parallel", "arbitrary"),
        name="gla_core",
    )(x, ng, w_in, s0_t, wa, ba, g, tri)


def _pad_cols(w, width):
    return jnp.pad(w, ((0, 0), (0, width - w.shape[1]))).astype(BF16)


def _pad_axis(x, axis, size):
    pad = [(0, 0)] * x.ndim
    pad[axis] = (0, size - x.shape[axis])
    return jnp.pad(x, pad)


def _group_diag(q, t_seq):
    nseq = q.shape[0] // t_seq
    qg = q.reshape(nseq, t_seq, NSA_GROUPS, NSA_REP, HEAD_DIM).transpose(0, 2, 1, 3, 4)
    eye = jnp.eye(NSA_GROUPS, dtype=q.dtype)
    out = qg[:, :, :, :, None, :] * eye[None, :, None, None, :, None]
    return out.reshape(nseq, NSA_GROUPS * t_seq * NSA_REP, NSA_GROUPS * HEAD_DIM)


def _group_undiag(o, t_seq):
    nseq = o.shape[0]
    o6 = o.reshape(nseq, NSA_GROUPS, t_seq, NSA_REP, NSA_GROUPS, HEAD_DIM)
    od = jnp.stack([o6[:, g, :, :, g, :] for g in range(NSA_GROUPS)], axis=1)
    return od.transpose(0, 2, 1, 3, 4).reshape(nseq * t_seq, NSA_HEADS * HEAD_DIM)


def _seq_cols(x_t, nseq, t_seq):
    r = x_t.shape[0]
    return _pad_axis(x_t.reshape(r, nseq, t_seq).transpose(1, 0, 2), 2, PAGE_SIZE)


def _nsa_layer(s, xp, xs, seq_p, t_s, past_len, cache_t, state_win_t, page_table, norm_g, w_in, qk_g, pe, w_phi, w_o):
    bp = xp.shape[0] // seq_p
    bs = xs.shape[0] // t_s
    w_o_b = w_o.astype(BF16)
    npg = past_len // PAGE_SIZE

    qn, qr, gates, rows_t, win_t, ksel, kwin = nsa_project(xp, norm_g, w_in, qk_g, jnp.arange(seq_p), bp)
    own = _PageIndex(jnp.zeros((1, 1), jnp.int32), lambda b, p, pt: (b, 0, p))
    kc, vc = nsa_compress(rows_t, own, bp, seq_p // PAGE_SIZE, pe, w_phi, qk_g[3])
    qn3, qr3 = qn.reshape(bp, seq_p, 1024), qr.reshape(bp, seq_p, 1024)
    oc, sel = nsa_cmp_select(qn3, kc, vc, 0, seq_p // NSA_BLOCK)
    os_ = nsa_selected_prompt(qr3, sel, ksel)
    ow = nsa_window_prompt(qr3, kwin)
    xp_new = nsa_out(oc.reshape(-1, 1024), os_.reshape(-1, 1024), ow.reshape(-1, 1024), gates, w_o_b, xp)
    kv_p = rows_t.reshape(bp, 4, NSA_GROUPS, HEAD_DIM, seq_p).transpose(0, 4, 1, 2, 3)
    wlen = min(NSA_WINDOW, seq_p)
    win_p = win_t[:, :, seq_p - wlen:].reshape(bp, 2, NSA_GROUPS, HEAD_DIM, wlen).transpose(0, 4, 1, 2, 3)

    pos_s = past_len + jnp.arange(t_s)
    qn, qr, gates, rows_t, win_t, _, _ = nsa_project(xs, norm_g, w_in, qk_g, jnp.tile(pos_s, bs), 1)
    rows_t, win_t = rows_t[0], win_t[0]
    layer_pages = cache_t.shape[0] // (state_win_t.shape[0] // bs)
    table = page_table + s * layer_pages
    paged = _PageIndex(table, lambda b, p, pt: (pt[b, p], 0, 0))
    kc, vc = nsa_compress(cache_t, paged, bs, npg, pe, w_phi, qk_g[3])
    ns = -(-(past_len + t_s) // NSA_BLOCK)
    oc, sel = nsa_cmp_select_sample(qn, kc, vc, past_len, t_s, 256)
    nrow = NSA_GROUPS * t_s * NSA_REP
    sel_f = sel[:, :, :ns].astype(F32).reshape(NSA_GROUPS, bs, t_s, ns).transpose(1, 0, 2, 3)
    by_row = lambda a: jnp.broadcast_to(a[:, :, :, None, :], a.shape[:3] + (NSA_REP, a.shape[-1])).reshape(
        a.shape[0], nrow, a.shape[-1])
    new_ok = jnp.arange(PAGE_SIZE)[None, :] <= jnp.arange(t_s)[:, None]
    new_bias = by_row(jnp.where(new_ok[None, None], sel_f[..., past_len // NSA_BLOCK][..., None], NEG_INF))
    q_bd = _group_diag(qr, t_s)
    npp = min(NSA_PAGES_PER_STEP, npg)
    kpage = lambda blk: (lambda u: pl.BlockSpec((None, NSA_GW, PAGE_SIZE),
                                                lambda b, p, pt: (pt[b, jnp.minimum(p * npp + u, npg - 1)], blk, 0)))
    k_new = _seq_cols(rows_t[2 * NSA_GW:3 * NSA_GW], bs, t_s).astype(BF16)
    v_new = _seq_cols(rows_t[3 * NSA_GW:4 * NSA_GW], bs, t_s).astype(BF16)
    nblk = sel.shape[-1]
    sel_rows = by_row(sel.reshape(NSA_GROUPS, bs, t_s, nblk).transpose(1, 0, 2, 3))
    q_aug = jnp.concatenate([q_bd, sel_rows], axis=2)
    k_new_aug = jnp.concatenate([k_new, jnp.zeros((bs, nblk, PAGE_SIZE), BF16)], axis=1)
    wstep = npp * PAGE_SIZE
    key_blk = (jnp.arange(npg // npp)[:, None, None] * wstep + jnp.arange(wstep)[None, None, :]) // NSA_BLOCK
    e_steps = (jnp.arange(nblk)[None, :, None] == key_blk).astype(BF16)
    os_ = _group_undiag(paged_attention(q_aug, cache_t, table, npg, npp, kpage(2), kpage(3), True, k_new_aug, v_new,
                                        e_steps, new_bias, True, extra_rows=True), t_s)
    wbuf = state_win_t.shape[-1]
    wpg = wbuf // PAGE_SIZE
    wpage = lambda blk: (lambda u: pl.BlockSpec((None, NSA_GW, PAGE_SIZE),
                                                lambda b, p, pt: (s * bs + b, blk, jnp.minimum(p * wpg + u, wpg - 1))))
    kidx = jnp.arange(wbuf + PAGE_SIZE)
    wpos = jnp.where(kidx < wbuf, past_len - wbuf + kidx, past_len + kidx - wbuf)
    w_ok = ((wpos[None, :] <= pos_s[:, None]) & (wpos[None, :] > pos_s[:, None] - NSA_WINDOW) & (wpos[None, :] >= 0)
            & (kidx[None, :] < wbuf + t_s))
    wbias = by_row(jnp.broadcast_to(jnp.where(w_ok, 0.0, NEG_INF).astype(F32)[None, None],
                                    (1, NSA_GROUPS, t_s, wbuf + PAGE_SIZE)))
    kw_new = _seq_cols(win_t[0:NSA_GW], bs, t_s).astype(BF16)
    vw_new = _seq_cols(win_t[NSA_GW:2 * NSA_GW], bs, t_s).astype(BF16)
    ow = _group_undiag(paged_attention(q_bd, state_win_t, jnp.zeros((1, 1), jnp.int32), wpg, wpg, wpage(0), wpage(1), True,
                                       kw_new, vw_new, wbias[:, :, :wbuf], wbias[:, :, wbuf:], True), t_s)
    xs_new = nsa_out(oc, os_, ow, gates, w_o_b, xs)
    kv_s = rows_t.reshape(4, NSA_GROUPS, HEAD_DIM, bs, t_s).transpose(3, 4, 0, 1, 2)
    win_new = win_t.reshape(2 * NSA_GW, bs, t_s).transpose(1, 0, 2)
    win_all = jnp.concatenate([state_win_t[s * bs:(s + 1) * bs], win_new], axis=2)
    wlen = min(NSA_WINDOW, win_all.shape[2])
    win_s = win_all[:, :, win_all.shape[2] - wlen:].reshape(bs, 2, NSA_GROUPS, HEAD_DIM, wlen).transpose(0, 4, 1, 2, 3)
    return xp_new, xs_new, kv_p, kv_s, win_p, win_s


def _diff_layer(layer, xp, xs, seq_p, t_s, past_len, cache, page_table, norm_g, w_in, qk_g, lam, sub_g, w_o):
    bp = xp.shape[0] // seq_p
    bs = xs.shape[0] // t_s
    npg = past_len // PAGE_SIZE
    lam_init = 0.8 - 0.6 * math.exp(-0.3 * layer)
    w_in_b = w_in.astype(BF16)
    w_o_b = w_o.astype(BF16)
    cos_p, sin_p = _rope_tables(jnp.arange(seq_p))
    q, kvf, kvb = diff_project(xp, norm_g, w_in_b, jnp.tile(cos_p, (bp, 1)), jnp.tile(sin_p, (bp, 1)), qk_g)
    o = diff_flash_prompt(q.reshape(bp, seq_p, DIFF_WIDTH), kvb.reshape(bp, seq_p, 2 * DIFF_WIDTH), lam, lam_init)
    xp_new = diff_out(o.reshape(-1, DIFF_WIDTH), sub_g, w_o_b, xp, lam_init)
    kv_p = kvf.reshape(bp, seq_p, 2, DIFF_HEADS, 2 * HEAD_DIM)
    pos_s = past_len + jnp.arange(t_s)
    cos_s, sin_s = _rope_tables(pos_s)
    q, kvf, kvb = diff_project(xs, norm_g, w_in_b, jnp.tile(cos_s, (bs, 1)), jnp.tile(sin_s, (bs, 1)), qk_g)
    nrow = DIFF_HEADS * 2 * t_s
    q5 = q.reshape(bs, t_s, DIFF_HEADS, 2, HEAD_DIM).transpose(0, 2, 3, 1, 4)
    q_rows = (q5[:, :, :, :, None, :] * jnp.eye(2, dtype=q.dtype)[None, None, :, None, :, None]).reshape(
        bs, nrow, 2 * HEAD_DIM)
    kv5 = kvb.reshape(bs, t_s, 2, DIFF_HEADS, 2 * HEAD_DIM)
    k_new = _pad_axis(kv5[:, :, 0], 1, PAGE_SIZE).reshape(bs, PAGE_SIZE * DIFF_HEADS, 2 * HEAD_DIM)
    v_new = _pad_axis(kv5[:, :, 1], 1, PAGE_SIZE).reshape(bs, PAGE_SIZE * DIFF_HEADS, 2 * HEAD_DIM)
    npp = min(DIFF_PAGES_PER_STEP, npg)
    row_h = jnp.arange(nrow) // (2 * t_s)
    row_t = jnp.arange(nrow) % t_s
    slot_h = jnp.arange(PAGE_SIZE * DIFF_HEADS) % DIFF_HEADS
    slot_tok = jnp.arange(PAGE_SIZE * DIFF_HEADS) // DIFF_HEADS
    same_head = row_h[:, None] == slot_h[None, :]
    bias_page = jnp.where(same_head, 0.0, NEG_INF).astype(F32)
    bias_past = jnp.tile(bias_page, (1, npp))[None]
    bias_new = jnp.where(same_head & (slot_tok[None, :] <= row_t[:, None]), 0.0, NEG_INF).astype(F32)[None]
    page = lambda slot: (lambda u: pl.BlockSpec(
        (None, PAGE_SIZE, None, DIFF_HEADS, 2 * HEAD_DIM),
        lambda b, p, pt: (pt[b, jnp.minimum(p * npp + u, npg - 1)], 0, slot, 0, 0)))
    o = paged_attention(q_rows, cache, page_table, npg, npp, page(0), page(1), False, k_new, v_new,
                        bias_past, bias_new, False)
    od = o.reshape(bs, DIFF_HEADS, 2, t_s, 2 * HEAD_DIM).transpose(2, 0, 3, 1, 4).reshape(2, bs * t_s, DIFF_WIDTH)
    xs_new = diff_out(od[0], sub_g, w_o_b, xs, lam_init, o1=od[1], lam=lam)
    kv_s = kvf.reshape(bs, t_s, 2, DIFF_HEADS, 2 * HEAD_DIM)
    return xp_new, xs_new, kv_p, kv_s


def _gla_layer(xp, xs, seq_p, t_s, state, norm_g, w_in, w_a2, b_a, out_g, w_o):
    bp = xp.shape[0] // seq_p
    bs = xs.shape[0] // t_s
    w_in_b = _pad_cols(w_in, GLA_IN_PAD)
    w_o_b = w_o.astype(BF16)
    d = xp.shape[1]
    chunk = min(GLA_CHUNK, seq_p)
    s0 = jnp.zeros((bp, GLA_HEADS, GLA_DV, GLA_DK), F32)
    y, st = gla_core(xp.reshape(bp, seq_p, d), norm_g, w_in_b, s0, w_a2, b_a, out_g, chunk,
                     4 if seq_p % (4 * chunk) == 0 else 1, chunk)
    xp_new = matmul_residual(y.reshape(-1, GLA_HEADS * GLA_DV), w_o_b, xp)
    st_p = st.transpose(0, 1, 3, 2)
    chunk_s = GLA_CHUNK
    xs_pad = _pad_axis(xs.reshape(bs, t_s, d), 1, chunk_s)
    y, st = gla_core(xs_pad, norm_g, w_in_b, state.transpose(0, 1, 3, 2), w_a2, b_a, out_g, chunk_s, 1, t_s)
    xs_new = matmul_residual(y[:, :t_s].reshape(-1, GLA_HEADS * GLA_DV), w_o_b, xs)
    st_s = st.transpose(0, 1, 3, 2)
    return xp_new, xs_new, st_p, st_s


def kernel(x_prompt, x_sample, cache_nsa_kv, state_nsa_win, cache_diff_kv, state_gla, state_ffn, page_table, norm_g, ffn_w_up, ffn_conv_w, ffn_conv_b, ffn_w_down, nsa_w_in, nsa_qk_g, nsa_pe, nsa_w_phi, nsa_w_o, diff_w_in, diff_qk_g, diff_lam, diff_sub_g, diff_w_o, gla_w_in, gla_w_a2, gla_b_a, gla_out_g, gla_w_o):
    bp, seq_p, d = x_prompt.shape
    bs, t_s, _ = x_sample.shape
    past_len = page_table.shape[1] * PAGE_SIZE
    xp = x_prompt.reshape(bp * seq_p, d)
    xs = x_sample.reshape(bs * t_s, d)
    cache_t = cache_nsa_kv.transpose(0, 1, 3, 4, 5, 2).reshape(-1, 4 * NSA_GW, PAGE_SIZE)
    win_t = state_nsa_win.transpose(0, 1, 3, 4, 5, 2).reshape(-1, 2 * NSA_GW, state_nsa_win.shape[2])
    nsa_kv_p, nsa_kv_s, nsa_win_p, nsa_win_s = [], [], [], []
    diff_kv_p, diff_kv_s, gla_p, gla_s, ffn_p, ffn_s = [], [], [], [], [], []
    for i in range(DEPTH):
        kind, s = i % N_MIXERS, i // N_MIXERS
        if kind == 0:
            xp, xs, kvp, kvs, wp, ws = _nsa_layer(
                s, xp, xs, seq_p, t_s, past_len, cache_t, win_t, page_table,
                norm_g[i, 0], nsa_w_in[s], nsa_qk_g[s], nsa_pe[s], nsa_w_phi[s], nsa_w_o[s])
            nsa_kv_p.append(kvp); nsa_kv_s.append(kvs); nsa_win_p.append(wp); nsa_win_s.append(ws)
        elif kind == 1:
            xp, xs, kvp, kvs = _diff_layer(
                i, xp, xs, seq_p, t_s, past_len, cache_diff_kv[s], page_table, norm_g[i, 0], diff_w_in[s],
                diff_qk_g[s], diff_lam[s], diff_sub_g[s], diff_w_o[s])
            diff_kv_p.append(kvp); diff_kv_s.append(kvs)
        else:
            xp, xs, stp, sts = _gla_layer(xp, xs, seq_p, t_s, state_gla[s], norm_g[i, 0], gla_w_in[s], gla_w_a2[s],
                                          gla_b_a[s], gla_out_g[s], gla_w_o[s])
            gla_p.append(stp); gla_s.append(sts)
        w_up_b = ffn_w_up[i].astype(BF16)
        w_dn_b = ffn_w_down[i].astype(BF16)
        xp, tail_p = ffn_prompt(xp, norm_g[i, 1], w_up_b, ffn_conv_w[i], ffn_conv_b[i], w_dn_b, seq_p)
        xs, tail_s = ffn_sample(xs, norm_g[i, 1], w_up_b, ffn_conv_w[i], ffn_conv_b[i], w_dn_b, state_ffn[i], t_s)
        ffn_p.append(tail_p); ffn_s.append(tail_s)
    return (xp.reshape(bp, seq_p, d), xs.reshape(bs, t_s, d),
            jnp.stack(nsa_kv_p), jnp.stack(nsa_kv_s), jnp.stack(nsa_win_p), jnp.stack(nsa_win_s),
            jnp.stack(diff_kv_p), jnp.stack(diff_kv_s), jnp.stack(gla_p), jnp.stack(gla_s),
            jnp.stack(ffn_p), jnp.stack(ffn_s))
```

```python
import functools
import math

import jax
import jax.numpy as jnp
from jax import lax
from jax.experimental import pallas as pl
from jax.experimental.pallas import tpu as pltpu

F32 = jnp.float32
BF16 = jnp.bfloat16

HEAD_DIM = 64
ROPE_THETA = 10000.0
NORM_EPS = 1e-6
NEG_INF = -1e30
DEPTH = 4
N_MIXERS = 3
PAGE_SIZE = 128
NSA_HEADS = 16
NSA_GROUPS = 4
NSA_REP = 4
NSA_BLOCK = 64
NSA_TOPN = 16
NSA_WINDOW = 512
NSA_FORCED = 1e9
NSA_NQ = NSA_HEADS * HEAD_DIM
NSA_NKV = 6 * NSA_GROUPS * HEAD_DIM
NSA_NGATE = 3 * NSA_HEADS
NSA_GW = NSA_GROUPS * HEAD_DIM
DIFF_HEADS = 8
DIFF_WIDTH = 1024
GLA_HEADS = 4
GLA_DK = 128
GLA_DV = 256
GLA_RANK = 16
GLA_TAU = 16.0
GLA_CHUNK = 64
GLA_MAIN = 2 * GLA_HEADS * GLA_DK + 2 * GLA_HEADS * GLA_DV
GLA_IN_PAD = GLA_MAIN + 128
D_FF = 2816
CONV_W = 3
LANES = 128
VMEM_LIMIT = 56 * 1024 * 1024
SEL_TK = 512
NSA_PAGES_PER_STEP = 16
DIFF_PAGES_PER_STEP = 8
CMP_PITCH = HEAD_DIM + 8


def _cparams(*sem):
    return pltpu.CompilerParams(dimension_semantics=sem, vmem_limit_bytes=VMEM_LIMIT)


def _dot(a, b):
    return jnp.dot(a, b, preferred_element_type=F32)


def _dot_nt(a, b):
    return lax.dot_general(a, b, (((1,), (1,)), ((), ())), preferred_element_type=F32)


def _split_dot(x, m):
    hi = x.astype(BF16)
    lo = (x - hi.astype(F32)).astype(BF16)
    return _dot(hi, m) + _dot(lo, m)


def _split_dot_left(m, x):
    hi = x.astype(BF16)
    lo = (x - hi.astype(F32)).astype(BF16)
    return _dot(m, hi) + _dot(m, lo)


def _rms_rows(x, g):
    ms = jnp.mean(x * x, axis=-1, keepdims=True)
    return x * lax.rsqrt(ms + NORM_EPS) * g


def _seg64_norm(x, g, seg_ones):
    ms = _split_dot(x * x, seg_ones) * (1.0 / HEAD_DIM)
    return x * lax.rsqrt(ms + NORM_EPS) * g


def _rope_slab(x, cos, sin_signed):
    lane = lax.broadcasted_iota(jnp.int32, x.shape, 1)
    first = (lane & 63) < 32
    partner = jnp.where(first, pltpu.roll(x, 96, 1), pltpu.roll(x, 32, 1))
    return x * cos + partner * sin_signed


def _rope_angles(pos):
    half = HEAD_DIM // 2
    inv = ROPE_THETA ** (-jnp.arange(half, dtype=F32) / half)
    ang = pos.astype(F32)[:, None] * inv
    return jnp.cos(ang), jnp.sin(ang)


def _rope_tables(pos):
    cos, sin = _rope_angles(pos)
    return jnp.tile(cos, (1, 4)), jnp.tile(jnp.concatenate([-sin, sin], axis=1), (1, 2))


def _seg_ones():
    i = jnp.arange(LANES)
    return (i[:, None] // HEAD_DIM == i[None, :] // HEAD_DIM).astype(BF16)


def _lanes(x, width):
    return x if width == LANES else jnp.tile(x, (1, width // LANES))


def _softmax_update(s, v_dot, m_ref, l_ref, acc_ref):
    m = m_ref[...]
    m_new = jnp.maximum(m, jnp.max(s, axis=-1, keepdims=True))
    alpha = jnp.exp2(m - m_new)
    p = jnp.exp2(s - _lanes(m_new, s.shape[1]))
    m_ref[...] = m_new
    l_ref[...] = alpha * l_ref[...] + jnp.sum(p, axis=-1, keepdims=True)
    acc_ref[...] = _lanes(alpha, acc_ref.shape[-1]) * acc_ref[...] + v_dot(p.astype(BF16))


def _softmax_result(l_ref, acc_ref):
    return acc_ref[...] / _lanes(l_ref[...], acc_ref.shape[-1])


def _softmax_update_vsum(s, v_dot, m_ref, acc_ref):
    m = m_ref[...]
    m_new = jnp.maximum(m, jnp.max(s, axis=-1, keepdims=True))
    alpha = jnp.exp2(m - m_new)
    p = jnp.exp2(s - _lanes(m_new, s.shape[1]))
    m_ref[...] = m_new
    acc_ref[...] = _lanes(alpha, acc_ref.shape[-1]) * acc_ref[...] + v_dot(p.astype(BF16))


def _softmax_init(m_ref, acc_ref, l_ref=None):
    m_ref[...] = jnp.full_like(m_ref, NEG_INF)
    acc_ref[...] = jnp.zeros_like(acc_ref)
    if l_ref is not None:
        l_ref[...] = jnp.zeros_like(l_ref)


DIAG_WIDTH_STEP = 256


def _round_up(x, m):
    return -(-x // m) * m


QK_SCALE = HEAD_DIM ** -0.5
QK_SCALE_LOG2 = QK_SCALE * math.log2(math.e)


def _matmul_res_kernel(a_ref, w_ref, r_ref, o_ref):
    o_ref[...] = r_ref[...] + _dot(a_ref[...].astype(BF16), w_ref[...])


def matmul_residual(a, w, res):
    n, k = a.shape
    d = w.shape[1]
    tm = min(512, n)
    return pl.pallas_call(
        _matmul_res_kernel,
        grid=(n // tm,),
        in_specs=[pl.BlockSpec((tm, k), lambda i: (i, 0)),
                  pl.BlockSpec((k, d), lambda i: (0, 0)),
                  pl.BlockSpec((tm, d), lambda i: (i, 0))],
        out_specs=pl.BlockSpec((tm, d), lambda i: (i, 0)),
        out_shape=jax.ShapeDtypeStruct((n, d), F32),
        compiler_params=_cparams("parallel"),
        name="matmul_residual",
    )(a, w, res)


FFN_CHUNK = 256


def _silu(x):
    return x / (1.0 + jnp.exp(-x))


def _causal_conv(u, prev1, prev2, use1, use2, cw_ref, cb_ref):
    u1 = jnp.where(use1, prev1, pltpu.roll(u, 1, 0))
    u2 = jnp.where(use2, prev2, pltpu.roll(u, 2, 0))
    return cb_ref[...] + cw_ref[0:1, :] * u2 + cw_ref[1:2, :] * u1 + cw_ref[2:3, :] * u


def _ffn_prompt_kernel(tiles_per_seq, x_ref, g_ref, wg_ref, wu_ref, cwg_ref, cwu_ref, cbg_ref, cbu_ref,
                       wd_ref, o_ref, tg_ref, tu_ref, cg_ref, cu_ref, act_ref):
    i = pl.program_id(0)
    tm = x_ref.shape[0]
    nj, _, c = wg_ref.shape
    x = x_ref[...]
    h = _rms_rows(x, g_ref[...]).astype(BF16)
    row = lax.broadcasted_iota(jnp.int32, (tm, c), 0)

    @pl.when((i % tiles_per_seq) == 0)
    def _():
        cg_ref[...] = jnp.zeros_like(cg_ref)
        cu_ref[...] = jnp.zeros_like(cu_ref)

    def conv(u, prev, cw_ref, cb_ref):
        prev2 = jnp.where(row == 0, prev[0:1, :], prev[1:2, :])
        return _causal_conv(u, prev[1:2, :], prev2, row == 0, row < 2, cw_ref, cb_ref)

    for j in range(nj):
        ug = _dot(h, wg_ref[j])
        uu = _dot(h, wu_ref[j])
        act = _silu(conv(ug, cg_ref[j], cwg_ref.at[j], cbg_ref.at[j])) * conv(uu, cu_ref[j], cwu_ref.at[j], cbu_ref.at[j])
        act_ref[:, j * c:(j + 1) * c] = act.astype(BF16)
        cg_ref[j] = ug[tm - 2:tm, :]
        cu_ref[j] = uu[tm - 2:tm, :]
        tg_ref[0, :, j * c:(j + 1) * c] = ug[tm - 2:tm, :]
        tu_ref[0, :, j * c:(j + 1) * c] = uu[tm - 2:tm, :]
    o_ref[...] = x + _dot(act_ref[...], wd_ref[...])


def _resident(a):
    return pl.BlockSpec(a.shape, lambda i: (0,) * a.ndim, pipeline_mode=pl.Buffered(1))


def ffn_prompt(x, g, w_up, conv_w, conv_b, w_down, seq_len):
    n, d = x.shape
    c = FFN_CHUNK
    nj = D_FF // c
    tm = min(512, seq_len)
    tps = seq_len // tm
    chunks = lambda a: a.reshape(a.shape[0], 2, nj, c).transpose(1, 2, 0, 3)
    wg, wu = chunks(w_up)
    cwg, cwu = chunks(conv_w)
    cbg, cbu = chunks(conv_b.reshape(1, 2 * D_FF))
    g2 = g.reshape(1, d)
    consts = [g2, wg, wu, cwg, cwu, cbg, cbu, w_down]
    out, tg, tu = pl.pallas_call(
        functools.partial(_ffn_prompt_kernel, tps),
        grid=(n // tm,),
        in_specs=[pl.BlockSpec((tm, d), lambda i: (i, 0))] + [_resident(a) for a in consts],
        out_specs=[pl.BlockSpec((tm, d), lambda i: (i, 0)),
                   pl.BlockSpec((1, 2, D_FF), lambda i: (i, 0, 0)),
                   pl.BlockSpec((1, 2, D_FF), lambda i: (i, 0, 0))],
        out_shape=[jax.ShapeDtypeStruct((n, d), F32),
                   jax.ShapeDtypeStruct((n // tm, 2, D_FF), F32),
                   jax.ShapeDtypeStruct((n // tm, 2, D_FF), F32)],
        scratch_shapes=[pltpu.VMEM((nj, 2, c), F32),
                        pltpu.VMEM((nj, 2, c), F32),
                        pltpu.VMEM((tm, D_FF), BF16)],
        compiler_params=_cparams("arbitrary"),
        name="ffn_prompt",
    )(x, *consts)
    return out, jnp.concatenate([tg, tu], axis=-1)[tps - 1::tps]


def _ffn_sample_kernel(t_seq, x_ref, g_ref, wg_ref, wu_ref, cwg_ref, cwu_ref, cbg_ref, cbu_ref, wd_ref,
                       p1g_ref, p2g_ref, p1u_ref, p2u_ref, o_ref, ug_ref, uu_ref, h_ref):
    j = pl.program_id(0)

    @pl.when(j == 0)
    def _():
        h_ref[...] = _rms_rows(x_ref[...], g_ref[...]).astype(BF16)

    h = h_ref[...]
    ug = _dot(h, wg_ref[...])
    uu = _dot(h, wu_ref[...])
    t = lax.broadcasted_iota(jnp.int32, ug.shape, 0) & (t_seq - 1)
    cg = _causal_conv(ug, p1g_ref[...], p2g_ref[...], t == 0, t < 2, cwg_ref, cbg_ref)
    cu = _causal_conv(uu, p1u_ref[...], p2u_ref[...], t == 0, t < 2, cwu_ref, cbu_ref)
    part = _dot((_silu(cg) * cu).astype(BF16), wd_ref[...])
    ug_ref[...] = ug
    uu_ref[...] = uu

    @pl.when(j == 0)
    def _():
        o_ref[...] = x_ref[...] + part

    @pl.when(j > 0)
    def _():
        o_ref[...] += part


def ffn_sample(x, g, w_up, conv_w, conv_b, w_down, buf, t_seq):
    n, d = x.shape
    nseq = n // t_seq
    c = FFN_CHUNK
    nj = D_FF // c
    cb = conv_b.reshape(1, 2 * D_FF)
    reps = t_seq // 2
    prev1 = jnp.broadcast_to(buf[:, 1:2], (nseq, t_seq, 2 * D_FF)).reshape(n, 2 * D_FF)
    prev2 = jnp.concatenate([buf] * reps, axis=1).reshape(n, 2 * D_FF)
    full = lambda blk, off=0: pl.BlockSpec(blk, lambda j: (0, j + off))
    out, ug, uu = pl.pallas_call(
        functools.partial(_ffn_sample_kernel, t_seq),
        grid=(nj,),
        in_specs=[pl.BlockSpec((n, d), lambda j: (0, 0)),
                  pl.BlockSpec((1, d), lambda j: (0, 0)),
                  full((d, c)), full((d, c), nj),
                  full((CONV_W, c)), full((CONV_W, c), nj),
                  full((1, c)), full((1, c), nj),
                  pl.BlockSpec((c, d), lambda j: (j, 0)),
                  full((n, c)), full((n, c)), full((n, c), nj), full((n, c), nj)],
        out_specs=[pl.BlockSpec((n, d), lambda j: (0, 0)),
                   full((n, c)), full((n, c))],
        out_shape=[jax.ShapeDtypeStruct((n, d), F32),
                   jax.ShapeDtypeStruct((n, D_FF), F32),
                   jax.ShapeDtypeStruct((n, D_FF), F32)],
        scratch_shapes=[pltpu.VMEM((n, d), BF16)],
        compiler_params=_cparams("arbitrary"),
        name="ffn_sample",
    )(x, g.reshape(1, d), w_up, w_up, conv_w, conv_w, cb, cb, w_down, prev1, prev2, prev1, prev2)
    u = jnp.concatenate([ug, uu], axis=-1).reshape(nseq, t_seq, 2 * D_FF)
    return out, u[:, t_seq - (CONV_W - 1):]


def _head_norm_t(x, g):
    ms = jnp.mean(x * x, axis=0, keepdims=True)
    return x * lax.rsqrt(ms + NORM_EPS) * g


def _rope_t(x, cos, sin):
    half = HEAD_DIM // 2
    x1, x2 = x[0:half, :], x[half:HEAD_DIM, :]
    return jnp.concatenate([x1 * cos - x2 * sin, x1 * sin + x2 * cos], axis=0)


def _nsa_project_kernel(x_ref, g_ref, wq_ref, wkv_ref, wg_ref, cos_ref, sin_ref, cost_ref, sint_ref, gq_ref, gk_ref,
                        so_ref, qn_ref, qr_ref, gate_ref, rows_ref, win_ref, ksel_ref, kwin_ref):
    hn = _rms_rows(x_ref[...], g_ref[...]).astype(BF16)
    cos, sin = cos_ref[...], sin_ref[...]
    so = so_ref[...]
    q_all = _dot(hn, wq_ref[...])
    for s in range(NSA_NQ // LANES):
        sl = slice(s * LANES, (s + 1) * LANES)
        q = _seg64_norm(q_all[:, sl], gq_ref[...], so)
        qn_ref[:, sl] = (q * QK_SCALE).astype(BF16)
        qr_ref[:, sl] = (_rope_slab(q, cos, sin) * QK_SCALE_LOG2).astype(BF16)
    gate_ref[...] = 1.0 / (1.0 + jnp.exp(-_dot(hn, wg_ref[...])))
    kvt = _dot_nt(wkv_ref[...], hn)
    gw = NSA_GW
    cost, sint = cost_ref[...], sint_ref[...]
    rows_ref[0, 0:2 * gw, :] = kvt[0:2 * gw, :]
    rows_ref[0, 3 * gw:4 * gw, :] = kvt[3 * gw:4 * gw, :]
    win_ref[0, gw:2 * gw, :] = kvt[5 * gw:6 * gw, :]
    for g in range(NSA_GROUPS):
        hs = slice(g * HEAD_DIM, (g + 1) * HEAD_DIM)
        rows_ref[0, 2 * gw + g * HEAD_DIM:2 * gw + (g + 1) * HEAD_DIM, :] = _rope_t(
            _head_norm_t(kvt[2 * gw + g * HEAD_DIM:2 * gw + (g + 1) * HEAD_DIM, :], gk_ref[0]), cost, sint)
        win_ref[0, hs, :] = _rope_t(
            _head_norm_t(kvt[4 * gw + g * HEAD_DIM:4 * gw + (g + 1) * HEAD_DIM, :], gk_ref[1]), cost, sint)
    tm = kvt.shape[1]
    for g in range(NSA_GROUPS):
        hs = slice(g * HEAD_DIM, (g + 1) * HEAD_DIM)
        ksel_ref[0, 0, g, 0:HEAD_DIM, :] = rows_ref[0, 2 * gw + g * HEAD_DIM:2 * gw + (g + 1) * HEAD_DIM, :].astype(BF16)
        ksel_ref[0, 0, g, HEAD_DIM:2 * HEAD_DIM, :] = jnp.zeros((HEAD_DIM, tm), BF16)
        ksel_ref[0, 0, g, 2 * HEAD_DIM:3 * HEAD_DIM, :] = kvt[3 * gw + g * HEAD_DIM:3 * gw + (g + 1) * HEAD_DIM, :].astype(BF16)
        ksel_ref[0, 0, g, 3 * HEAD_DIM:4 * HEAD_DIM, :] = jnp.ones((HEAD_DIM, tm), BF16)
    wb = win_ref[0].astype(BF16)
    for u in range(kwin_ref.shape[1]):
        ts = slice(u * LANES, (u + 1) * LANES)
        for g in range(NSA_GROUPS):
            kwin_ref[0, u, g, 0:HEAD_DIM, :] = wb[g * HEAD_DIM:(g + 1) * HEAD_DIM, ts]
            kwin_ref[0, u, g, HEAD_DIM:2 * HEAD_DIM, :] = jnp.zeros((HEAD_DIM, LANES), BF16)
            kwin_ref[0, u, g, 2 * HEAD_DIM:3 * HEAD_DIM, :] = wb[gw + g * HEAD_DIM:gw + (g + 1) * HEAD_DIM, ts]
            kwin_ref[0, u, g, 3 * HEAD_DIM:4 * HEAD_DIM, :] = jnp.ones((HEAD_DIM, LANES), BF16)


def nsa_project(x, norm_g, w_in, qk_g, pos, nseq):
    n, d = x.shape
    t = n // nseq
    tm = min(SEL_TK, t)
    nt = t // tm
    wq = w_in[:, :NSA_NQ].astype(BF16)
    wkv_t = w_in[:, NSA_NQ:NSA_NQ + NSA_NKV].T.astype(BF16)
    wg = jnp.pad(w_in[:, NSA_NQ + NSA_NKV:], ((0, 0), (0, LANES - NSA_NGATE))).astype(BF16)
    cos, sin = _rope_angles(pos)
    cos_q, sin_q = jnp.tile(cos, (1, 4)), jnp.tile(jnp.concatenate([-sin, sin], axis=1), (1, 2))
    gq = jnp.tile(qk_g[0:1], (1, 2))
    gk = qk_g[1:3].reshape(2, HEAD_DIM, 1)
    so = _seg_ones()
    const = lambda a: pl.BlockSpec(a.shape, lambda b, i: (0,) * a.ndim)
    row = lambda w: pl.BlockSpec((tm, w), lambda b, i: (b * nt + i, 0))
    return pl.pallas_call(
        _nsa_project_kernel,
        grid=(nseq, nt),
        in_specs=[row(d), const(norm_g.reshape(1, d)), const(wq), const(wkv_t), const(wg),
                  pl.BlockSpec((tm, LANES), lambda b, i: (i, 0)), pl.BlockSpec((tm, LANES), lambda b, i: (i, 0)),
                  pl.BlockSpec((HEAD_DIM // 2, tm), lambda b, i: (0, i)),
                  pl.BlockSpec((HEAD_DIM // 2, tm), lambda b, i: (0, i)),
                  const(gq), const(gk), const(so)],
        out_specs=[row(NSA_NQ), row(NSA_NQ), row(LANES),
                   pl.BlockSpec((1, 4 * NSA_GW, tm), lambda b, i: (b, 0, i)),
                   pl.BlockSpec((1, 2 * NSA_GW, tm), lambda b, i: (b, 0, i)),
                   pl.BlockSpec((1, 1, NSA_GROUPS, 4 * HEAD_DIM, tm), lambda b, i: (b, i, 0, 0, 0)),
                   pl.BlockSpec((1, tm // LANES, NSA_GROUPS, 4 * HEAD_DIM, LANES), lambda b, i: (b, i, 0, 0, 0))],
        out_shape=[jax.ShapeDtypeStruct((n, NSA_NQ), BF16), jax.ShapeDtypeStruct((n, NSA_NQ), BF16),
                   jax.ShapeDtypeStruct((n, LANES), F32),
                   jax.ShapeDtypeStruct((nseq, 4 * NSA_GW, t), F32),
                   jax.ShapeDtypeStruct((nseq, 2 * NSA_GW, t), F32),
                   jax.ShapeDtypeStruct((nseq, nt, NSA_GROUPS, 4 * HEAD_DIM, tm), BF16),
                   jax.ShapeDtypeStruct((nseq, t // LANES, NSA_GROUPS, 4 * HEAD_DIM, LANES), BF16)],
        compiler_params=_cparams("parallel", "parallel"),
        name="nsa_project",
    )(x, norm_g.reshape(1, d), wq, wkv_t, wg, cos_q, sin_q, cos.T, sin.T, gq, gk, so)


def _nsa_compress_kernel(npp, pt_ref, *refs):
    page_refs = refs[:npp]
    pe_ref, w_ref, g_ref, so_ref, kc_ref, vc_ref, seqk_ref, seqv_ref, acc_ref = refs[npp:]
    p = pl.program_id(1)
    gw = NSA_GW
    pitch = CMP_PITCH
    for u in range(npp):
        for g in range(NSA_GROUPS):
            lo = pl.multiple_of(((p * npp + u) * NSA_GROUPS + g) * pitch, 8)
            seqk_ref[pl.ds(lo, HEAD_DIM), :] = page_refs[u][g * HEAD_DIM:(g + 1) * HEAD_DIM, :]
            seqv_ref[pl.ds(lo, HEAD_DIM), :] = page_refs[u][gw + g * HEAD_DIM:gw + (g + 1) * HEAD_DIM, :]

    @pl.when(p == pl.num_programs(1) - 1)
    def _():
        nrow = seqk_ref.shape[0] // pitch
        acc_ref[...] = jnp.zeros_like(acc_ref)
        for dd in range(HEAD_DIM):
            a = jnp.concatenate([seqk_ref[pl.ds(dd, nrow, stride=pitch), :],
                                 seqv_ref[pl.ds(dd, nrow, stride=pitch), :]], axis=1) + pe_ref[dd:dd + 1, :]
            acc_ref[...] += _dot(a.astype(BF16), w_ref[dd])
        kc_ref[0] = _seg64_norm(acc_ref[:, 0:LANES], g_ref[...], so_ref[...])
        vc_ref[0] = acc_ref[:, LANES:2 * LANES]


def nsa_compress(pages_t, page_index, nseq, npg, pe, w_phi, g_c):
    npp = min(NSA_PAGES_PER_STEP, npg)
    nrow = npg * NSA_GROUPS
    pe_t = jnp.concatenate([jnp.tile(pe[0].T, (1, 2)), jnp.tile(pe[1].T, (1, 2))], axis=1)
    eye4 = jnp.eye(4, dtype=F32)
    w4 = jnp.stack([w_phi[0], w_phi[0], w_phi[1], w_phi[1]])
    w = jnp.einsum('ab,alde->dalbe', eye4, w4).reshape(HEAD_DIM, 2 * LANES, 2 * LANES).astype(BF16)
    g2 = jnp.tile(g_c.reshape(1, HEAD_DIM), (1, 2))
    so = _seg_ones()
    const = lambda a: pl.BlockSpec(a.shape, lambda b, p, pt: (0,) * a.ndim, pipeline_mode=pl.Buffered(1))
    page = lambda u: pl.BlockSpec((None, 2 * NSA_GW, PAGE_SIZE), lambda b, p, pt: page_index(b, p * npp + u, pt))
    out = pl.BlockSpec((1, nrow, LANES), lambda b, p, pt: (b, 0, 0))
    table = page_index.table
    kc, vc = pl.pallas_call(
        functools.partial(_nsa_compress_kernel, npp),
        grid_spec=pltpu.PrefetchScalarGridSpec(
            num_scalar_prefetch=1,
            grid=(nseq, npg // npp),
            in_specs=[page(u) for u in range(npp)] + [const(pe_t), const(w), const(g2), const(so)],
            out_specs=[out, out],
            scratch_shapes=[pltpu.VMEM((nrow * CMP_PITCH, PAGE_SIZE), F32), pltpu.VMEM((nrow * CMP_PITCH, PAGE_SIZE), F32),
                            pltpu.VMEM((nrow, 2 * LANES), F32)]),
        out_shape=[jax.ShapeDtypeStruct((nseq, nrow, LANES), F32)] * 2,
        compiler_params=_cparams("arbitrary", "arbitrary"),
        name="nsa_compress",
    )(table, *([pages_t] * npp), pe_t, w, g2, so)
    fix = lambda a: a.reshape(nseq, npg, NSA_GROUPS, 2, HEAD_DIM).transpose(0, 2, 1, 3, 4).reshape(
        nseq, NSA_GROUPS, 2 * npg, HEAD_DIM)
    return fix(kc), fix(vc)


class _PageIndex:
    def __init__(self, table, fn):
        self.table = table
        self._fn = fn

    def __call__(self, b, p, pt):
        return self._fn(b, p, pt)


def _stack_heads(q):
    head = lax.broadcasted_iota(jnp.int32, q.shape, 1) >> 6
    qf = q.astype(F32)
    return jnp.concatenate([jnp.where(head == r, qf, 0.0) for r in range(NSA_REP)], axis=0).astype(BF16)


def _unstack_heads(o4, tq):
    head = lax.broadcasted_iota(jnp.int32, (tq, o4.shape[1]), 1) >> 6
    out = jnp.zeros((tq, o4.shape[1]), F32)
    for r in range(NSA_REP):
        out = jnp.where(head == r, o4[r * tq:(r + 1) * tq, :], out)
    return out


def _nsa_cmp_kernel(q0, ns_rows, q_ref, kc_ref, vc_ref, oc_ref, sel_ref):
    i = pl.program_id(2)
    tq = q_ref.shape[1]
    nc = kc_ref.shape[2]
    qst = _stack_heads(q_ref[0])
    kc, vc = kc_ref[0, 0], vc_ref[0, 0]
    base = q0 + i * tq
    qpos = base + (lax.broadcasted_iota(jnp.int32, (NSA_REP * tq, nc), 0) & (tq - 1))
    blk_end = (lax.broadcasted_iota(jnp.int32, (NSA_REP * tq, nc), 1) + 1) * NSA_BLOCK - 1
    ok = blk_end <= qpos
    s = jnp.where(ok, _dot_nt(qst, kc), NEG_INF)
    e = jnp.exp(s - jnp.max(s, axis=-1, keepdims=True))
    p = jnp.where(ok, e / jnp.sum(e, axis=-1, keepdims=True), 0.0)
    oc_ref[0] = _unstack_heads(_dot(p.astype(BF16), vc), tq)
    imp = p[0:tq, :]
    for r in range(1, NSA_REP):
        imp = imp + p[r * tq:(r + 1) * tq, :]
    qpos_col = base + lax.broadcasted_iota(jnp.int32, (ns_rows, tq), 1)
    sel_ref[0, 0] = _select_blocks(imp.T, qpos_col, ns_rows).T.astype(BF16)


def _select_blocks(imp, qpos, ns_rows):
    nc, cols = imp.shape
    if ns_rows > nc:
        imp = jnp.concatenate([imp, jnp.zeros((ns_rows - nc, cols), F32)], axis=0)
    blk = lax.broadcasted_iota(jnp.int32, (ns_rows, cols), 0)
    cur = qpos >> 6
    forced = (blk == 0) | (blk == cur) | (blk == cur - 1)
    imp = jnp.where(forced, NSA_FORCED, imp)
    imp = jnp.where(blk <= cur, imp, NEG_INF)
    taken = jnp.float32(-3e38)
    blk_f = blk.astype(F32)

    def pick(_, imp):
        m = jnp.max(imp, axis=0, keepdims=True)
        first = jnp.min(jnp.where(imp == m, blk_f, float(ns_rows)), axis=0, keepdims=True)
        return jnp.where(blk_f == first, taken, imp)

    imp = lax.fori_loop(0, NSA_TOPN, pick, imp)
    return jnp.where(imp == taken, 0.0, NEG_INF)


def _nsa_cmp_sample_kernel(q0, t_seq, ns_rows, q_ref, kc_ref, vc_ref, oc_ref, sel_ref):
    ntok = q_ref.shape[0]
    nkey = kc_ref.shape[1]
    nc = nkey // (ntok // t_seq)
    rows = NSA_REP * ntok
    qst = _stack_heads(q_ref[...])
    row = lax.broadcasted_iota(jnp.int32, (rows, nkey), 0) & (ntok - 1)
    col = lax.broadcasted_iota(jnp.int32, (rows, nkey), 1)
    qpos = q0 + (row & (t_seq - 1))
    same_seq = (row >> (t_seq.bit_length() - 1)) == (col >> (nc.bit_length() - 1))
    ok = same_seq & (((col & (nc - 1)) + 1) * NSA_BLOCK - 1 <= qpos)
    s = jnp.where(ok, _dot_nt(qst, kc_ref[0]), NEG_INF)
    e = jnp.exp(s - jnp.max(s, axis=-1, keepdims=True))
    p = jnp.where(ok, e / jnp.sum(e, axis=-1, keepdims=True), 0.0)
    oc_ref[...] = _unstack_heads(_dot(p.astype(BF16), vc_ref[0]), ntok)
    own = p[:, 0:nc]
    for u in range(1, nkey // nc):
        own = own + p[:, u * nc:(u + 1) * nc]
    imp = own[0:ntok, :]
    for r in range(1, NSA_REP):
        imp = imp + own[r * ntok:(r + 1) * ntok, :]
    qpos_col = q0 + (lax.broadcasted_iota(jnp.int32, (ns_rows, ntok), 1) & (t_seq - 1))
    sel_ref[0] = _select_blocks(imp.T, qpos_col, ns_rows).T.astype(BF16)


def nsa_cmp_select_sample(qn, kc, vc, q0, t_seq, ns_rows):
    ntok = qn.shape[0]
    nseq, _, nc, _ = kc.shape
    flat = lambda a: _tile_lanes4(a.transpose(1, 0, 2, 3).reshape(1, NSA_GROUPS, nseq * nc, HEAD_DIM))[0]
    kv = pl.BlockSpec((1, nseq * nc, 256), lambda g: (g, 0, 0))
    return pl.pallas_call(
        functools.partial(_nsa_cmp_sample_kernel, q0, t_seq, ns_rows),
        grid=(NSA_GROUPS,),
        in_specs=[pl.BlockSpec((ntok, 256), lambda g: (0, g)), kv, kv],
        out_specs=[pl.BlockSpec((ntok, 256), lambda g: (0, g)),
                   pl.BlockSpec((1, ntok, ns_rows), lambda g: (g, 0, 0))],
        out_shape=[jax.ShapeDtypeStruct((ntok, 1024), F32),
                   jax.ShapeDtypeStruct((NSA_GROUPS, ntok, ns_rows), BF16)],
        compiler_params=_cparams("parallel"),
        name="nsa_cmp_select_sample",
    )(qn, flat(kc), flat(vc))


def _tile_lanes4(x):
    return jnp.tile(x, (1, 1, 1, NSA_REP)).astype(BF16)


def nsa_cmp_select(qn, kc, vc, q0, ns_rows):
    b, t, _ = qn.shape
    tq = min(512, t)
    nc = kc.shape[2]
    kv = pl.BlockSpec((1, 1, nc, 256), lambda bi, g, i: (bi, g, 0, 0))
    return pl.pallas_call(
        functools.partial(_nsa_cmp_kernel, q0, ns_rows),
        grid=(b, NSA_GROUPS, t // tq),
        in_specs=[pl.BlockSpec((1, tq, 256), lambda bi, g, i: (bi, i, g)), kv, kv],
        out_specs=[pl.BlockSpec((1, tq, 256), lambda bi, g, i: (bi, i, g)),
                   pl.BlockSpec((1, 1, tq, ns_rows), lambda bi, g, i: (bi, g, i, 0))],
        out_shape=[jax.ShapeDtypeStruct((b, t, 1024), F32),
                   jax.ShapeDtypeStruct((b, NSA_GROUPS, t, ns_rows), BF16)],
        compiler_params=_cparams("parallel", "parallel", "parallel"),
        name="nsa_cmp_select",
    )(qn, _tile_lanes4(kc), _tile_lanes4(vc))


def _head_rows(q):
    lane = lax.broadcasted_iota(jnp.int32, (q.shape[0], LANES), 1)
    out = []
    for r in range(NSA_REP):
        slab = q[:, (r // 2) * LANES:(r // 2 + 1) * LANES]
        slab = pltpu.roll(slab, HEAD_DIM, 1) if r % 2 else slab
        out.append(jnp.where(lane < HEAD_DIM, slab, 0.0))
    return out


def _store_heads(o_ref, row0, acc, tq):
    lane = lax.broadcasted_iota(jnp.int32, (tq, LANES), 1)
    o4 = acc / pltpu.roll(acc, HEAD_DIM, 1)
    for half in range(NSA_REP // 2):
        even = o4[(2 * half) * tq:(2 * half + 1) * tq, :]
        odd = pltpu.roll(o4[(2 * half + 1) * tq:(2 * half + 2) * tq, :], HEAD_DIM, 1)
        o_ref[0, row0:row0 + tq, half * LANES:(half + 1) * LANES] = jnp.where(lane < HEAD_DIM, even, odd)


WIN_SUB = 8


def _nsa_window_kernel(nsub, q_ref, kv_ref, o_ref):
    i = pl.program_id(2)
    tq = q_ref.shape[1] // nsub
    ntile = (NSA_WINDOW + 2 * tq) // LANES
    span = ntile * LANES
    rows = NSA_REP * tq
    for pair in range(nsub // 2):
        j0 = jnp.maximum((i * nsub + 2 * pair) * tq - NSA_WINDOW, 0) // LANES
        k_aug = jnp.concatenate([kv_ref[0, j0 + u, 0, 0:2 * HEAD_DIM, :] for u in range(ntile)], axis=1)
        v_aug = jnp.concatenate([kv_ref[0, j0 + u, 0, 2 * HEAD_DIM:4 * HEAD_DIM, :] for u in range(ntile)], axis=1)
        kpos = j0 * LANES + lax.broadcasted_iota(jnp.int32, (rows, span), 1)
        for a in range(2 * pair, 2 * pair + 2):
            lhs = jnp.concatenate(_head_rows(q_ref[0, a * tq:(a + 1) * tq, :].astype(F32)), axis=0).astype(BF16)
            qpos = (i * nsub + a) * tq + (lax.broadcasted_iota(jnp.int32, (rows, span), 0) & (tq - 1))
            ok = (kpos <= qpos) & (kpos > qpos - NSA_WINDOW)
            s = jnp.where(ok, _dot(lhs, k_aug), NEG_INF)
            e = jnp.exp2(s - jnp.max(s, axis=-1, keepdims=True))
            _store_heads(o_ref, a * tq, _dot_nt(e.astype(BF16), v_aug), tq)


def nsa_window_prompt(qr, kwin):
    b, t, _ = qr.shape
    tq = WIN_SUB * 128
    nt = kwin.shape[1]
    return pl.pallas_call(
        functools.partial(_nsa_window_kernel, WIN_SUB),
        grid=(b, NSA_GROUPS, t // tq),
        in_specs=[pl.BlockSpec((1, tq, 256), lambda bi, g, i: (bi, i, g)),
                  pl.BlockSpec((1, nt, 1, 4 * HEAD_DIM, LANES), lambda bi, g, i: (bi, 0, g, 0, 0))],
        out_specs=pl.BlockSpec((1, tq, 256), lambda bi, g, i: (bi, i, g)),
        out_shape=jax.ShapeDtypeStruct((b, t, 1024), F32),
        compiler_params=_cparams("parallel", "parallel", "arbitrary"),
        name="nsa_window",
    )(qr, kwin)


SEL_SUB = 8
SEL_KT = 2


def _nsa_selected_kernel(kt, q_ref, sel_ref, kv_ref, e_ref, o_ref, m_ref, acc_ref):
    i = pl.program_id(2)
    nsub = m_ref.shape[0]
    tq = q_ref.shape[1] // nsub
    tk = kt * kv_ref.shape[4]
    rows = NSA_REP * tq

    def lhs(a):
        sel = sel_ref[0, 0, a * tq:(a + 1) * tq, :].astype(F32)
        heads = _head_rows(q_ref[0, a * tq:(a + 1) * tq, :].astype(F32))
        return jnp.concatenate([jnp.concatenate([h, sel], axis=1) for h in heads], axis=0).astype(BF16)

    qs = [lhs(a) for a in range(nsub)]
    _softmax_init(m_ref, acc_ref)

    def tile(j, causal):
        span = lambda piece: jnp.concatenate([piece(j * kt + u) for u in range(kt)], axis=1)
        k_aug = jnp.concatenate([span(lambda jj: kv_ref[0, jj, 0, 0:2 * HEAD_DIM, :]), span(lambda jj: e_ref[jj])],
                                axis=0)
        v_aug = span(lambda jj: kv_ref[0, jj, 0, 2 * HEAD_DIM:4 * HEAD_DIM, :])
        for a in range(nsub):
            w = min(tk, _round_up((a + 1) * tq, DIAG_WIDTH_STEP)) if causal else tk
            ka, va = k_aug[:, 0:w], v_aug[:, 0:w]
            s = _dot(qs[a], ka)
            if causal:
                qpos = (i * nsub + a) * tq + (lax.broadcasted_iota(jnp.int32, (rows, w), 0) & (tq - 1))
                kpos = j * tk + lax.broadcasted_iota(jnp.int32, (rows, w), 1)
                s = jnp.where(kpos <= qpos, s, NEG_INF)
            _softmax_update_vsum(s, lambda p, va=va: _dot_nt(p, va), m_ref.at[a], acc_ref.at[a])

    nfull = (i * nsub * tq) // tk

    def body(j, c):
        tile(j, False)
        return c

    lax.fori_loop(0, nfull, body, 0)
    tile(nfull, True)
    for a in range(nsub):
        _store_heads(o_ref, a * tq, acc_ref[a], tq)


def nsa_selected_prompt(qr, sel, ksel):
    b, t, _ = qr.shape
    sub = 128
    tq = SEL_SUB * sub
    nt, _, _, tk = ksel.shape[1:]
    assert SEL_KT * tk == tq and nt % SEL_KT == 0
    nblk = sel.shape[-1]
    e3 = (jnp.arange(nblk)[None, :, None] == (jnp.arange(nt)[:, None, None] * tk + jnp.arange(tk)[None, None, :]) // NSA_BLOCK
          ).astype(BF16)
    return pl.pallas_call(
        functools.partial(_nsa_selected_kernel, SEL_KT),
        grid=(b, NSA_GROUPS, t // tq),
        in_specs=[pl.BlockSpec((1, tq, 256), lambda bi, g, i: (bi, i, g)),
                  pl.BlockSpec((1, 1, tq, nblk), lambda bi, g, i: (bi, g, i, 0)),
                  pl.BlockSpec((1, nt, 1, 4 * HEAD_DIM, tk), lambda bi, g, i: (bi, 0, g, 0, 0)),
                  pl.BlockSpec(e3.shape, lambda bi, g, i: (0, 0, 0))],
        out_specs=pl.BlockSpec((1, tq, 256), lambda bi, g, i: (bi, i, g)),
        out_shape=jax.ShapeDtypeStruct((b, t, 1024), F32),
        scratch_shapes=[pltpu.VMEM((SEL_SUB, NSA_REP * sub, LANES), F32),
                        pltpu.VMEM((SEL_SUB, NSA_REP * sub, LANES), F32)],
        compiler_params=_cparams("parallel", "parallel", "arbitrary"),
        name="nsa_selected",
    )(qr, sel, ksel, e3)


def _nsa_out_kernel(oc_ref, os_ref, ow_ref, gate_ref, ex_ref, w_ref, r_ref, o_ref):
    gate = gate_ref[...]
    comb = (_split_dot(gate, ex_ref[0]) * oc_ref[...] + _split_dot(gate, ex_ref[1]) * os_ref[...]
            + _split_dot(gate, ex_ref[2]) * ow_ref[...])
    o_ref[...] = r_ref[...] + _dot(comb.astype(BF16), w_ref[...])


def nsa_out(oc, os_, ow, gates, w_o, res):
    n, d = res.shape
    tm = min(256, n)
    lane = jnp.arange(1024) // HEAD_DIM
    ex = jnp.stack([(jnp.arange(LANES)[:, None] == lane[None, :] * 3 + k) for k in range(3)]).astype(BF16)
    row = lambda w: pl.BlockSpec((tm, w), lambda i: (i, 0))
    return pl.pallas_call(
        _nsa_out_kernel,
        grid=(n // tm,),
        in_specs=[row(1024), row(1024), row(1024), row(LANES),
                  pl.BlockSpec(ex.shape, lambda i: (0, 0, 0)),
                  pl.BlockSpec(w_o.shape, lambda i: (0, 0)), row(d)],
        out_specs=row(d),
        out_shape=jax.ShapeDtypeStruct((n, d), F32),
        compiler_params=_cparams("parallel"),
        name="nsa_out",
    )(oc, os_, ow, gates, ex, w_o, res)


def _paged_attn_kernel(npp, kv_t, extra_rows, pt_ref, q_ref, *refs):
    k_refs, v_refs = refs[:npp], refs[npp:2 * npp]
    kn_ref, vn_ref, bp_ref, bn_ref, o_ref, m_ref, l_ref, acc_ref = refs[2 * npp:]
    p = pl.program_id(1)
    last = pl.num_programs(1) - 1

    @pl.when(p == 0)
    def _():
        _softmax_init(m_ref, acc_ref, l_ref)

    def step(k, v, bias):
        if kv_t:
            s = _dot(q_ref[0], k)
            s = s if bias is None else s + bias
            _softmax_update(s, lambda e: _dot_nt(e, v), m_ref, l_ref, acc_ref)
        else:
            s = _dot_nt(q_ref[0], k) + bias
            _softmax_update(s, lambda e: _dot(e, v), m_ref, l_ref, acc_ref)

    def load(r):
        x = r[...]
        return x if kv_t else x.reshape(-1, x.shape[-1])

    @pl.when(p < last)
    def _():
        axis = 1 if kv_t else 0
        k = jnp.concatenate([load(r) for r in k_refs], axis=axis).astype(BF16)
        v = jnp.concatenate([load(r) for r in v_refs], axis=axis).astype(BF16)
        if extra_rows:
            step(jnp.concatenate([k, bp_ref[0]], axis=0), v, None)
        else:
            step(k, v, bp_ref[0])

    @pl.when(p == last)
    def _():
        step(kn_ref[0], vn_ref[0], bn_ref[0])
        o_ref[0] = _softmax_result(l_ref, acc_ref)


def paged_attention(q, pages, table, npg, npp, k_spec, v_spec, kv_t, k_new, v_new, bias_past, bias_new, past_per_step,
                    extra_rows=False):
    nseq, rows, _ = q.shape
    lv = acc_w = v_new.shape[1] if kv_t else v_new.shape[2]
    seq = lambda a: pl.BlockSpec((1,) + a.shape[1:], lambda b, p, pt: (b, 0, 0))
    nstep = npg // npp
    if extra_rows:
        bp_spec = pl.BlockSpec((1,) + bias_past.shape[1:], lambda b, p, pt: (jnp.minimum(p, nstep - 1), 0, 0))
    else:
        wpast = bias_past.shape[-1] if not past_per_step else bias_past.shape[-1] // nstep
        bp_spec = pl.BlockSpec((1, rows, wpast), lambda b, p, pt: (
            b if bias_past.shape[0] > 1 else 0, 0, jnp.minimum(p, nstep - 1) if past_per_step else 0))
    bn_spec = pl.BlockSpec((1,) + bias_new.shape[1:], lambda b, p, pt: (b if bias_new.shape[0] > 1 else 0, 0, 0))
    return pl.pallas_call(
        functools.partial(_paged_attn_kernel, npp, kv_t, extra_rows),
        grid_spec=pltpu.PrefetchScalarGridSpec(
            num_scalar_prefetch=1,
            grid=(nseq, npg // npp + 1),
            in_specs=[seq(q)] + [k_spec(u) for u in range(npp)] + [v_spec(u) for u in range(npp)]
                     + [seq(k_new), seq(v_new), bp_spec, bn_spec],
            out_specs=pl.BlockSpec((1, rows, acc_w), lambda b, p, pt: (b, 0, 0)),
            scratch_shapes=[pltpu.VMEM((rows, LANES), F32), pltpu.VMEM((rows, LANES), F32),
                            pltpu.VMEM((rows, acc_w), F32)]),
        out_shape=jax.ShapeDtypeStruct((nseq, rows, lv), F32),
        compiler_params=_cparams("arbitrary", "arbitrary"),
        name="paged_attention",
    )(table, q, *([pages] * (2 * npp)), k_new, v_new, bias_past, bias_new)


def _diff_project_kernel(x_ref, ng_ref, w_ref, cos_ref, sin_ref, g_ref, so_ref, q_ref, kvf_ref, kvb_ref):
    hn = _rms_rows(x_ref[...], ng_ref[...]).astype(BF16)
    cos, sin = cos_ref[...], sin_ref[...]
    so = so_ref[...]
    qp = _dot(hn, w_ref[:, 0:DIFF_WIDTH])
    kp = _dot(hn, w_ref[:, DIFF_WIDTH:2 * DIFF_WIDTH])
    v = _dot(hn, w_ref[:, 2 * DIFF_WIDTH:3 * DIFF_WIDTH])
    for s in range(DIFF_WIDTH // LANES):
        sl = slice(s * LANES, (s + 1) * LANES)
        q = _rope_slab(_seg64_norm(qp[:, sl], g_ref[0:1, :], so), cos, sin)
        q_ref[:, sl] = (q * QK_SCALE_LOG2).astype(BF16)
        k = _rope_slab(_seg64_norm(kp[:, sl], g_ref[1:2, :], so), cos, sin)
        kvf_ref[:, sl] = k
        kvb_ref[:, sl] = k.astype(BF16)
    kvf_ref[:, DIFF_WIDTH:] = v
    kvb_ref[:, DIFF_WIDTH:] = v.astype(BF16)


def diff_project(x, norm_g, w_in, cos, sin, qk_g):
    n, d = x.shape
    tm = min(512, n)
    g2 = jnp.tile(qk_g, (1, 2))
    so = _seg_ones()
    ng = norm_g.reshape(1, d)
    row = lambda w: pl.BlockSpec((tm, w), lambda i: (i, 0))
    return pl.pallas_call(
        _diff_project_kernel,
        grid=(n // tm,),
        in_specs=[row(d), _resident(ng), _resident(w_in), row(LANES), row(LANES), _resident(g2), _resident(so)],
        out_specs=[row(DIFF_WIDTH), row(2 * DIFF_WIDTH), row(2 * DIFF_WIDTH)],
        out_shape=[jax.ShapeDtypeStruct((n, DIFF_WIDTH), BF16), jax.ShapeDtypeStruct((n, 2 * DIFF_WIDTH), F32),
                   jax.ShapeDtypeStruct((n, 2 * DIFF_WIDTH), BF16)],
        compiler_params=_cparams("parallel"),
        name="diff_project",
    )(x, ng, w_in, cos, sin, g2, so)


def _diff_lambda(lam_ref, lam_init):
    lf = lam_ref[...]
    a = jnp.sum(lf[0:1, :] * lf[1:2, :], axis=-1, keepdims=True)
    b = jnp.sum(lf[2:3, :] * lf[3:4, :], axis=-1, keepdims=True)
    return jnp.exp(a) - jnp.exp(b) + lam_init


DIFF_SUB = 8
DIFF_SUB_TQ = 128
DIFF_TK = 1024


def _diff_flash_kernel(lam_init, tk, q_ref, k_ref, v_ref, lam_ref, o_ref, m_ref, acc_ref):
    i = pl.program_id(2)
    nsub = m_ref.shape[0]
    tq = q_ref.shape[1] // nsub
    rows = 2 * tq

    def stack_components(q):
        comp = lax.broadcasted_iota(jnp.int32, q.shape, 1) >> 6
        return jnp.concatenate([jnp.where(comp == c, q, 0.0) for c in range(2)], axis=0).astype(BF16)

    qst = [stack_components(q_ref[0, a * tq:(a + 1) * tq, :].astype(F32)) for a in range(nsub)]
    _softmax_init(m_ref, acc_ref)
    ones = jnp.ones((tk, LANES), BF16)

    def tile(j, causal):
        lo = pl.multiple_of(j * tk, tk)
        k = k_ref[0, pl.ds(lo, tk), :]
        v_aug = jnp.concatenate([v_ref[0, pl.ds(lo, tk), :], ones], axis=1)
        for a in range(nsub):
            w = min(tk, _round_up((a + 1) * tq, DIAG_WIDTH_STEP)) if causal else tk
            ka, va = k[0:w, :], v_aug[0:w, :]
            s = _dot_nt(qst[a], ka)
            if causal:
                qpos = (i * nsub + a) * tq + (lax.broadcasted_iota(jnp.int32, (rows, w), 0) & (tq - 1))
                kpos = j * tk + lax.broadcasted_iota(jnp.int32, (rows, w), 1)
                s = jnp.where(kpos <= qpos, s, NEG_INF)
            _softmax_update_vsum(s, lambda p, va=va: _dot(p, va), m_ref.at[a], acc_ref.at[a])

    nfull = (i * nsub * tq) // tk

    def body(j, c):
        tile(j, False)
        return c

    lax.fori_loop(0, nfull, body, 0)
    tile(nfull, True)
    lam = _diff_lambda(lam_ref, lam_init)
    for a in range(nsub):
        acc = acc_ref[a]
        o = acc[:, 0:LANES] / acc[:, LANES:2 * LANES]
        o_ref[0, a * tq:(a + 1) * tq, :] = o[0:tq, :] - lam * o[tq:rows, :]


def diff_flash_prompt(q, kvb, lam, lam_init):
    b, t, _ = q.shape
    sub = DIFF_SUB_TQ
    tq = DIFF_SUB * sub
    tk = DIFF_TK
    assert tk == tq and t % tk == 0
    return pl.pallas_call(
        functools.partial(_diff_flash_kernel, lam_init, tk),
        grid=(b, DIFF_HEADS, t // tq),
        in_specs=[pl.BlockSpec((1, tq, LANES), lambda bi, h, i: (bi, i, h)),
                  pl.BlockSpec((1, t, LANES), lambda bi, h, i: (bi, 0, h)),
                  pl.BlockSpec((1, t, LANES), lambda bi, h, i: (bi, 0, DIFF_HEADS + h)),
                  pl.BlockSpec(lam.shape, lambda bi, h, i: (0, 0))],
        out_specs=pl.BlockSpec((1, tq, LANES), lambda bi, h, i: (bi, i, h)),
        out_shape=jax.ShapeDtypeStruct((b, t, DIFF_WIDTH), F32),
        scratch_shapes=[pltpu.VMEM((DIFF_SUB, 2 * sub, LANES), F32), pltpu.VMEM((DIFF_SUB, 2 * sub, 2 * LANES), F32)],
        compiler_params=_cparams("parallel", "parallel", "arbitrary"),
        name="diff_flash",
    )(q, kvb, kvb, lam)


def _diff_out_kernel(lam_init, two, *refs):
    if two:
        o0_ref, o1_ref, lam_ref, g_ref, w_ref, r_ref, out_ref, h_ref = refs
        o = o0_ref[...] - _diff_lambda(lam_ref, lam_init) * o1_ref[...]
    else:
        o0_ref, g_ref, w_ref, r_ref, out_ref, h_ref = refs
        o = o0_ref[...]
    for s in range(DIFF_HEADS):
        sl = slice(s * LANES, (s + 1) * LANES)
        h_ref[:, sl] = (_rms_rows(o[:, sl], g_ref[...]) * (1.0 - lam_init)).astype(BF16)
    out_ref[...] = r_ref[...] + _dot(h_ref[...], w_ref[...])


def diff_out(o, sub_g, w_o, res, lam_init, o1=None, lam=None):
    n, d = res.shape
    tm = min(512, n)
    two = o1 is not None
    row = lambda w: pl.BlockSpec((tm, w), lambda i: (i, 0))
    const = lambda a: pl.BlockSpec(a.shape, lambda i: (0, 0))
    g = sub_g.reshape(1, LANES)
    ins = [o, o1, lam, g, w_o, res] if two else [o, g, w_o, res]
    specs = ([row(DIFF_WIDTH), row(DIFF_WIDTH), const(lam)] if two else [row(DIFF_WIDTH)]) + [const(g), const(w_o), row(d)]
    return pl.pallas_call(
        functools.partial(_diff_out_kernel, lam_init, two),
        grid=(n // tm,),
        in_specs=specs,
        out_specs=row(d),
        out_shape=jax.ShapeDtypeStruct((n, d), F32),
        scratch_shapes=[pltpu.VMEM((tm, DIFF_WIDTH), BF16)],
        compiler_params=_cparams("parallel"),
        name="diff_out",
    )(*ins)


def _gla_kernel(chunk, nchunk, t_valid, x_ref, ng_ref, w_ref, s0_ref, wa_ref, ba_ref, g_ref, tri_ref,
                y_ref, sfin_ref, st_ref, p_ref):
    step = pl.program_id(1)

    @pl.when(step == 0)
    def _():
        st_ref[...] = s0_ref[0]

    p_ref[0] = _dot(_rms_rows(x_ref[0], ng_ref[...]).astype(BF16), w_ref[...])

    nk = GLA_HEADS * GLA_DK
    nv = GLA_HEADS * GLA_DV
    tri = tri_ref[...]
    causal = lax.broadcasted_iota(jnp.int32, (chunk, chunk), 0) >= lax.broadcasted_iota(jnp.int32, (chunk, chunk), 1)
    for ci in range(nchunk):
        r0 = ci * chunk
        rows = slice(r0, r0 + chunk)
        a1 = p_ref[0, rows, GLA_MAIN:GLA_MAIN + LANES].astype(BF16)
        z = _dot(a1, wa_ref[...]) + ba_ref[...]
        log_a = (jnp.minimum(z, 0.0) - jnp.log(1.0 + jnp.exp(-jnp.abs(z)))) * (1.0 / GLA_TAU)
        if t_valid < chunk:
            log_a = jnp.where(lax.broadcasted_iota(jnp.int32, log_a.shape, 0) < t_valid, log_a, 0.0)
        cum = _split_dot_left(tri, log_a)
        for h in range(GLA_HEADS):
            ksl = slice(h * GLA_DK, (h + 1) * GLA_DK)
            vsl = slice(h * GLA_DV, (h + 1) * GLA_DV)
            q = p_ref[0, rows, h * GLA_DK:(h + 1) * GLA_DK] * (GLA_DK ** -0.5)
            k = p_ref[0, rows, nk + h * GLA_DK:nk + (h + 1) * GLA_DK]
            v = p_ref[0, rows, 2 * nk + h * GLA_DV:2 * nk + (h + 1) * GLA_DV]
            r = p_ref[0, rows, 2 * nk + nv + h * GLA_DV:2 * nk + nv + (h + 1) * GLA_DV]
            cm = cum[:, ksl]
            last = cm[chunk - 1:chunk, :]
            qe = (q * jnp.exp(cm)).astype(BF16)
            ke = (k * jnp.exp(-cm)).astype(BF16)
            vb = v.astype(BF16)
            att = jnp.where(causal, _dot_nt(qe, ke), 0.0)
            st = st_ref[h]
            o = _dot_nt(qe, st.astype(BF16)) + _dot(att.astype(BF16), vb)
            kd = (k * jnp.exp(last - cm)).astype(BF16)
            st_ref[h] = st * jnp.exp(last) + _dot(v.T.astype(BF16), kd)
            y_ref[0, rows, vsl] = _rms_rows(o, g_ref[...]) * _silu(r)

    @pl.when(step == pl.num_programs(1) - 1)
    def _():
        sfin_ref[0] = st_ref[...]


def gla_core(x, norm_g, w_in, s0_t, w_a2, b_a, out_g, chunk, nchunk, t_valid):
    b, t, d = x.shape
    rows = chunk * nchunk
    ng = norm_g.reshape(1, d)
    wa = jnp.zeros((LANES, GLA_HEADS * GLA_DK), F32).at[:GLA_RANK].set(w_a2).astype(BF16)
    ba = b_a.reshape(1, -1)
    g = out_g.reshape(1, GLA_DV)
    tri = (jnp.arange(chunk)[:, None] >= jnp.arange(chunk)[None, :]).astype(BF16)
    const = lambda a: pl.BlockSpec(a.shape, lambda bi, s: (0,) * a.ndim)
    st_spec = pl.BlockSpec((1, GLA_HEADS, GLA_DV, GLA_DK), lambda bi, s: (bi, 0, 0, 0))
    return pl.pallas_call(
        functools.partial(_gla_kernel, chunk, nchunk, t_valid),
        grid=(b, t // rows),
        in_specs=[pl.BlockSpec((1, rows, d), lambda bi, s: (bi, s, 0)), const(ng), const(w_in), st_spec,
                  const(wa), const(ba), const(g), const(tri)],
        out_specs=[pl.BlockSpec((1, rows, GLA_HEADS * GLA_DV), lambda bi, s: (bi, s, 0)), st_spec],
        out_shape=[jax.ShapeDtypeStruct((b, t, GLA_HEADS * GLA_DV), F32),
                   jax.ShapeDtypeStruct((b, GLA_HEADS, GLA_DV, GLA_DK), F32)],
        scratch_shapes=[pltpu.VMEM((GLA_HEADS, GLA_DV, GLA_DK), F32), pltpu.VMEM((1, rows, GLA_IN_PAD), F32)],
        compiler_params=_cparams("parallel", "arbitrary"),
        name="gla_core",
    )(x, ng, w_in, s0_t, wa, ba, g, tri)


def _pad_cols(w, width):
    return jnp.pad(w, ((0, 0), (0, width - w.shape[1]))).astype(BF16)


def _pad_axis(x, axis, size):
    pad = [(0, 0)] * x.ndim
    pad[axis] = (0, size - x.shape[axis])
    return jnp.pad(x, pad)


def _group_diag(q, t_seq):
    nseq = q.shape[0] // t_seq
    qg = q.reshape(nseq, t_seq, NSA_GROUPS, NSA_REP, HEAD_DIM).transpose(0, 2, 1, 3, 4)
    eye = jnp.eye(NSA_GROUPS, dtype=q.dtype)
    out = qg[:, :, :, :, None, :] * eye[None, :, None, None, :, None]
    return out.reshape(nseq, NSA_GROUPS * t_seq * NSA_REP, NSA_GROUPS * HEAD_DIM)


def _group_undiag(o, t_seq):
    nseq = o.shape[0]
    o6 = o.reshape(nseq, NSA_GROUPS, t_seq, NSA_REP, NSA_GROUPS, HEAD_DIM)
    od = jnp.stack([o6[:, g, :, :, g, :] for g in range(NSA_GROUPS)], axis=1)
    return od.transpose(0, 2, 1, 3, 4).reshape(nseq * t_seq, NSA_HEADS * HEAD_DIM)


def _seq_cols(x_t, nseq, t_seq):
    r = x_t.shape[0]
    return _pad_axis(x_t.reshape(r, nseq, t_seq).transpose(1, 0, 2), 2, PAGE_SIZE)


def _nsa_layer(s, xp, xs, seq_p, t_s, past_len, cache_t, state_win_t, page_table, norm_g, w_in, qk_g, pe, w_phi, w_o):
    bp = xp.shape[0] // seq_p
    bs = xs.shape[0] // t_s
    w_o_b = w_o.astype(BF16)
    npg = past_len // PAGE_SIZE

    qn, qr, gates, rows_t, win_t, ksel, kwin = nsa_project(xp, norm_g, w_in, qk_g, jnp.arange(seq_p), bp)
    own = _PageIndex(jnp.zeros((1, 1), jnp.int32), lambda b, p, pt: (b, 0, p))
    kc, vc = nsa_compress(rows_t, own, bp, seq_p // PAGE_SIZE, pe, w_phi, qk_g[3])
    qn3, qr3 = qn.reshape(bp, seq_p, 1024), qr.reshape(bp, seq_p, 1024)
    oc, sel = nsa_cmp_select(qn3, kc, vc, 0, seq_p // NSA_BLOCK)
    os_ = nsa_selected_prompt(qr3, sel, ksel)
    ow = nsa_window_prompt(qr3, kwin)
    xp_new = nsa_out(oc.reshape(-1, 1024), os_.reshape(-1, 1024), ow.reshape(-1, 1024), gates, w_o_b, xp)
    kv_p = rows_t.reshape(bp, 4, NSA_GROUPS, HEAD_DIM, seq_p).transpose(0, 4, 1, 2, 3)
    wlen = min(NSA_WINDOW, seq_p)
    win_p = win_t[:, :, seq_p - wlen:].reshape(bp, 2, NSA_GROUPS, HEAD_DIM, wlen).transpose(0, 4, 1, 2, 3)

    pos_s = past_len + jnp.arange(t_s)
    qn, qr, gates, rows_t, win_t, _, _ = nsa_project(xs, norm_g, w_in, qk_g, jnp.tile(pos_s, bs), 1)
    rows_t, win_t = rows_t[0], win_t[0]
    layer_pages = cache_t.shape[0] // (state_win_t.shape[0] // bs)
    table = page_table + s * layer_pages
    paged = _PageIndex(table, lambda b, p, pt: (pt[b, p], 0, 0))
    kc, vc = nsa_compress(cache_t, paged, bs, npg, pe, w_phi, qk_g[3])
    ns = -(-(past_len + t_s) // NSA_BLOCK)
    oc, sel = nsa_cmp_select_sample(qn, kc, vc, past_len, t_s, 256)
    nrow = NSA_GROUPS * t_s * NSA_REP
    sel_f = sel[:, :, :ns].astype(F32).reshape(NSA_GROUPS, bs, t_s, ns).transpose(1, 0, 2, 3)
    by_row = lambda a: jnp.broadcast_to(a[:, :, :, None, :], a.shape[:3] + (NSA_REP, a.shape[-1])).reshape(
        a.shape[0], nrow, a.shape[-1])
    new_ok = jnp.arange(PAGE_SIZE)[None, :] <= jnp.arange(t_s)[:, None]
    new_bias = by_row(jnp.where(new_ok[None, None], sel_f[..., past_len // NSA_BLOCK][..., None], NEG_INF))
    q_bd = _group_diag(qr, t_s)
    npp = min(NSA_PAGES_PER_STEP, npg)
    kpage = lambda blk: (lambda u: pl.BlockSpec((None, NSA_GW, PAGE_SIZE),
                                                lambda b, p, pt: (pt[b, jnp.minimum(p * npp + u, npg - 1)], blk, 0)))
    k_new = _seq_cols(rows_t[2 * NSA_GW:3 * NSA_GW], bs, t_s).astype(BF16)
    v_new = _seq_cols(rows_t[3 * NSA_GW:4 * NSA_GW], bs, t_s).astype(BF16)
    nblk = sel.shape[-1]
    sel_rows = by_row(sel.reshape(NSA_GROUPS, bs, t_s, nblk).transpose(1, 0, 2, 3))
    q_aug = jnp.concatenate([q_bd, sel_rows], axis=2)
    k_new_aug = jnp.concatenate([k_new, jnp.zeros((bs, nblk, PAGE_SIZE), BF16)], axis=1)
    wstep = npp * PAGE_SIZE
    key_blk = (jnp.arange(npg // npp)[:, None, None] * wstep + jnp.arange(wstep)[None, None, :]) // NSA_BLOCK
    e_steps = (jnp.arange(nblk)[None, :, None] == key_blk).astype(BF16)
    os_ = _group_undiag(paged_attention(q_aug, cache_t, table, npg, npp, kpage(2), kpage(3), True, k_new_aug, v_new,
                                        e_steps, new_bias, True, extra_rows=True), t_s)
    wbuf = state_win_t.shape[-1]
    wpg = wbuf // PAGE_SIZE
    wpage = lambda blk: (lambda u: pl.BlockSpec((None, NSA_GW, PAGE_SIZE),
                                                lambda b, p, pt: (s * bs + b, blk, jnp.minimum(p * wpg + u, wpg - 1))))
    kidx = jnp.arange(wbuf + PAGE_SIZE)
    wpos = jnp.where(kidx < wbuf, past_len - wbuf + kidx, past_len + kidx - wbuf)
    w_ok = ((wpos[None, :] <= pos_s[:, None]) & (wpos[None, :] > pos_s[:, None] - NSA_WINDOW) & (wpos[None, :] >= 0)
            & (kidx[None, :] < wbuf + t_s))
    wbias = by_row(jnp.broadcast_to(jnp.where(w_ok, 0.0, NEG_INF).astype(F32)[None, None],
                                    (1, NSA_GROUPS, t_s, wbuf + PAGE_SIZE)))
    kw_new = _seq_cols(win_t[0:NSA_GW], bs, t_s).astype(BF16)
    vw_new = _seq_cols(win_t[NSA_GW:2 * NSA_GW], bs, t_s).astype(BF16)
    ow = _group_undiag(paged_attention(q_bd, state_win_t, jnp.zeros((1, 1), jnp.int32), wpg, wpg, wpage(0), wpage(1), True,
                                       kw_new, vw_new, wbias[:, :, :wbuf], wbias[:, :, wbuf:], True), t_s)
    xs_new = nsa_out(oc, os_, ow, gates, w_o_b, xs)
    kv_s = rows_t.reshape(4, NSA_GROUPS, HEAD_DIM, bs, t_s).transpose(3, 4, 0, 1, 2)
    win_new = win_t.reshape(2 * NSA_GW, bs, t_s).transpose(1, 0, 2)
    win_all = jnp.concatenate([state_win_t[s * bs:(s + 1) * bs], win_new], axis=2)
    wlen = min(NSA_WINDOW, win_all.shape[2])
    win_s = win_all[:, :, win_all.shape[2] - wlen:].reshape(bs, 2, NSA_GROUPS, HEAD_DIM, wlen).transpose(0, 4, 1, 2, 3)
    return xp_new, xs_new, kv_p, kv_s, win_p, win_s


def _diff_layer(layer, xp, xs, seq_p, t_s, past_len, cache, page_table, norm_g, w_in, qk_g, lam, sub_g, w_o):
    bp = xp.shape[0] // seq_p
    bs = xs.shape[0] // t_s
    npg = past_len // PAGE_SIZE
    lam_init = 0.8 - 0.6 * math.exp(-0.3 * layer)
    w_in_b = w_in.astype(BF16)
    w_o_b = w_o.astype(BF16)
    cos_p, sin_p = _rope_tables(jnp.arange(seq_p))
    q, kvf, kvb = diff_project(xp, norm_g, w_in_b, jnp.tile(cos_p, (bp, 1)), jnp.tile(sin_p, (bp, 1)), qk_g)
    o = diff_flash_prompt(q.reshape(bp, seq_p, DIFF_WIDTH), kvb.reshape(bp, seq_p, 2 * DIFF_WIDTH), lam, lam_init)
    xp_new = diff_out(o.reshape(-1, DIFF_WIDTH), sub_g, w_o_b, xp, lam_init)
    kv_p = kvf.reshape(bp, seq_p, 2, DIFF_HEADS, 2 * HEAD_DIM)
    pos_s = past_len + jnp.arange(t_s)
    cos_s, sin_s = _rope_tables(pos_s)
    q, kvf, kvb = diff_project(xs, norm_g, w_in_b, jnp.tile(cos_s, (bs, 1)), jnp.tile(sin_s, (bs, 1)), qk_g)
    nrow = DIFF_HEADS * 2 * t_s
    q5 = q.reshape(bs, t_s, DIFF_HEADS, 2, HEAD_DIM).transpose(0, 2, 3, 1, 4)
    q_rows = (q5[:, :, :, :, None, :] * jnp.eye(2, dtype=q.dtype)[None, None, :, None, :, None]).reshape(
        bs, nrow, 2 * HEAD_DIM)
    kv5 = kvb.reshape(bs, t_s, 2, DIFF_HEADS, 2 * HEAD_DIM)
    k_new = _pad_axis(kv5[:, :, 0], 1, PAGE_SIZE).reshape(bs, PAGE_SIZE * DIFF_HEADS, 2 * HEAD_DIM)
    v_new = _pad_axis(kv5[:, :, 1], 1, PAGE_SIZE).reshape(bs, PAGE_SIZE * DIFF_HEADS, 2 * HEAD_DIM)
    npp = min(DIFF_PAGES_PER_STEP, npg)
    row_h = jnp.arange(nrow) // (2 * t_s)
    row_t = jnp.arange(nrow) % t_s
    slot_h = jnp.arange(PAGE_SIZE * DIFF_HEADS) % DIFF_HEADS
    slot_tok = jnp.arange(PAGE_SIZE * DIFF_HEADS) // DIFF_HEADS
    same_head = row_h[:, None] == slot_h[None, :]
    bias_page = jnp.where(same_head, 0.0, NEG_INF).astype(F32)
    bias_past = jnp.tile(bias_page, (1, npp))[None]
    bias_new = jnp.where(same_head & (slot_tok[None, :] <= row_t[:, None]), 0.0, NEG_INF).astype(F32)[None]
    page = lambda slot: (lambda u: pl.BlockSpec(
        (None, PAGE_SIZE, None, DIFF_HEADS, 2 * HEAD_DIM),
        lambda b, p, pt: (pt[b, jnp.minimum(p * npp + u, npg - 1)], 0, slot, 0, 0)))
    o = paged_attention(q_rows, cache, page_table, npg, npp, page(0), page(1), False, k_new, v_new,
                        bias_past, bias_new, False)
    od = o.reshape(bs, DIFF_HEADS, 2, t_s, 2 * HEAD_DIM).transpose(2, 0, 3, 1, 4).reshape(2, bs * t_s, DIFF_WIDTH)
    xs_new = diff_out(od[0], sub_g, w_o_b, xs, lam_init, o1=od[1], lam=lam)
    kv_s = kvf.reshape(bs, t_s, 2, DIFF_HEADS, 2 * HEAD_DIM)
    return xp_new, xs_new, kv_p, kv_s


def _gla_layer(xp, xs, seq_p, t_s, state, norm_g, w_in, w_a2, b_a, out_g, w_o):
    bp = xp.shape[0] // seq_p
    bs = xs.shape[0] // t_s
    w_in_b = _pad_cols(w_in, GLA_IN_PAD)
    w_o_b = w_o.astype(BF16)
    d = xp.shape[1]
    chunk = min(GLA_CHUNK, seq_p)
    s0 = jnp.zeros((bp, GLA_HEADS, GLA_DV, GLA_DK), F32)
    y, st = gla_core(xp.reshape(bp, seq_p, d), norm_g, w_in_b, s0, w_a2, b_a, out_g, chunk,
                     4 if seq_p % (4 * chunk) == 0 else 1, chunk)
    xp_new = matmul_residual(y.reshape(-1, GLA_HEADS * GLA_DV), w_o_b, xp)
    st_p = st.transpose(0, 1, 3, 2)
    chunk_s = GLA_CHUNK
    xs_pad = _pad_axis(xs.reshape(bs, t_s, d), 1, chunk_s)
    y, st = gla_core(xs_pad, norm_g, w_in_b, state.transpose(0, 1, 3, 2), w_a2, b_a, out_g, chunk_s, 1, t_s)
    xs_new = matmul_residual(y[:, :t_s].reshape(-1, GLA_HEADS * GLA_DV), w_o_b, xs)
    st_s = st.transpose(0, 1, 3, 2)
    return xp_new, xs_new, st_p, st_s


def kernel(x_prompt, x_sample, cache_nsa_kv, state_nsa_win, cache_diff_kv, state_gla, state_ffn, page_table, norm_g, ffn_w_up, ffn_conv_w, ffn_conv_b, ffn_w_down, nsa_w_in, nsa_qk_g, nsa_pe, nsa_w_phi, nsa_w_o, diff_w_in, diff_qk_g, diff_lam, diff_sub_g, diff_w_o, gla_w_in, gla_w_a2, gla_b_a, gla_out_g, gla_w_o):
    bp, seq_p, d = x_prompt.shape
    bs, t_s, _ = x_sample.shape
    past_len = page_table.shape[1] * PAGE_SIZE
    xp = x_prompt.reshape(bp * seq_p, d)
    xs = x_sample.reshape(bs * t_s, d)
    cache_t = cache_nsa_kv.transpose(0, 1, 3, 4, 5, 2).reshape(-1, 4 * NSA_GW, PAGE_SIZE)
    win_t = state_nsa_win.transpose(0, 1, 3, 4, 5, 2).reshape(-1, 2 * NSA_GW, state_nsa_win.shape[2])
    nsa_kv_p, nsa_kv_s, nsa_win_p, nsa_win_s = [], [], [], []
    diff_kv_p, diff_kv_s, gla_p, gla_s, ffn_p, ffn_s = [], [], [], [], [], []
    for i in range(DEPTH):
        kind, s = i % N_MIXERS, i // N_MIXERS
        if kind == 0:
            xp, xs, kvp, kvs, wp, ws = _nsa_layer(
                s, xp, xs, seq_p, t_s, past_len, cache_t, win_t, page_table,
                norm_g[i, 0], nsa_w_in[s], nsa_qk_g[s], nsa_pe[s], nsa_w_phi[s], nsa_w_o[s])
            nsa_kv_p.append(kvp); nsa_kv_s.append(kvs); nsa_win_p.append(wp); nsa_win_s.append(ws)
        elif kind == 1:
            xp, xs, kvp, kvs = _diff_layer(
                i, xp, xs, seq_p, t_s, past_len, cache_diff_kv[s], page_table, norm_g[i, 0], diff_w_in[s],
                diff_qk_g[s], diff_lam[s], diff_sub_g[s], diff_w_o[s])
            diff_kv_p.append(kvp); diff_kv_s.append(kvs)
        else:
            xp, xs, stp, sts = _gla_layer(xp, xs, seq_p, t_s, state_gla[s], norm_g[i, 0], gla_w_in[s], gla_w_a2[s],
                                          gla_b_a[s], gla_out_g[s], gla_w_o[s])
            gla_p.append(stp); gla_s.append(sts)
        w_up_b = ffn_w_up[i].astype(BF16)
        w_dn_b = ffn_w_down[i].astype(BF16)
        xp, tail_p = ffn_prompt(xp, norm_g[i, 1], w_up_b, ffn_conv_w[i], ffn_conv_b[i], w_dn_b, seq_p)
        xs, tail_s = ffn_sample(xs, norm_g[i, 1], w_up_b, ffn_conv_w[i], ffn_conv_b[i], w_dn_b, state_ffn[i], t_s)
        ffn_p.append(tail_p); ffn_s.append(tail_s)
    return (xp.reshape(bp, seq_p, d), xs.reshape(bs, t_s, d),
            jnp.stack(nsa_kv_p), jnp.stack(nsa_kv_s), jnp.stack(nsa_win_p), jnp.stack(nsa_win_s),
            jnp.stack(diff_kv_p), jnp.stack(diff_kv_s), jnp.stack(gla_p), jnp.stack(gla_s),
            jnp.stack(ffn_p), jnp.stack(ffn_s))
```
